```python
import math
import jax, jax.numpy as jnp
from jax import lax
import numpy as np

D_MODEL = 1024
BATCH = 8
SEQ = 2048
DEPTH = 1
DEC_BATCH = 128
DEC_SEQ = 8
PAST_LEN = 16384
PAGE_SIZE = 128

MIX_WIDTH = D_MODEL
RWKV_WIDTH = MIX_WIDTH // 2
RWKV_HEAD = 64
RWKV_HEADS = RWKV_WIDTH // RWKV_HEAD
HGRN_WIDTH = MIX_WIDTH - RWKV_WIDTH
HGRN_HEAD = 128
HGRN_HEADS = HGRN_WIDTH // HGRN_HEAD
DECAY_RANK = 64
AICL_RANK = 64
GATE_RANK = 128
RWKV_PROJ = 3 * RWKV_WIDTH + DECAY_RANK + AICL_RANK + GATE_RANK
HGRN_PROJ = 4 * HGRN_WIDTH
PROJ = RWKV_PROJ + HGRN_PROJ
RWKV_SPLITS = (RWKV_WIDTH, 2 * RWKV_WIDTH, 3 * RWKV_WIDTH, 3 * RWKV_WIDTH + DECAY_RANK,
               3 * RWKV_WIDTH + DECAY_RANK + AICL_RANK)
HGRN_SPLITS = (HGRN_WIDTH, 2 * HGRN_WIDTH, 3 * HGRN_WIDTH)
D_FF = 4 * D_MODEL
HGRN_CHUNK = 16
ALPHA = (2.0 * DEPTH) ** 0.25
BETA = (8.0 * DEPTH) ** -0.25
LN_EPS = 1e-5
GN_EPS = RWKV_HEAD * 1e-5
RMS_EPS = 1e-6

kernel_name = 'hybrid_rwkv7_hgrn2_deepnorm_step'


def layer_norm(x, g, b):
    x = x.astype(jnp.float32)
    mu = jnp.mean(x, -1, keepdims=True)
    var = jnp.mean(jnp.square(x - mu), -1, keepdims=True)
    return (x - mu) * lax.rsqrt(var + LN_EPS) * g + b


def rwkv7_recurrence(r, w, k, v, a, b, s0):
    def step(S, inp):
        r_t, w_t, k_t, v_t, a_t, b_t = inp
        sa = jnp.einsum('bhij,bhj->bhi', S, a_t)
        S = S * w_t[:, :, None, :] + sa[..., None] * b_t[:, :, None, :] + v_t[..., None] * k_t[:, :, None, :]
        y = jnp.einsum('bhij,bhj->bhi', S, r_t)
        return S, y
    xs = tuple(jnp.swapaxes(t, 0, 1) for t in (r, w, k, v, a, b))
    S, ys = lax.scan(step, s0, xs)
    return jnp.swapaxes(ys, 0, 1), S


def hgrn2_chunked(q, logf, k, i, s0):
    B, T, H, K = q.shape
    C = HGRN_CHUNK
    pad = (-T) % C
    padw = ((0, 0), (0, pad), (0, 0), (0, 0))
    q, logf, k, i = (jnp.pad(t, padw) for t in (q, logf, k, i))
    nc = (T + pad) // C
    def chunks(t):
        return t.reshape(B, nc, C, H, t.shape[-1]).transpose(1, 0, 3, 2, 4)
    mask = jnp.tril(jnp.ones((C, C), dtype=bool))
    def step(S, inp):
        qc, gc, kc, ic = inp
        bc = jnp.cumsum(gc, axis=2)
        diff = bc[:, :, :, None, :] - bc[:, :, None, :, :]
        dec = jnp.exp(jnp.where(mask[:, :, None], diff, -jnp.inf))
        att = jnp.einsum('bhtk,bhsk,bhtsk->bhts', qc, kc, dec)
        o = jnp.einsum('bhts,bhsv->bhtv', att, ic) + jnp.einsum('bhtk,bhkv->bhtv', qc * jnp.exp(bc), S)
        blast = bc[:, :, -1:, :]
        S = jnp.exp(blast[:, :, 0, :])[..., None] * S + jnp.einsum('bhsk,bhsv->bhkv', kc * jnp.exp(blast - bc), ic)
        return S, o
    S, os_ = lax.scan(step, s0, (chunks(q), chunks(logf), chunks(k), chunks(i)))
    o = os_.transpose(1, 0, 3, 2, 4).reshape(B, nc * C, H, -1)[:, :T]
    return o, S


def mixer(x, s_rwkv, s_hgrn, s_shift, lb, w_in, shift_mu, w0, w1u, a0, a1u, g1u, k_k, k_a, r_k,
          ln_x_w, ln_x_b, hg_norm_w, w_out):
    f32 = jnp.float32
    B, T, _ = x.shape
    proj = jnp.einsum('btd,dp->btp', x, w_in).astype(f32)
    p_rw, p_hg = proj[..., :RWKV_PROJ], proj[..., RWKV_PROJ:]
    prev = jnp.concatenate([s_shift[:, None, :].astype(f32), p_rw[:, :-1]], axis=1)
    xs = p_rw + shift_mu * (prev - p_rw)
    r, k, v, wd, ad, gd = jnp.split(xs, RWKV_SPLITS, axis=-1)
    w_log = -jax.nn.softplus(-(w0 + jnp.tanh(wd) @ w1u)) - 0.5
    decay = jnp.exp(-jnp.exp(w_log))
    a_lr = jax.nn.sigmoid(a0 + ad @ a1u)
    g = jax.nn.sigmoid(gd) @ g1u
    def heads(t):
        return t.reshape(B, T, RWKV_HEADS, RWKV_HEAD)
    kk = heads(k * k_k)
    kk = kk / jnp.maximum(jnp.sqrt(jnp.sum(jnp.square(kk), -1, keepdims=True)), 1e-12)
    k = k * (1.0 + (a_lr - 1.0) * k_a)
    rh, kh, vh, ah = heads(r), heads(k), heads(v), heads(a_lr)
    y, s_rw_new = rwkv7_recurrence(rh, heads(decay), kh, vh, -kk, kk * ah, s_rwkv.astype(f32))
    mu = jnp.mean(y, -1, keepdims=True)
    var = jnp.mean(jnp.square(y - mu), -1, keepdims=True)
    yn = ((y - mu) * lax.rsqrt(var + GN_EPS)).reshape(B, T, RWKV_WIDTH) * ln_x_w + ln_x_b
    bonus = jnp.sum(rh * kh * r_k, -1, keepdims=True) * vh
    o_rw = (yn + bonus.reshape(B, T, RWKV_WIDTH)) * g
    q_h, f_h, i_h, g_h = jnp.split(p_hg, HGRN_SPLITS, axis=-1)
    fg = lb + (1.0 - lb) * jax.nn.sigmoid(f_h)
    def hh(t):
        return t.reshape(B, T, HGRN_HEADS, HGRN_HEAD)
    o_h, s_hg_new = hgrn2_chunked(hh(jax.nn.silu(q_h)), hh(jnp.log(fg)), hh(1.0 - fg), hh(i_h),
                                  s_hgrn.astype(f32))
    o_h = o_h * lax.rsqrt(jnp.mean(jnp.square(o_h), -1, keepdims=True) + RMS_EPS)
    o_hg = o_h.reshape(B, T, HGRN_WIDTH) * hg_norm_w * jax.nn.silu(g_h)
    mix = jnp.einsum('btm,md->btd', jnp.concatenate([o_rw, o_hg], axis=-1), w_out)
    return mix, s_rw_new, s_hg_new, p_rw[:, -1]


def run_group(x, s_rwkv, s_hgrn, s_shift, lb_logits, weights):
    (w_in, shift_mu, w0, w1u, a0, a1u, g1u, k_k, k_a, r_k, ln_x_w, ln_x_b, hg_norm_w, w_out,
     ln1_g, ln1_b, w_up, w_down, ln2_g, ln2_b) = weights
    lb_all = jnp.cumsum(jax.nn.softmax(lb_logits.astype(jnp.float32), axis=0), axis=0)
    h = x
    rws, hgs, shs = [], [], []
    for l in range(DEPTH):
        mix, srw, shg, ssh = mixer(h, s_rwkv[l], s_hgrn[l], s_shift[l], lb_all[l], w_in[l], shift_mu[l],
                                   w0[l], w1u[l], a0[l], a1u[l], g1u[l], k_k[l], k_a[l], r_k[l],
                                   ln_x_w[l], ln_x_b[l], hg_norm_w[l], w_out[l])
        h1 = layer_norm(ALPHA * h.astype(jnp.float32) + mix, ln1_g[l], ln1_b[l])
        up = jnp.square(jax.nn.relu(jnp.einsum('btd,df->btf', h1, w_up[l])))
        ff = jnp.einsum('btf,fd->btd', up, w_down[l])
        h = layer_norm(ALPHA * h1 + ff, ln2_g[l], ln2_b[l]).astype(x.dtype)
        rws.append(srw)
        hgs.append(shg)
        shs.append(ssh)
    return (h, jnp.stack(rws).astype(s_rwkv.dtype), jnp.stack(hgs).astype(s_hgrn.dtype),
            jnp.stack(shs).astype(s_shift.dtype))


def setup_inputs(seed: int = 0) -> dict:
    key = jax.random.key(seed)
    ks = jax.random.split(key, 32)
    n = jax.random.normal
    L = DEPTH
    w_in = n(ks[0], (L, D_MODEL, PROJ)) * D_MODEL ** -0.5
    v_lo = 2 * RWKV_WIDTH
    i_lo = RWKV_PROJ + 2 * HGRN_WIDTH
    w_in = w_in.at[:, :, v_lo:v_lo + RWKV_WIDTH].multiply(BETA).at[:, :, i_lo:i_lo + HGRN_WIDTH].multiply(BETA)
    return {
        'x_prompt': n(ks[1], (BATCH, SEQ, D_MODEL)),
        'x_sample': n(ks[2], (DEC_BATCH, DEC_SEQ, D_MODEL)),
        'state_rwkv': 0.5 * n(ks[3], (L, DEC_BATCH, RWKV_HEADS, RWKV_HEAD, RWKV_HEAD)),
        'state_hgrn': 0.5 * n(ks[4], (L, DEC_BATCH, HGRN_HEADS, HGRN_HEAD, HGRN_HEAD)),
        'state_shift': n(ks[5], (L, DEC_BATCH, RWKV_PROJ)),
        'w_in': w_in,
        'shift_mu': jax.random.uniform(ks[6], (L, RWKV_PROJ)),
        'w0': jax.random.uniform(ks[7], (L, RWKV_WIDTH), minval=-6.0, maxval=-0.5),
        'w1u': 0.1 * n(ks[8], (L, DECAY_RANK, RWKV_WIDTH)),
        'a0': 0.1 * n(ks[9], (L, RWKV_WIDTH)),
        'a1u': 0.1 * n(ks[10], (L, AICL_RANK, RWKV_WIDTH)),
        'g1u': n(ks[11], (L, GATE_RANK, RWKV_WIDTH)) * GATE_RANK ** -0.5,
        'k_k': 0.85 + 0.02 * n(ks[12], (L, RWKV_WIDTH)),
        'k_a': 1.0 + 0.02 * n(ks[13], (L, RWKV_WIDTH)),
        'r_k': 0.1 * n(ks[14], (L, RWKV_HEADS, RWKV_HEAD)),
        'ln_x_w': 1.0 + 0.02 * n(ks[15], (L, RWKV_WIDTH)),
        'ln_x_b': 0.02 * n(ks[16], (L, RWKV_WIDTH)),
        'lb_logits': 0.1 * n(ks[17], (L + 1, HGRN_WIDTH)),
        'hg_norm_w': 1.0 + 0.02 * n(ks[18], (L, HGRN_WIDTH)),
        'w_out': n(ks[19], (L, MIX_WIDTH, D_MODEL)) * MIX_WIDTH ** -0.5 * BETA,
        'ln1_g': 1.0 + 0.02 * n(ks[20], (L, D_MODEL)),
        'ln1_b': 0.02 * n(ks[21], (L, D_MODEL)),
        'w_up': n(ks[22], (L, D_MODEL, D_FF)) * D_MODEL ** -0.5,
        'w_down': n(ks[23], (L, D_FF, D_MODEL)) * D_FF ** -0.5 * BETA,
        'ln2_g': 1.0 + 0.02 * n(ks[24], (L, D_MODEL)),
        'ln2_b': 0.02 * n(ks[25], (L, D_MODEL)),
    }


def reference(x_prompt, x_sample, state_rwkv, state_hgrn, state_shift, w_in, shift_mu, w0, w1u, a0, a1u,
              g1u, k_k, k_a, r_k, ln_x_w, ln_x_b, lb_logits, hg_norm_w, w_out, ln1_g, ln1_b, w_up, w_down,
              ln2_g, ln2_b):
    weights = (w_in, shift_mu, w0, w1u, a0, a1u, g1u, k_k, k_a, r_k, ln_x_w, ln_x_b, hg_norm_w, w_out,
               ln1_g, ln1_b, w_up, w_down, ln2_g, ln2_b)
    bp = x_prompt.shape[0]
    z_rw = jnp.zeros((DEPTH, bp, RWKV_HEADS, RWKV_HEAD, RWKV_HEAD), jnp.float32)
    z_hg = jnp.zeros((DEPTH, bp, HGRN_HEADS, HGRN_HEAD, HGRN_HEAD), jnp.float32)
    z_sh = jnp.zeros((DEPTH, bp, RWKV_PROJ), jnp.float32)
    y_prompt, rw_p, hg_p, sh_p = run_group(x_prompt, z_rw, z_hg, z_sh, lb_logits, weights)
    y_sample, rw_s, hg_s, sh_s = run_group(x_sample, state_rwkv, state_hgrn, state_shift, lb_logits, weights)
    return (y_prompt, y_sample, rw_p, rw_s, hg_p, hg_s, sh_p, sh_s)
```

```python
import functools
import math

import jax
import jax.numpy as jnp
from jax import lax
from jax.experimental import pallas as pl
from jax.experimental.pallas import tpu as pltpu

F32 = jnp.float32
BF16 = jnp.bfloat16
HI = lax.Precision.HIGHEST

D_MODEL = 1024
RWKV_WIDTH = 512
RWKV_HEAD = 64
RWKV_HEADS = 8
HGRN_WIDTH = 512
HGRN_HEAD = 128
HGRN_HEADS = 4
DECAY_RANK = 64
AICL_RANK = 64
GATE_RANK = 128
RWKV_PROJ = 3 * RWKV_WIDTH + DECAY_RANK + AICL_RANK + GATE_RANK
HGRN_PROJ = 4 * HGRN_WIDTH
PROJ = RWKV_PROJ + HGRN_PROJ
D_FF = 4 * D_MODEL
DEPTH = 1
ALPHA = (2.0 * DEPTH) ** 0.25
LN_EPS = 1e-5
GN_EPS = RWKV_HEAD * 1e-5
RMS_EPS = 1e-6
DECAY_SCALE = math.exp(-0.5)

LANES = 128
PAIRS = RWKV_HEADS // 2
LORA_COL = 3 * RWKV_WIDTH
VMEM_LIMIT = 56 * 1024 * 1024


def _dot(a, b, dims, prec):
    return lax.dot_general(a, b, (dims, ((), ())), precision=prec, preferred_element_type=F32)


def _mm(a, b, prec=HI):
    return _dot(a, b, ((1,), (0,)), prec)


def _mm_nt(a, b, prec=HI):
    return _dot(a, b, ((1,), (1,)), prec)


def _mm_tn(a, b, prec=HI):
    return _dot(a, b, ((0,), (0,)), prec)


def _sigmoid(x):
    return 1.0 / (1.0 + jnp.exp(-x))


def _proj_kernel(x_ref, w_ref, o_ref):
    o_ref[...] = jnp.dot(x_ref[...].astype(BF16), w_ref[...], preferred_element_type=F32)


def _proj(x2, w_in_bf, tm):
    n = x2.shape[0]
    return pl.pallas_call(
        _proj_kernel,
        grid=(n // tm,),
        in_specs=[pl.BlockSpec((tm, D_MODEL), lambda i: (i, 0)),
                  pl.BlockSpec((D_MODEL, PROJ), lambda i: (0, 0))],
        out_specs=pl.BlockSpec((tm, PROJ), lambda i: (i, 0)),
        out_shape=jax.ShapeDtypeStruct((n, PROJ), F32),
        compiler_params=pltpu.CompilerParams(dimension_semantics=("arbitrary",),
                                             vmem_limit_bytes=VMEM_LIMIT),
        name="proj",
    )(x2, w_in_bf)


def _shift_rows(p, first):
    row = lax.broadcasted_iota(jnp.int32, p.shape, 0)
    return jnp.where(row == 0, first, pltpu.roll(p, 1, 0))


def _rwkv_kernel(pr_ref, pk_ref, pv_ref, pl_ref, shr_ref, shk_ref, shv_ref, shl_ref,
                 mur_ref, muk_ref, muv_ref, mul_ref, w0_ref, a0_ref, kk_ref, ka_ref, rk_ref,
                 lnw_ref, lnb_ref, w1_ref, a1_ref, g1_ref, s0_ref,
                 o_ref, st_ref, s_scr, cr_scr, ck_scr, cv_scr, cl_scr, *, bb, C):
    c = pl.program_id(2)
    last = pl.num_programs(2) - 1
    H = RWKV_HEAD

    @pl.when(c == 0)
    def _init():
        z = jnp.zeros((H, H), F32)
        for i in range(bb):
            top = jnp.concatenate([s0_ref[i, 0], z], axis=1)
            bot = jnp.concatenate([z, s0_ref[i, 1]], axis=1)
            s_scr[i] = jnp.concatenate([top, bot], axis=0)
            cr_scr[i] = shr_ref[i]
            ck_scr[i] = shk_ref[i]
            cv_scr[i] = shv_ref[i]
            cl_scr[i] = shl_ref[i]

    lane = lax.broadcasted_iota(jnp.int32, (C, LANES), 1)
    m0 = lane < H
    rowc = lax.broadcasted_iota(jnp.int32, (C, C), 0)
    colc = lax.broadcasted_iota(jnp.int32, (C, C), 1)
    strict = rowc > colc
    incl = rowc >= colc
    tri = jnp.where(incl, 1.0, 0.0).astype(F32)
    eye = jnp.where(rowc == colc, 1.0, 0.0).astype(F32)
    bl_r = lax.broadcasted_iota(jnp.int32, (LANES, LANES), 0) < H
    bl_c = lax.broadcasted_iota(jnp.int32, (LANES, LANES), 1) < H
    blockdiag = bl_r == bl_c

    def head_sum(x):
        s0 = jnp.sum(jnp.where(m0, x, 0.0), axis=-1, keepdims=True)
        s1 = jnp.sum(jnp.where(m0, 0.0, x), axis=-1, keepdims=True)
        return jnp.where(m0, s0, s1)

    def sel(x, h):
        return jnp.where(m0, x, 0.0) if h == 0 else jnp.where(m0, 0.0, x)

    mu_r, mu_k, mu_v, mu_l = mur_ref[...], muk_ref[...], muv_ref[...], mul_ref[...]
    w1 = w1_ref[...].astype(BF16)
    a1 = a1_ref[...].astype(BF16)
    g1 = g1_ref[...].astype(BF16)

    for i in range(bb):
        p_r, p_k, p_v, p_l = pr_ref[i], pk_ref[i], pv_ref[i], pl_ref[i]
        xr = p_r + mu_r * (_shift_rows(p_r, cr_scr[i]) - p_r)
        xk = p_k + mu_k * (_shift_rows(p_k, ck_scr[i]) - p_k)
        xv = p_v + mu_v * (_shift_rows(p_v, cv_scr[i]) - p_v)
        xl = p_l + mu_l * (_shift_rows(p_l, cl_scr[i]) - p_l)
        cr_scr[i] = p_r[C - 1:C]
        ck_scr[i] = p_k[C - 1:C]
        cv_scr[i] = p_v[C - 1:C]
        cl_scr[i] = p_l[C - 1:C]

        xl_lo = xl[:, :LANES]
        y_w = w0_ref[...] + jnp.dot(jnp.tanh(xl_lo).astype(BF16), w1, preferred_element_type=F32)
        lw = -DECAY_SCALE * _sigmoid(y_w)
        a_lr = _sigmoid(a0_ref[...] + jnp.dot(xl_lo.astype(BF16), a1, preferred_element_type=F32))
        gate = jnp.dot(_sigmoid(xl[:, LANES:]).astype(BF16), g1, preferred_element_type=F32)

        kk = xk * kk_ref[...]
        kk = kk / jnp.maximum(jnp.sqrt(head_sum(kk * kk)), 1e-12)
        k2 = xk * (1.0 + (a_lr - 1.0) * ka_ref[...])
        a = -kk
        b = kk * a_lr
        r = xr
        v = xv

        cum = _mm(tri, lw)
        cum_last = cum[C - 1:C]
        e_in = jnp.exp(-cum)
        e_out = jnp.exp(cum_last - cum)
        at = a * jnp.exp(cum - lw)
        rt = r * jnp.exp(cum)
        bt = b * e_in
        kt = k2 * e_in
        bh = b * e_out
        kh = k2 * e_out

        S = s_scr[i]
        w_pre = _mm_nt(at, S)
        o_acc = _mm_nt(rt, S)
        t_inv = []
        a_rb = []
        a_rk = []
        for h in range(2):
            at_h = sel(at, h)
            rt_h = sel(rt, h)
            n_ab = jnp.where(strict, _mm_nt(at_h, bt), 0.0)
            a_ak = jnp.where(strict, _mm_nt(at_h, kt), 0.0)
            a_rb.append(jnp.where(incl, _mm_nt(rt_h, bt), 0.0))
            a_rk.append(jnp.where(incl, _mm_nt(rt_h, kt), 0.0))
            v_h = sel(v, h)
            w_pre = w_pre + _mm(a_ak, v_h)
            o_acc = o_acc + _mm(a_rk[h], v_h)
            inv = eye + n_ab
            pw = n_ab
            span = 2
            while span < C:
                pw = _mm(pw, pw)
                inv = inv + _mm(inv, pw)
                span *= 2
            t_inv.append(inv)
        u = _mm(t_inv[0], sel(w_pre, 0)) + _mm(t_inv[1], sel(w_pre, 1))
        y = o_acc + _mm(a_rb[0], sel(u, 0)) + _mm(a_rb[1], sel(u, 1))
        upd = _mm_tn(u, bh) + _mm_tn(v, kh)
        s_new = S * jnp.exp(cum_last) + jnp.where(blockdiag, upd, 0.0)
        s_scr[i] = s_new

        mu = head_sum(y) * (1.0 / H)
        d = y - mu
        var = head_sum(d * d) * (1.0 / H)
        yn = d * lax.rsqrt(var + GN_EPS) * lnw_ref[...] + lnb_ref[...]
        bonus = head_sum(r * k2 * rk_ref[...]) * v
        o_ref[i] = (yn + bonus) * gate

        @pl.when(c == last)
        def _fin():
            st_ref[i, 0] = s_new[:H, :H]
            st_ref[i, 1] = s_new[H:, H:]


def _rwkv(proj3, s_shift3, s_rwkv, p, bb, C):
    B, T, _ = proj3.shape
    grid = (B // bb, PAIRS, T // C)
    nb = RWKV_WIDTH // LANES

    def tok(col0):
        return pl.BlockSpec((bb, C, LANES), lambda b, h, c: (b, c, col0 + h))

    def first(col0):
        return pl.BlockSpec((bb, 1, LANES), lambda b, h, c: (b, 0, col0 + h))

    def vec(col0):
        return pl.BlockSpec((1, LANES), lambda b, h, c: (0, col0 + h))

    lora_blk = LORA_COL // (2 * LANES)
    in_specs = [
        tok(0), tok(nb), tok(2 * nb),
        pl.BlockSpec((bb, C, 2 * LANES), lambda b, h, c: (b, c, lora_blk)),
        first(0), first(nb), first(2 * nb),
        pl.BlockSpec((bb, 1, 2 * LANES), lambda b, h, c: (b, 0, lora_blk)),
        vec(0), vec(nb), vec(2 * nb),
        pl.BlockSpec((1, 2 * LANES), lambda b, h, c: (0, lora_blk)),
        vec(0), vec(0), vec(0), vec(0), vec(0), vec(0), vec(0),
        pl.BlockSpec((LANES, LANES), lambda b, h, c: (0, h)),
        pl.BlockSpec((LANES, LANES), lambda b, h, c: (0, h)),
        pl.BlockSpec((GATE_RANK, LANES), lambda b, h, c: (0, h)),
        pl.BlockSpec((bb, 2, RWKV_HEAD, RWKV_HEAD), lambda b, h, c: (b, h, 0, 0)),
    ]
    out_specs = [
        pl.BlockSpec((bb, C, LANES), lambda b, h, c: (b, c, h)),
        pl.BlockSpec((bb, 2, RWKV_HEAD, RWKV_HEAD), lambda b, h, c: (b, h, 0, 0)),
    ]
    out_shape = [jax.ShapeDtypeStruct((B, T, RWKV_WIDTH), F32),
                 jax.ShapeDtypeStruct((B, RWKV_HEADS, RWKV_HEAD, RWKV_HEAD), F32)]
    scratch = [pltpu.VMEM((bb, LANES, LANES), F32),
               pltpu.VMEM((bb, 1, LANES), F32), pltpu.VMEM((bb, 1, LANES), F32),
               pltpu.VMEM((bb, 1, LANES), F32), pltpu.VMEM((bb, 1, 2 * LANES), F32)]
    return pl.pallas_call(
        functools.partial(_rwkv_kernel, bb=bb, C=C),
        grid=grid, in_specs=in_specs, out_specs=out_specs, out_shape=out_shape,
        scratch_shapes=scratch,
        compiler_params=pltpu.CompilerParams(
            dimension_semantics=("arbitrary", "arbitrary", "arbitrary"),
            vmem_limit_bytes=VMEM_LIMIT),
        name="rwkv",
    )(proj3, proj3, proj3, proj3, s_shift3, s_shift3, s_shift3, s_shift3,
      p["shift_mu"], p["shift_mu"], p["shift_mu"], p["shift_mu"],
      p["w0"], p["a0"], p["k_k"], p["k_a"], p["r_k"], p["ln_x_w"], p["ln_x_b"],
      p["w1u_pad"], p["a1u_pad"], p["g1u"], s_rwkv)


def _hgrn_kernel(q_ref, f_ref, i_ref, g_ref, lbl_ref, hgw_ref, s0_ref,
                 o_ref, st_ref, s_scr, *, bb, C):
    c = pl.program_id(2)
    last = pl.num_programs(2) - 1

    @pl.when(c == 0)
    def _init():
        for i in range(bb):
            s_scr[i] = s0_ref[i, 0].T

    logits = lbl_ref[...]
    ex = jnp.exp(logits - jnp.max(logits, axis=0, keepdims=True))
    lb = ex[0:1] / jnp.sum(ex, axis=0, keepdims=True)

    row = lax.broadcasted_iota(jnp.int32, (C, LANES), 0)
    rowc = lax.broadcasted_iota(jnp.int32, (C, C), 0)
    colc = lax.broadcasted_iota(jnp.int32, (C, C), 1)
    tri = jnp.where(rowc >= colc, 1.0, 0.0).astype(F32)

    for i in range(bb):
        q_h, f_h, i_h, g_h = q_ref[i], f_ref[i], i_ref[i], g_ref[i]
        q = q_h * _sigmoid(q_h)
        fg = lb + (1.0 - lb) * _sigmoid(f_h)
        kin = 1.0 - fg
        bc = _mm(tri, jnp.log(fg))
        b_last = bc[C - 1:C]

        o = jnp.zeros((C, LANES), F32)
        for s in range(C):
            dec = jnp.where(row >= s, jnp.exp(jnp.minimum(bc - bc[s:s + 1], 0.0)), 0.0)
            att = jnp.sum(q * dec * kin[s:s + 1], axis=-1, keepdims=True)
            o = o + att * i_h[s:s + 1]

        St = s_scr[i]
        o = o + _mm_nt(q * jnp.exp(bc), St)
        s_new = St * jnp.exp(b_last) + _mm_tn(i_h, kin * jnp.exp(b_last - bc))
        s_scr[i] = s_new

        ms = jnp.mean(o * o, axis=-1, keepdims=True)
        o_ref[i] = o * lax.rsqrt(ms + RMS_EPS) * hgw_ref[...] * (g_h * _sigmoid(g_h))

        @pl.when(c == last)
        def _fin():
            st_ref[i, 0] = s_new.T


def _hgrn(proj3, s_hgrn, p, bb, C):
    B, T, _ = proj3.shape
    grid = (B // bb, HGRN_HEADS, T // C)
    base = RWKV_PROJ // LANES
    nb = HGRN_WIDTH // LANES

    def tok(col0):
        return pl.BlockSpec((bb, C, LANES), lambda b, h, c: (b, c, col0 + h))

    in_specs = [
        tok(base), tok(base + nb), tok(base + 2 * nb), tok(base + 3 * nb),
        pl.BlockSpec((DEPTH + 1, LANES), lambda b, h, c: (0, h)),
        pl.BlockSpec((1, LANES), lambda b, h, c: (0, h)),
        pl.BlockSpec((bb, 1, HGRN_HEAD, HGRN_HEAD), lambda b, h, c: (b, h, 0, 0)),
    ]
    out_specs = [
        pl.BlockSpec((bb, C, LANES), lambda b, h, c: (b, c, h)),
        pl.BlockSpec((bb, 1, HGRN_HEAD, HGRN_HEAD), lambda b, h, c: (b, h, 0, 0)),
    ]
    out_shape = [jax.ShapeDtypeStruct((B, T, HGRN_WIDTH), F32),
                 jax.ShapeDtypeStruct((B, HGRN_HEADS, HGRN_HEAD, HGRN_HEAD), F32)]
    return pl.pallas_call(
        functools.partial(_hgrn_kernel, bb=bb, C=C),
        grid=grid, in_specs=in_specs, out_specs=out_specs, out_shape=out_shape,
        scratch_shapes=[pltpu.VMEM((bb, HGRN_HEAD, HGRN_HEAD), F32)],
        compiler_params=pltpu.CompilerParams(
            dimension_semantics=("arbitrary", "arbitrary", "arbitrary"),
            vmem_limit_bytes=VMEM_LIMIT),
        name="hgrn",
    )(proj3, proj3, proj3, proj3, p["lb_logits"], p["hg_norm_w"], s_hgrn)


def _layer_norm(x, g, b):
    mu = jnp.mean(x, axis=-1, keepdims=True)
    d = x - mu
    var = jnp.mean(d * d, axis=-1, keepdims=True)
    return d * lax.rsqrt(var + LN_EPS) * g + b


def _post_kernel(x_ref, orw_ref, ohg_ref, wo1_ref, wo2_ref, g1_ref, b1_ref, wup_ref, wdn_ref,
                 g2_ref, b2_ref, y_ref):
    mix = (jnp.dot(orw_ref[...].astype(BF16), wo1_ref[...], preferred_element_type=F32)
           + jnp.dot(ohg_ref[...].astype(BF16), wo2_ref[...], preferred_element_type=F32))
    h1 = _layer_norm(ALPHA * x_ref[...] + mix, g1_ref[...], b1_ref[...])
    up = jnp.dot(h1.astype(BF16), wup_ref[...], preferred_element_type=F32)
    up = jnp.square(jnp.maximum(up, 0.0))
    ff = jnp.dot(up.astype(BF16), wdn_ref[...], preferred_element_type=F32)
    y_ref[...] = _layer_norm(ALPHA * h1 + ff, g2_ref[...], b2_ref[...])


def _post(x2, orw2, ohg2, p, tm):
    n = x2.shape[0]

    def const(shape):
        return pl.BlockSpec(shape, lambda i: (0, 0), pipeline_mode=pl.Buffered(1))

    return pl.pallas_call(
        _post_kernel,
        grid=(n // tm,),
        in_specs=[pl.BlockSpec((tm, D_MODEL), lambda i: (i, 0)),
                  pl.BlockSpec((tm, RWKV_WIDTH), lambda i: (i, 0)),
                  pl.BlockSpec((tm, HGRN_WIDTH), lambda i: (i, 0)),
                  const((RWKV_WIDTH, D_MODEL)), const((HGRN_WIDTH, D_MODEL)),
                  const((1, D_MODEL)), const((1, D_MODEL)),
                  const((D_MODEL, D_FF)), const((D_FF, D_MODEL)),
                  const((1, D_MODEL)), const((1, D_MODEL))],
        out_specs=pl.BlockSpec((tm, D_MODEL), lambda i: (i, 0)),
        out_shape=jax.ShapeDtypeStruct((n, D_MODEL), F32),
        compiler_params=pltpu.CompilerParams(dimension_semantics=("arbitrary",),
                                             vmem_limit_bytes=VMEM_LIMIT),
        name="post",
    )(x2, orw2, ohg2, p["wo_rw"], p["wo_hg"], p["ln1_g"], p["ln1_b"], p["w_up"], p["w_down"],
      p["ln2_g"], p["ln2_b"])


def _prep_params(w_in, shift_mu, w0, w1u, a0, a1u, g1u, k_k, k_a, r_k, ln_x_w, ln_x_b, lb_logits,
                 hg_norm_w, w_out, ln1_g, ln1_b, w_up, w_down, ln2_g, ln2_b):
    zw = jnp.zeros((LANES - DECAY_RANK, RWKV_WIDTH), F32)
    za = jnp.zeros((LANES - AICL_RANK, RWKV_WIDTH), F32)
    w_out_bf = w_out[0].astype(BF16)
    return {
        "w_in": w_in[0].astype(BF16),
        "shift_mu": shift_mu[0].reshape(1, RWKV_PROJ),
        "w0": w0[0].reshape(1, RWKV_WIDTH), "a0": a0[0].reshape(1, RWKV_WIDTH),
        "k_k": k_k[0].reshape(1, RWKV_WIDTH), "k_a": k_a[0].reshape(1, RWKV_WIDTH),
        "r_k": r_k[0].reshape(1, RWKV_WIDTH),
        "ln_x_w": ln_x_w[0].reshape(1, RWKV_WIDTH), "ln_x_b": ln_x_b[0].reshape(1, RWKV_WIDTH),
        "w1u_pad": jnp.concatenate([w1u[0], zw], axis=0),
        "a1u_pad": jnp.concatenate([za, a1u[0]], axis=0),
        "g1u": g1u[0],
        "lb_logits": lb_logits.astype(F32),
        "hg_norm_w": hg_norm_w[0].reshape(1, HGRN_WIDTH),
        "wo_rw": w_out_bf[:RWKV_WIDTH], "wo_hg": w_out_bf[RWKV_WIDTH:],
        "ln1_g": ln1_g[0].reshape(1, D_MODEL), "ln1_b": ln1_b[0].reshape(1, D_MODEL),
        "w_up": w_up[0].astype(BF16), "w_down": w_down[0].astype(BF16),
        "ln2_g": ln2_g[0].reshape(1, D_MODEL), "ln2_b": ln2_b[0].reshape(1, D_MODEL),
    }


def _run_group(x, s_rwkv, s_hgrn, s_shift, p, *, tm, rw_bb, rw_c, hg_bb, hg_c):
    B, T, _ = x.shape
    x2 = x.reshape(B * T, D_MODEL)
    proj2 = _proj(x2, p["w_in"], tm)
    proj3 = proj2.reshape(B, T, PROJ)
    o_rw, st_rw = _rwkv(proj3, s_shift.reshape(B, 1, RWKV_PROJ), s_rwkv, p, rw_bb, rw_c)
    o_hg, st_hg = _hgrn(proj3, s_hgrn, p, hg_bb, hg_c)
    y2 = _post(x2, o_rw.reshape(B * T, RWKV_WIDTH), o_hg.reshape(B * T, HGRN_WIDTH), p, tm)
    sh = proj3[:, T - 1, :RWKV_PROJ]
    return y2.reshape(B, T, D_MODEL), st_rw[None], st_hg[None], sh[None]


PROMPT_CFG = dict(tm=256, rw_bb=4, rw_c=64, hg_bb=8, hg_c=16)
SAMPLE_CFG = dict(tm=256, rw_bb=8, rw_c=8, hg_bb=8, hg_c=8)


def kernel(x_prompt, x_sample, state_rwkv, state_hgrn, state_shift, w_in, shift_mu, w0, w1u, a0, a1u, g1u, k_k, k_a, r_k, ln_x_w, ln_x_b, lb_logits, hg_norm_w, w_out, ln1_g, ln1_b, w_up, w_down, ln2_g, ln2_b):
    assert w_in.shape[0] == DEPTH
    p = _prep_params(w_in, shift_mu, w0, w1u, a0, a1u, g1u, k_k, k_a, r_k, ln_x_w, ln_x_b, lb_logits,
                     hg_norm_w, w_out, ln1_g, ln1_b, w_up, w_down, ln2_g, ln2_b)
    bp = x_prompt.shape[0]
    z_rw = jnp.zeros((bp, RWKV_HEADS, RWKV_HEAD, RWKV_HEAD), F32)
    z_hg = jnp.zeros((bp, HGRN_HEADS, HGRN_HEAD, HGRN_HEAD), F32)
    z_sh = jnp.zeros((bp, RWKV_PROJ), F32)
    y_p, rw_p, hg_p, sh_p = _run_group(x_prompt, z_rw, z_hg, z_sh, p, **PROMPT_CFG)
    y_s, rw_s, hg_s, sh_s = _run_group(x_sample, state_rwkv[0].astype(F32), state_hgrn[0].astype(F32),
                                       state_shift[0].astype(F32), p, **SAMPLE_CFG)
    return (y_p, y_s, rw_p, rw_s, hg_p, hg_s, sh_p, sh_s)
```

```python
import functools
import math

import jax
import jax.numpy as jnp
from jax import lax
from jax.experimental import pallas as pl
from jax.experimental.pallas import tpu as pltpu

F32 = jnp.float32
BF16 = jnp.bfloat16
HI = lax.Precision.HIGHEST

D_MODEL = 1024
RWKV_WIDTH = 512
RWKV_HEAD = 64
RWKV_HEADS = 8
HGRN_WIDTH = 512
HGRN_HEAD = 128
HGRN_HEADS = 4
DECAY_RANK = 64
AICL_RANK = 64
GATE_RANK = 128
RWKV_PROJ = 3 * RWKV_WIDTH + DECAY_RANK + AICL_RANK + GATE_RANK
HGRN_PROJ = 4 * HGRN_WIDTH
PROJ = RWKV_PROJ + HGRN_PROJ
D_FF = 4 * D_MODEL
DEPTH = 1
ALPHA = (2.0 * DEPTH) ** 0.25
LN_EPS = 1e-5
GN_EPS = RWKV_HEAD * 1e-5
RMS_EPS = 1e-6
DECAY_SCALE = math.exp(-0.5)

LANES = 128
PAIRS = RWKV_HEADS // 2
LORA_COL = 3 * RWKV_WIDTH
VMEM_LIMIT = 56 * 1024 * 1024


def _dot(a, b, dims, prec):
    return lax.dot_general(a, b, (dims, ((), ())), precision=prec, preferred_element_type=F32)


def _mm(a, b, prec=HI):
    return _dot(a, b, ((1,), (0,)), prec)


def _mm_nt(a, b, prec=HI):
    return _dot(a, b, ((1,), (1,)), prec)


def _mm_tn(a, b, prec=HI):
    return _dot(a, b, ((0,), (0,)), prec)


NN = ((1,), (0,))
NT = ((1,), (1,))
TN = ((0,), (0,))
REC_PASSES = 3


def _pieces(x, n=None):
    n = (2 if REC_PASSES == 3 else 1) if n is None else n
    out = []
    rem = x
    for j in range(n):
        pc = rem.astype(BF16)
        out.append(pc)
        if j + 1 < n:
            rem = rem - pc.astype(F32)
    return tuple(out)


def _dotp(a, b, dims):
    d = lambda x, y: lax.dot_general(x, y, (dims, ((), ())), preferred_element_type=F32)
    acc = d(a[0], b[0])
    if len(a) > 1 and len(b) > 1:
        acc = acc + (d(a[0], b[1]) + d(a[1], b[0]))
    elif len(b) > 1:
        for pc in b[1:]:
            acc = acc + d(a[0], pc)
    elif len(a) > 1:
        for pc in a[1:]:
            acc = acc + d(pc, b[0])
    return acc


def _sigmoid(x):
    return 1.0 / (1.0 + jnp.exp(-x))


def _proj_kernel(x_ref, w_ref, o_ref):
    o_ref[...] = jnp.dot(x_ref[...].astype(BF16), w_ref[...], preferred_element_type=F32)


def _proj(x2, w_in_bf, tm):
    n = x2.shape[0]
    return pl.pallas_call(
        _proj_kernel,
        grid=(n // tm,),
        in_specs=[pl.BlockSpec((tm, D_MODEL), lambda i: (i, 0)),
                  pl.BlockSpec((D_MODEL, PROJ), lambda i: (0, 0))],
        out_specs=pl.BlockSpec((tm, PROJ), lambda i: (i, 0)),
        out_shape=jax.ShapeDtypeStruct((n, PROJ), F32),
        compiler_params=pltpu.CompilerParams(dimension_semantics=("arbitrary",),
                                             vmem_limit_bytes=VMEM_LIMIT),
        name="proj",
    )(x2, w_in_bf)


def _rwkv_kernel(pr_ref, pk_ref, pv_ref, pl_ref, shr_ref, shk_ref, shv_ref, shl_ref,
                 mur_ref, muk_ref, muv_ref, mul_ref, w0_ref, a0_ref, kk_ref, ka_ref, rk_ref,
                 lnw_ref, lnb_ref, w1_ref, a1_ref, g1_ref, s0_ref,
                 o_ref, st_ref, s_scr, cr_scr, ck_scr, cv_scr, cl_scr, *, bb, C):
    c = pl.program_id(2)
    last = pl.num_programs(2) - 1
    H = RWKV_HEAD

    @pl.when(c == 0)
    def _init():
        z = jnp.zeros((H, H), F32)
        for i in range(bb):
            top = jnp.concatenate([s0_ref[i, 0], z], axis=1)
            bot = jnp.concatenate([z, s0_ref[i, 1]], axis=1)
            s_scr[i] = jnp.concatenate([top, bot], axis=0)
            cr_scr[i] = shr_ref[i]
            ck_scr[i] = shk_ref[i]
            cv_scr[i] = shv_ref[i]
            cl_scr[i] = shl_ref[i]

    lane = lax.broadcasted_iota(jnp.int32, (C, LANES), 1)
    m0 = lane < H
    rowc = lax.broadcasted_iota(jnp.int32, (C, C), 0)
    colc = lax.broadcasted_iota(jnp.int32, (C, C), 1)
    tri = (jnp.where(rowc >= colc, 1.0, 0.0).astype(BF16),)
    r2 = lax.broadcasted_iota(jnp.int32, (2 * C, 2 * C), 0)
    c2 = lax.broadcasted_iota(jnp.int32, (2 * C, 2 * C), 1)
    t2 = jnp.bitwise_and(r2, C - 1)
    s2 = jnp.bitwise_and(c2, C - 1)
    strict2 = t2 > s2
    incl2 = t2 >= s2
    same_head = (r2 >= C) == (c2 >= C)
    eye2 = jnp.where(r2 == c2, 1.0, 0.0).astype(F32)
    bl_r = lax.broadcasted_iota(jnp.int32, (LANES, LANES), 0) < H
    bl_c = lax.broadcasted_iota(jnp.int32, (LANES, LANES), 1) < H
    blockdiag = bl_r == bl_c

    def head_sum(x):
        s0 = jnp.sum(jnp.where(m0, x, 0.0), axis=-1, keepdims=True)
        s1 = jnp.sum(jnp.where(m0, 0.0, x), axis=-1, keepdims=True)
        return jnp.where(m0, s0, s1)

    def stack_heads(x):
        return jnp.concatenate([jnp.where(m0, x, 0.0), jnp.where(m0, 0.0, x)], axis=-2)

    def merge_heads(x):
        return jnp.where(m0, x[:C], x[C:])

    mu_r, mu_k, mu_v, mu_l = mur_ref[...], muk_ref[...], muv_ref[...], mul_ref[...]
    w1 = w1_ref[...].astype(BF16)
    a1 = a1_ref[...].astype(BF16)
    g1 = g1_ref[...].astype(BF16)

    ids = range(bb)

    def shifted(p_ref, carry_ref):
        p = p_ref[...]
        row = lax.broadcasted_iota(jnp.int32, p.shape, 1)
        flat = p.reshape(bb * C, p.shape[-1])
        prev = jnp.where(row == 0, carry_ref[...], pltpu.roll(flat, 1, 0).reshape(p.shape))
        carry_ref[...] = p[:, C - 1:C, :]
        return p, prev

    p_r, prev_r = shifted(pr_ref, cr_scr)
    p_k, prev_k = shifted(pk_ref, ck_scr)
    p_v, prev_v = shifted(pv_ref, cv_scr)
    p_l, prev_l = shifted(pl_ref, cl_scr)
    r = p_r + mu_r * (prev_r - p_r)
    xk = p_k + mu_k * (prev_k - p_k)
    v = p_v + mu_v * (prev_v - p_v)
    xl = (p_l + mu_l * (prev_l - p_l)).reshape(bb * C, 2 * LANES)

    xl_lo = xl[:, :LANES]
    y_w = w0_ref[...] + jnp.dot(jnp.tanh(xl_lo).astype(BF16), w1, preferred_element_type=F32)
    lw = (-DECAY_SCALE * _sigmoid(y_w)).reshape(bb, C, LANES)
    a_lr = _sigmoid(a0_ref[...] + jnp.dot(xl_lo.astype(BF16), a1, preferred_element_type=F32))
    a_lr = a_lr.reshape(bb, C, LANES)
    gate = jnp.dot(_sigmoid(xl[:, LANES:]).astype(BF16), g1, preferred_element_type=F32)
    gate = gate.reshape(bb, C, LANES)

    kk = xk * kk_ref[...]
    kk = kk / jnp.maximum(jnp.sqrt(head_sum(kk * kk)), 1e-12)
    k2 = xk * (1.0 + (a_lr - 1.0) * ka_ref[...])
    a = -kk
    b = kk * a_lr

    lw_wide = jnp.concatenate([lw[i] for i in ids], axis=1)
    cum_wide = _dotp(tri, _pieces(lw_wide, 3), NN)
    cum = jnp.stack([cum_wide[:, i * LANES:(i + 1) * LANES] for i in ids])
    cum_last = cum[:, C - 1:C, :]
    e_in = jnp.exp(-cum)
    e_out = jnp.exp(cum_last - cum)
    at = a * jnp.exp(cum - lw)
    rt = r * jnp.exp(cum)
    bt = b * e_in
    kt = k2 * e_in

    lhs = _pieces(jnp.concatenate([stack_heads(at), stack_heads(rt)], axis=1))
    rhs = _pieces(jnp.concatenate([bt, kt], axis=1))
    bt_hat = _pieces(stack_heads(bt))
    g = [_dotp(tuple(pc[i] for pc in lhs), tuple(pc[i] for pc in rhs), NT) for i in ids]
    n_bd = [_dotp(tuple(pc[i, :2 * C] for pc in lhs), tuple(pc[i] for pc in bt_hat), NT) for i in ids]
    g_a = [_pieces(jnp.where(strict2, g[i][:2 * C], 0.0)) for i in ids]
    g_r = [_pieces(jnp.where(incl2, g[i][2 * C:], 0.0)) for i in ids]
    pw = [jnp.where(jnp.logical_and(strict2, same_head), n_bd[i], 0.0) for i in ids]
    inv = [eye2 + pw[i] for i in ids]
    span = 2
    while span < C:
        pw_p = [_pieces(pw[i]) for i in ids]
        pw = [_dotp(pw_p[i], pw_p[i], NN) for i in ids]
        inv = [inv[i] + _dotp(_pieces(inv[i]), _pieces(pw[i]), NN) for i in ids]
        span *= 2

    S = [s_scr[i] for i in ids]
    atrt = _pieces(jnp.concatenate([at, rt], axis=1))
    pq = [_dotp(tuple(pc[i] for pc in atrt), _pieces(S[i]), NT) for i in ids]
    v0 = _pieces(jnp.concatenate([jnp.zeros_like(v), v], axis=1))
    w = jnp.stack([pq[i][:C] + merge_heads(_dotp(g_a[i], tuple(pc[i] for pc in v0), NN)) for i in ids])
    w_hat = _pieces(stack_heads(w))
    u2 = [_dotp(_pieces(inv[i]), tuple(pc[i] for pc in w_hat), NN) for i in ids]
    u = jnp.stack([u2[i][:C] + u2[i][C:] for i in ids])
    z = _pieces(jnp.concatenate([u, v], axis=1))
    y = jnp.stack([pq[i][C:] + merge_heads(_dotp(g_r[i], tuple(pc[i] for pc in z), NN)) for i in ids])
    bkh = _pieces(jnp.concatenate([b * e_out, k2 * e_out], axis=1))
    for i in ids:
        upd = _dotp(tuple(pc[i] for pc in z), tuple(pc[i] for pc in bkh), TN)
        s_scr[i] = S[i] * jnp.exp(cum_last[i]) + jnp.where(blockdiag, upd, 0.0)

    mu = head_sum(y) * (1.0 / H)
    d = y - mu
    var = head_sum(d * d) * (1.0 / H)
    yn = d * lax.rsqrt(var + GN_EPS) * lnw_ref[...] + lnb_ref[...]
    bonus = head_sum(r * k2 * rk_ref[...]) * v
    o_ref[...] = (yn + bonus) * gate

    @pl.when(c == last)
    def _fin():
        for i in range(bb):
            s_fin = s_scr[i]
            st_ref[i, 0] = s_fin[:H, :H]
            st_ref[i, 1] = s_fin[H:, H:]


def _rwkv(proj3, s_shift3, s_rwkv, p, bb, C):
    B, T, _ = proj3.shape
    grid = (B // bb, PAIRS, T // C)
    nb = RWKV_WIDTH // LANES

    def tok(col0):
        return pl.BlockSpec((bb, C, LANES), lambda b, h, c: (b, c, col0 + h))

    def first(col0):
        return pl.BlockSpec((bb, 1, LANES), lambda b, h, c: (b, 0, col0 + h))

    def vec(col0):
        return pl.BlockSpec((1, LANES), lambda b, h, c: (0, col0 + h))

    lora_blk = LORA_COL // (2 * LANES)
    in_specs = [
        tok(0), tok(nb), tok(2 * nb),
        pl.BlockSpec((bb, C, 2 * LANES), lambda b, h, c: (b, c, lora_blk)),
        first(0), first(nb), first(2 * nb),
        pl.BlockSpec((bb, 1, 2 * LANES), lambda b, h, c: (b, 0, lora_blk)),
        vec(0), vec(nb), vec(2 * nb),
        pl.BlockSpec((1, 2 * LANES), lambda b, h, c: (0, lora_blk)),
        vec(0), vec(0), vec(0), vec(0), vec(0), vec(0), vec(0),
        pl.BlockSpec((LANES, LANES), lambda b, h, c: (0, h)),
        pl.BlockSpec((LANES, LANES), lambda b, h, c: (0, h)),
        pl.BlockSpec((GATE_RANK, LANES), lambda b, h, c: (0, h)),
        pl.BlockSpec((bb, 2, RWKV_HEAD, RWKV_HEAD), lambda b, h, c: (b, h, 0, 0)),
    ]
    out_specs = [
        pl.BlockSpec((bb, C, LANES), lambda b, h, c: (b, c, h)),
        pl.BlockSpec((bb, 2, RWKV_HEAD, RWKV_HEAD), lambda b, h, c: (b, h, 0, 0)),
    ]
    out_shape = [jax.ShapeDtypeStruct((B, T, RWKV_WIDTH), F32),
                 jax.ShapeDtypeStruct((B, RWKV_HEADS, RWKV_HEAD, RWKV_HEAD), F32)]
    scratch = [pltpu.VMEM((bb, LANES, LANES), F32),
               pltpu.VMEM((bb, 1, LANES), F32), pltpu.VMEM((bb, 1, LANES), F32),
               pltpu.VMEM((bb, 1, LANES), F32), pltpu.VMEM((bb, 1, 2 * LANES), F32)]
    return pl.pallas_call(
        functools.partial(_rwkv_kernel, bb=bb, C=C),
        grid=grid, in_specs=in_specs, out_specs=out_specs, out_shape=out_shape,
        scratch_shapes=scratch,
        compiler_params=pltpu.CompilerParams(
            dimension_semantics=("arbitrary", "arbitrary", "arbitrary"),
            vmem_limit_bytes=VMEM_LIMIT),
        name="rwkv",
    )(proj3, proj3, proj3, proj3, s_shift3, s_shift3, s_shift3, s_shift3,
      p["shift_mu"], p["shift_mu"], p["shift_mu"], p["shift_mu"],
      p["w0"], p["a0"], p["k_k"], p["k_a"], p["r_k"], p["ln_x_w"], p["ln_x_b"],
      p["w1u_pad"], p["a1u_pad"], p["g1u"], s_rwkv)


def _hgrn_kernel(q_ref, f_ref, i_ref, g_ref, lbl_ref, hgw_ref, s0_ref,
                 o_ref, st_ref, s_scr, *, bb, C):
    c = pl.program_id(2)
    last = pl.num_programs(2) - 1

    @pl.when(c == 0)
    def _init():
        for i in range(bb):
            s_scr[i] = s0_ref[i, 0].T

    logits = lbl_ref[...]
    ex = jnp.exp(logits - jnp.max(logits, axis=0, keepdims=True))
    lb = ex[0:1] / jnp.sum(ex, axis=0, keepdims=True)

    row = lax.broadcasted_iota(jnp.int32, (C, LANES), 0)
    rowc = lax.broadcasted_iota(jnp.int32, (C, C), 0)
    colc = lax.broadcasted_iota(jnp.int32, (C, C), 1)
    tri = jnp.where(rowc >= colc, 1.0, 0.0).astype(F32)

    for i in range(bb):
        q_h, f_h, i_h, g_h = q_ref[i], f_ref[i], i_ref[i], g_ref[i]
        q = q_h * _sigmoid(q_h)
        fg = lb + (1.0 - lb) * _sigmoid(f_h)
        kin = 1.0 - fg
        bc = _mm(tri, jnp.log(fg))
        b_last = bc[C - 1:C]

        o = jnp.zeros((C, LANES), F32)
        for s in range(C):
            dec = jnp.where(row >= s, jnp.exp(jnp.minimum(bc - bc[s:s + 1], 0.0)), 0.0)
            att = jnp.sum(q * dec * kin[s:s + 1], axis=-1, keepdims=True)
            o = o + att * i_h[s:s + 1]

        St = s_scr[i]
        o = o + _mm_nt(q * jnp.exp(bc), St)
        s_new = St * jnp.exp(b_last) + _mm_tn(i_h, kin * jnp.exp(b_last - bc))
        s_scr[i] = s_new

        ms = jnp.mean(o * o, axis=-1, keepdims=True)
        o_ref[i] = o * lax.rsqrt(ms + RMS_EPS) * hgw_ref[...] * (g_h * _sigmoid(g_h))

    @pl.when(c == last)
    def _fin():
        for i in range(bb):
            st_ref[i, 0] = s_scr[i].T


def _hgrn(proj3, s_hgrn, p, bb, C):
    B, T, _ = proj3.shape
    grid = (B // bb, HGRN_HEADS, T // C)
    base = RWKV_PROJ // LANES
    nb = HGRN_WIDTH // LANES

    def tok(col0):
        return pl.BlockSpec((bb, C, LANES), lambda b, h, c: (b, c, col0 + h))

    in_specs = [
        tok(base), tok(base + nb), tok(base + 2 * nb), tok(base + 3 * nb),
        pl.BlockSpec((DEPTH + 1, LANES), lambda b, h, c: (0, h)),
        pl.BlockSpec((1, LANES), lambda b, h, c: (0, h)),
        pl.BlockSpec((bb, 1, HGRN_HEAD, HGRN_HEAD), lambda b, h, c: (b, h, 0, 0)),
    ]
    out_specs = [
        pl.BlockSpec((bb, C, LANES), lambda b, h, c: (b, c, h)),
        pl.BlockSpec((bb, 1, HGRN_HEAD, HGRN_HEAD), lambda b, h, c: (b, h, 0, 0)),
    ]
    out_shape = [jax.ShapeDtypeStruct((B, T, HGRN_WIDTH), F32),
                 jax.ShapeDtypeStruct((B, HGRN_HEADS, HGRN_HEAD, HGRN_HEAD), F32)]
    return pl.pallas_call(
        functools.partial(_hgrn_kernel, bb=bb, C=C),
        grid=grid, in_specs=in_specs, out_specs=out_specs, out_shape=out_shape,
        scratch_shapes=[pltpu.VMEM((bb, HGRN_HEAD, HGRN_HEAD), F32)],
        compiler_params=pltpu.CompilerParams(
            dimension_semantics=("arbitrary", "arbitrary", "arbitrary"),
            vmem_limit_bytes=VMEM_LIMIT),
        name="hgrn",
    )(proj3, proj3, proj3, proj3, p["lb_logits"], p["hg_norm_w"], s_hgrn)


def _layer_norm(x, g, b):
    mu = jnp.mean(x, axis=-1, keepdims=True)
    d = x - mu
    var = jnp.mean(d * d, axis=-1, keepdims=True)
    return d * lax.rsqrt(var + LN_EPS) * g + b


def _post_kernel(x_ref, orw_ref, ohg_ref, wo1_ref, wo2_ref, g1_ref, b1_ref, wup_ref, wdn_ref,
                 g2_ref, b2_ref, y_ref):
    mix = (jnp.dot(orw_ref[...].astype(BF16), wo1_ref[...], preferred_element_type=F32)
           + jnp.dot(ohg_ref[...].astype(BF16), wo2_ref[...], preferred_element_type=F32))
    h1 = _layer_norm(ALPHA * x_ref[...] + mix, g1_ref[...], b1_ref[...])
    up = jnp.dot(h1.astype(BF16), wup_ref[...], preferred_element_type=F32)
    up = jnp.square(jnp.maximum(up, 0.0))
    ff = jnp.dot(up.astype(BF16), wdn_ref[...], preferred_element_type=F32)
    y_ref[...] = _layer_norm(ALPHA * h1 + ff, g2_ref[...], b2_ref[...])


def _post(x2, orw2, ohg2, p, tm):
    n = x2.shape[0]

    def const(shape):
        return pl.BlockSpec(shape, lambda i: (0, 0), pipeline_mode=pl.Buffered(1))

    return pl.pallas_call(
        _post_kernel,
        grid=(n // tm,),
        in_specs=[pl.BlockSpec((tm, D_MODEL), lambda i: (i, 0)),
                  pl.BlockSpec((tm, RWKV_WIDTH), lambda i: (i, 0)),
                  pl.BlockSpec((tm, HGRN_WIDTH), lambda i: (i, 0)),
                  const((RWKV_WIDTH, D_MODEL)), const((HGRN_WIDTH, D_MODEL)),
                  const((1, D_MODEL)), const((1, D_MODEL)),
                  const((D_MODEL, D_FF)), const((D_FF, D_MODEL)),
                  const((1, D_MODEL)), const((1, D_MODEL))],
        out_specs=pl.BlockSpec((tm, D_MODEL), lambda i: (i, 0)),
        out_shape=jax.ShapeDtypeStruct((n, D_MODEL), F32),
        compiler_params=pltpu.CompilerParams(dimension_semantics=("arbitrary",),
                                             vmem_limit_bytes=VMEM_LIMIT),
        name="post",
    )(x2, orw2, ohg2, p["wo_rw"], p["wo_hg"], p["ln1_g"], p["ln1_b"], p["w_up"], p["w_down"],
      p["ln2_g"], p["ln2_b"])


def _prep_params(w_in, shift_mu, w0, w1u, a0, a1u, g1u, k_k, k_a, r_k, ln_x_w, ln_x_b, lb_logits,
                 hg_norm_w, w_out, ln1_g, ln1_b, w_up, w_down, ln2_g, ln2_b):
    zw = jnp.zeros((LANES - DECAY_RANK, RWKV_WIDTH), F32)
    za = jnp.zeros((LANES - AICL_RANK, RWKV_WIDTH), F32)
    w_out_bf = w_out[0].astype(BF16)
    return {
        "w_in": w_in[0].astype(BF16),
        "shift_mu": shift_mu[0].reshape(1, RWKV_PROJ),
        "w0": w0[0].reshape(1, RWKV_WIDTH), "a0": a0[0].reshape(1, RWKV_WIDTH),
        "k_k": k_k[0].reshape(1, RWKV_WIDTH), "k_a": k_a[0].reshape(1, RWKV_WIDTH),
        "r_k": r_k[0].reshape(1, RWKV_WIDTH),
        "ln_x_w": ln_x_w[0].reshape(1, RWKV_WIDTH), "ln_x_b": ln_x_b[0].reshape(1, RWKV_WIDTH),
        "w1u_pad": jnp.concatenate([w1u[0], zw], axis=0),
        "a1u_pad": jnp.concatenate([za, a1u[0]], axis=0),
        "g1u": g1u[0],
        "lb_logits": lb_logits.astype(F32),
        "hg_norm_w": hg_norm_w[0].reshape(1, HGRN_WIDTH),
        "wo_rw": w_out_bf[:RWKV_WIDTH], "wo_hg": w_out_bf[RWKV_WIDTH:],
        "ln1_g": ln1_g[0].reshape(1, D_MODEL), "ln1_b": ln1_b[0].reshape(1, D_MODEL),
        "w_up": w_up[0].astype(BF16), "w_down": w_down[0].astype(BF16),
        "ln2_g": ln2_g[0].reshape(1, D_MODEL), "ln2_b": ln2_b[0].reshape(1, D_MODEL),
    }


def _run_group(x, s_rwkv, s_hgrn, s_shift, p, *, tm, rw_bb, rw_c, hg_bb, hg_c):
    B, T, _ = x.shape
    x2 = x.reshape(B * T, D_MODEL)
    proj2 = _proj(x2, p["w_in"], tm)
    proj3 = proj2.reshape(B, T, PROJ)
    o_rw, st_rw = _rwkv(proj3, s_shift.reshape(B, 1, RWKV_PROJ), s_rwkv, p, rw_bb, rw_c)
    o_hg, st_hg = _hgrn(proj3, s_hgrn, p, hg_bb, hg_c)
    y2 = _post(x2, o_rw.reshape(B * T, RWKV_WIDTH), o_hg.reshape(B * T, HGRN_WIDTH), p, tm)
    sh = proj3[:, T - 1, :RWKV_PROJ]
    return y2.reshape(B, T, D_MODEL), st_rw[None], st_hg[None], sh[None]


PROMPT_CFG = dict(tm=256, rw_bb=8, rw_c=64, hg_bb=8, hg_c=16)
SAMPLE_CFG = dict(tm=256, rw_bb=16, rw_c=8, hg_bb=8, hg_c=8)


def kernel(x_prompt, x_sample, state_rwkv, state_hgrn, state_shift, w_in, shift_mu, w0, w1u, a0, a1u, g1u, k_k, k_a, r_k, ln_x_w, ln_x_b, lb_logits, hg_norm_w, w_out, ln1_g, ln1_b, w_up, w_down, ln2_g, ln2_b):
    assert w_in.shape[0] == DEPTH
    p = _prep_params(w_in, shift_mu, w0, w1u, a0, a1u, g1u, k_k, k_a, r_k, ln_x_w, ln_x_b, lb_logits,
                     hg_norm_w, w_out, ln1_g, ln1_b, w_up, w_down, ln2_g, ln2_b)
    bp = x_prompt.shape[0]
    z_rw = jnp.zeros((bp, RWKV_HEADS, RWKV_HEAD, RWKV_HEAD), F32)
    z_hg = jnp.zeros((bp, HGRN_HEADS, HGRN_HEAD, HGRN_HEAD), F32)
    z_sh = jnp.zeros((bp, RWKV_PROJ), F32)
    y_p, rw_p, hg_p, sh_p = _run_group(x_prompt, z_rw, z_hg, z_sh, p, **PROMPT_CFG)
    y_s, rw_s, hg_s, sh_s = _run_group(x_sample, state_rwkv[0].astype(F32), state_hgrn[0].astype(F32),
                                       state_shift[0].astype(F32), p, **SAMPLE_CFG)
    return (y_p, y_s, rw_p, rw_s, hg_p, hg_s, sh_p, sh_s)
```

```python
import functools
import math

import jax
import jax.numpy as jnp
from jax import lax
from jax.experimental import pallas as pl
from jax.experimental.pallas import tpu as pltpu

F32 = jnp.float32
BF16 = jnp.bfloat16
HI = lax.Precision.HIGHEST

D_MODEL = 1024
RWKV_WIDTH = 512
RWKV_HEAD = 64
RWKV_HEADS = 8
HGRN_WIDTH = 512
HGRN_HEAD = 128
HGRN_HEADS = 4
DECAY_RANK = 64
AICL_RANK = 64
GATE_RANK = 128
RWKV_PROJ = 3 * RWKV_WIDTH + DECAY_RANK + AICL_RANK + GATE_RANK
HGRN_PROJ = 4 * HGRN_WIDTH
PROJ = RWKV_PROJ + HGRN_PROJ
D_FF = 4 * D_MODEL
DEPTH = 1
ALPHA = (2.0 * DEPTH) ** 0.25
LN_EPS = 1e-5
GN_EPS = RWKV_HEAD * 1e-5
RMS_EPS = 1e-6
DECAY_SCALE = math.exp(-0.5)

LANES = 128
SUBLANES = 8
HGRN_SUB = 16
PAIRS = RWKV_HEADS // 2
LORA_COL = 3 * RWKV_WIDTH
VMEM_LIMIT = 56 * 1024 * 1024


def _dot(a, b, dims, prec):
    return lax.dot_general(a, b, (dims, ((), ())), precision=prec, preferred_element_type=F32)


def _mm(a, b, prec=HI):
    return _dot(a, b, ((1,), (0,)), prec)


def _mm_nt(a, b, prec=HI):
    return _dot(a, b, ((1,), (1,)), prec)


def _mm_tn(a, b, prec=HI):
    return _dot(a, b, ((0,), (0,)), prec)


NN = ((1,), (0,))
NT = ((1,), (1,))
TN = ((0,), (0,))
REC_PASSES = 3


def _pieces(x, n=None):
    n = (2 if REC_PASSES == 3 else 1) if n is None else n
    out = []
    rem = x
    for j in range(n):
        pc = rem.astype(BF16)
        out.append(pc)
        if j + 1 < n:
            rem = rem - pc.astype(F32)
    return tuple(out)


def _dotp(a, b, dims):
    d = lambda x, y: lax.dot_general(x, y, (dims, ((), ())), preferred_element_type=F32)
    acc = d(a[0], b[0])
    if len(a) > 1 and len(b) > 1:
        acc = acc + (d(a[0], b[1]) + d(a[1], b[0]))
    elif len(b) > 1:
        for pc in b[1:]:
            acc = acc + d(a[0], pc)
    elif len(a) > 1:
        for pc in a[1:]:
            acc = acc + d(pc, b[0])
    return acc


def _sigmoid(x):
    return 1.0 / (1.0 + jnp.exp(-x))


def _proj_kernel(x_ref, w_ref, o_ref):
    o_ref[...] = jnp.dot(x_ref[...].astype(BF16), w_ref[...], preferred_element_type=F32)


def _proj(x2, w_in_bf, tm):
    n = x2.shape[0]
    return pl.pallas_call(
        _proj_kernel,
        grid=(n // tm,),
        in_specs=[pl.BlockSpec((tm, D_MODEL), lambda i: (i, 0)),
                  pl.BlockSpec((D_MODEL, PROJ), lambda i: (0, 0))],
        out_specs=pl.BlockSpec((tm, PROJ), lambda i: (i, 0)),
        out_shape=jax.ShapeDtypeStruct((n, PROJ), F32),
        compiler_params=pltpu.CompilerParams(dimension_semantics=("arbitrary",),
                                             vmem_limit_bytes=VMEM_LIMIT),
        name="proj",
    )(x2, w_in_bf)


def _rwkv_kernel(pr_ref, pk_ref, pv_ref, pl_ref, shr_ref, shk_ref, shv_ref, shl_ref,
                 mur_ref, muk_ref, muv_ref, mul_ref, w0_ref, a0_ref, kk_ref, ka_ref, rk_ref,
                 lnw_ref, lnb_ref, w1_ref, a1_ref, g1_ref, s0_ref,
                 o_ref, st_ref, s_scr, cr_scr, ck_scr, cv_scr, cl_scr, *, bb, C):
    c = pl.program_id(2)
    last = pl.num_programs(2) - 1
    H = RWKV_HEAD

    @pl.when(c == 0)
    def _init():
        z = jnp.zeros((H, H), F32)
        for i in range(bb):
            top = jnp.concatenate([s0_ref[i, 0], z], axis=1)
            bot = jnp.concatenate([z, s0_ref[i, 1]], axis=1)
            s_scr[i] = jnp.concatenate([top, bot], axis=0)
            cr_scr[i] = shr_ref[i]
            ck_scr[i] = shk_ref[i]
            cv_scr[i] = shv_ref[i]
            cl_scr[i] = shl_ref[i]

    lane = lax.broadcasted_iota(jnp.int32, (C, LANES), 1)
    m0 = lane < H
    rowc = lax.broadcasted_iota(jnp.int32, (C, C), 0)
    colc = lax.broadcasted_iota(jnp.int32, (C, C), 1)
    tri = (jnp.where(rowc >= colc, 1.0, 0.0).astype(BF16),)
    r2 = lax.broadcasted_iota(jnp.int32, (2 * C, 2 * C), 0)
    c2 = lax.broadcasted_iota(jnp.int32, (2 * C, 2 * C), 1)
    t2 = jnp.bitwise_and(r2, C - 1)
    s2 = jnp.bitwise_and(c2, C - 1)
    strict2 = t2 > s2
    incl2 = t2 >= s2
    same_head = (r2 >= C) == (c2 >= C)
    eye2 = jnp.where(r2 == c2, 1.0, 0.0).astype(F32)
    bl_r = lax.broadcasted_iota(jnp.int32, (LANES, LANES), 0) < H
    bl_c = lax.broadcasted_iota(jnp.int32, (LANES, LANES), 1) < H
    blockdiag = bl_r == bl_c

    def head_sum(x):
        s0 = jnp.sum(jnp.where(m0, x, 0.0), axis=-1, keepdims=True)
        s1 = jnp.sum(jnp.where(m0, 0.0, x), axis=-1, keepdims=True)
        return jnp.where(m0, s0, s1)

    def stack_heads(x):
        return jnp.concatenate([jnp.where(m0, x, 0.0), jnp.where(m0, 0.0, x)], axis=-2)

    def merge_heads(x):
        return jnp.where(m0, x[:C], x[C:])

    mu_r, mu_k, mu_v, mu_l = mur_ref[...], muk_ref[...], muv_ref[...], mul_ref[...]
    w1 = w1_ref[...].astype(BF16)
    a1 = a1_ref[...].astype(BF16)
    g1 = g1_ref[...].astype(BF16)

    ids = range(bb)

    def shifted(p_ref, carry_ref):
        p = p_ref[...]
        row = lax.broadcasted_iota(jnp.int32, p.shape, 1)
        flat = p.reshape(bb * C, p.shape[-1])
        prev = jnp.where(row == 0, carry_ref[...], pltpu.roll(flat, 1, 0).reshape(p.shape))
        carry_ref[...] = p[:, C - 1:C, :]
        return p, prev

    p_r, prev_r = shifted(pr_ref, cr_scr)
    p_k, prev_k = shifted(pk_ref, ck_scr)
    p_v, prev_v = shifted(pv_ref, cv_scr)
    p_l, prev_l = shifted(pl_ref, cl_scr)
    r = p_r + mu_r * (prev_r - p_r)
    xk = p_k + mu_k * (prev_k - p_k)
    v = p_v + mu_v * (prev_v - p_v)
    xl = (p_l + mu_l * (prev_l - p_l)).reshape(bb * C, 2 * LANES)

    xl_lo = xl[:, :LANES]
    y_w = w0_ref[...] + jnp.dot(jnp.tanh(xl_lo).astype(BF16), w1, preferred_element_type=F32)
    lw = (-DECAY_SCALE * _sigmoid(y_w)).reshape(bb, C, LANES)
    a_lr = _sigmoid(a0_ref[...] + jnp.dot(xl_lo.astype(BF16), a1, preferred_element_type=F32))
    a_lr = a_lr.reshape(bb, C, LANES)
    gate = jnp.dot(_sigmoid(xl[:, LANES:]).astype(BF16), g1, preferred_element_type=F32)
    gate = gate.reshape(bb, C, LANES)

    kk = xk * kk_ref[...]
    kk = kk / jnp.maximum(jnp.sqrt(head_sum(kk * kk)), 1e-12)
    k2 = xk * (1.0 + (a_lr - 1.0) * ka_ref[...])
    a = -kk
    b = kk * a_lr

    lw_wide = jnp.concatenate([lw[i] for i in ids], axis=1)
    cum_wide = _dotp(tri, _pieces(lw_wide, 3), NN)
    cum = jnp.stack([cum_wide[:, i * LANES:(i + 1) * LANES] for i in ids])
    cum_last = cum[:, C - 1:C, :]
    e_in = jnp.exp(-cum)
    e_out = jnp.exp(cum_last - cum)
    at = a * jnp.exp(cum - lw)
    rt = r * jnp.exp(cum)
    bt = b * e_in
    kt = k2 * e_in

    lhs = _pieces(jnp.concatenate([stack_heads(at), stack_heads(rt)], axis=1))
    rhs = _pieces(jnp.concatenate([bt, kt], axis=1))
    bt_hat = _pieces(stack_heads(bt))
    g = [_dotp(tuple(pc[i] for pc in lhs), tuple(pc[i] for pc in rhs), NT) for i in ids]
    n_bd = [_dotp(tuple(pc[i, :2 * C] for pc in lhs), tuple(pc[i] for pc in bt_hat), NT) for i in ids]
    g_a = [_pieces(jnp.where(strict2, g[i][:2 * C], 0.0)) for i in ids]
    g_r = [_pieces(jnp.where(incl2, g[i][2 * C:], 0.0)) for i in ids]
    pw = [jnp.where(jnp.logical_and(strict2, same_head), n_bd[i], 0.0) for i in ids]
    inv = [eye2 + pw[i] for i in ids]
    span = 2
    while span < C:
        pw_p = [_pieces(pw[i]) for i in ids]
        pw = [_dotp(pw_p[i], pw_p[i], NN) for i in ids]
        inv = [inv[i] + _dotp(_pieces(inv[i]), _pieces(pw[i]), NN) for i in ids]
        span *= 2

    S = [s_scr[i] for i in ids]
    atrt = _pieces(jnp.concatenate([at, rt], axis=1))
    pq = [_dotp(tuple(pc[i] for pc in atrt), _pieces(S[i]), NT) for i in ids]
    v0 = _pieces(jnp.concatenate([jnp.zeros_like(v), v], axis=1))
    w = jnp.stack([pq[i][:C] + merge_heads(_dotp(g_a[i], tuple(pc[i] for pc in v0), NN)) for i in ids])
    w_hat = _pieces(stack_heads(w))
    u2 = [_dotp(_pieces(inv[i]), tuple(pc[i] for pc in w_hat), NN) for i in ids]
    u = jnp.stack([u2[i][:C] + u2[i][C:] for i in ids])
    z = _pieces(jnp.concatenate([u, v], axis=1))
    y = jnp.stack([pq[i][C:] + merge_heads(_dotp(g_r[i], tuple(pc[i] for pc in z), NN)) for i in ids])
    bkh = _pieces(jnp.concatenate([b * e_out, k2 * e_out], axis=1))
    for i in ids:
        upd = _dotp(tuple(pc[i] for pc in z), tuple(pc[i] for pc in bkh), TN)
        s_scr[i] = S[i] * jnp.exp(cum_last[i]) + jnp.where(blockdiag, upd, 0.0)

    mu = head_sum(y) * (1.0 / H)
    d = y - mu
    var = head_sum(d * d) * (1.0 / H)
    yn = d * lax.rsqrt(var + GN_EPS) * lnw_ref[...] + lnb_ref[...]
    bonus = head_sum(r * k2 * rk_ref[...]) * v
    o_ref[...] = (yn + bonus) * gate

    @pl.when(c == last)
    def _fin():
        for i in range(bb):
            s_fin = s_scr[i]
            st_ref[i, 0] = s_fin[:H, :H]
            st_ref[i, 1] = s_fin[H:, H:]


def _rwkv(proj3, s_shift3, s_rwkv, p, bb, C):
    B, T, _ = proj3.shape
    grid = (B // bb, PAIRS, T // C)
    nb = RWKV_WIDTH // LANES

    def tok(col0):
        return pl.BlockSpec((bb, C, LANES), lambda b, h, c: (b, c, col0 + h))

    def first(col0):
        return pl.BlockSpec((bb, 1, LANES), lambda b, h, c: (b, 0, col0 + h))

    def vec(col0):
        return pl.BlockSpec((1, LANES), lambda b, h, c: (0, col0 + h))

    lora_blk = LORA_COL // (2 * LANES)
    in_specs = [
        tok(0), tok(nb), tok(2 * nb),
        pl.BlockSpec((bb, C, 2 * LANES), lambda b, h, c: (b, c, lora_blk)),
        first(0), first(nb), first(2 * nb),
        pl.BlockSpec((bb, 1, 2 * LANES), lambda b, h, c: (b, 0, lora_blk)),
        vec(0), vec(nb), vec(2 * nb),
        pl.BlockSpec((1, 2 * LANES), lambda b, h, c: (0, lora_blk)),
        vec(0), vec(0), vec(0), vec(0), vec(0), vec(0), vec(0),
        pl.BlockSpec((LANES, LANES), lambda b, h, c: (0, h)),
        pl.BlockSpec((LANES, LANES), lambda b, h, c: (0, h)),
        pl.BlockSpec((GATE_RANK, LANES), lambda b, h, c: (0, h)),
        pl.BlockSpec((bb, 2, RWKV_HEAD, RWKV_HEAD), lambda b, h, c: (b, h, 0, 0)),
    ]
    out_specs = [
        pl.BlockSpec((bb, C, LANES), lambda b, h, c: (b, c, h)),
        pl.BlockSpec((bb, 2, RWKV_HEAD, RWKV_HEAD), lambda b, h, c: (b, h, 0, 0)),
    ]
    out_shape = [jax.ShapeDtypeStruct((B, T, RWKV_WIDTH), F32),
                 jax.ShapeDtypeStruct((B, RWKV_HEADS, RWKV_HEAD, RWKV_HEAD), F32)]
    scratch = [pltpu.VMEM((bb, LANES, LANES), F32),
               pltpu.VMEM((bb, 1, LANES), F32), pltpu.VMEM((bb, 1, LANES), F32),
               pltpu.VMEM((bb, 1, LANES), F32), pltpu.VMEM((bb, 1, 2 * LANES), F32)]
    return pl.pallas_call(
        functools.partial(_rwkv_kernel, bb=bb, C=C),
        grid=grid, in_specs=in_specs, out_specs=out_specs, out_shape=out_shape,
        scratch_shapes=scratch,
        compiler_params=pltpu.CompilerParams(
            dimension_semantics=("arbitrary", "arbitrary", "arbitrary"),
            vmem_limit_bytes=VMEM_LIMIT),
        name="rwkv",
    )(proj3, proj3, proj3, proj3, s_shift3, s_shift3, s_shift3, s_shift3,
      p["shift_mu"], p["shift_mu"], p["shift_mu"], p["shift_mu"],
      p["w0"], p["a0"], p["k_k"], p["k_a"], p["r_k"], p["ln_x_w"], p["ln_x_b"],
      p["w1u_pad"], p["a1u_pad"], p["g1u"], s_rwkv)


def _hgrn_kernel(q_ref, f_ref, i_ref, g_ref, lbl_ref, hgw_ref, s0_ref,
                 o_ref, st_ref, s_scr, *, bb, C, CS):
    c = pl.program_id(2)
    last = pl.num_programs(2) - 1

    @pl.when(c == 0)
    def _init():
        for i in range(bb):
            s_scr[i] = s0_ref[i, 0].T

    logits = lbl_ref[...]
    ex = jnp.exp(logits - jnp.max(logits, axis=0, keepdims=True))
    lb = ex[0:1] / jnp.sum(ex, axis=0, keepdims=True)

    ids = range(bb)
    ns = C // CS
    rowc = lax.broadcasted_iota(jnp.int32, (C, C), 0)
    colc = lax.broadcasted_iota(jnp.int32, (C, C), 1)
    tri = (jnp.where(rowc >= colc, 1.0, 0.0).astype(BF16),)
    row8 = lax.broadcasted_iota(jnp.int32, (SUBLANES, LANES), 0)

    q_h, f_h, i_h, g_h = q_ref[...], f_ref[...], i_ref[...], g_ref[...]
    q = q_h * _sigmoid(q_h)
    fg = lb + (1.0 - lb) * _sigmoid(f_h)
    kin = 1.0 - fg
    lf = jnp.log(fg)

    lf_wide = jnp.concatenate([lf[i] for i in ids], axis=1)
    bc_wide = _dotp(tri, _pieces(lf_wide, 3), NN)
    bc = jnp.stack([bc_wide[:, i * LANES:(i + 1) * LANES] for i in ids])
    b_last = bc[:, C - 1:C, :]

    def rows_of(vals):
        return jnp.concatenate([jnp.broadcast_to(x, (bb, CS, LANES)) for x in vals], axis=1)

    ends = [bc[:, (I + 1) * CS - 1:(I + 1) * CS, :] for I in range(ns)]
    starts = [jnp.zeros((bb, 1, LANES), F32)] + ends[:-1]
    m_lo = rows_of(starts)
    m_hi = rows_of(ends)
    lbc = bc - m_lo
    qd = q * jnp.exp(lbc)
    kd = kin * jnp.exp(m_hi - bc)
    qs = _pieces(qd * jnp.exp(m_lo))
    ks = _pieces(kd * jnp.exp(b_last - m_hi))
    iv = _pieces(i_h)

    St = [s_scr[i] for i in ids]
    o_state = [_dotp(tuple(pc[i] for pc in qs), _pieces(St[i]), NT) for i in ids]
    for i in ids:
        upd = _dotp(tuple(pc[i] for pc in iv), tuple(pc[i] for pc in ks), TN)
        s_scr[i] = St[i] * jnp.exp(b_last[i]) + upd

    o_inter = [[None] * ns for _ in ids]
    for I in range(1, ns):
        k_resc = jnp.concatenate(
            [kd[:, J * CS:(J + 1) * CS, :] * jnp.exp(starts[I] - ends[J]) for J in range(I)], axis=1)
        k_resc = _pieces(k_resc)
        q_I = _pieces(qd[:, I * CS:(I + 1) * CS, :])
        att = [_dotp(tuple(pc[i] for pc in q_I), tuple(pc[i] for pc in k_resc), NT) for i in ids]
        for i in ids:
            o_inter[i][I] = _dotp(_pieces(att[i]), tuple(pc[i, :I * CS] for pc in iv), NN)

    groups = CS // SUBLANES
    for i in ids:
        blocks = []
        for I in range(ns):
            acc = [None] * groups
            for s in range(CS):
                r_s = I * CS + s
                lbc_s = lbc[i, r_s:r_s + 1, :]
                kin_s = kin[i, r_s:r_s + 1, :]
                i_s = i_h[i, r_s:r_s + 1, :]
                for gi in range(groups):
                    lo = gi * SUBLANES
                    if lo + SUBLANES - 1 < s:
                        continue
                    r0 = I * CS + lo
                    dec = jnp.exp(lbc[i, r0:r0 + SUBLANES, :] - lbc_s)
                    if lo < s:
                        dec = jnp.where(row8 >= s - lo, dec, 0.0)
                    att = jnp.sum(q[i, r0:r0 + SUBLANES, :] * dec * kin_s, axis=-1, keepdims=True)
                    term = att * i_s
                    acc[gi] = term if acc[gi] is None else acc[gi] + term
            blk = jnp.concatenate(acc, axis=0) if groups > 1 else acc[0]
            if o_inter[i][I] is not None:
                blk = blk + o_inter[i][I]
            blocks.append(blk)
        o = (jnp.concatenate(blocks, axis=0) if ns > 1 else blocks[0]) + o_state[i]
        ms = jnp.mean(o * o, axis=-1, keepdims=True)
        g_i = g_h[i]
        o_ref[i] = o * lax.rsqrt(ms + RMS_EPS) * hgw_ref[...] * (g_i * _sigmoid(g_i))

    @pl.when(c == last)
    def _fin():
        for i in range(bb):
            st_ref[i, 0] = s_scr[i].T


def _hgrn(proj3, s_hgrn, p, bb, C):
    B, T, _ = proj3.shape
    grid = (B // bb, HGRN_HEADS, T // C)
    base = RWKV_PROJ // LANES
    nb = HGRN_WIDTH // LANES

    def tok(col0):
        return pl.BlockSpec((bb, C, LANES), lambda b, h, c: (b, c, col0 + h))

    in_specs = [
        tok(base), tok(base + nb), tok(base + 2 * nb), tok(base + 3 * nb),
        pl.BlockSpec((DEPTH + 1, LANES), lambda b, h, c: (0, h)),
        pl.BlockSpec((1, LANES), lambda b, h, c: (0, h)),
        pl.BlockSpec((bb, 1, HGRN_HEAD, HGRN_HEAD), lambda b, h, c: (b, h, 0, 0)),
    ]
    out_specs = [
        pl.BlockSpec((bb, C, LANES), lambda b, h, c: (b, c, h)),
        pl.BlockSpec((bb, 1, HGRN_HEAD, HGRN_HEAD), lambda b, h, c: (b, h, 0, 0)),
    ]
    out_shape = [jax.ShapeDtypeStruct((B, T, HGRN_WIDTH), F32),
                 jax.ShapeDtypeStruct((B, HGRN_HEADS, HGRN_HEAD, HGRN_HEAD), F32)]
    return pl.pallas_call(
        functools.partial(_hgrn_kernel, bb=bb, C=C, CS=min(C, HGRN_SUB)),
        grid=grid, in_specs=in_specs, out_specs=out_specs, out_shape=out_shape,
        scratch_shapes=[pltpu.VMEM((bb, HGRN_HEAD, HGRN_HEAD), F32)],
        compiler_params=pltpu.CompilerParams(
            dimension_semantics=("arbitrary", "arbitrary", "arbitrary"),
            vmem_limit_bytes=VMEM_LIMIT),
        name="hgrn",
    )(proj3, proj3, proj3, proj3, p["lb_logits"], p["hg_norm_w"], s_hgrn)


def _layer_norm(x, g, b):
    mu = jnp.mean(x, axis=-1, keepdims=True)
    d = x - mu
    var = jnp.mean(d * d, axis=-1, keepdims=True)
    return d * lax.rsqrt(var + LN_EPS) * g + b


def _post_kernel(x_ref, orw_ref, ohg_ref, wo1_ref, wo2_ref, g1_ref, b1_ref, wup_ref, wdn_ref,
                 g2_ref, b2_ref, y_ref):
    mix = (jnp.dot(orw_ref[...].astype(BF16), wo1_ref[...], preferred_element_type=F32)
           + jnp.dot(ohg_ref[...].astype(BF16), wo2_ref[...], preferred_element_type=F32))
    h1 = _layer_norm(ALPHA * x_ref[...] + mix, g1_ref[...], b1_ref[...])
    up = jnp.dot(h1.astype(BF16), wup_ref[...], preferred_element_type=F32)
    up = jnp.square(jnp.maximum(up, 0.0))
    ff = jnp.dot(up.astype(BF16), wdn_ref[...], preferred_element_type=F32)
    y_ref[...] = _layer_norm(ALPHA * h1 + ff, g2_ref[...], b2_ref[...])


def _post(x2, orw2, ohg2, p, tm):
    n = x2.shape[0]

    def const(shape):
        return pl.BlockSpec(shape, lambda i: (0, 0), pipeline_mode=pl.Buffered(1))

    return pl.pallas_call(
        _post_kernel,
        grid=(n // tm,),
        in_specs=[pl.BlockSpec((tm, D_MODEL), lambda i: (i, 0)),
                  pl.BlockSpec((tm, RWKV_WIDTH), lambda i: (i, 0)),
                  pl.BlockSpec((tm, HGRN_WIDTH), lambda i: (i, 0)),
                  const((RWKV_WIDTH, D_MODEL)), const((HGRN_WIDTH, D_MODEL)),
                  const((1, D_MODEL)), const((1, D_MODEL)),
                  const((D_MODEL, D_FF)), const((D_FF, D_MODEL)),
                  const((1, D_MODEL)), const((1, D_MODEL))],
        out_specs=pl.BlockSpec((tm, D_MODEL), lambda i: (i, 0)),
        out_shape=jax.ShapeDtypeStruct((n, D_MODEL), F32),
        compiler_params=pltpu.CompilerParams(dimension_semantics=("arbitrary",),
                                             vmem_limit_bytes=VMEM_LIMIT),
        name="post",
    )(x2, orw2, ohg2, p["wo_rw"], p["wo_hg"], p["ln1_g"], p["ln1_b"], p["w_up"], p["w_down"],
      p["ln2_g"], p["ln2_b"])


def _prep_params(w_in, shift_mu, w0, w1u, a0, a1u, g1u, k_k, k_a, r_k, ln_x_w, ln_x_b, lb_logits,
                 hg_norm_w, w_out, ln1_g, ln1_b, w_up, w_down, ln2_g, ln2_b):
    zw = jnp.zeros((LANES - DECAY_RANK, RWKV_WIDTH), F32)
    za = jnp.zeros((LANES - AICL_RANK, RWKV_WIDTH), F32)
    w_out_bf = w_out[0].astype(BF16)
    return {
        "w_in": w_in[0].astype(BF16),
        "shift_mu": shift_mu[0].reshape(1, RWKV_PROJ),
        "w0": w0[0].reshape(1, RWKV_WIDTH), "a0": a0[0].reshape(1, RWKV_WIDTH),
        "k_k": k_k[0].reshape(1, RWKV_WIDTH), "k_a": k_a[0].reshape(1, RWKV_WIDTH),
        "r_k": r_k[0].reshape(1, RWKV_WIDTH),
        "ln_x_w": ln_x_w[0].reshape(1, RWKV_WIDTH), "ln_x_b": ln_x_b[0].reshape(1, RWKV_WIDTH),
        "w1u_pad": jnp.concatenate([w1u[0], zw], axis=0),
        "a1u_pad": jnp.concatenate([za, a1u[0]], axis=0),
        "g1u": g1u[0],
        "lb_logits": lb_logits.astype(F32),
        "hg_norm_w": hg_norm_w[0].reshape(1, HGRN_WIDTH),
        "wo_rw": w_out_bf[:RWKV_WIDTH], "wo_hg": w_out_bf[RWKV_WIDTH:],
        "ln1_g": ln1_g[0].reshape(1, D_MODEL), "ln1_b": ln1_b[0].reshape(1, D_MODEL),
        "w_up": w_up[0].astype(BF16), "w_down": w_down[0].astype(BF16),
        "ln2_g": ln2_g[0].reshape(1, D_MODEL), "ln2_b": ln2_b[0].reshape(1, D_MODEL),
    }


def _run_group(x, s_rwkv, s_hgrn, s_shift, p, *, tm, rw_bb, rw_c, hg_bb, hg_c):
    B, T, _ = x.shape
    x2 = x.reshape(B * T, D_MODEL)
    proj2 = _proj(x2, p["w_in"], tm)
    proj3 = proj2.reshape(B, T, PROJ)
    o_rw, st_rw = _rwkv(proj3, s_shift.reshape(B, 1, RWKV_PROJ), s_rwkv, p, rw_bb, rw_c)
    o_hg, st_hg = _hgrn(proj3, s_hgrn, p, hg_bb, hg_c)
    y2 = _post(x2, o_rw.reshape(B * T, RWKV_WIDTH), o_hg.reshape(B * T, HGRN_WIDTH), p, tm)
    sh = proj3[:, T - 1, :RWKV_PROJ]
    return y2.reshape(B, T, D_MODEL), st_rw[None], st_hg[None], sh[None]


PROMPT_CFG = dict(tm=256, rw_bb=8, rw_c=64, hg_bb=8, hg_c=64)
SAMPLE_CFG = dict(tm=256, rw_bb=16, rw_c=8, hg_bb=8, hg_c=8)


def kernel(x_prompt, x_sample, state_rwkv, state_hgrn, state_shift, w_in, shift_mu, w0, w1u, a0, a1u, g1u, k_k, k_a, r_k, ln_x_w, ln_x_b, lb_logits, hg_norm_w, w_out, ln1_g, ln1_b, w_up, w_down, ln2_g, ln2_b):
    assert w_in.shape[0] == DEPTH
    p = _prep_params(w_in, shift_mu, w0, w1u, a0, a1u, g1u, k_k, k_a, r_k, ln_x_w, ln_x_b, lb_logits,
                     hg_norm_w, w_out, ln1_g, ln1_b, w_up, w_down, ln2_g, ln2_b)
    bp = x_prompt.shape[0]
    z_rw = jnp.zeros((bp, RWKV_HEADS, RWKV_HEAD, RWKV_HEAD), F32)
    z_hg = jnp.zeros((bp, HGRN_HEADS, HGRN_HEAD, HGRN_HEAD), F32)
    z_sh = jnp.zeros((bp, RWKV_PROJ), F32)
    y_p, rw_p, hg_p, sh_p = _run_group(x_prompt, z_rw, z_hg, z_sh, p, **PROMPT_CFG)
    y_s, rw_s, hg_s, sh_s = _run_group(x_sample, state_rwkv[0].astype(F32), state_hgrn[0].astype(F32),
                                       state_shift[0].astype(F32), p, **SAMPLE_CFG)
    return (y_p, y_s, rw_p, rw_s, hg_p, hg_s, sh_p, sh_s)
```

```python
import functools
import math

import jax
import jax.numpy as jnp
from jax import lax
from jax.experimental import pallas as pl
from jax.experimental.pallas import tpu as pltpu

F32 = jnp.float32
BF16 = jnp.bfloat16
HI = lax.Precision.HIGHEST

D_MODEL = 1024
RWKV_WIDTH = 512
RWKV_HEAD = 64
RWKV_HEADS = 8
HGRN_WIDTH = 512
HGRN_HEAD = 128
HGRN_HEADS = 4
DECAY_RANK = 64
AICL_RANK = 64
GATE_RANK = 128
RWKV_PROJ = 3 * RWKV_WIDTH + DECAY_RANK + AICL_RANK + GATE_RANK
HGRN_PROJ = 4 * HGRN_WIDTH
PROJ = RWKV_PROJ + HGRN_PROJ
D_FF = 4 * D_MODEL
DEPTH = 1
ALPHA = (2.0 * DEPTH) ** 0.25
LN_EPS = 1e-5
GN_EPS = RWKV_HEAD * 1e-5
RMS_EPS = 1e-6
DECAY_SCALE = math.exp(-0.5)

LANES = 128
SUBLANES = 8
HGRN_SUB = 16
PAIRS = RWKV_HEADS // 2
LORA_COL = 3 * RWKV_WIDTH
VMEM_LIMIT = 56 * 1024 * 1024


def _dot(a, b, dims, prec):
    return lax.dot_general(a, b, (dims, ((), ())), precision=prec, preferred_element_type=F32)


def _mm(a, b, prec=HI):
    return _dot(a, b, ((1,), (0,)), prec)


def _mm_nt(a, b, prec=HI):
    return _dot(a, b, ((1,), (1,)), prec)


def _mm_tn(a, b, prec=HI):
    return _dot(a, b, ((0,), (0,)), prec)


NN = ((1,), (0,))
NT = ((1,), (1,))
TN = ((0,), (0,))
REC_PASSES = 1


def _pieces(x, n=None):
    n = (2 if REC_PASSES == 3 else 1) if n is None else n
    out = []
    rem = x
    for j in range(n):
        pc = rem.astype(BF16)
        out.append(pc)
        if j + 1 < n:
            rem = rem - pc.astype(F32)
    return tuple(out)


def _dotp(a, b, dims):
    d = lambda x, y: lax.dot_general(x, y, (dims, ((), ())), preferred_element_type=F32)
    acc = d(a[0], b[0])
    if len(a) > 1 and len(b) > 1:
        acc = acc + (d(a[0], b[1]) + d(a[1], b[0]))
    elif len(b) > 1:
        for pc in b[1:]:
            acc = acc + d(a[0], pc)
    elif len(a) > 1:
        for pc in a[1:]:
            acc = acc + d(pc, b[0])
    return acc


def _sigmoid(x):
    return 1.0 / (1.0 + jnp.exp(-x))


def _proj_kernel(x_ref, w_ref, o_ref):
    o_ref[...] = jnp.dot(x_ref[...].astype(BF16), w_ref[...], preferred_element_type=F32)


def _proj(x2, w_in_bf, tm):
    n = x2.shape[0]
    return pl.pallas_call(
        _proj_kernel,
        grid=(n // tm,),
        in_specs=[pl.BlockSpec((tm, D_MODEL), lambda i: (i, 0)),
                  pl.BlockSpec((D_MODEL, PROJ), lambda i: (0, 0))],
        out_specs=pl.BlockSpec((tm, PROJ), lambda i: (i, 0)),
        out_shape=jax.ShapeDtypeStruct((n, PROJ), F32),
        compiler_params=pltpu.CompilerParams(dimension_semantics=("arbitrary",),
                                             vmem_limit_bytes=VMEM_LIMIT),
        name="proj",
    )(x2, w_in_bf)


def _rwkv_kernel(pr_ref, pk_ref, pv_ref, pl_ref, shr_ref, shk_ref, shv_ref, shl_ref,
                 mur_ref, muk_ref, muv_ref, mul_ref, w0_ref, a0_ref, kk_ref, ka_ref, rk_ref,
                 lnw_ref, lnb_ref, w1_ref, a1_ref, g1_ref, s0_ref,
                 o_ref, st_ref, s_scr, cr_scr, ck_scr, cv_scr, cl_scr, *, bb, C):
    c = pl.program_id(2)
    last = pl.num_programs(2) - 1
    H = RWKV_HEAD

    @pl.when(c == 0)
    def _init():
        z = jnp.zeros((H, H), F32)
        for i in range(bb):
            top = jnp.concatenate([s0_ref[i, 0], z], axis=1)
            bot = jnp.concatenate([z, s0_ref[i, 1]], axis=1)
            s_scr[i] = jnp.concatenate([top, bot], axis=0)
            cr_scr[i] = shr_ref[i]
            ck_scr[i] = shk_ref[i]
            cv_scr[i] = shv_ref[i]
            cl_scr[i] = shl_ref[i]

    lane = lax.broadcasted_iota(jnp.int32, (C, LANES), 1)
    m0 = lane < H
    rowc = lax.broadcasted_iota(jnp.int32, (C, C), 0)
    colc = lax.broadcasted_iota(jnp.int32, (C, C), 1)
    tri = (jnp.where(rowc >= colc, 1.0, 0.0).astype(BF16),)
    r2 = lax.broadcasted_iota(jnp.int32, (2 * C, 2 * C), 0)
    c2 = lax.broadcasted_iota(jnp.int32, (2 * C, 2 * C), 1)
    t2 = jnp.bitwise_and(r2, C - 1)
    s2 = jnp.bitwise_and(c2, C - 1)
    strict2 = t2 > s2
    incl2 = t2 >= s2
    same_head = (r2 >= C) == (c2 >= C)
    eye2 = jnp.where(r2 == c2, 1.0, 0.0).astype(F32)
    bl_r = lax.broadcasted_iota(jnp.int32, (LANES, LANES), 0) < H
    bl_c = lax.broadcasted_iota(jnp.int32, (LANES, LANES), 1) < H
    blockdiag = bl_r == bl_c

    def head_sum(x):
        s0 = jnp.sum(jnp.where(m0, x, 0.0), axis=-1, keepdims=True)
        s1 = jnp.sum(jnp.where(m0, 0.0, x), axis=-1, keepdims=True)
        return jnp.where(m0, s0, s1)

    def stack_heads(x):
        return jnp.concatenate([jnp.where(m0, x, 0.0), jnp.where(m0, 0.0, x)], axis=-2)

    def merge_heads(x):
        return jnp.where(m0, x[:C], x[C:])

    mu_r, mu_k, mu_v, mu_l = mur_ref[...], muk_ref[...], muv_ref[...], mul_ref[...]
    w1 = w1_ref[...].astype(BF16)
    a1 = a1_ref[...].astype(BF16)
    g1 = g1_ref[...].astype(BF16)

    ids = range(bb)

    def shifted(p_ref, carry_ref):
        p = p_ref[...]
        row = lax.broadcasted_iota(jnp.int32, p.shape, 1)
        flat = p.reshape(bb * C, p.shape[-1])
        prev = jnp.where(row == 0, carry_ref[...], pltpu.roll(flat, 1, 0).reshape(p.shape))
        carry_ref[...] = p[:, C - 1:C, :]
        return p, prev

    p_r, prev_r = shifted(pr_ref, cr_scr)
    p_k, prev_k = shifted(pk_ref, ck_scr)
    p_v, prev_v = shifted(pv_ref, cv_scr)
    p_l, prev_l = shifted(pl_ref, cl_scr)
    r = p_r + mu_r * (prev_r - p_r)
    xk = p_k + mu_k * (prev_k - p_k)
    v = p_v + mu_v * (prev_v - p_v)
    xl = (p_l + mu_l * (prev_l - p_l)).reshape(bb * C, 2 * LANES)

    xl_lo = xl[:, :LANES]
    y_w = w0_ref[...] + jnp.dot(jnp.tanh(xl_lo).astype(BF16), w1, preferred_element_type=F32)
    lw = (-DECAY_SCALE * _sigmoid(y_w)).reshape(bb, C, LANES)
    a_lr = _sigmoid(a0_ref[...] + jnp.dot(xl_lo.astype(BF16), a1, preferred_element_type=F32))
    a_lr = a_lr.reshape(bb, C, LANES)
    gate = jnp.dot(_sigmoid(xl[:, LANES:]).astype(BF16), g1, preferred_element_type=F32)
    gate = gate.reshape(bb, C, LANES)

    kk = xk * kk_ref[...]
    kk = kk / jnp.maximum(jnp.sqrt(head_sum(kk * kk)), 1e-12)
    k2 = xk * (1.0 + (a_lr - 1.0) * ka_ref[...])
    a = -kk
    b = kk * a_lr

    lw_wide = jnp.concatenate([lw[i] for i in ids], axis=1)
    cum_wide = _dotp(tri, _pieces(lw_wide, 3), NN)
    cum = jnp.stack([cum_wide[:, i * LANES:(i + 1) * LANES] for i in ids])
    cum_last = cum[:, C - 1:C, :]
    e_in = jnp.exp(-cum)
    e_out = jnp.exp(cum_last - cum)
    at = a * jnp.exp(cum - lw)
    rt = r * jnp.exp(cum)
    bt = b * e_in
    kt = k2 * e_in

    lhs = _pieces(jnp.concatenate([stack_heads(at), stack_heads(rt)], axis=1))
    rhs = _pieces(jnp.concatenate([bt, kt], axis=1))
    bt_hat = _pieces(stack_heads(bt))
    g = [_dotp(tuple(pc[i] for pc in lhs), tuple(pc[i] for pc in rhs), NT) for i in ids]
    n_bd = [_dotp(tuple(pc[i, :2 * C] for pc in lhs), tuple(pc[i] for pc in bt_hat), NT) for i in ids]
    g_a = [_pieces(jnp.where(strict2, g[i][:2 * C], 0.0)) for i in ids]
    g_r = [_pieces(jnp.where(incl2, g[i][2 * C:], 0.0)) for i in ids]
    pw = [jnp.where(jnp.logical_and(strict2, same_head), n_bd[i], 0.0) for i in ids]
    inv = [eye2 + pw[i] for i in ids]
    span = 2
    while span < C:
        pw_p = [_pieces(pw[i]) for i in ids]
        pw = [_dotp(pw_p[i], pw_p[i], NN) for i in ids]
        inv = [inv[i] + _dotp(_pieces(inv[i]), _pieces(pw[i]), NN) for i in ids]
        span *= 2

    S = [s_scr[i] for i in ids]
    atrt = _pieces(jnp.concatenate([at, rt], axis=1))
    pq = [_dotp(tuple(pc[i] for pc in atrt), _pieces(S[i]), NT) for i in ids]
    v0 = _pieces(jnp.concatenate([jnp.zeros_like(v), v], axis=1))
    w = jnp.stack([pq[i][:C] + merge_heads(_dotp(g_a[i], tuple(pc[i] for pc in v0), NN)) for i in ids])
    w_hat = _pieces(stack_heads(w))
    u2 = [_dotp(_pieces(inv[i]), tuple(pc[i] for pc in w_hat), NN) for i in ids]
    u = jnp.stack([u2[i][:C] + u2[i][C:] for i in ids])
    z = _pieces(jnp.concatenate([u, v], axis=1))
    y = jnp.stack([pq[i][C:] + merge_heads(_dotp(g_r[i], tuple(pc[i] for pc in z), NN)) for i in ids])
    bkh = _pieces(jnp.concatenate([b * e_out, k2 * e_out], axis=1))
    for i in ids:
        upd = _dotp(tuple(pc[i] for pc in z), tuple(pc[i] for pc in bkh), TN)
        s_scr[i] = S[i] * jnp.exp(cum_last[i]) + jnp.where(blockdiag, upd, 0.0)

    mu = head_sum(y) * (1.0 / H)
    d = y - mu
    var = head_sum(d * d) * (1.0 / H)
    yn = d * lax.rsqrt(var + GN_EPS) * lnw_ref[...] + lnb_ref[...]
    bonus = head_sum(r * k2 * rk_ref[...]) * v
    o_ref[...] = (yn + bonus) * gate

    @pl.when(c == last)
    def _fin():
        for i in range(bb):
            s_fin = s_scr[i]
            st_ref[i, 0] = s_fin[:H, :H]
            st_ref[i, 1] = s_fin[H:, H:]


def _rwkv(proj3, s_shift3, s_rwkv, p, bb, C):
    B, T, _ = proj3.shape
    grid = (B // bb, PAIRS, T // C)
    nb = RWKV_WIDTH // LANES

    def tok(col0):
        return pl.BlockSpec((bb, C, LANES), lambda b, h, c: (b, c, col0 + h))

    def first(col0):
        return pl.BlockSpec((bb, 1, LANES), lambda b, h, c: (b, 0, col0 + h))

    def vec(col0):
        return pl.BlockSpec((1, LANES), lambda b, h, c: (0, col0 + h))

    lora_blk = LORA_COL // (2 * LANES)
    in_specs = [
        tok(0), tok(nb), tok(2 * nb),
        pl.BlockSpec((bb, C, 2 * LANES), lambda b, h, c: (b, c, lora_blk)),
        first(0), first(nb), first(2 * nb),
        pl.BlockSpec((bb, 1, 2 * LANES), lambda b, h, c: (b, 0, lora_blk)),
        vec(0), vec(nb), vec(2 * nb),
        pl.BlockSpec((1, 2 * LANES), lambda b, h, c: (0, lora_blk)),
        vec(0), vec(0), vec(0), vec(0), vec(0), vec(0), vec(0),
        pl.BlockSpec((LANES, LANES), lambda b, h, c: (0, h)),
        pl.BlockSpec((LANES, LANES), lambda b, h, c: (0, h)),
        pl.BlockSpec((GATE_RANK, LANES), lambda b, h, c: (0, h)),
        pl.BlockSpec((bb, 2, RWKV_HEAD, RWKV_HEAD), lambda b, h, c: (b, h, 0, 0)),
    ]
    out_specs = [
        pl.BlockSpec((bb, C, LANES), lambda b, h, c: (b, c, h)),
        pl.BlockSpec((bb, 2, RWKV_HEAD, RWKV_HEAD), lambda b, h, c: (b, h, 0, 0)),
    ]
    out_shape = [jax.ShapeDtypeStruct((B, T, RWKV_WIDTH), F32),
                 jax.ShapeDtypeStruct((B, RWKV_HEADS, RWKV_HEAD, RWKV_HEAD), F32)]
    scratch = [pltpu.VMEM((bb, LANES, LANES), F32),
               pltpu.VMEM((bb, 1, LANES), F32), pltpu.VMEM((bb, 1, LANES), F32),
               pltpu.VMEM((bb, 1, LANES), F32), pltpu.VMEM((bb, 1, 2 * LANES), F32)]
    return pl.pallas_call(
        functools.partial(_rwkv_kernel, bb=bb, C=C),
        grid=grid, in_specs=in_specs, out_specs=out_specs, out_shape=out_shape,
        scratch_shapes=scratch,
        compiler_params=pltpu.CompilerParams(
            dimension_semantics=("arbitrary", "arbitrary", "arbitrary"),
            vmem_limit_bytes=VMEM_LIMIT),
        name="rwkv",
    )(proj3, proj3, proj3, proj3, s_shift3, s_shift3, s_shift3, s_shift3,
      p["shift_mu"], p["shift_mu"], p["shift_mu"], p["shift_mu"],
      p["w0"], p["a0"], p["k_k"], p["k_a"], p["r_k"], p["ln_x_w"], p["ln_x_b"],
      p["w1u_pad"], p["a1u_pad"], p["g1u"], s_rwkv)


def _hgrn_kernel(q_ref, f_ref, i_ref, g_ref, lbl_ref, hgw_ref, s0_ref,
                 o_ref, st_ref, s_scr, *, bb, C, CS):
    c = pl.program_id(2)
    last = pl.num_programs(2) - 1

    @pl.when(c == 0)
    def _init():
        for i in range(bb):
            s_scr[i] = s0_ref[i, 0].T

    logits = lbl_ref[...]
    ex = jnp.exp(logits - jnp.max(logits, axis=0, keepdims=True))
    lb = ex[0:1] / jnp.sum(ex, axis=0, keepdims=True)

    ids = range(bb)
    ns = C // CS
    rowc = lax.broadcasted_iota(jnp.int32, (C, C), 0)
    colc = lax.broadcasted_iota(jnp.int32, (C, C), 1)
    tri = (jnp.where(rowc >= colc, 1.0, 0.0).astype(BF16),)
    row8 = lax.broadcasted_iota(jnp.int32, (SUBLANES, LANES), 0)

    q_h, f_h, i_h, g_h = q_ref[...], f_ref[...], i_ref[...], g_ref[...]
    q = q_h * _sigmoid(q_h)
    fg = lb + (1.0 - lb) * _sigmoid(f_h)
    kin = 1.0 - fg
    lf = jnp.log(fg)

    lf_wide = jnp.concatenate([lf[i] for i in ids], axis=1)
    bc_wide = _dotp(tri, _pieces(lf_wide, 3), NN)
    bc = jnp.stack([bc_wide[:, i * LANES:(i + 1) * LANES] for i in ids])
    b_last = bc[:, C - 1:C, :]

    def rows_of(vals):
        return jnp.concatenate([jnp.broadcast_to(x, (bb, CS, LANES)) for x in vals], axis=1)

    ends = [bc[:, (I + 1) * CS - 1:(I + 1) * CS, :] for I in range(ns)]
    starts = [jnp.zeros((bb, 1, LANES), F32)] + ends[:-1]
    m_lo = rows_of(starts)
    m_hi = rows_of(ends)
    lbc = bc - m_lo
    qd = q * jnp.exp(lbc)
    kd = kin * jnp.exp(m_hi - bc)
    qs = _pieces(qd * jnp.exp(m_lo))
    ks = _pieces(kd * jnp.exp(b_last - m_hi))
    iv = _pieces(i_h)

    St = [s_scr[i] for i in ids]
    o_state = [_dotp(tuple(pc[i] for pc in qs), _pieces(St[i]), NT) for i in ids]
    for i in ids:
        upd = _dotp(tuple(pc[i] for pc in iv), tuple(pc[i] for pc in ks), TN)
        s_scr[i] = St[i] * jnp.exp(b_last[i]) + upd

    o_inter = [[None] * ns for _ in ids]
    for I in range(1, ns):
        k_resc = jnp.concatenate(
            [kd[:, J * CS:(J + 1) * CS, :] * jnp.exp(starts[I] - ends[J]) for J in range(I)], axis=1)
        k_resc = _pieces(k_resc)
        q_I = _pieces(qd[:, I * CS:(I + 1) * CS, :])
        att = [_dotp(tuple(pc[i] for pc in q_I), tuple(pc[i] for pc in k_resc), NT) for i in ids]
        for i in ids:
            o_inter[i][I] = _dotp(_pieces(att[i]), tuple(pc[i, :I * CS] for pc in iv), NN)

    groups = CS // SUBLANES
    for i in ids:
        blocks = []
        for I in range(ns):
            acc = [None] * groups
            for s in range(CS):
                r_s = I * CS + s
                lbc_s = lbc[i, r_s:r_s + 1, :]
                kin_s = kin[i, r_s:r_s + 1, :]
                i_s = i_h[i, r_s:r_s + 1, :]
                for gi in range(groups):
                    lo = gi * SUBLANES
                    if lo + SUBLANES - 1 < s:
                        continue
                    r0 = I * CS + lo
                    dec = jnp.exp(lbc[i, r0:r0 + SUBLANES, :] - lbc_s)
                    if lo < s:
                        dec = jnp.where(row8 >= s - lo, dec, 0.0)
                    att = jnp.sum(q[i, r0:r0 + SUBLANES, :] * dec * kin_s, axis=-1, keepdims=True)
                    term = att * i_s
                    acc[gi] = term if acc[gi] is None else acc[gi] + term
            blk = jnp.concatenate(acc, axis=0) if groups > 1 else acc[0]
            if o_inter[i][I] is not None:
                blk = blk + o_inter[i][I]
            blocks.append(blk)
        o = (jnp.concatenate(blocks, axis=0) if ns > 1 else blocks[0]) + o_state[i]
        ms = jnp.mean(o * o, axis=-1, keepdims=True)
        g_i = g_h[i]
        o_ref[i] = o * lax.rsqrt(ms + RMS_EPS) * hgw_ref[...] * (g_i * _sigmoid(g_i))

    @pl.when(c == last)
    def _fin():
        for i in range(bb):
            st_ref[i, 0] = s_scr[i].T


def _hgrn(proj3, s_hgrn, p, bb, C):
    B, T, _ = proj3.shape
    grid = (B // bb, HGRN_HEADS, T // C)
    base = RWKV_PROJ // LANES
    nb = HGRN_WIDTH // LANES

    def tok(col0):
        return pl.BlockSpec((bb, C, LANES), lambda b, h, c: (b, c, col0 + h))

    in_specs = [
        tok(base), tok(base + nb), tok(base + 2 * nb), tok(base + 3 * nb),
        pl.BlockSpec((DEPTH + 1, LANES), lambda b, h, c: (0, h)),
        pl.BlockSpec((1, LANES), lambda b, h, c: (0, h)),
        pl.BlockSpec((bb, 1, HGRN_HEAD, HGRN_HEAD), lambda b, h, c: (b, h, 0, 0)),
    ]
    out_specs = [
        pl.BlockSpec((bb, C, LANES), lambda b, h, c: (b, c, h)),
        pl.BlockSpec((bb, 1, HGRN_HEAD, HGRN_HEAD), lambda b, h, c: (b, h, 0, 0)),
    ]
    out_shape = [jax.ShapeDtypeStruct((B, T, HGRN_WIDTH), F32),
                 jax.ShapeDtypeStruct((B, HGRN_HEADS, HGRN_HEAD, HGRN_HEAD), F32)]
    return pl.pallas_call(
        functools.partial(_hgrn_kernel, bb=bb, C=C, CS=min(C, HGRN_SUB)),
        grid=grid, in_specs=in_specs, out_specs=out_specs, out_shape=out_shape,
        scratch_shapes=[pltpu.VMEM((bb, HGRN_HEAD, HGRN_HEAD), F32)],
        compiler_params=pltpu.CompilerParams(
            dimension_semantics=("arbitrary", "arbitrary", "arbitrary"),
            vmem_limit_bytes=VMEM_LIMIT),
        name="hgrn",
    )(proj3, proj3, proj3, proj3, p["lb_logits"], p["hg_norm_w"], s_hgrn)


def _layer_norm(x, g, b):
    mu = jnp.mean(x, axis=-1, keepdims=True)
    d = x - mu
    var = jnp.mean(d * d, axis=-1, keepdims=True)
    return d * lax.rsqrt(var + LN_EPS) * g + b


def _post_kernel(x_ref, orw_ref, ohg_ref, wo1_ref, wo2_ref, g1_ref, b1_ref, wup_ref, wdn_ref,
                 g2_ref, b2_ref, y_ref):
    mix = (jnp.dot(orw_ref[...].astype(BF16), wo1_ref[...], preferred_element_type=F32)
           + jnp.dot(ohg_ref[...].astype(BF16), wo2_ref[...], preferred_element_type=F32))
    h1 = _layer_norm(ALPHA * x_ref[...] + mix, g1_ref[...], b1_ref[...])
    up = jnp.dot(h1.astype(BF16), wup_ref[...], preferred_element_type=F32)
    up = jnp.square(jnp.maximum(up, 0.0))
    ff = jnp.dot(up.astype(BF16), wdn_ref[...], preferred_element_type=F32)
    y_ref[...] = _layer_norm(ALPHA * h1 + ff, g2_ref[...], b2_ref[...])


def _post(x2, orw2, ohg2, p, tm):
    n = x2.shape[0]

    def const(shape):
        return pl.BlockSpec(shape, lambda i: (0, 0), pipeline_mode=pl.Buffered(1))

    return pl.pallas_call(
        _post_kernel,
        grid=(n // tm,),
        in_specs=[pl.BlockSpec((tm, D_MODEL), lambda i: (i, 0)),
                  pl.BlockSpec((tm, RWKV_WIDTH), lambda i: (i, 0)),
                  pl.BlockSpec((tm, HGRN_WIDTH), lambda i: (i, 0)),
                  const((RWKV_WIDTH, D_MODEL)), const((HGRN_WIDTH, D_MODEL)),
                  const((1, D_MODEL)), const((1, D_MODEL)),
                  const((D_MODEL, D_FF)), const((D_FF, D_MODEL)),
                  const((1, D_MODEL)), const((1, D_MODEL))],
        out_specs=pl.BlockSpec((tm, D_MODEL), lambda i: (i, 0)),
        out_shape=jax.ShapeDtypeStruct((n, D_MODEL), F32),
        compiler_params=pltpu.CompilerParams(dimension_semantics=("arbitrary",),
                                             vmem_limit_bytes=VMEM_LIMIT),
        name="post",
    )(x2, orw2, ohg2, p["wo_rw"], p["wo_hg"], p["ln1_g"], p["ln1_b"], p["w_up"], p["w_down"],
      p["ln2_g"], p["ln2_b"])


def _prep_params(w_in, shift_mu, w0, w1u, a0, a1u, g1u, k_k, k_a, r_k, ln_x_w, ln_x_b, lb_logits,
                 hg_norm_w, w_out, ln1_g, ln1_b, w_up, w_down, ln2_g, ln2_b):
    zw = jnp.zeros((LANES - DECAY_RANK, RWKV_WIDTH), F32)
    za = jnp.zeros((LANES - AICL_RANK, RWKV_WIDTH), F32)
    w_out_bf = w_out[0].astype(BF16)
    return {
        "w_in": w_in[0].astype(BF16),
        "shift_mu": shift_mu[0].reshape(1, RWKV_PROJ),
        "w0": w0[0].reshape(1, RWKV_WIDTH), "a0": a0[0].reshape(1, RWKV_WIDTH),
        "k_k": k_k[0].reshape(1, RWKV_WIDTH), "k_a": k_a[0].reshape(1, RWKV_WIDTH),
        "r_k": r_k[0].reshape(1, RWKV_WIDTH),
        "ln_x_w": ln_x_w[0].reshape(1, RWKV_WIDTH), "ln_x_b": ln_x_b[0].reshape(1, RWKV_WIDTH),
        "w1u_pad": jnp.concatenate([w1u[0], zw], axis=0),
        "a1u_pad": jnp.concatenate([za, a1u[0]], axis=0),
        "g1u": g1u[0],
        "lb_logits": lb_logits.astype(F32),
        "hg_norm_w": hg_norm_w[0].reshape(1, HGRN_WIDTH),
        "wo_rw": w_out_bf[:RWKV_WIDTH], "wo_hg": w_out_bf[RWKV_WIDTH:],
        "ln1_g": ln1_g[0].reshape(1, D_MODEL), "ln1_b": ln1_b[0].reshape(1, D_MODEL),
        "w_up": w_up[0].astype(BF16), "w_down": w_down[0].astype(BF16),
        "ln2_g": ln2_g[0].reshape(1, D_MODEL), "ln2_b": ln2_b[0].reshape(1, D_MODEL),
    }


def _run_group(x, s_rwkv, s_hgrn, s_shift, p, *, tm, rw_bb, rw_c, hg_bb, hg_c):
    B, T, _ = x.shape
    x2 = x.reshape(B * T, D_MODEL)
    proj2 = _proj(x2, p["w_in"], tm)
    proj3 = proj2.reshape(B, T, PROJ)
    o_rw, st_rw = _rwkv(proj3, s_shift.reshape(B, 1, RWKV_PROJ), s_rwkv, p, rw_bb, rw_c)
    o_hg, st_hg = _hgrn(proj3, s_hgrn, p, hg_bb, hg_c)
    y2 = _post(x2, o_rw.reshape(B * T, RWKV_WIDTH), o_hg.reshape(B * T, HGRN_WIDTH), p, tm)
    sh = proj3[:, T - 1, :RWKV_PROJ]
    return y2.reshape(B, T, D_MODEL), st_rw[None], st_hg[None], sh[None]


PROMPT_CFG = dict(tm=256, rw_bb=8, rw_c=64, hg_bb=8, hg_c=64)
SAMPLE_CFG = dict(tm=256, rw_bb=16, rw_c=8, hg_bb=8, hg_c=8)


def kernel(x_prompt, x_sample, state_rwkv, state_hgrn, state_shift, w_in, shift_mu, w0, w1u, a0, a1u, g1u, k_k, k_a, r_k, ln_x_w, ln_x_b, lb_logits, hg_norm_w, w_out, ln1_g, ln1_b, w_up, w_down, ln2_g, ln2_b):
    assert w_in.shape[0] == DEPTH
    p = _prep_params(w_in, shift_mu, w0, w1u, a0, a1u, g1u, k_k, k_a, r_k, ln_x_w, ln_x_b, lb_logits,
                     hg_norm_w, w_out, ln1_g, ln1_b, w_up, w_down, ln2_g, ln2_b)
    bp = x_prompt.shape[0]
    z_rw = jnp.zeros((bp, RWKV_HEADS, RWKV_HEAD, RWKV_HEAD), F32)
    z_hg = jnp.zeros((bp, HGRN_HEADS, HGRN_HEAD, HGRN_HEAD), F32)
    z_sh = jnp.zeros((bp, RWKV_PROJ), F32)
    y_p, rw_p, hg_p, sh_p = _run_group(x_prompt, z_rw, z_hg, z_sh, p, **PROMPT_CFG)
    y_s, rw_s, hg_s, sh_s = _run_group(x_sample, state_rwkv[0].astype(F32), state_hgrn[0].astype(F32),
                                       state_shift[0].astype(F32), p, **SAMPLE_CFG)
    return (y_p, y_s, rw_p, rw_s, hg_p, hg_s, sh_p, sh_s)
```

```python
import functools
import math

import jax
import jax.numpy as jnp
from jax import lax
from jax.experimental import pallas as pl
from jax.experimental.pallas import tpu as pltpu

F32 = jnp.float32
BF16 = jnp.bfloat16

D_MODEL = 1024
RWKV_WIDTH = 512
RWKV_HEAD = 64
RWKV_HEADS = 8
HGRN_WIDTH = 512
HGRN_HEAD = 128
HGRN_HEADS = 4
DECAY_RANK = 64
AICL_RANK = 64
GATE_RANK = 128
RWKV_PROJ = 3 * RWKV_WIDTH + DECAY_RANK + AICL_RANK + GATE_RANK
HGRN_PROJ = 4 * HGRN_WIDTH
PROJ = RWKV_PROJ + HGRN_PROJ
D_FF = 4 * D_MODEL
DEPTH = 1
ALPHA = (2.0 * DEPTH) ** 0.25
LN_EPS = 1e-5
GN_EPS = RWKV_HEAD * 1e-5
RMS_EPS = 1e-6
DECAY_SCALE = math.exp(-0.5)

LANES = 128
SUBLANES = 8
PAIRS = RWKV_HEADS // 2
LORA_COL = 3 * RWKV_WIDTH
VMEM_LIMIT = 56 * 1024 * 1024

NN = ((1,), (0,))
NT = ((1,), (1,))
TN = ((0,), (0,))


def _dot(a, b, dims):
    return lax.dot_general(a, b, (dims, ((), ())), preferred_element_type=F32)


def _bf(x):
    return x.astype(BF16)


def _cumsum_rows(tri, x):
    p0 = _bf(x)
    r1 = x - p0.astype(F32)
    p1 = _bf(r1)
    p2 = _bf(r1 - p1.astype(F32))
    return _dot(tri, p0, NN) + _dot(tri, p1, NN) + _dot(tri, p2, NN)


def _sigmoid(x):
    return 0.5 * jnp.tanh(0.5 * x) + 0.5


def _interleave(tasks):
    gens = list(tasks)
    nxt = [next(g, None) for g in gens]
    spent = {"M": 0.0, "V": 0.0}
    turn = 0
    while any(k is not None for k in nxt):
        ready = {kind: [j for j, k in enumerate(nxt) if k is not None and k[0] == kind] for kind in spent}
        want = "M" if (ready["M"] and (spent["M"] <= spent["V"] or not ready["V"])) else "V"
        if want == "M":
            cands = ready["M"]
            t = cands[turn % len(cands)]
            turn += 1
        else:
            t = ready["V"][0]
        spent[want] += nxt[t][1]
        nxt[t] = next(gens[t], None)


def _proj_kernel(x_ref, w_ref, o_ref):
    o_ref[...] = jnp.dot(x_ref[...].astype(BF16), w_ref[...], preferred_element_type=F32)


def _proj(x2, w_in_bf, tm):
    n = x2.shape[0]
    return pl.pallas_call(
        _proj_kernel,
        grid=(n // tm,),
        in_specs=[pl.BlockSpec((tm, D_MODEL), lambda i: (i, 0)),
                  pl.BlockSpec((D_MODEL, PROJ), lambda i: (0, 0))],
        out_specs=pl.BlockSpec((tm, PROJ), lambda i: (i, 0)),
        out_shape=jax.ShapeDtypeStruct((n, PROJ), F32),
        compiler_params=pltpu.CompilerParams(dimension_semantics=("arbitrary",),
                                             vmem_limit_bytes=VMEM_LIMIT),
        name="proj",
    )(x2, w_in_bf)


def _rwkv_task(lo, n, C, pr_ref, pk_ref, pv_ref, pl_ref, mur_ref, muk_ref, muv_ref, mul_ref,
               w0_ref, a0_ref, kk_ref, ka_ref, rk_ref, lnw_ref, lnb_ref, w1_ref, a1_ref, g1_ref,
               o_ref, s_scr, cr_scr, ck_scr, cv_scr, cl_scr):
    H = RWKV_HEAD
    ids = range(n)
    grp = slice(lo, lo + n)
    lane = lax.broadcasted_iota(jnp.int32, (C, LANES), 1)
    m0 = lane < H
    rowc = lax.broadcasted_iota(jnp.int32, (C, C), 0)
    colc = lax.broadcasted_iota(jnp.int32, (C, C), 1)
    tri = jnp.where(rowc >= colc, 1.0, 0.0).astype(BF16)
    r2 = lax.broadcasted_iota(jnp.int32, (2 * C, 2 * C), 0)
    c2 = lax.broadcasted_iota(jnp.int32, (2 * C, 2 * C), 1)
    t2 = jnp.bitwise_and(r2, C - 1)
    s2 = jnp.bitwise_and(c2, C - 1)
    strict2 = t2 > s2
    incl2 = t2 >= s2
    strict_bd = jnp.logical_and(strict2, (r2 >= C) == (c2 >= C))
    eye2 = jnp.where(r2 == c2, 1.0, 0.0).astype(F32)
    bl_r = lax.broadcasted_iota(jnp.int32, (LANES, LANES), 0) < H
    bl_c = lax.broadcasted_iota(jnp.int32, (LANES, LANES), 1) < H
    blockdiag = bl_r == bl_c
    dot_cost = 30.0 * n

    def head_sum(x):
        s0 = jnp.sum(jnp.where(m0, x, 0.0), axis=-1, keepdims=True)
        s1 = jnp.sum(jnp.where(m0, 0.0, x), axis=-1, keepdims=True)
        return jnp.where(m0, s0, s1)

    def stack_heads(x):
        return jnp.concatenate([jnp.where(m0, x, 0.0), jnp.where(m0, 0.0, x)], axis=-2)

    def merge_heads(x):
        return jnp.where(m0, x[:C], x[C:])

    def shifted(p_ref, carry_ref, mu_ref):
        p = p_ref[grp]
        row = lax.broadcasted_iota(jnp.int32, p.shape, 1)
        flat = p.reshape(n * C, p.shape[-1])
        prev = jnp.where(row == 0, carry_ref[grp], pltpu.roll(flat, 1, 0).reshape(p.shape))
        carry_ref[grp] = p[:, C - 1:C, :]
        return p + mu_ref[...] * (prev - p)

    w1 = _bf(w1_ref[...])
    a1 = _bf(a1_ref[...])
    g1 = _bf(g1_ref[...])
    d = {}

    yield ("V", 60.0 * n)
    d["r"] = shifted(pr_ref, cr_scr, mur_ref)
    d["xk"] = shifted(pk_ref, ck_scr, muk_ref)
    d["v"] = shifted(pv_ref, cv_scr, muv_ref)
    xl = shifted(pl_ref, cl_scr, mul_ref).reshape(n * C, 2 * LANES)
    xl_lo = xl[:, :LANES]
    d["dw"] = jnp.dot(_bf(jnp.tanh(xl_lo)), w1, preferred_element_type=F32)
    d["da"] = jnp.dot(_bf(xl_lo), a1, preferred_element_type=F32)
    d["gate"] = jnp.dot(_bf(_sigmoid(xl[:, LANES:])), g1, preferred_element_type=F32).reshape(n, C, LANES)

    yield ("V", 60.0 * n)
    lw = -DECAY_SCALE * _sigmoid(w0_ref[...] + d.pop("dw"))
    d["lw"] = lw.reshape(n, C, LANES)
    a_lr = _sigmoid(a0_ref[...] + d.pop("da")).reshape(n, C, LANES)
    xk = d.pop("xk")
    kk = xk * kk_ref[...]
    d["kk"] = kk * jnp.minimum(lax.rsqrt(head_sum(kk * kk)), 1e12)
    d["k2"] = xk * (1.0 + (a_lr - 1.0) * ka_ref[...])
    d["b"] = d["kk"] * a_lr
    d["cum"] = jnp.stack([_cumsum_rows(tri, d["lw"][j]) for j in ids])

    yield ("V", 80.0 * n)
    cum, lw, k2, b, r, v = d.pop("cum"), d.pop("lw"), d["k2"], d.pop("b"), d["r"], d["v"]
    cum_last = cum[:, C - 1:C, :]
    e_in = jnp.exp(-cum)
    e_out = jnp.exp(cum_last - cum)
    at = -d.pop("kk") * jnp.exp(cum - lw)
    rt = r * jnp.exp(cum)
    bt = b * e_in
    kt = k2 * e_in
    d["lhs"] = _bf(jnp.concatenate([stack_heads(at), stack_heads(rt)], axis=1))
    d["rhs"] = _bf(jnp.concatenate([bt, kt], axis=1))
    d["bt_hat"] = _bf(stack_heads(bt))
    d["atrt"] = _bf(jnp.concatenate([at, rt], axis=1))
    d["v0"] = _bf(jnp.concatenate([jnp.zeros_like(v), v], axis=1))
    d["bkh"] = _bf(jnp.concatenate([b * e_out, k2 * e_out], axis=1))
    d["dec"] = jnp.exp(cum_last)
    d["bonus"] = head_sum(r * k2 * rk_ref[...]) * v
    d.pop("k2")
    d.pop("r")

    yield ("M", dot_cost)
    g = [_dot(d["lhs"][j], d["rhs"][j], NT) for j in ids]
    d.pop("rhs")
    g_a = [_bf(jnp.where(strict2, g[j][:2 * C], 0.0)) for j in ids]
    g_r = [_bf(jnp.where(incl2, g[j][2 * C:], 0.0)) for j in ids]
    yield ("M", dot_cost)
    pw = [jnp.where(strict_bd, _dot(d["lhs"][j, :2 * C], d["bt_hat"][j], NT), 0.0) for j in ids]
    d.pop("lhs")
    d.pop("bt_hat")
    inv = [eye2 + pw[j] for j in ids]
    span = 2
    while span < C:
        yield ("M", dot_cost)
        pw = [_bf(pw[j]) for j in ids]
        pw = [_dot(pw[j], pw[j], NN) for j in ids]
        yield ("M", dot_cost)
        inv = [inv[j] + _dot(_bf(inv[j]), _bf(pw[j]), NN) for j in ids]
        span *= 2
    yield ("M", dot_cost)
    S = [s_scr[lo + j] for j in ids]
    pq = [_dot(d["atrt"][j], _bf(S[j]), NT) for j in ids]
    d.pop("atrt")
    yield ("M", dot_cost)
    w = [pq[j][:C] + merge_heads(_dot(g_a[j], d["v0"][j], NN)) for j in ids]
    d.pop("v0")
    yield ("M", dot_cost)
    u2 = [_dot(_bf(inv[j]), _bf(stack_heads(w[j])), NN) for j in ids]
    z = [_bf(jnp.concatenate([u2[j][:C] + u2[j][C:], d["v"][j]], axis=0)) for j in ids]
    yield ("M", dot_cost)
    y = [pq[j][C:] + merge_heads(_dot(g_r[j], z[j], NN)) for j in ids]
    yield ("M", dot_cost)
    for j in ids:
        upd = _dot(z[j], d["bkh"][j], TN)
        s_scr[lo + j] = S[j] * d["dec"][j] + jnp.where(blockdiag, upd, 0.0)

    yield ("V", 40.0 * n)
    y = jnp.stack(y)
    mu = head_sum(y) * (1.0 / H)
    dy = y - mu
    var = head_sum(dy * dy) * (1.0 / H)
    yn = dy * lax.rsqrt(var + GN_EPS) * lnw_ref[...] + lnb_ref[...]
    o_ref[grp] = (yn + d.pop("bonus")) * d.pop("gate")


def _mid_rows(x, h):
    R = x.shape[0]
    row = lax.broadcasted_iota(jnp.int32, x.shape, 0)
    if 2 * h >= SUBLANES:
        return jnp.concatenate(
            [jnp.broadcast_to(x[b * 2 * h + h - 1:b * 2 * h + h], (2 * h, LANES)) for b in range(R // (2 * h))],
            axis=0)
    if h == 1:
        return jnp.where(jnp.bitwise_and(row, 1) == 1, pltpu.roll(x, 1, 0), x)
    picks = []
    for half in range(SUBLANES // (2 * h)):
        r0 = half * 2 * h + h - 1
        picks.append(jnp.concatenate(
            [jnp.broadcast_to(x[g * SUBLANES + r0:g * SUBLANES + r0 + 1], (SUBLANES, LANES))
             for g in range(R // SUBLANES)], axis=0))
    out = picks[-1]
    sub = jnp.bitwise_and(row, SUBLANES - 1)
    for half in range(len(picks) - 2, -1, -1):
        out = jnp.where(sub < (half + 1) * 2 * h, picks[half], out)
    return out


def _hgrn_task(lo, n, C, q_ref, f_ref, i_ref, g_ref, lbl_ref, hgw_ref, o_ref, s_scr):
    ids = range(n)
    grp = slice(lo, lo + n)
    logits = lbl_ref[...]
    ex = jnp.exp(logits - jnp.max(logits, axis=0, keepdims=True))
    lb = ex[0:1] / jnp.sum(ex, axis=0, keepdims=True)
    t_i = lax.broadcasted_iota(jnp.int32, (C, C), 0)
    s_i = lax.broadcasted_iota(jnp.int32, (C, C), 1)
    tri = jnp.where(t_i >= s_i, 1.0, 0.0).astype(BF16)
    halves = []
    h = C // 2
    while h >= 1:
        halves.append(h)
        h //= 2

    def level_mask(h):
        same = jnp.bitwise_and(t_i, -2 * h) == jnp.bitwise_and(s_i, -2 * h)
        return jnp.logical_and(same, jnp.logical_and(jnp.bitwise_and(t_i, h) != 0, jnp.bitwise_and(s_i, h) == 0))

    dot_cost = 20.0 * n
    d = {}

    yield ("V", 40.0 * n)
    q_h = q_ref[grp].reshape(n * C, LANES)
    f_h = f_ref[grp].reshape(n * C, LANES)
    q = q_h * _sigmoid(q_h)
    fg = lb + (1.0 - lb) * _sigmoid(f_h)
    kin = 1.0 - fg
    lf = jnp.log(fg)
    bc_list = [_cumsum_rows(tri, lf[j * C:(j + 1) * C]) for j in ids]

    yield ("V", 30.0 * n)
    bc = jnp.concatenate(bc_list, axis=0)
    b_last = jnp.concatenate([jnp.broadcast_to(bc_list[j][C - 1:C], (C, LANES)) for j in ids], axis=0)
    qs = q * jnp.exp(bc)
    ks = kin * jnp.exp(b_last - bc)
    iv = _bf(i_ref[grp])

    def seq(x, j):
        return _bf(x[j * C:(j + 1) * C])

    yield ("M", dot_cost)
    S = [s_scr[lo + j] for j in ids]
    o_state = [_dot(seq(qs, j), _bf(S[j]), NT) for j in ids]
    yield ("M", dot_cost)
    for j in ids:
        s_scr[lo + j] = S[j] * jnp.exp(bc_list[j][C - 1:C]) + _dot(iv[j], seq(ks, j), TN)
    yield ("M", dot_cost)
    att = [jnp.where(t_i == s_i, _dot(seq(q, j), seq(kin, j), NT), 0.0) for j in ids]
    for h in halves:
        yield ("V", 25.0 * n)
        dmid = jnp.exp(-jnp.abs(bc - _mid_rows(bc, h)))
        ql = q * dmid
        kl = kin * dmid
        yield ("M", dot_cost)
        msk = level_mask(h)
        att = [jnp.where(msk, _dot(seq(ql, j), seq(kl, j), NT), att[j]) for j in ids]
    yield ("M", dot_cost)
    o_intra = [_dot(_bf(att[j]), iv[j], NN) for j in ids]

    yield ("V", 25.0 * n)
    o = jnp.stack([o_intra[j] + o_state[j] for j in ids])
    ms = jnp.mean(o * o, axis=-1, keepdims=True)
    g_h = g_ref[grp]
    o_ref[grp] = o * lax.rsqrt(ms + RMS_EPS) * hgw_ref[...] * (g_h * _sigmoid(g_h))


def _mixer_kernel(pr_ref, pk_ref, pv_ref, pl_ref, shr_ref, shk_ref, shv_ref, shl_ref,
                  mur_ref, muk_ref, muv_ref, mul_ref, w0_ref, a0_ref, kk_ref, ka_ref, rk_ref,
                  lnw_ref, lnb_ref, w1_ref, a1_ref, g1_ref, srw0_ref,
                  q_ref, f_ref, i_ref, g_ref, lbl_ref, hgw_ref, shg0_ref,
                  orw_ref, strw_ref, ohg_ref, sthg_ref,
                  srw_scr, cr_scr, ck_scr, cv_scr, cl_scr, shg_scr, *, bb, C, n_groups):
    c = pl.program_id(2)
    last = pl.num_programs(2) - 1
    H = RWKV_HEAD

    @pl.when(c == 0)
    def _init():
        z = jnp.zeros((H, H), F32)
        for i in range(bb):
            top = jnp.concatenate([srw0_ref[i, 0], z], axis=1)
            bot = jnp.concatenate([z, srw0_ref[i, 1]], axis=1)
            srw_scr[i] = jnp.concatenate([top, bot], axis=0)
            shg_scr[i] = shg0_ref[i, 0].T
        cr_scr[...] = shr_ref[...]
        ck_scr[...] = shk_ref[...]
        cv_scr[...] = shv_ref[...]
        cl_scr[...] = shl_ref[...]

    per = bb // n_groups
    tasks = []
    for gidx in range(n_groups):
        tasks.append(_rwkv_task(gidx * per, per, C, pr_ref, pk_ref, pv_ref, pl_ref,
                                mur_ref, muk_ref, muv_ref, mul_ref,
                                w0_ref, a0_ref, kk_ref, ka_ref, rk_ref, lnw_ref, lnb_ref,
                                w1_ref, a1_ref, g1_ref, orw_ref, srw_scr, cr_scr, ck_scr, cv_scr, cl_scr))
        tasks.append(_hgrn_task(gidx * per, per, C, q_ref, f_ref, i_ref, g_ref, lbl_ref, hgw_ref,
                                ohg_ref, shg_scr))
    _interleave(tasks)

    @pl.when(c == last)
    def _fin():
        for i in range(bb):
            s_fin = srw_scr[i]
            strw_ref[i, 0] = s_fin[:H, :H]
            strw_ref[i, 1] = s_fin[H:, H:]
            sthg_ref[i, 0] = shg_scr[i].T


def _mixer(proj3, s_shift3, s_rwkv, s_hgrn, p, bb, C, n_groups):
    B, T, _ = proj3.shape
    grid = (B // bb, PAIRS, T // C)
    nb = RWKV_WIDTH // LANES
    hg0 = RWKV_PROJ // LANES
    lora_blk = LORA_COL // (2 * LANES)

    def tok(col0):
        return pl.BlockSpec((bb, C, LANES), lambda b, h, c: (b, c, col0 + h))

    def first(col0):
        return pl.BlockSpec((bb, 1, LANES), lambda b, h, c: (b, 0, col0 + h))

    def vec(col0):
        return pl.BlockSpec((1, LANES), lambda b, h, c: (0, col0 + h))

    in_specs = [
        tok(0), tok(nb), tok(2 * nb),
        pl.BlockSpec((bb, C, 2 * LANES), lambda b, h, c: (b, c, lora_blk)),
        first(0), first(nb), first(2 * nb),
        pl.BlockSpec((bb, 1, 2 * LANES), lambda b, h, c: (b, 0, lora_blk)),
        vec(0), vec(nb), vec(2 * nb),
        pl.BlockSpec((1, 2 * LANES), lambda b, h, c: (0, lora_blk)),
        vec(0), vec(0), vec(0), vec(0), vec(0), vec(0), vec(0),
        pl.BlockSpec((LANES, LANES), lambda b, h, c: (0, h)),
        pl.BlockSpec((LANES, LANES), lambda b, h, c: (0, h)),
        pl.BlockSpec((GATE_RANK, LANES), lambda b, h, c: (0, h)),
        pl.BlockSpec((bb, 2, RWKV_HEAD, RWKV_HEAD), lambda b, h, c: (b, h, 0, 0)),
        tok(hg0), tok(hg0 + nb), tok(hg0 + 2 * nb), tok(hg0 + 3 * nb),
        pl.BlockSpec((DEPTH + 1, LANES), lambda b, h, c: (0, h)),
        vec(0),
        pl.BlockSpec((bb, 1, HGRN_HEAD, HGRN_HEAD), lambda b, h, c: (b, h, 0, 0)),
    ]
    out_specs = [
        pl.BlockSpec((bb, C, LANES), lambda b, h, c: (b, c, h)),
        pl.BlockSpec((bb, 2, RWKV_HEAD, RWKV_HEAD), lambda b, h, c: (b, h, 0, 0)),
        pl.BlockSpec((bb, C, LANES), lambda b, h, c: (b, c, h)),
        pl.BlockSpec((bb, 1, HGRN_HEAD, HGRN_HEAD), lambda b, h, c: (b, h, 0, 0)),
    ]
    out_shape = [jax.ShapeDtypeStruct((B, T, RWKV_WIDTH), F32),
                 jax.ShapeDtypeStruct((B, RWKV_HEADS, RWKV_HEAD, RWKV_HEAD), F32),
                 jax.ShapeDtypeStruct((B, T, HGRN_WIDTH), F32),
                 jax.ShapeDtypeStruct((B, HGRN_HEADS, HGRN_HEAD, HGRN_HEAD), F32)]
    scratch = [pltpu.VMEM((bb, LANES, LANES), F32),
               pltpu.VMEM((bb, 1, LANES), F32), pltpu.VMEM((bb, 1, LANES), F32),
               pltpu.VMEM((bb, 1, LANES), F32), pltpu.VMEM((bb, 1, 2 * LANES), F32),
               pltpu.VMEM((bb, HGRN_HEAD, HGRN_HEAD), F32)]
    return pl.pallas_call(
        functools.partial(_mixer_kernel, bb=bb, C=C, n_groups=n_groups),
        grid=grid, in_specs=in_specs, out_specs=out_specs, out_shape=out_shape,
        scratch_shapes=scratch,
        compiler_params=pltpu.CompilerParams(
            dimension_semantics=("arbitrary", "arbitrary", "arbitrary"),
            vmem_limit_bytes=VMEM_LIMIT),
        name="mixer",
    )(proj3, proj3, proj3, proj3, s_shift3, s_shift3, s_shift3, s_shift3,
      p["shift_mu"], p["shift_mu"], p["shift_mu"], p["shift_mu"],
      p["w0"], p["a0"], p["k_k"], p["k_a"], p["r_k"], p["ln_x_w"], p["ln_x_b"],
      p["w1u_pad"], p["a1u_pad"], p["g1u"], s_rwkv,
      proj3, proj3, proj3, proj3, p["lb_logits"], p["hg_norm_w"], s_hgrn)


def _layer_norm(x, g, b):
    mu = jnp.mean(x, axis=-1, keepdims=True)
    d = x - mu
    var = jnp.mean(d * d, axis=-1, keepdims=True)
    return d * lax.rsqrt(var + LN_EPS) * g + b


def _post_kernel(x_ref, orw_ref, ohg_ref, wo1_ref, wo2_ref, g1_ref, b1_ref, wup_ref, wdn_ref,
                 g2_ref, b2_ref, y_ref):
    mix = (jnp.dot(orw_ref[...].astype(BF16), wo1_ref[...], preferred_element_type=F32)
           + jnp.dot(ohg_ref[...].astype(BF16), wo2_ref[...], preferred_element_type=F32))
    h1 = _layer_norm(ALPHA * x_ref[...] + mix, g1_ref[...], b1_ref[...])
    up = jnp.dot(h1.astype(BF16), wup_ref[...], preferred_element_type=F32)
    up = jnp.square(jnp.maximum(up, 0.0))
    ff = jnp.dot(up.astype(BF16), wdn_ref[...], preferred_element_type=F32)
    y_ref[...] = _layer_norm(ALPHA * h1 + ff, g2_ref[...], b2_ref[...])


def _post(x2, orw2, ohg2, p, tm):
    n = x2.shape[0]

    def const(shape):
        return pl.BlockSpec(shape, lambda i: (0, 0), pipeline_mode=pl.Buffered(1))

    return pl.pallas_call(
        _post_kernel,
        grid=(n // tm,),
        in_specs=[pl.BlockSpec((tm, D_MODEL), lambda i: (i, 0)),
                  pl.BlockSpec((tm, RWKV_WIDTH), lambda i: (i, 0)),
                  pl.BlockSpec((tm, HGRN_WIDTH), lambda i: (i, 0)),
                  const((RWKV_WIDTH, D_MODEL)), const((HGRN_WIDTH, D_MODEL)),
                  const((1, D_MODEL)), const((1, D_MODEL)),
                  const((D_MODEL, D_FF)), const((D_FF, D_MODEL)),
                  const((1, D_MODEL)), const((1, D_MODEL))],
        out_specs=pl.BlockSpec((tm, D_MODEL), lambda i: (i, 0)),
        out_shape=jax.ShapeDtypeStruct((n, D_MODEL), F32),
        compiler_params=pltpu.CompilerParams(dimension_semantics=("arbitrary",),
                                             vmem_limit_bytes=VMEM_LIMIT),
        name="post",
    )(x2, orw2, ohg2, p["wo_rw"], p["wo_hg"], p["ln1_g"], p["ln1_b"], p["w_up"], p["w_down"],
      p["ln2_g"], p["ln2_b"])


def _prep_params(w_in, shift_mu, w0, w1u, a0, a1u, g1u, k_k, k_a, r_k, ln_x_w, ln_x_b, lb_logits,
                 hg_norm_w, w_out, ln1_g, ln1_b, w_up, w_down, ln2_g, ln2_b):
    zw = jnp.zeros((LANES - DECAY_RANK, RWKV_WIDTH), F32)
    za = jnp.zeros((LANES - AICL_RANK, RWKV_WIDTH), F32)
    w_out_bf = w_out[0].astype(BF16)
    return {
        "w_in": w_in[0].astype(BF16),
        "shift_mu": shift_mu[0].reshape(1, RWKV_PROJ),
        "w0": w0[0].reshape(1, RWKV_WIDTH), "a0": a0[0].reshape(1, RWKV_WIDTH),
        "k_k": k_k[0].reshape(1, RWKV_WIDTH), "k_a": k_a[0].reshape(1, RWKV_WIDTH),
        "r_k": r_k[0].reshape(1, RWKV_WIDTH),
        "ln_x_w": ln_x_w[0].reshape(1, RWKV_WIDTH), "ln_x_b": ln_x_b[0].reshape(1, RWKV_WIDTH),
        "w1u_pad": jnp.concatenate([w1u[0], zw], axis=0),
        "a1u_pad": jnp.concatenate([za, a1u[0]], axis=0),
        "g1u": g1u[0],
        "lb_logits": lb_logits.astype(F32),
        "hg_norm_w": hg_norm_w[0].reshape(1, HGRN_WIDTH),
        "wo_rw": w_out_bf[:RWKV_WIDTH], "wo_hg": w_out_bf[RWKV_WIDTH:],
        "ln1_g": ln1_g[0].reshape(1, D_MODEL), "ln1_b": ln1_b[0].reshape(1, D_MODEL),
        "w_up": w_up[0].astype(BF16), "w_down": w_down[0].astype(BF16),
        "ln2_g": ln2_g[0].reshape(1, D_MODEL), "ln2_b": ln2_b[0].reshape(1, D_MODEL),
    }


def _run_group(x, s_rwkv, s_hgrn, s_shift, p, *, tm, bb, chunk, n_groups):
    B, T, _ = x.shape
    x2 = x.reshape(B * T, D_MODEL)
    proj2 = _proj(x2, p["w_in"], tm)
    proj3 = proj2.reshape(B, T, PROJ)
    o_rw, st_rw, o_hg, st_hg = _mixer(proj3, s_shift.reshape(B, 1, RWKV_PROJ), s_rwkv, s_hgrn, p,
                                      bb, chunk, n_groups)
    y2 = _post(x2, o_rw.reshape(B * T, RWKV_WIDTH), o_hg.reshape(B * T, HGRN_WIDTH), p, tm)
    sh = proj3[:, T - 1, :RWKV_PROJ]
    return y2.reshape(B, T, D_MODEL), st_rw[None], st_hg[None], sh[None]


PROMPT_CFG = dict(tm=256, bb=8, chunk=64, n_groups=2)
SAMPLE_CFG = dict(tm=256, bb=16, chunk=8, n_groups=2)


def kernel(x_prompt, x_sample, state_rwkv, state_hgrn, state_shift, w_in, shift_mu, w0, w1u, a0, a1u, g1u, k_k, k_a, r_k, ln_x_w, ln_x_b, lb_logits, hg_norm_w, w_out, ln1_g, ln1_b, w_up, w_down, ln2_g, ln2_b):
    assert w_in.shape[0] == DEPTH
    p = _prep_params(w_in, shift_mu, w0, w1u, a0, a1u, g1u, k_k, k_a, r_k, ln_x_w, ln_x_b, lb_logits,
                     hg_norm_w, w_out, ln1_g, ln1_b, w_up, w_down, ln2_g, ln2_b)
    bp = x_prompt.shape[0]
    z_rw = jnp.zeros((bp, RWKV_HEADS, RWKV_HEAD, RWKV_HEAD), F32)
    z_hg = jnp.zeros((bp, HGRN_HEADS, HGRN_HEAD, HGRN_HEAD), F32)
    z_sh = jnp.zeros((bp, RWKV_PROJ), F32)
    y_p, rw_p, hg_p, sh_p = _run_group(x_prompt, z_rw, z_hg, z_sh, p, **PROMPT_CFG)
    y_s, rw_s, hg_s, sh_s = _run_group(x_sample, state_rwkv[0].astype(F32), state_hgrn[0].astype(F32),
                                       state_shift[0].astype(F32), p, **SAMPLE_CFG)
    return (y_p, y_s, rw_p, rw_s, hg_p, hg_s, sh_p, sh_s)
```

```python
import functools
import math

import jax
import jax.numpy as jnp
from jax import lax
from jax.experimental import pallas as pl
from jax.experimental.pallas import tpu as pltpu

F32 = jnp.float32
BF16 = jnp.bfloat16

D_MODEL = 1024
RWKV_WIDTH = 512
RWKV_HEAD = 64
RWKV_HEADS = 8
HGRN_WIDTH = 512
HGRN_HEAD = 128
HGRN_HEADS = 4
DECAY_RANK = 64
AICL_RANK = 64
GATE_RANK = 128
RWKV_PROJ = 3 * RWKV_WIDTH + DECAY_RANK + AICL_RANK + GATE_RANK
HGRN_PROJ = 4 * HGRN_WIDTH
PROJ = RWKV_PROJ + HGRN_PROJ
D_FF = 4 * D_MODEL
DEPTH = 1
ALPHA = (2.0 * DEPTH) ** 0.25
LN_EPS = 1e-5
GN_EPS = RWKV_HEAD * 1e-5
RMS_EPS = 1e-6
DECAY_SCALE = math.exp(-0.5)

LANES = 128
SUBLANES = 8
PAIRS = RWKV_HEADS // 2
LORA_COL = 3 * RWKV_WIDTH
VMEM_LIMIT = 56 * 1024 * 1024

NN = ((1,), (0,))
NT = ((1,), (1,))
TN = ((0,), (0,))

RWKV_STASH = (("lhs", 4, BF16), ("rhs", 2, BF16), ("bt_hat", 2, BF16), ("atrt", 2, BF16), ("vb", 1, BF16),
              ("bkh", 2, BF16), ("dec", 0, F32), ("bonus", 1, F32), ("gate", 1, F32))
HGRN_STASH = (("qs", 1, BF16), ("ks", 1, BF16), ("iv", 1, BF16), ("q", 1, F32), ("kin", 1, F32),
              ("bc", 1, F32), ("dec", 0, F32))


def _dot(a, b, dims):
    return lax.dot_general(a, b, (dims, ((), ())), preferred_element_type=F32)


def _bf(x):
    return x.astype(BF16)


def _cumsum_rows(tri, x):
    p0 = _bf(x)
    r1 = x - p0.astype(F32)
    p1 = _bf(r1)
    p2 = _bf(r1 - p1.astype(F32))
    return _dot(tri, p0, NN) + _dot(tri, p1, NN) + _dot(tri, p2, NN)


def _sigmoid(x):
    return 0.5 * jnp.tanh(0.5 * x) + 0.5


def _interleave(tasks):
    gens = list(tasks)
    nxt = [next(g, None) for g in gens]
    spent = {"M": 0.0, "V": 0.0}
    turn = 0
    while any(k is not None for k in nxt):
        ready = {kind: [j for j, k in enumerate(nxt) if k is not None and k[0] == kind] for kind in spent}
        want = "M" if (ready["M"] and (spent["M"] <= spent["V"] or not ready["V"])) else "V"
        if want == "M":
            cands = ready["M"]
            t = cands[turn % len(cands)]
            turn += 1
        else:
            t = ready["V"][0]
        spent[want] += nxt[t][1]
        nxt[t] = next(gens[t], None)


def _chain(*gens):
    for g in gens:
        yield from g


class _ValueStash:
    def __init__(self):
        self.vals = {}

    def put(self, name, val):
        self.vals[name] = val

    def get(self, name, j=None):
        return self.vals[name] if j is None else self.vals[name][j]


class _RefStash:
    def __init__(self, refs, slot, lo, n):
        self.refs, self.slot, self.lo, self.n = refs, slot, lo, n

    def put(self, name, val):
        self.refs[name][self.slot, self.lo:self.lo + self.n] = val

    def get(self, name, j=None):
        if j is None:
            return self.refs[name][self.slot, self.lo:self.lo + self.n]
        return self.refs[name][self.slot, self.lo + j]


def _proj_kernel(x_ref, w_ref, o_ref):
    o_ref[...] = jnp.dot(x_ref[...].astype(BF16), w_ref[...], preferred_element_type=F32)


def _proj(x2, w_in_bf, tm):
    n = x2.shape[0]
    return pl.pallas_call(
        _proj_kernel,
        grid=(n // tm,),
        in_specs=[pl.BlockSpec((tm, D_MODEL), lambda i: (i, 0)),
                  pl.BlockSpec((D_MODEL, PROJ), lambda i: (0, 0))],
        out_specs=pl.BlockSpec((tm, PROJ), lambda i: (i, 0)),
        out_shape=jax.ShapeDtypeStruct((n, PROJ), F32),
        compiler_params=pltpu.CompilerParams(dimension_semantics=("arbitrary",),
                                             vmem_limit_bytes=VMEM_LIMIT),
        name="proj",
    )(x2, w_in_bf)


def _head_helpers(C):
    lane = lax.broadcasted_iota(jnp.int32, (C, LANES), 1)
    m0 = lane < RWKV_HEAD

    def head_sum(x):
        s0 = jnp.sum(jnp.where(m0, x, 0.0), axis=-1, keepdims=True)
        s1 = jnp.sum(jnp.where(m0, 0.0, x), axis=-1, keepdims=True)
        return jnp.where(m0, s0, s1)

    def stack_heads(x):
        return jnp.concatenate([jnp.where(m0, x, 0.0), jnp.where(m0, 0.0, x)], axis=-2)

    def merge_heads(x):
        return jnp.where(m0, x[:C], x[C:])

    return head_sum, stack_heads, merge_heads


def _rwkv_produce(stash, lo, n, C, pr_ref, pk_ref, pv_ref, pl_ref, mur_ref, muk_ref, muv_ref, mul_ref,
                  w0_ref, a0_ref, kk_ref, ka_ref, rk_ref, w1_ref, a1_ref, g1_ref,
                  cr_scr, ck_scr, cv_scr, cl_scr):
    ids = range(n)
    grp = slice(lo, lo + n)
    head_sum, stack_heads, _ = _head_helpers(C)
    rowc = lax.broadcasted_iota(jnp.int32, (C, C), 0)
    colc = lax.broadcasted_iota(jnp.int32, (C, C), 1)
    tri = jnp.where(rowc >= colc, 1.0, 0.0).astype(BF16)

    def shifted(p_ref, carry_ref, mu_ref):
        p = p_ref[grp]
        row = lax.broadcasted_iota(jnp.int32, p.shape, 1)
        flat = p.reshape(n * C, p.shape[-1])
        prev = jnp.where(row == 0, carry_ref[grp], pltpu.roll(flat, 1, 0).reshape(p.shape))
        carry_ref[grp] = p[:, C - 1:C, :]
        return p + mu_ref[...] * (prev - p)

    w1 = _bf(w1_ref[...])
    a1 = _bf(a1_ref[...])
    g1 = _bf(g1_ref[...])

    yield ("V", 60.0 * n)
    r = shifted(pr_ref, cr_scr, mur_ref)
    xk = shifted(pk_ref, ck_scr, muk_ref)
    v = shifted(pv_ref, cv_scr, muv_ref)
    xl = shifted(pl_ref, cl_scr, mul_ref).reshape(n * C, 2 * LANES)
    xl_lo = xl[:, :LANES]
    dw = jnp.dot(_bf(jnp.tanh(xl_lo)), w1, preferred_element_type=F32)
    da = jnp.dot(_bf(xl_lo), a1, preferred_element_type=F32)
    gate = jnp.dot(_bf(_sigmoid(xl[:, LANES:])), g1, preferred_element_type=F32)
    stash.put("gate", gate.reshape(n, C, LANES))

    yield ("V", 60.0 * n)
    lw = (-DECAY_SCALE * _sigmoid(w0_ref[...] + dw)).reshape(n, C, LANES)
    a_lr = _sigmoid(a0_ref[...] + da).reshape(n, C, LANES)
    kk = xk * kk_ref[...]
    kk = kk * jnp.minimum(lax.rsqrt(head_sum(kk * kk)), 1e12)
    k2 = xk * (1.0 + (a_lr - 1.0) * ka_ref[...])
    b = kk * a_lr
    cum = jnp.stack([_cumsum_rows(tri, lw[j]) for j in ids])

    yield ("V", 80.0 * n)
    cum_last = cum[:, C - 1:C, :]
    e_in = jnp.exp(-cum)
    e_out = jnp.exp(cum_last - cum)
    at = -kk * jnp.exp(cum - lw)
    rt = r * jnp.exp(cum)
    bt = b * e_in
    kt = k2 * e_in
    stash.put("lhs", _bf(jnp.concatenate([stack_heads(at), stack_heads(rt)], axis=1)))
    stash.put("rhs", _bf(jnp.concatenate([bt, kt], axis=1)))
    stash.put("bt_hat", _bf(stack_heads(bt)))
    stash.put("atrt", _bf(jnp.concatenate([at, rt], axis=1)))
    stash.put("vb", _bf(v))
    stash.put("bkh", _bf(jnp.concatenate([b * e_out, k2 * e_out], axis=1)))
    stash.put("dec", jnp.exp(cum_last))
    stash.put("bonus", head_sum(r * k2 * rk_ref[...]) * v)


def _rwkv_consume(stash, lo, n, C, lnw_ref, lnb_ref, o_ref, s_scr):
    H = RWKV_HEAD
    ids = range(n)
    grp = slice(lo, lo + n)
    head_sum, stack_heads, merge_heads = _head_helpers(C)
    r2 = lax.broadcasted_iota(jnp.int32, (2 * C, 2 * C), 0)
    c2 = lax.broadcasted_iota(jnp.int32, (2 * C, 2 * C), 1)
    t2 = jnp.bitwise_and(r2, C - 1)
    s2 = jnp.bitwise_and(c2, C - 1)
    strict2 = t2 > s2
    incl2 = t2 >= s2
    strict_bd = jnp.logical_and(strict2, (r2 >= C) == (c2 >= C))
    eye2 = jnp.where(r2 == c2, 1.0, 0.0).astype(F32)
    bl_r = lax.broadcasted_iota(jnp.int32, (LANES, LANES), 0) < H
    bl_c = lax.broadcasted_iota(jnp.int32, (LANES, LANES), 1) < H
    blockdiag = bl_r == bl_c
    zeros_c = jnp.zeros((C, LANES), BF16)
    dot_cost = 20.0 * n

    yield ("M", dot_cost)
    lhs = [stash.get("lhs", j) for j in ids]
    g = [_dot(lhs[j], stash.get("rhs", j), NT) for j in ids]
    g_a = [_bf(jnp.where(strict2, g[j][:2 * C], 0.0)) for j in ids]
    g_r = [_bf(jnp.where(incl2, g[j][2 * C:], 0.0)) for j in ids]
    yield ("M", dot_cost)
    pw = [jnp.where(strict_bd, _dot(lhs[j][:2 * C], stash.get("bt_hat", j), NT), 0.0) for j in ids]
    inv = [eye2 + pw[j] for j in ids]
    span = 2
    while span < C:
        yield ("M", dot_cost)
        pw = [_bf(pw[j]) for j in ids]
        pw = [_dot(pw[j], pw[j], NN) for j in ids]
        yield ("M", dot_cost)
        inv = [inv[j] + _dot(_bf(inv[j]), _bf(pw[j]), NN) for j in ids]
        span *= 2
    yield ("M", dot_cost)
    S = [s_scr[lo + j] for j in ids]
    pq = [_dot(stash.get("atrt", j), _bf(S[j]), NT) for j in ids]
    yield ("M", dot_cost)
    vb = [stash.get("vb", j) for j in ids]
    w = [pq[j][:C] + merge_heads(_dot(g_a[j], jnp.concatenate([zeros_c, vb[j]], axis=0), NN)) for j in ids]
    yield ("M", dot_cost)
    u2 = [_dot(_bf(inv[j]), _bf(stack_heads(w[j])), NN) for j in ids]
    z = [jnp.concatenate([_bf(u2[j][:C] + u2[j][C:]), vb[j]], axis=0) for j in ids]
    yield ("M", dot_cost)
    y = [pq[j][C:] + merge_heads(_dot(g_r[j], z[j], NN)) for j in ids]
    yield ("M", dot_cost)
    for j in ids:
        upd = _dot(z[j], stash.get("bkh", j), TN)
        s_scr[lo + j] = S[j] * stash.get("dec", j) + jnp.where(blockdiag, upd, 0.0)

    yield ("V", 40.0 * n)
    y = jnp.stack(y)
    mu = head_sum(y) * (1.0 / H)
    dy = y - mu
    var = head_sum(dy * dy) * (1.0 / H)
    yn = dy * lax.rsqrt(var + GN_EPS) * lnw_ref[...] + lnb_ref[...]
    o_ref[grp] = (yn + stash.get("bonus")) * stash.get("gate")


def _mid_rows(x, h):
    R = x.shape[0]
    row = lax.broadcasted_iota(jnp.int32, x.shape, 0)
    if 2 * h >= SUBLANES:
        return jnp.concatenate(
            [jnp.broadcast_to(x[b * 2 * h + h - 1:b * 2 * h + h], (2 * h, LANES)) for b in range(R // (2 * h))],
            axis=0)
    if h == 1:
        return jnp.where(jnp.bitwise_and(row, 1) == 1, pltpu.roll(x, 1, 0), x)
    picks = []
    for half in range(SUBLANES // (2 * h)):
        r0 = half * 2 * h + h - 1
        picks.append(jnp.concatenate(
            [jnp.broadcast_to(x[g * SUBLANES + r0:g * SUBLANES + r0 + 1], (SUBLANES, LANES))
             for g in range(R // SUBLANES)], axis=0))
    out = picks[-1]
    sub = jnp.bitwise_and(row, SUBLANES - 1)
    for half in range(len(picks) - 2, -1, -1):
        out = jnp.where(sub < (half + 1) * 2 * h, picks[half], out)
    return out


def _hgrn_produce(stash, lo, n, C, q_ref, f_ref, i_ref, lbl_ref):
    ids = range(n)
    grp = slice(lo, lo + n)
    logits = lbl_ref[...]
    ex = jnp.exp(logits - jnp.max(logits, axis=0, keepdims=True))
    lb = ex[0:1] / jnp.sum(ex, axis=0, keepdims=True)
    t_i = lax.broadcasted_iota(jnp.int32, (C, C), 0)
    s_i = lax.broadcasted_iota(jnp.int32, (C, C), 1)
    tri = jnp.where(t_i >= s_i, 1.0, 0.0).astype(BF16)

    yield ("V", 40.0 * n)
    q_h = q_ref[grp]
    f_h = f_ref[grp]
    q = q_h * _sigmoid(q_h)
    fg = lb + (1.0 - lb) * _sigmoid(f_h)
    kin = 1.0 - fg
    lf = jnp.log(fg)
    bc = jnp.stack([_cumsum_rows(tri, lf[j]) for j in ids])

    yield ("V", 30.0 * n)
    b_last = bc[:, C - 1:C, :]
    stash.put("qs", _bf(q * jnp.exp(bc)))
    stash.put("ks", _bf(kin * jnp.exp(b_last - bc)))
    stash.put("iv", _bf(i_ref[grp]))
    stash.put("q", q)
    stash.put("kin", kin)
    stash.put("bc", bc)
    stash.put("dec", jnp.exp(b_last))


def _hgrn_consume(stash, lo, n, C, g_ref, hgw_ref, o_ref, s_scr):
    ids = range(n)
    grp = slice(lo, lo + n)
    t_i = lax.broadcasted_iota(jnp.int32, (C, C), 0)
    s_i = lax.broadcasted_iota(jnp.int32, (C, C), 1)
    halves = []
    h = C // 2
    while h >= 1:
        halves.append(h)
        h //= 2

    def level_mask(h):
        same = jnp.bitwise_and(t_i, -2 * h) == jnp.bitwise_and(s_i, -2 * h)
        return jnp.logical_and(same, jnp.logical_and(jnp.bitwise_and(t_i, h) != 0, jnp.bitwise_and(s_i, h) == 0))

    def seq(x, j):
        return _bf(x[j * C:(j + 1) * C])

    dot_cost = 14.0 * n

    yield ("M", dot_cost)
    S = [s_scr[lo + j] for j in ids]
    o_state = [_dot(stash.get("qs", j), _bf(S[j]), NT) for j in ids]
    yield ("M", dot_cost)
    iv = [stash.get("iv", j) for j in ids]
    for j in ids:
        s_scr[lo + j] = S[j] * stash.get("dec", j) + _dot(iv[j], stash.get("ks", j), TN)
    yield ("M", dot_cost)
    q = stash.get("q").reshape(n * C, LANES)
    kin = stash.get("kin").reshape(n * C, LANES)
    bc = stash.get("bc").reshape(n * C, LANES)
    att = [jnp.where(t_i == s_i, _dot(seq(q, j), seq(kin, j), NT), 0.0) for j in ids]
    for h in halves:
        yield ("V", 25.0 * n)
        dmid = jnp.exp(-jnp.abs(bc - _mid_rows(bc, h)))
        ql = q * dmid
        kl = kin * dmid
        yield ("M", dot_cost)
        msk = level_mask(h)
        att = [jnp.where(msk, _dot(seq(ql, j), seq(kl, j), NT), att[j]) for j in ids]
    yield ("M", dot_cost)
    o_intra = [_dot(_bf(att[j]), iv[j], NN) for j in ids]

    yield ("V", 25.0 * n)
    o = jnp.stack([o_intra[j] + o_state[j] for j in ids])
    ms = jnp.mean(o * o, axis=-1, keepdims=True)
    g_h = g_ref[grp]
    o_ref[grp] = o * lax.rsqrt(ms + RMS_EPS) * hgw_ref[...] * (g_h * _sigmoid(g_h))


N_MIXER_IN = 30
N_MIXER_OUT = 4
N_MIXER_STATE_SCRATCH = 6


def _mixer_kernel(*refs, bb, C, n_groups, pipelined):
    (pr_ref, pk_ref, pv_ref, pl_ref, shr_ref, shk_ref, shv_ref, shl_ref,
     mur_ref, muk_ref, muv_ref, mul_ref, w0_ref, a0_ref, kk_ref, ka_ref, rk_ref,
     lnw_ref, lnb_ref, w1_ref, a1_ref, g1_ref, srw0_ref,
     q_ref, f_ref, i_ref, g_ref, lbl_ref, hgw_ref, shg0_ref) = refs[:N_MIXER_IN]
    orw_ref, strw_ref, ohg_ref, sthg_ref = refs[N_MIXER_IN:N_MIXER_IN + N_MIXER_OUT]
    scr = refs[N_MIXER_IN + N_MIXER_OUT:]
    srw_scr, cr_scr, ck_scr, cv_scr, cl_scr, shg_scr = scr[:N_MIXER_STATE_SCRATCH]
    stash_refs = scr[N_MIXER_STATE_SCRATCH:]
    if pipelined:
        rw_refs = {name: stash_refs[k] for k, (name, _, _) in enumerate(RWKV_STASH)}
        hg_refs = {name: stash_refs[len(RWKV_STASH) + k] for k, (name, _, _) in enumerate(HGRN_STASH)}

    c = pl.program_id(2)
    last = pl.num_programs(2) - 1
    H = RWKV_HEAD

    @pl.when(c == 0)
    def _init():
        z = jnp.zeros((H, H), F32)
        for i in range(bb):
            top = jnp.concatenate([srw0_ref[i, 0], z], axis=1)
            bot = jnp.concatenate([z, srw0_ref[i, 1]], axis=1)
            srw_scr[i] = jnp.concatenate([top, bot], axis=0)
            shg_scr[i] = shg0_ref[i, 0].T
        cr_scr[...] = shr_ref[...]
        ck_scr[...] = shk_ref[...]
        cv_scr[...] = shv_ref[...]
        cl_scr[...] = shl_ref[...]
        if pipelined:
            for table in (rw_refs, hg_refs):
                for name, ref in table.items():
                    fill = jnp.ones if name == "dec" else jnp.zeros
                    ref[1] = fill(ref.shape[1:], ref.dtype)

    per = bb // n_groups
    groups = [(gidx * per, per) for gidx in range(n_groups)]

    def rwkv_produce(stash, lo, n):
        return _rwkv_produce(stash, lo, n, C, pr_ref, pk_ref, pv_ref, pl_ref, mur_ref, muk_ref, muv_ref, mul_ref,
                             w0_ref, a0_ref, kk_ref, ka_ref, rk_ref, w1_ref, a1_ref, g1_ref,
                             cr_scr, ck_scr, cv_scr, cl_scr)

    def rwkv_consume(stash, lo, n):
        return _rwkv_consume(stash, lo, n, C, lnw_ref, lnb_ref, orw_ref, srw_scr)

    def hgrn_produce(stash, lo, n):
        return _hgrn_produce(stash, lo, n, C, q_ref, f_ref, i_ref, lbl_ref)

    def hgrn_consume(stash, lo, n):
        return _hgrn_consume(stash, lo, n, C, g_ref, hgw_ref, ohg_ref, shg_scr)

    if not pipelined:
        tasks = []
        for lo, n in groups:
            rw, hg = _ValueStash(), _ValueStash()
            tasks.append(_chain(rwkv_produce(rw, lo, n), rwkv_consume(rw, lo, n)))
            tasks.append(_chain(hgrn_produce(hg, lo, n), hgrn_consume(hg, lo, n)))
        _interleave(tasks)
    else:
        def step(write_slot):
            read_slot = 1 - write_slot
            tasks = []
            for lo, n in groups:
                tasks.append(rwkv_consume(_RefStash(rw_refs, read_slot, lo, n), lo, n))
                tasks.append(hgrn_consume(_RefStash(hg_refs, read_slot, lo, n), lo, n))
            for lo, n in groups:
                tasks.append(rwkv_produce(_RefStash(rw_refs, write_slot, lo, n), lo, n))
                tasks.append(hgrn_produce(_RefStash(hg_refs, write_slot, lo, n), lo, n))
            _interleave(tasks)

        parity = lax.rem(c, 2)
        pl.when(parity == 0)(functools.partial(step, 0))
        pl.when(parity == 1)(functools.partial(step, 1))

    @pl.when(c == last)
    def _fin():
        for i in range(bb):
            s_fin = srw_scr[i]
            strw_ref[i, 0] = s_fin[:H, :H]
            strw_ref[i, 1] = s_fin[H:, H:]
            sthg_ref[i, 0] = shg_scr[i].T


def _mixer(proj3, s_shift3, s_rwkv, s_hgrn, p, bb, C, n_groups, pipelined):
    B, T, _ = proj3.shape
    nc = T // C
    grid = (B // bb, PAIRS, nc + 1 if pipelined else nc)
    nb = RWKV_WIDTH // LANES
    hg0 = RWKV_PROJ // LANES
    lora_blk = LORA_COL // (2 * LANES)

    if pipelined:
        def produced(c):
            return jnp.minimum(c, nc - 1)

        def consumed(c):
            return jnp.maximum(c - 1, 0)
    else:
        produced = consumed = lambda c: c

    def tok(col0, chunk=produced):
        return pl.BlockSpec((bb, C, LANES), lambda b, h, c: (b, chunk(c), col0 + h))

    def first(col0):
        return pl.BlockSpec((bb, 1, LANES), lambda b, h, c: (b, 0, col0 + h))

    def vec(col0):
        return pl.BlockSpec((1, LANES), lambda b, h, c: (0, col0 + h))

    in_specs = [
        tok(0), tok(nb), tok(2 * nb),
        pl.BlockSpec((bb, C, 2 * LANES), lambda b, h, c: (b, produced(c), lora_blk)),
        first(0), first(nb), first(2 * nb),
        pl.BlockSpec((bb, 1, 2 * LANES), lambda b, h, c: (b, 0, lora_blk)),
        vec(0), vec(nb), vec(2 * nb),
        pl.BlockSpec((1, 2 * LANES), lambda b, h, c: (0, lora_blk)),
        vec(0), vec(0), vec(0), vec(0), vec(0), vec(0), vec(0),
        pl.BlockSpec((LANES, LANES), lambda b, h, c: (0, h)),
        pl.BlockSpec((LANES, LANES), lambda b, h, c: (0, h)),
        pl.BlockSpec((GATE_RANK, LANES), lambda b, h, c: (0, h)),
        pl.BlockSpec((bb, 2, RWKV_HEAD, RWKV_HEAD), lambda b, h, c: (b, h, 0, 0)),
        tok(hg0), tok(hg0 + nb), tok(hg0 + 2 * nb), tok(hg0 + 3 * nb, consumed),
        pl.BlockSpec((DEPTH + 1, LANES), lambda b, h, c: (0, h)),
        vec(0),
        pl.BlockSpec((bb, 1, HGRN_HEAD, HGRN_HEAD), lambda b, h, c: (b, h, 0, 0)),
    ]
    assert len(in_specs) == N_MIXER_IN
    out_specs = [
        pl.BlockSpec((bb, C, LANES), lambda b, h, c: (b, consumed(c), h)),
        pl.BlockSpec((bb, 2, RWKV_HEAD, RWKV_HEAD), lambda b, h, c: (b, h, 0, 0)),
        pl.BlockSpec((bb, C, LANES), lambda b, h, c: (b, consumed(c), h)),
        pl.BlockSpec((bb, 1, HGRN_HEAD, HGRN_HEAD), lambda b, h, c: (b, h, 0, 0)),
    ]
    out_shape = [jax.ShapeDtypeStruct((B, T, RWKV_WIDTH), F32),
                 jax.ShapeDtypeStruct((B, RWKV_HEADS, RWKV_HEAD, RWKV_HEAD), F32),
                 jax.ShapeDtypeStruct((B, T, HGRN_WIDTH), F32),
                 jax.ShapeDtypeStruct((B, HGRN_HEADS, HGRN_HEAD, HGRN_HEAD), F32)]
    scratch = [pltpu.VMEM((bb, LANES, LANES), F32),
               pltpu.VMEM((bb, 1, LANES), F32), pltpu.VMEM((bb, 1, LANES), F32),
               pltpu.VMEM((bb, 1, LANES), F32), pltpu.VMEM((bb, 1, 2 * LANES), F32),
               pltpu.VMEM((bb, HGRN_HEAD, HGRN_HEAD), F32)]
    assert len(scratch) == N_MIXER_STATE_SCRATCH
    if pipelined:
        scratch += [pltpu.VMEM((2, bb, rows * C if rows else 1, LANES), dt)
                    for _, rows, dt in RWKV_STASH + HGRN_STASH]
    return pl.pallas_call(
        functools.partial(_mixer_kernel, bb=bb, C=C, n_groups=n_groups, pipelined=pipelined),
        grid=grid, in_specs=in_specs, out_specs=out_specs, out_shape=out_shape,
        scratch_shapes=scratch,
        compiler_params=pltpu.CompilerParams(
            dimension_semantics=("arbitrary", "arbitrary", "arbitrary"),
            vmem_limit_bytes=VMEM_LIMIT),
        name="mixer",
    )(proj3, proj3, proj3, proj3, s_shift3, s_shift3, s_shift3, s_shift3,
      p["shift_mu"], p["shift_mu"], p["shift_mu"], p["shift_mu"],
      p["w0"], p["a0"], p["k_k"], p["k_a"], p["r_k"], p["ln_x_w"], p["ln_x_b"],
      p["w1u_pad"], p["a1u_pad"], p["g1u"], s_rwkv,
      proj3, proj3, proj3, proj3, p["lb_logits"], p["hg_norm_w"], s_hgrn)


def _layer_norm(x, g, b):
    mu = jnp.mean(x, axis=-1, keepdims=True)
    d = x - mu
    var = jnp.mean(d * d, axis=-1, keepdims=True)
    return d * lax.rsqrt(var + LN_EPS) * g + b


def _post_kernel(x_ref, orw_ref, ohg_ref, wo1_ref, wo2_ref, g1_ref, b1_ref, wup_ref, wdn_ref,
                 g2_ref, b2_ref, y_ref):
    mix = (jnp.dot(orw_ref[...].astype(BF16), wo1_ref[...], preferred_element_type=F32)
           + jnp.dot(ohg_ref[...].astype(BF16), wo2_ref[...], preferred_element_type=F32))
    h1 = _layer_norm(ALPHA * x_ref[...] + mix, g1_ref[...], b1_ref[...])
    up = jnp.dot(h1.astype(BF16), wup_ref[...], preferred_element_type=F32)
    up = jnp.square(jnp.maximum(up, 0.0))
    ff = jnp.dot(up.astype(BF16), wdn_ref[...], preferred_element_type=F32)
    y_ref[...] = _layer_norm(ALPHA * h1 + ff, g2_ref[...], b2_ref[...])


def _post(x2, orw2, ohg2, p, tm):
    n = x2.shape[0]

    def const(shape):
        return pl.BlockSpec(shape, lambda i: (0, 0), pipeline_mode=pl.Buffered(1))

    return pl.pallas_call(
        _post_kernel,
        grid=(n // tm,),
        in_specs=[pl.BlockSpec((tm, D_MODEL), lambda i: (i, 0)),
                  pl.BlockSpec((tm, RWKV_WIDTH), lambda i: (i, 0)),
                  pl.BlockSpec((tm, HGRN_WIDTH), lambda i: (i, 0)),
                  const((RWKV_WIDTH, D_MODEL)), const((HGRN_WIDTH, D_MODEL)),
                  const((1, D_MODEL)), const((1, D_MODEL)),
                  const((D_MODEL, D_FF)), const((D_FF, D_MODEL)),
                  const((1, D_MODEL)), const((1, D_MODEL))],
        out_specs=pl.BlockSpec((tm, D_MODEL), lambda i: (i, 0)),
        out_shape=jax.ShapeDtypeStruct((n, D_MODEL), F32),
        compiler_params=pltpu.CompilerParams(dimension_semantics=("arbitrary",),
                                             vmem_limit_bytes=VMEM_LIMIT),
        name="post",
    )(x2, orw2, ohg2, p["wo_rw"], p["wo_hg"], p["ln1_g"], p["ln1_b"], p["w_up"], p["w_down"],
      p["ln2_g"], p["ln2_b"])


def _prep_params(w_in, shift_mu, w0, w1u, a0, a1u, g1u, k_k, k_a, r_k, ln_x_w, ln_x_b, lb_logits,
                 hg_norm_w, w_out, ln1_g, ln1_b, w_up, w_down, ln2_g, ln2_b):
    zw = jnp.zeros((LANES - DECAY_RANK, RWKV_WIDTH), F32)
    za = jnp.zeros((LANES - AICL_RANK, RWKV_WIDTH), F32)
    w_out_bf = w_out[0].astype(BF16)
    return {
        "w_in": w_in[0].astype(BF16),
        "shift_mu": shift_mu[0].reshape(1, RWKV_PROJ),
        "w0": w0[0].reshape(1, RWKV_WIDTH), "a0": a0[0].reshape(1, RWKV_WIDTH),
        "k_k": k_k[0].reshape(1, RWKV_WIDTH), "k_a": k_a[0].reshape(1, RWKV_WIDTH),
        "r_k": r_k[0].reshape(1, RWKV_WIDTH),
        "ln_x_w": ln_x_w[0].reshape(1, RWKV_WIDTH), "ln_x_b": ln_x_b[0].reshape(1, RWKV_WIDTH),
        "w1u_pad": jnp.concatenate([w1u[0], zw], axis=0),
        "a1u_pad": jnp.concatenate([za, a1u[0]], axis=0),
        "g1u": g1u[0],
        "lb_logits": lb_logits.astype(F32),
        "hg_norm_w": hg_norm_w[0].reshape(1, HGRN_WIDTH),
        "wo_rw": w_out_bf[:RWKV_WIDTH], "wo_hg": w_out_bf[RWKV_WIDTH:],
        "ln1_g": ln1_g[0].reshape(1, D_MODEL), "ln1_b": ln1_b[0].reshape(1, D_MODEL),
        "w_up": w_up[0].astype(BF16), "w_down": w_down[0].astype(BF16),
        "ln2_g": ln2_g[0].reshape(1, D_MODEL), "ln2_b": ln2_b[0].reshape(1, D_MODEL),
    }


def _run_group(x, s_rwkv, s_hgrn, s_shift, p, *, tm, bb, chunk, n_groups, pipelined):
    B, T, _ = x.shape
    x2 = x.reshape(B * T, D_MODEL)
    proj2 = _proj(x2, p["w_in"], tm)
    proj3 = proj2.reshape(B, T, PROJ)
    o_rw, st_rw, o_hg, st_hg = _mixer(proj3, s_shift.reshape(B, 1, RWKV_PROJ), s_rwkv, s_hgrn, p,
                                      bb, chunk, n_groups, pipelined)
    y2 = _post(x2, o_rw.reshape(B * T, RWKV_WIDTH), o_hg.reshape(B * T, HGRN_WIDTH), p, tm)
    sh = proj3[:, T - 1, :RWKV_PROJ]
    return y2.reshape(B, T, D_MODEL), st_rw[None], st_hg[None], sh[None]


PROMPT_CFG = dict(tm=256, bb=8, chunk=64, n_groups=2, pipelined=True)
SAMPLE_CFG = dict(tm=256, bb=16, chunk=8, n_groups=2, pipelined=False)


def kernel(x_prompt, x_sample, state_rwkv, state_hgrn, state_shift, w_in, shift_mu, w0, w1u, a0, a1u, g1u, k_k, k_a, r_k, ln_x_w, ln_x_b, lb_logits, hg_norm_w, w_out, ln1_g, ln1_b, w_up, w_down, ln2_g, ln2_b):
    assert w_in.shape[0] == DEPTH
    p = _prep_params(w_in, shift_mu, w0, w1u, a0, a1u, g1u, k_k, k_a, r_k, ln_x_w, ln_x_b, lb_logits,
                     hg_norm_w, w_out, ln1_g, ln1_b, w_up, w_down, ln2_g, ln2_b)
    bp = x_prompt.shape[0]
    z_rw = jnp.zeros((bp, RWKV_HEADS, RWKV_HEAD, RWKV_HEAD), F32)
    z_hg = jnp.zeros((bp, HGRN_HEADS, HGRN_HEAD, HGRN_HEAD), F32)
    z_sh = jnp.zeros((bp, RWKV_PROJ), F32)
    y_p, rw_p, hg_p, sh_p = _run_group(x_prompt, z_rw, z_hg, z_sh, p, **PROMPT_CFG)
    y_s, rw_s, hg_s, sh_s = _run_group(x_sample, state_rwkv[0].astype(F32), state_hgrn[0].astype(F32),
                                       state_shift[0].astype(F32), p, **SAMPLE_CFG)
    return (y_p, y_s, rw_p, rw_s, hg_p, hg_s, sh_p, sh_s)
```

```python
import functools
import math

import jax
import jax.numpy as jnp
from jax import lax
from jax.experimental import pallas as pl
from jax.experimental.pallas import tpu as pltpu

F32 = jnp.float32
BF16 = jnp.bfloat16

D_MODEL = 1024
RWKV_WIDTH = 512
RWKV_HEAD = 64
RWKV_HEADS = 8
HGRN_WIDTH = 512
HGRN_HEAD = 128
HGRN_HEADS = 4
DECAY_RANK = 64
AICL_RANK = 64
GATE_RANK = 128
RWKV_PROJ = 3 * RWKV_WIDTH + DECAY_RANK + AICL_RANK + GATE_RANK
HGRN_PROJ = 4 * HGRN_WIDTH
PROJ = RWKV_PROJ + HGRN_PROJ
D_FF = 4 * D_MODEL
DEPTH = 1
ALPHA = (2.0 * DEPTH) ** 0.25
LN_EPS = 1e-5
GN_EPS = RWKV_HEAD * 1e-5
RMS_EPS = 1e-6
DECAY_SCALE = math.exp(-0.5)

LANES = 128
SUBLANES = 8
PAIRS = RWKV_HEADS // 2
LORA_COL = 3 * RWKV_WIDTH
VMEM_LIMIT = 56 * 1024 * 1024

NN = ((1,), (0,))
NT = ((1,), (1,))
TN = ((0,), (0,))

RWKV_STASH = (("lhs", 4, BF16), ("rhs", 2, BF16), ("bt_hat", 2, BF16), ("atrt", 2, BF16), ("vb", 1, BF16),
              ("bkh", 2, BF16), ("dec", 0, F32), ("bonus", 1, F32), ("gate", 1, F32))
HGRN_STASH = (("qs", 1, BF16), ("ks", 1, BF16), ("iv", 1, BF16), ("q", 1, F32), ("kin", 1, F32),
              ("bc", 1, F32), ("dec", 0, F32))


def _dot(a, b, dims):
    return lax.dot_general(a, b, (dims, ((), ())), preferred_element_type=F32)


def _bf(x):
    return x.astype(BF16)


def _cumsum_rows(tri, x):
    p0 = _bf(x)
    r1 = x - p0.astype(F32)
    p1 = _bf(r1)
    p2 = _bf(r1 - p1.astype(F32))
    return _dot(tri, p0, NN) + _dot(tri, p1, NN) + _dot(tri, p2, NN)


def _sigmoid(x):
    return 0.5 * jnp.tanh(0.5 * x) + 0.5


def _interleave(tasks):
    gens = list(tasks)
    nxt = [next(g, None) for g in gens]
    spent = {"M": 0.0, "V": 0.0}
    turn = 0
    while any(k is not None for k in nxt):
        ready = {kind: [j for j, k in enumerate(nxt) if k is not None and k[0] == kind] for kind in spent}
        want = "M" if (ready["M"] and (spent["M"] <= spent["V"] or not ready["V"])) else "V"
        if want == "M":
            cands = ready["M"]
            t = cands[turn % len(cands)]
            turn += 1
        else:
            t = ready["V"][0]
        spent[want] += nxt[t][1]
        nxt[t] = next(gens[t], None)


def _chain(*gens):
    for g in gens:
        yield from g


class _ValueStash:
    def __init__(self):
        self.vals = {}

    def put(self, name, val):
        self.vals[name] = val

    def get(self, name, j=None):
        return self.vals[name] if j is None else self.vals[name][j]


class _RefStash:
    def __init__(self, refs, slot, lo, n):
        self.refs, self.slot, self.lo, self.n = refs, slot, lo, n

    def put(self, name, val):
        self.refs[name][self.slot, self.lo:self.lo + self.n] = val

    def get(self, name, j=None):
        if j is None:
            return self.refs[name][self.slot, self.lo:self.lo + self.n]
        return self.refs[name][self.slot, self.lo + j]


def _proj_kernel(x_ref, w_ref, o_ref):
    o_ref[...] = jnp.dot(x_ref[...].astype(BF16), w_ref[...], preferred_element_type=F32)


def _proj(x2, w_in_bf, tm):
    n = x2.shape[0]
    return pl.pallas_call(
        _proj_kernel,
        grid=(n // tm,),
        in_specs=[pl.BlockSpec((tm, D_MODEL), lambda i: (i, 0)),
                  pl.BlockSpec((D_MODEL, PROJ), lambda i: (0, 0))],
        out_specs=pl.BlockSpec((tm, PROJ), lambda i: (i, 0)),
        out_shape=jax.ShapeDtypeStruct((n, PROJ), F32),
        compiler_params=pltpu.CompilerParams(dimension_semantics=("arbitrary",),
                                             vmem_limit_bytes=VMEM_LIMIT),
        name="proj",
    )(x2, w_in_bf)


def _head_helpers(C):
    lane = lax.broadcasted_iota(jnp.int32, (C, LANES), 1)
    m0 = lane < RWKV_HEAD

    def head_sum(x):
        s0 = jnp.sum(jnp.where(m0, x, 0.0), axis=-1, keepdims=True)
        s1 = jnp.sum(jnp.where(m0, 0.0, x), axis=-1, keepdims=True)
        return jnp.where(m0, s0, s1)

    def stack_heads(x):
        return jnp.concatenate([jnp.where(m0, x, 0.0), jnp.where(m0, 0.0, x)], axis=-2)

    def merge_heads(x):
        return jnp.where(m0, x[:C], x[C:])

    return head_sum, stack_heads, merge_heads


def _rwkv_produce(stash, lo, n, C, pr_ref, pk_ref, pv_ref, pl_ref, mur_ref, muk_ref, muv_ref, mul_ref,
                  w0_ref, a0_ref, kk_ref, ka_ref, rk_ref, w1_ref, a1_ref, g1_ref,
                  cr_scr, ck_scr, cv_scr, cl_scr):
    ids = range(n)
    grp = slice(lo, lo + n)
    head_sum, stack_heads, _ = _head_helpers(C)
    rowc = lax.broadcasted_iota(jnp.int32, (C, C), 0)
    colc = lax.broadcasted_iota(jnp.int32, (C, C), 1)
    tri = jnp.where(rowc >= colc, 1.0, 0.0).astype(BF16)

    def shifted(p_ref, carry_ref, mu_ref):
        p = p_ref[grp]
        row = lax.broadcasted_iota(jnp.int32, p.shape, 1)
        flat = p.reshape(n * C, p.shape[-1])
        prev = jnp.where(row == 0, carry_ref[grp], pltpu.roll(flat, 1, 0).reshape(p.shape))
        carry_ref[grp] = p[:, C - 1:C, :]
        return p + mu_ref[...] * (prev - p)

    w1 = _bf(w1_ref[...])
    a1 = _bf(a1_ref[...])
    g1 = _bf(g1_ref[...])

    yield ("V", 60.0 * n)
    r = shifted(pr_ref, cr_scr, mur_ref)
    xk = shifted(pk_ref, ck_scr, muk_ref)
    v = shifted(pv_ref, cv_scr, muv_ref)
    xl = shifted(pl_ref, cl_scr, mul_ref).reshape(n * C, 2 * LANES)
    xl_lo = xl[:, :LANES]
    dw = jnp.dot(_bf(jnp.tanh(xl_lo)), w1, preferred_element_type=F32)
    da = jnp.dot(_bf(xl_lo), a1, preferred_element_type=F32)
    gate = jnp.dot(_bf(_sigmoid(xl[:, LANES:])), g1, preferred_element_type=F32)
    stash.put("gate", gate.reshape(n, C, LANES))

    yield ("V", 60.0 * n)
    lw = (-DECAY_SCALE * _sigmoid(w0_ref[...] + dw)).reshape(n, C, LANES)
    a_lr = _sigmoid(a0_ref[...] + da).reshape(n, C, LANES)
    kk = xk * kk_ref[...]
    kk = kk * jnp.minimum(lax.rsqrt(head_sum(kk * kk)), 1e12)
    k2 = xk * (1.0 + (a_lr - 1.0) * ka_ref[...])
    b = kk * a_lr
    cum = jnp.stack([_cumsum_rows(tri, lw[j]) for j in ids])

    yield ("V", 80.0 * n)
    cum_last = cum[:, C - 1:C, :]
    e_in = jnp.exp(-cum)
    e_out = jnp.exp(cum_last - cum)
    at = -kk * jnp.exp(cum - lw)
    rt = r * jnp.exp(cum)
    bt = b * e_in
    kt = k2 * e_in
    stash.put("lhs", _bf(jnp.concatenate([stack_heads(at), stack_heads(rt)], axis=1)))
    stash.put("rhs", _bf(jnp.concatenate([bt, kt], axis=1)))
    stash.put("bt_hat", _bf(stack_heads(bt)))
    stash.put("atrt", _bf(jnp.concatenate([at, rt], axis=1)))
    stash.put("vb", _bf(v))
    stash.put("bkh", _bf(jnp.concatenate([b * e_out, k2 * e_out], axis=1)))
    stash.put("dec", jnp.exp(cum_last))
    stash.put("bonus", head_sum(r * k2 * rk_ref[...]) * v)


def _rwkv_consume(stash, lo, n, C, lnw_ref, lnb_ref, o_ref, s_scr):
    H = RWKV_HEAD
    ids = range(n)
    grp = slice(lo, lo + n)
    head_sum, stack_heads, merge_heads = _head_helpers(C)
    r2 = lax.broadcasted_iota(jnp.int32, (2 * C, 2 * C), 0)
    c2 = lax.broadcasted_iota(jnp.int32, (2 * C, 2 * C), 1)
    t2 = jnp.bitwise_and(r2, C - 1)
    s2 = jnp.bitwise_and(c2, C - 1)
    strict2 = t2 > s2
    incl2 = t2 >= s2
    strict_bd = jnp.logical_and(strict2, (r2 >= C) == (c2 >= C))
    eye2 = jnp.where(r2 == c2, 1.0, 0.0).astype(F32)
    bl_r = lax.broadcasted_iota(jnp.int32, (LANES, LANES), 0) < H
    bl_c = lax.broadcasted_iota(jnp.int32, (LANES, LANES), 1) < H
    blockdiag = bl_r == bl_c
    zeros_c = jnp.zeros((C, LANES), BF16)
    dot_cost = 20.0 * n

    yield ("M", dot_cost)
    lhs = [stash.get("lhs", j) for j in ids]
    g = [_dot(lhs[j], stash.get("rhs", j), NT) for j in ids]
    g_a = [_bf(jnp.where(strict2, g[j][:2 * C], 0.0)) for j in ids]
    g_r = [_bf(jnp.where(incl2, g[j][2 * C:], 0.0)) for j in ids]
    yield ("M", dot_cost)
    pw = [jnp.where(strict_bd, _dot(lhs[j][:2 * C], stash.get("bt_hat", j), NT), 0.0) for j in ids]
    inv = [eye2 + pw[j] for j in ids]
    span = 2
    while span < C:
        yield ("M", dot_cost)
        pw = [_bf(pw[j]) for j in ids]
        pw = [_dot(pw[j], pw[j], NN) for j in ids]
        yield ("M", dot_cost)
        inv = [inv[j] + _dot(_bf(inv[j]), _bf(pw[j]), NN) for j in ids]
        span *= 2
    yield ("M", dot_cost)
    S = [s_scr[lo + j] for j in ids]
    pq = [_dot(stash.get("atrt", j), _bf(S[j]), NT) for j in ids]
    yield ("M", dot_cost)
    vb = [stash.get("vb", j) for j in ids]
    w = [pq[j][:C] + merge_heads(_dot(g_a[j], jnp.concatenate([zeros_c, vb[j]], axis=0), NN)) for j in ids]
    yield ("M", dot_cost)
    u2 = [_dot(_bf(inv[j]), _bf(stack_heads(w[j])), NN) for j in ids]
    z = [jnp.concatenate([_bf(u2[j][:C] + u2[j][C:]), vb[j]], axis=0) for j in ids]
    yield ("M", dot_cost)
    y = [pq[j][C:] + merge_heads(_dot(g_r[j], z[j], NN)) for j in ids]
    yield ("M", dot_cost)
    for j in ids:
        upd = _dot(z[j], stash.get("bkh", j), TN)
        s_scr[lo + j] = S[j] * stash.get("dec", j) + jnp.where(blockdiag, upd, 0.0)

    yield ("V", 40.0 * n)
    y = jnp.stack(y)
    mu = head_sum(y) * (1.0 / H)
    dy = y - mu
    var = head_sum(dy * dy) * (1.0 / H)
    yn = dy * lax.rsqrt(var + GN_EPS) * lnw_ref[...] + lnb_ref[...]
    o_ref[grp] = (yn + stash.get("bonus")) * stash.get("gate")


def _mid_rows(x, h):
    R = x.shape[0]
    row = lax.broadcasted_iota(jnp.int32, x.shape, 0)
    if 2 * h >= SUBLANES:
        return jnp.concatenate(
            [jnp.broadcast_to(x[b * 2 * h + h - 1:b * 2 * h + h], (2 * h, LANES)) for b in range(R // (2 * h))],
            axis=0)
    if h == 1:
        return jnp.where(jnp.bitwise_and(row, 1) == 1, pltpu.roll(x, 1, 0), x)
    picks = []
    for half in range(SUBLANES // (2 * h)):
        r0 = half * 2 * h + h - 1
        picks.append(jnp.concatenate(
            [jnp.broadcast_to(x[g * SUBLANES + r0:g * SUBLANES + r0 + 1], (SUBLANES, LANES))
             for g in range(R // SUBLANES)], axis=0))
    out = picks[-1]
    sub = jnp.bitwise_and(row, SUBLANES - 1)
    for half in range(len(picks) - 2, -1, -1):
        out = jnp.where(sub < (half + 1) * 2 * h, picks[half], out)
    return out


def _hgrn_produce(stash, lo, n, C, q_ref, f_ref, i_ref, lbl_ref):
    ids = range(n)
    grp = slice(lo, lo + n)
    logits = lbl_ref[...]
    ex = jnp.exp(logits - jnp.max(logits, axis=0, keepdims=True))
    lb = ex[0:1] / jnp.sum(ex, axis=0, keepdims=True)
    t_i = lax.broadcasted_iota(jnp.int32, (C, C), 0)
    s_i = lax.broadcasted_iota(jnp.int32, (C, C), 1)
    tri = jnp.where(t_i >= s_i, 1.0, 0.0).astype(BF16)

    yield ("V", 40.0 * n)
    q_h = q_ref[grp]
    f_h = f_ref[grp]
    q = q_h * _sigmoid(q_h)
    fg = lb + (1.0 - lb) * _sigmoid(f_h)
    kin = 1.0 - fg
    lf = jnp.log(fg)
    bc = jnp.stack([_cumsum_rows(tri, lf[j]) for j in ids])

    yield ("V", 30.0 * n)
    b_last = bc[:, C - 1:C, :]
    stash.put("qs", _bf(q * jnp.exp(bc)))
    stash.put("ks", _bf(kin * jnp.exp(b_last - bc)))
    stash.put("iv", _bf(i_ref[grp]))
    stash.put("q", q)
    stash.put("kin", kin)
    stash.put("bc", bc)
    stash.put("dec", jnp.exp(b_last))


def _hgrn_consume(stash, lo, n, C, g_ref, hgw_ref, o_ref, s_scr):
    ids = range(n)
    grp = slice(lo, lo + n)
    t_i = lax.broadcasted_iota(jnp.int32, (C, C), 0)
    s_i = lax.broadcasted_iota(jnp.int32, (C, C), 1)
    halves = []
    h = C // 2
    while h >= 1:
        halves.append(h)
        h //= 2

    def level_mask(h):
        same = jnp.bitwise_and(t_i, -2 * h) == jnp.bitwise_and(s_i, -2 * h)
        return jnp.logical_and(same, jnp.logical_and(jnp.bitwise_and(t_i, h) != 0, jnp.bitwise_and(s_i, h) == 0))

    def seq(x, j):
        return _bf(x[j * C:(j + 1) * C])

    dot_cost = 14.0 * n

    yield ("M", dot_cost)
    S = [s_scr[lo + j] for j in ids]
    o_state = [_dot(stash.get("qs", j), _bf(S[j]), NT) for j in ids]
    yield ("M", dot_cost)
    iv = [stash.get("iv", j) for j in ids]
    for j in ids:
        s_scr[lo + j] = S[j] * stash.get("dec", j) + _dot(iv[j], stash.get("ks", j), TN)
    yield ("M", dot_cost)
    q = stash.get("q").reshape(n * C, LANES)
    kin = stash.get("kin").reshape(n * C, LANES)
    bc = stash.get("bc").reshape(n * C, LANES)
    att = [jnp.where(t_i == s_i, _dot(seq(q, j), seq(kin, j), NT), 0.0) for j in ids]
    for h in halves:
        yield ("V", 25.0 * n)
        dmid = jnp.exp(-jnp.abs(bc - _mid_rows(bc, h)))
        ql = q * dmid
        kl = kin * dmid
        yield ("M", dot_cost)
        msk = level_mask(h)
        att = [jnp.where(msk, _dot(seq(ql, j), seq(kl, j), NT), att[j]) for j in ids]
    yield ("M", dot_cost)
    o_intra = [_dot(_bf(att[j]), iv[j], NN) for j in ids]

    yield ("V", 25.0 * n)
    o = jnp.stack([o_intra[j] + o_state[j] for j in ids])
    ms = jnp.mean(o * o, axis=-1, keepdims=True)
    g_h = g_ref[grp]
    o_ref[grp] = o * lax.rsqrt(ms + RMS_EPS) * hgw_ref[...] * (g_h * _sigmoid(g_h))


N_MIXER_IN = 30
N_MIXER_OUT = 4
N_MIXER_STATE_SCRATCH = 6


def _mixer_kernel(*refs, bb, C, n_groups, pipelined):
    (pr_ref, pk_ref, pv_ref, pl_ref, shr_ref, shk_ref, shv_ref, shl_ref,
     mur_ref, muk_ref, muv_ref, mul_ref, w0_ref, a0_ref, kk_ref, ka_ref, rk_ref,
     lnw_ref, lnb_ref, w1_ref, a1_ref, g1_ref, srw0_ref,
     q_ref, f_ref, i_ref, g_ref, lbl_ref, hgw_ref, shg0_ref) = refs[:N_MIXER_IN]
    orw_ref, strw_ref, ohg_ref, sthg_ref = refs[N_MIXER_IN:N_MIXER_IN + N_MIXER_OUT]
    scr = refs[N_MIXER_IN + N_MIXER_OUT:]
    srw_scr, cr_scr, ck_scr, cv_scr, cl_scr, shg_scr = scr[:N_MIXER_STATE_SCRATCH]
    stash_refs = scr[N_MIXER_STATE_SCRATCH:]
    if pipelined:
        rw_refs = {name: stash_refs[k] for k, (name, _, _) in enumerate(RWKV_STASH)}
        hg_refs = {name: stash_refs[len(RWKV_STASH) + k] for k, (name, _, _) in enumerate(HGRN_STASH)}

    c = pl.program_id(2)
    last = pl.num_programs(2) - 1
    H = RWKV_HEAD

    @pl.when(c == 0)
    def _init():
        z = jnp.zeros((H, H), F32)
        for i in range(bb):
            top = jnp.concatenate([srw0_ref[i, 0], z], axis=1)
            bot = jnp.concatenate([z, srw0_ref[i, 1]], axis=1)
            srw_scr[i] = jnp.concatenate([top, bot], axis=0)
            shg_scr[i] = shg0_ref[i, 0].T
        cr_scr[...] = shr_ref[...]
        ck_scr[...] = shk_ref[...]
        cv_scr[...] = shv_ref[...]
        cl_scr[...] = shl_ref[...]
        if pipelined:
            for table in (rw_refs, hg_refs):
                for name, ref in table.items():
                    fill = jnp.ones if name == "dec" else jnp.zeros
                    ref[1] = fill(ref.shape[1:], ref.dtype)

    per = bb // n_groups
    groups = [(gidx * per, per) for gidx in range(n_groups)]

    def rwkv_produce(stash, lo, n):
        return _rwkv_produce(stash, lo, n, C, pr_ref, pk_ref, pv_ref, pl_ref, mur_ref, muk_ref, muv_ref, mul_ref,
                             w0_ref, a0_ref, kk_ref, ka_ref, rk_ref, w1_ref, a1_ref, g1_ref,
                             cr_scr, ck_scr, cv_scr, cl_scr)

    def rwkv_consume(stash, lo, n):
        return _rwkv_consume(stash, lo, n, C, lnw_ref, lnb_ref, orw_ref, srw_scr)

    def hgrn_produce(stash, lo, n):
        return _hgrn_produce(stash, lo, n, C, q_ref, f_ref, i_ref, lbl_ref)

    def hgrn_consume(stash, lo, n):
        return _hgrn_consume(stash, lo, n, C, g_ref, hgw_ref, ohg_ref, shg_scr)

    if not pipelined:
        tasks = []
        for lo, n in groups:
            rw, hg = _ValueStash(), _ValueStash()
            tasks.append(_chain(rwkv_produce(rw, lo, n), rwkv_consume(rw, lo, n)))
            tasks.append(_chain(hgrn_produce(hg, lo, n), hgrn_consume(hg, lo, n)))
        _interleave(tasks)
    else:
        def step(write_slot):
            read_slot = 1 - write_slot
            tasks = []
            for lo, n in groups:
                tasks.append(rwkv_consume(_RefStash(rw_refs, read_slot, lo, n), lo, n))
                tasks.append(hgrn_consume(_RefStash(hg_refs, read_slot, lo, n), lo, n))
            for lo, n in groups:
                tasks.append(rwkv_produce(_RefStash(rw_refs, write_slot, lo, n), lo, n))
                tasks.append(hgrn_produce(_RefStash(hg_refs, write_slot, lo, n), lo, n))
            _interleave(tasks)

        step(lax.rem(c, 2))

    @pl.when(c == last)
    def _fin():
        for i in range(bb):
            s_fin = srw_scr[i]
            strw_ref[i, 0] = s_fin[:H, :H]
            strw_ref[i, 1] = s_fin[H:, H:]
            sthg_ref[i, 0] = shg_scr[i].T


def _mixer(proj3, s_shift3, s_rwkv, s_hgrn, p, bb, C, n_groups, pipelined):
    B, T, _ = proj3.shape
    nc = T // C
    grid = (B // bb, PAIRS, nc + 1 if pipelined else nc)
    nb = RWKV_WIDTH // LANES
    hg0 = RWKV_PROJ // LANES
    lora_blk = LORA_COL // (2 * LANES)

    if pipelined:
        def produced(c):
            return jnp.minimum(c, nc - 1)

        def consumed(c):
            return jnp.maximum(c - 1, 0)
    else:
        produced = consumed = lambda c: c

    def tok(col0, chunk=produced):
        return pl.BlockSpec((bb, C, LANES), lambda b, h, c: (b, chunk(c), col0 + h))

    def first(col0):
        return pl.BlockSpec((bb, 1, LANES), lambda b, h, c: (b, 0, col0 + h))

    def vec(col0):
        return pl.BlockSpec((1, LANES), lambda b, h, c: (0, col0 + h))

    in_specs = [
        tok(0), tok(nb), tok(2 * nb),
        pl.BlockSpec((bb, C, 2 * LANES), lambda b, h, c: (b, produced(c), lora_blk)),
        first(0), first(nb), first(2 * nb),
        pl.BlockSpec((bb, 1, 2 * LANES), lambda b, h, c: (b, 0, lora_blk)),
        vec(0), vec(nb), vec(2 * nb),
        pl.BlockSpec((1, 2 * LANES), lambda b, h, c: (0, lora_blk)),
        vec(0), vec(0), vec(0), vec(0), vec(0), vec(0), vec(0),
        pl.BlockSpec((LANES, LANES), lambda b, h, c: (0, h)),
        pl.BlockSpec((LANES, LANES), lambda b, h, c: (0, h)),
        pl.BlockSpec((GATE_RANK, LANES), lambda b, h, c: (0, h)),
        pl.BlockSpec((bb, 2, RWKV_HEAD, RWKV_HEAD), lambda b, h, c: (b, h, 0, 0)),
        tok(hg0), tok(hg0 + nb), tok(hg0 + 2 * nb), tok(hg0 + 3 * nb, consumed),
        pl.BlockSpec((DEPTH + 1, LANES), lambda b, h, c: (0, h)),
        vec(0),
        pl.BlockSpec((bb, 1, HGRN_HEAD, HGRN_HEAD), lambda b, h, c: (b, h, 0, 0)),
    ]
    assert len(in_specs) == N_MIXER_IN
    out_specs = [
        pl.BlockSpec((bb, C, LANES), lambda b, h, c: (b, consumed(c), h)),
        pl.BlockSpec((bb, 2, RWKV_HEAD, RWKV_HEAD), lambda b, h, c: (b, h, 0, 0)),
        pl.BlockSpec((bb, C, LANES), lambda b, h, c: (b, consumed(c), h)),
        pl.BlockSpec((bb, 1, HGRN_HEAD, HGRN_HEAD), lambda b, h, c: (b, h, 0, 0)),
    ]
    out_shape = [jax.ShapeDtypeStruct((B, T, RWKV_WIDTH), F32),
                 jax.ShapeDtypeStruct((B, RWKV_HEADS, RWKV_HEAD, RWKV_HEAD), F32),
                 jax.ShapeDtypeStruct((B, T, HGRN_WIDTH), F32),
                 jax.ShapeDtypeStruct((B, HGRN_HEADS, HGRN_HEAD, HGRN_HEAD), F32)]
    scratch = [pltpu.VMEM((bb, LANES, LANES), F32),
               pltpu.VMEM((bb, 1, LANES), F32), pltpu.VMEM((bb, 1, LANES), F32),
               pltpu.VMEM((bb, 1, LANES), F32), pltpu.VMEM((bb, 1, 2 * LANES), F32),
               pltpu.VMEM((bb, HGRN_HEAD, HGRN_HEAD), F32)]
    assert len(scratch) == N_MIXER_STATE_SCRATCH
    if pipelined:
        scratch += [pltpu.VMEM((2, bb, rows * C if rows else 1, LANES), dt)
                    for _, rows, dt in RWKV_STASH + HGRN_STASH]
    return pl.pallas_call(
        functools.partial(_mixer_kernel, bb=bb, C=C, n_groups=n_groups, pipelined=pipelined),
        grid=grid, in_specs=in_specs, out_specs=out_specs, out_shape=out_shape,
        scratch_shapes=scratch,
        compiler_params=pltpu.CompilerParams(
            dimension_semantics=("arbitrary", "arbitrary", "arbitrary"),
            vmem_limit_bytes=VMEM_LIMIT),
        name="mixer",
    )(proj3, proj3, proj3, proj3, s_shift3, s_shift3, s_shift3, s_shift3,
      p["shift_mu"], p["shift_mu"], p["shift_mu"], p["shift_mu"],
      p["w0"], p["a0"], p["k_k"], p["k_a"], p["r_k"], p["ln_x_w"], p["ln_x_b"],
      p["w1u_pad"], p["a1u_pad"], p["g1u"], s_rwkv,
      proj3, proj3, proj3, proj3, p["lb_logits"], p["hg_norm_w"], s_hgrn)


def _layer_norm(x, g, b):
    mu = jnp.mean(x, axis=-1, keepdims=True)
    d = x - mu
    var = jnp.mean(d * d, axis=-1, keepdims=True)
    return d * lax.rsqrt(var + LN_EPS) * g + b


def _post_kernel(x_ref, orw_ref, ohg_ref, wo1_ref, wo2_ref, g1_ref, b1_ref, wup_ref, wdn_ref,
                 g2_ref, b2_ref, y_ref):
    mix = (jnp.dot(orw_ref[...].astype(BF16), wo1_ref[...], preferred_element_type=F32)
           + jnp.dot(ohg_ref[...].astype(BF16), wo2_ref[...], preferred_element_type=F32))
    h1 = _layer_norm(ALPHA * x_ref[...] + mix, g1_ref[...], b1_ref[...])
    up = jnp.dot(h1.astype(BF16), wup_ref[...], preferred_element_type=F32)
    up = jnp.square(jnp.maximum(up, 0.0))
    ff = jnp.dot(up.astype(BF16), wdn_ref[...], preferred_element_type=F32)
    y_ref[...] = _layer_norm(ALPHA * h1 + ff, g2_ref[...], b2_ref[...])


def _post(x2, orw2, ohg2, p, tm):
    n = x2.shape[0]

    def const(shape):
        return pl.BlockSpec(shape, lambda i: (0, 0), pipeline_mode=pl.Buffered(1))

    return pl.pallas_call(
        _post_kernel,
        grid=(n // tm,),
        in_specs=[pl.BlockSpec((tm, D_MODEL), lambda i: (i, 0)),
                  pl.BlockSpec((tm, RWKV_WIDTH), lambda i: (i, 0)),
                  pl.BlockSpec((tm, HGRN_WIDTH), lambda i: (i, 0)),
                  const((RWKV_WIDTH, D_MODEL)), const((HGRN_WIDTH, D_MODEL)),
                  const((1, D_MODEL)), const((1, D_MODEL)),
                  const((D_MODEL, D_FF)), const((D_FF, D_MODEL)),
                  const((1, D_MODEL)), const((1, D_MODEL))],
        out_specs=pl.BlockSpec((tm, D_MODEL), lambda i: (i, 0)),
        out_shape=jax.ShapeDtypeStruct((n, D_MODEL), F32),
        compiler_params=pltpu.CompilerParams(dimension_semantics=("arbitrary",),
                                             vmem_limit_bytes=VMEM_LIMIT),
        name="post",
    )(x2, orw2, ohg2, p["wo_rw"], p["wo_hg"], p["ln1_g"], p["ln1_b"], p["w_up"], p["w_down"],
      p["ln2_g"], p["ln2_b"])


def _prep_params(w_in, shift_mu, w0, w1u, a0, a1u, g1u, k_k, k_a, r_k, ln_x_w, ln_x_b, lb_logits,
                 hg_norm_w, w_out, ln1_g, ln1_b, w_up, w_down, ln2_g, ln2_b):
    zw = jnp.zeros((LANES - DECAY_RANK, RWKV_WIDTH), F32)
    za = jnp.zeros((LANES - AICL_RANK, RWKV_WIDTH), F32)
    w_out_bf = w_out[0].astype(BF16)
    return {
        "w_in": w_in[0].astype(BF16),
        "shift_mu": shift_mu[0].reshape(1, RWKV_PROJ),
        "w0": w0[0].reshape(1, RWKV_WIDTH), "a0": a0[0].reshape(1, RWKV_WIDTH),
        "k_k": k_k[0].reshape(1, RWKV_WIDTH), "k_a": k_a[0].reshape(1, RWKV_WIDTH),
        "r_k": r_k[0].reshape(1, RWKV_WIDTH),
        "ln_x_w": ln_x_w[0].reshape(1, RWKV_WIDTH), "ln_x_b": ln_x_b[0].reshape(1, RWKV_WIDTH),
        "w1u_pad": jnp.concatenate([w1u[0], zw], axis=0),
        "a1u_pad": jnp.concatenate([za, a1u[0]], axis=0),
        "g1u": g1u[0],
        "lb_logits": lb_logits.astype(F32),
        "hg_norm_w": hg_norm_w[0].reshape(1, HGRN_WIDTH),
        "wo_rw": w_out_bf[:RWKV_WIDTH], "wo_hg": w_out_bf[RWKV_WIDTH:],
        "ln1_g": ln1_g[0].reshape(1, D_MODEL), "ln1_b": ln1_b[0].reshape(1, D_MODEL),
        "w_up": w_up[0].astype(BF16), "w_down": w_down[0].astype(BF16),
        "ln2_g": ln2_g[0].reshape(1, D_MODEL), "ln2_b": ln2_b[0].reshape(1, D_MODEL),
    }


def _run_group(x, s_rwkv, s_hgrn, s_shift, p, *, tm, bb, chunk, n_groups, pipelined):
    B, T, _ = x.shape
    x2 = x.reshape(B * T, D_MODEL)
    proj2 = _proj(x2, p["w_in"], tm)
    proj3 = proj2.reshape(B, T, PROJ)
    o_rw, st_rw, o_hg, st_hg = _mixer(proj3, s_shift.reshape(B, 1, RWKV_PROJ), s_rwkv, s_hgrn, p,
                                      bb, chunk, n_groups, pipelined)
    y2 = _post(x2, o_rw.reshape(B * T, RWKV_WIDTH), o_hg.reshape(B * T, HGRN_WIDTH), p, tm)
    sh = proj3[:, T - 1, :RWKV_PROJ]
    return y2.reshape(B, T, D_MODEL), st_rw[None], st_hg[None], sh[None]


PROMPT_CFG = dict(tm=256, bb=8, chunk=64, n_groups=2, pipelined=True)
SAMPLE_CFG = dict(tm=256, bb=16, chunk=8, n_groups=2, pipelined=False)


def kernel(x_prompt, x_sample, state_rwkv, state_hgrn, state_shift, w_in, shift_mu, w0, w1u, a0, a1u, g1u, k_k, k_a, r_k, ln_x_w, ln_x_b, lb_logits, hg_norm_w, w_out, ln1_g, ln1_b, w_up, w_down, ln2_g, ln2_b):
    assert w_in.shape[0] == DEPTH
    p = _prep_params(w_in, shift_mu, w0, w1u, a0, a1u, g1u, k_k, k_a, r_k, ln_x_w, ln_x_b, lb_logits,
                     hg_norm_w, w_out, ln1_g, ln1_b, w_up, w_down, ln2_g, ln2_b)
    bp = x_prompt.shape[0]
    z_rw = jnp.zeros((bp, RWKV_HEADS, RWKV_HEAD, RWKV_HEAD), F32)
    z_hg = jnp.zeros((bp, HGRN_HEADS, HGRN_HEAD, HGRN_HEAD), F32)
    z_sh = jnp.zeros((bp, RWKV_PROJ), F32)
    y_p, rw_p, hg_p, sh_p = _run_group(x_prompt, z_rw, z_hg, z_sh, p, **PROMPT_CFG)
    y_s, rw_s, hg_s, sh_s = _run_group(x_sample, state_rwkv[0].astype(F32), state_hgrn[0].astype(F32),
                                       state_shift[0].astype(F32), p, **SAMPLE_CFG)
    return (y_p, y_s, rw_p, rw_s, hg_p, hg_s, sh_p, sh_s)
```

```python
import functools
import math

import jax
import jax.numpy as jnp
from jax import lax
from jax.experimental import pallas as pl
from jax.experimental.pallas import tpu as pltpu

F32 = jnp.float32
BF16 = jnp.bfloat16

D_MODEL = 1024
RWKV_WIDTH = 512
RWKV_HEAD = 64
RWKV_HEADS = 8
HGRN_WIDTH = 512
HGRN_HEAD = 128
HGRN_HEADS = 4
DECAY_RANK = 64
AICL_RANK = 64
GATE_RANK = 128
RWKV_PROJ = 3 * RWKV_WIDTH + DECAY_RANK + AICL_RANK + GATE_RANK
HGRN_PROJ = 4 * HGRN_WIDTH
PROJ = RWKV_PROJ + HGRN_PROJ
D_FF = 4 * D_MODEL
DEPTH = 1
ALPHA = (2.0 * DEPTH) ** 0.25
LN_EPS = 1e-5
GN_EPS = RWKV_HEAD * 1e-5
RMS_EPS = 1e-6
DECAY_SCALE = math.exp(-0.5)

LANES = 128
SUBLANES = 8
PAIRS = RWKV_HEADS // 2
LORA_COL = 3 * RWKV_WIDTH
VMEM_LIMIT = 56 * 1024 * 1024

NN = ((1,), (0,))
NT = ((1,), (1,))
TN = ((0,), (0,))

RWKV_STASH = (("lhs", 4, BF16), ("rhs", 2, BF16), ("bt_hat", 2, BF16), ("atrt", 2, BF16), ("vb", 1, BF16),
              ("bkh", 2, BF16), ("dec", 0, F32), ("bonus", 1, F32), ("gate", 1, F32))
HGRN_STASH = (("qs", 1, BF16), ("ks", 1, BF16), ("iv", 1, BF16), ("q", 1, F32), ("kin", 1, F32),
              ("bc", 1, F32), ("dec", 0, F32))


def _dot(a, b, dims):
    return lax.dot_general(a, b, (dims, ((), ())), preferred_element_type=F32)


def _bf(x):
    return x.astype(BF16)


def _cumsum_rows(tri, x):
    p0 = _bf(x)
    r1 = x - p0.astype(F32)
    p1 = _bf(r1)
    p2 = _bf(r1 - p1.astype(F32))
    return _dot(tri, p0, NN) + _dot(tri, p1, NN) + _dot(tri, p2, NN)


def _sigmoid(x):
    return 0.5 * jnp.tanh(0.5 * x) + 0.5


def _interleave(tasks):
    gens = list(tasks)
    nxt = [next(g, None) for g in gens]
    spent = {"M": 0.0, "V": 0.0}
    turn = 0
    while any(k is not None for k in nxt):
        ready = {kind: [j for j, k in enumerate(nxt) if k is not None and k[0] == kind] for kind in spent}
        want = "M" if (ready["M"] and (spent["M"] <= spent["V"] or not ready["V"])) else "V"
        if want == "M":
            cands = ready["M"]
            t = cands[turn % len(cands)]
            turn += 1
        else:
            t = ready["V"][0]
        spent[want] += nxt[t][1]
        nxt[t] = next(gens[t], None)


def _chain(*gens):
    for g in gens:
        yield from g


class _ValueStash:
    def __init__(self):
        self.vals = {}

    def put(self, name, val):
        self.vals[name] = val

    def get(self, name, j=None):
        return self.vals[name] if j is None else self.vals[name][j]


class _RefStash:
    def __init__(self, refs, slot, lo, n):
        self.refs, self.slot, self.lo, self.n = refs, slot, lo, n

    def put(self, name, val):
        self.refs[name][self.slot, self.lo:self.lo + self.n] = val

    def get(self, name, j=None):
        if j is None:
            return self.refs[name][self.slot, self.lo:self.lo + self.n]
        return self.refs[name][self.slot, self.lo + j]


def _proj_kernel(x_ref, w_ref, o_ref):
    o_ref[...] = jnp.dot(x_ref[...].astype(BF16), w_ref[...], preferred_element_type=F32)


def _proj(x2, w_in_bf, tm):
    n = x2.shape[0]
    return pl.pallas_call(
        _proj_kernel,
        grid=(n // tm,),
        in_specs=[pl.BlockSpec((tm, D_MODEL), lambda i: (i, 0)),
                  pl.BlockSpec((D_MODEL, PROJ), lambda i: (0, 0), pipeline_mode=pl.Buffered(1))],
        out_specs=pl.BlockSpec((tm, PROJ), lambda i: (i, 0)),
        out_shape=jax.ShapeDtypeStruct((n, PROJ), F32),
        compiler_params=pltpu.CompilerParams(dimension_semantics=("arbitrary",),
                                             vmem_limit_bytes=VMEM_LIMIT),
        name="proj",
    )(x2, w_in_bf)


def _head_helpers(C):
    lane = lax.broadcasted_iota(jnp.int32, (C, LANES), 1)
    m0 = lane < RWKV_HEAD

    def head_sum(x):
        s0 = jnp.sum(jnp.where(m0, x, 0.0), axis=-1, keepdims=True)
        s1 = jnp.sum(jnp.where(m0, 0.0, x), axis=-1, keepdims=True)
        return jnp.where(m0, s0, s1)

    def stack_heads(x):
        return jnp.concatenate([jnp.where(m0, x, 0.0), jnp.where(m0, 0.0, x)], axis=-2)

    def merge_heads(x):
        return jnp.where(m0, x[:C], x[C:])

    return head_sum, stack_heads, merge_heads


def _rwkv_produce(stash, lo, n, C, pr_ref, pk_ref, pv_ref, pl_ref, mur_ref, muk_ref, muv_ref, mul_ref,
                  w0_ref, a0_ref, kk_ref, ka_ref, rk_ref, w1_ref, a1_ref, g1_ref,
                  cr_scr, ck_scr, cv_scr, cl_scr):
    ids = range(n)
    grp = slice(lo, lo + n)
    head_sum, stack_heads, _ = _head_helpers(C)
    rowc = lax.broadcasted_iota(jnp.int32, (C, C), 0)
    colc = lax.broadcasted_iota(jnp.int32, (C, C), 1)
    tri = jnp.where(rowc >= colc, 1.0, 0.0).astype(BF16)

    def shifted(p_ref, carry_ref, mu_ref):
        p = p_ref[grp]
        row = lax.broadcasted_iota(jnp.int32, p.shape, 1)
        flat = p.reshape(n * C, p.shape[-1])
        prev = jnp.where(row == 0, carry_ref[grp], pltpu.roll(flat, 1, 0).reshape(p.shape))
        carry_ref[grp] = p[:, C - 1:C, :]
        return p + mu_ref[...] * (prev - p)

    w1 = _bf(w1_ref[...])
    a1 = _bf(a1_ref[...])
    g1 = _bf(g1_ref[...])

    yield ("V", 60.0 * n)
    r = shifted(pr_ref, cr_scr, mur_ref)
    xk = shifted(pk_ref, ck_scr, muk_ref)
    v = shifted(pv_ref, cv_scr, muv_ref)
    xl = shifted(pl_ref, cl_scr, mul_ref).reshape(n * C, 2 * LANES)
    xl_lo = xl[:, :LANES]
    dw = jnp.dot(_bf(jnp.tanh(xl_lo)), w1, preferred_element_type=F32)
    da = jnp.dot(_bf(xl_lo), a1, preferred_element_type=F32)
    gate = jnp.dot(_bf(_sigmoid(xl[:, LANES:])), g1, preferred_element_type=F32)
    stash.put("gate", gate.reshape(n, C, LANES))

    yield ("V", 60.0 * n)
    lw = (-DECAY_SCALE * _sigmoid(w0_ref[...] + dw)).reshape(n, C, LANES)
    a_lr = _sigmoid(a0_ref[...] + da).reshape(n, C, LANES)
    kk = xk * kk_ref[...]
    kk = kk * jnp.minimum(lax.rsqrt(head_sum(kk * kk)), 1e12)
    k2 = xk * (1.0 + (a_lr - 1.0) * ka_ref[...])
    b = kk * a_lr
    cum = jnp.stack([_cumsum_rows(tri, lw[j]) for j in ids])

    yield ("V", 80.0 * n)
    cum_last = cum[:, C - 1:C, :]
    e_in = jnp.exp(-cum)
    e_out = jnp.exp(cum_last - cum)
    at = -kk * jnp.exp(cum - lw)
    rt = r * jnp.exp(cum)
    bt = b * e_in
    kt = k2 * e_in
    stash.put("lhs", _bf(jnp.concatenate([stack_heads(at), stack_heads(rt)], axis=1)))
    stash.put("rhs", _bf(jnp.concatenate([bt, kt], axis=1)))
    stash.put("bt_hat", _bf(stack_heads(bt)))
    stash.put("atrt", _bf(jnp.concatenate([at, rt], axis=1)))
    stash.put("vb", _bf(v))
    stash.put("bkh", _bf(jnp.concatenate([b * e_out, k2 * e_out], axis=1)))
    stash.put("dec", jnp.exp(cum_last))
    stash.put("bonus", head_sum(r * k2 * rk_ref[...]) * v)


def _rwkv_consume(stash, lo, n, C, dot_cost, lnw_ref, lnb_ref, o_ref, s_scr):
    H = RWKV_HEAD
    ids = range(n)
    grp = slice(lo, lo + n)
    head_sum, stack_heads, merge_heads = _head_helpers(C)
    r2 = lax.broadcasted_iota(jnp.int32, (2 * C, 2 * C), 0)
    c2 = lax.broadcasted_iota(jnp.int32, (2 * C, 2 * C), 1)
    t2 = jnp.bitwise_and(r2, C - 1)
    s2 = jnp.bitwise_and(c2, C - 1)
    strict2 = t2 > s2
    incl2 = t2 >= s2
    strict_bd = jnp.logical_and(strict2, (r2 >= C) == (c2 >= C))
    eye2 = jnp.where(r2 == c2, 1.0, 0.0).astype(F32)
    bl_r = lax.broadcasted_iota(jnp.int32, (LANES, LANES), 0) < H
    bl_c = lax.broadcasted_iota(jnp.int32, (LANES, LANES), 1) < H
    blockdiag = bl_r == bl_c
    zeros_c = jnp.zeros((C, LANES), BF16)

    yield ("M", dot_cost)
    lhs = [stash.get("lhs", j) for j in ids]
    g = [_dot(lhs[j], stash.get("rhs", j), NT) for j in ids]
    g_a = [_bf(jnp.where(strict2, g[j][:2 * C], 0.0)) for j in ids]
    g_r = [_bf(jnp.where(incl2, g[j][2 * C:], 0.0)) for j in ids]
    yield ("M", dot_cost)
    pw = [jnp.where(strict_bd, _dot(lhs[j][:2 * C], stash.get("bt_hat", j), NT), 0.0) for j in ids]
    inv = [eye2 + pw[j] for j in ids]
    span = 2
    while span < C:
        yield ("M", dot_cost)
        pw = [_bf(pw[j]) for j in ids]
        pw = [_dot(pw[j], pw[j], NN) for j in ids]
        yield ("M", dot_cost)
        inv = [inv[j] + _dot(_bf(inv[j]), _bf(pw[j]), NN) for j in ids]
        span *= 2
    yield ("M", dot_cost)
    S = [s_scr[lo + j] for j in ids]
    pq = [_dot(stash.get("atrt", j), _bf(S[j]), NT) for j in ids]
    yield ("M", dot_cost)
    vb = [stash.get("vb", j) for j in ids]
    w = [pq[j][:C] + merge_heads(_dot(g_a[j], jnp.concatenate([zeros_c, vb[j]], axis=0), NN)) for j in ids]
    yield ("M", dot_cost)
    u2 = [_dot(_bf(inv[j]), _bf(stack_heads(w[j])), NN) for j in ids]
    z = [jnp.concatenate([_bf(u2[j][:C] + u2[j][C:]), vb[j]], axis=0) for j in ids]
    yield ("M", dot_cost)
    y = [pq[j][C:] + merge_heads(_dot(g_r[j], z[j], NN)) for j in ids]
    yield ("M", dot_cost)
    for j in ids:
        upd = _dot(z[j], stash.get("bkh", j), TN)
        s_scr[lo + j] = S[j] * stash.get("dec", j) + jnp.where(blockdiag, upd, 0.0)

    yield ("V", 40.0 * n)
    y = jnp.stack(y)
    mu = head_sum(y) * (1.0 / H)
    dy = y - mu
    var = head_sum(dy * dy) * (1.0 / H)
    yn = dy * lax.rsqrt(var + GN_EPS) * lnw_ref[...] + lnb_ref[...]
    o_ref[grp] = (yn + stash.get("bonus")) * stash.get("gate")


def _mid_rows(x, h):
    R = x.shape[0]
    row = lax.broadcasted_iota(jnp.int32, x.shape, 0)
    if 2 * h >= SUBLANES:
        return jnp.concatenate(
            [jnp.broadcast_to(x[b * 2 * h + h - 1:b * 2 * h + h], (2 * h, LANES)) for b in range(R // (2 * h))],
            axis=0)
    if h == 1:
        return jnp.where(jnp.bitwise_and(row, 1) == 1, pltpu.roll(x, 1, 0), x)
    picks = []
    for half in range(SUBLANES // (2 * h)):
        r0 = half * 2 * h + h - 1
        picks.append(jnp.concatenate(
            [jnp.broadcast_to(x[g * SUBLANES + r0:g * SUBLANES + r0 + 1], (SUBLANES, LANES))
             for g in range(R // SUBLANES)], axis=0))
    out = picks[-1]
    sub = jnp.bitwise_and(row, SUBLANES - 1)
    for half in range(len(picks) - 2, -1, -1):
        out = jnp.where(sub < (half + 1) * 2 * h, picks[half], out)
    return out


def _hgrn_produce(stash, lo, n, C, q_ref, f_ref, i_ref, lbl_ref):
    ids = range(n)
    grp = slice(lo, lo + n)
    logits = lbl_ref[...]
    ex = jnp.exp(logits - jnp.max(logits, axis=0, keepdims=True))
    lb = ex[0:1] / jnp.sum(ex, axis=0, keepdims=True)
    t_i = lax.broadcasted_iota(jnp.int32, (C, C), 0)
    s_i = lax.broadcasted_iota(jnp.int32, (C, C), 1)
    tri = jnp.where(t_i >= s_i, 1.0, 0.0).astype(BF16)

    yield ("V", 40.0 * n)
    q_h = q_ref[grp]
    f_h = f_ref[grp]
    q = q_h * _sigmoid(q_h)
    fg = lb + (1.0 - lb) * _sigmoid(f_h)
    kin = 1.0 - fg
    lf = jnp.log(fg)
    bc = jnp.stack([_cumsum_rows(tri, lf[j]) for j in ids])

    yield ("V", 30.0 * n)
    b_last = bc[:, C - 1:C, :]
    stash.put("qs", _bf(q * jnp.exp(bc)))
    stash.put("ks", _bf(kin * jnp.exp(b_last - bc)))
    stash.put("iv", _bf(i_ref[grp]))
    stash.put("q", q)
    stash.put("kin", kin)
    stash.put("bc", bc)
    stash.put("dec", jnp.exp(b_last))


def _hgrn_consume(stash, lo, n, C, dot_cost, g_ref, hgw_ref, o_ref, s_scr):
    ids = range(n)
    grp = slice(lo, lo + n)
    t_i = lax.broadcasted_iota(jnp.int32, (C, C), 0)
    s_i = lax.broadcasted_iota(jnp.int32, (C, C), 1)
    halves = []
    h = C // 2
    while h >= 1:
        halves.append(h)
        h //= 2

    def level_mask(h):
        same = jnp.bitwise_and(t_i, -2 * h) == jnp.bitwise_and(s_i, -2 * h)
        return jnp.logical_and(same, jnp.logical_and(jnp.bitwise_and(t_i, h) != 0, jnp.bitwise_and(s_i, h) == 0))

    def seq(x, j):
        return _bf(x[j * C:(j + 1) * C])

    yield ("M", dot_cost)
    S = [s_scr[lo + j] for j in ids]
    o_state = [_dot(stash.get("qs", j), _bf(S[j]), NT) for j in ids]
    yield ("M", dot_cost)
    iv = [stash.get("iv", j) for j in ids]
    for j in ids:
        s_scr[lo + j] = S[j] * stash.get("dec", j) + _dot(iv[j], stash.get("ks", j), TN)
    yield ("M", dot_cost)
    q = stash.get("q").reshape(n * C, LANES)
    kin = stash.get("kin").reshape(n * C, LANES)
    bc = stash.get("bc").reshape(n * C, LANES)
    att = [jnp.where(t_i == s_i, _dot(seq(q, j), seq(kin, j), NT), 0.0) for j in ids]
    for h in halves:
        yield ("V", 25.0 * n)
        dmid = jnp.exp(-jnp.abs(bc - _mid_rows(bc, h)))
        ql = q * dmid
        kl = kin * dmid
        yield ("M", dot_cost)
        msk = level_mask(h)
        att = [jnp.where(msk, _dot(seq(ql, j), seq(kl, j), NT), att[j]) for j in ids]
    yield ("M", dot_cost)
    o_intra = [_dot(_bf(att[j]), iv[j], NN) for j in ids]

    yield ("V", 25.0 * n)
    o = jnp.stack([o_intra[j] + o_state[j] for j in ids])
    ms = jnp.mean(o * o, axis=-1, keepdims=True)
    g_h = g_ref[grp]
    o_ref[grp] = o * lax.rsqrt(ms + RMS_EPS) * hgw_ref[...] * (g_h * _sigmoid(g_h))


MXU_STAGE_COST = {True: (20.0, 14.0), False: (30.0, 20.0)}
N_MIXER_IN = 30
N_MIXER_OUT = 4
N_MIXER_STATE_SCRATCH = 6


def _mixer_kernel(*refs, bb, C, n_groups, pipelined):
    (pr_ref, pk_ref, pv_ref, pl_ref, shr_ref, shk_ref, shv_ref, shl_ref,
     mur_ref, muk_ref, muv_ref, mul_ref, w0_ref, a0_ref, kk_ref, ka_ref, rk_ref,
     lnw_ref, lnb_ref, w1_ref, a1_ref, g1_ref, srw0_ref,
     q_ref, f_ref, i_ref, g_ref, lbl_ref, hgw_ref, shg0_ref) = refs[:N_MIXER_IN]
    orw_ref, strw_ref, ohg_ref, sthg_ref = refs[N_MIXER_IN:N_MIXER_IN + N_MIXER_OUT]
    scr = refs[N_MIXER_IN + N_MIXER_OUT:]
    srw_scr, cr_scr, ck_scr, cv_scr, cl_scr, shg_scr = scr[:N_MIXER_STATE_SCRATCH]
    stash_refs = scr[N_MIXER_STATE_SCRATCH:]
    if pipelined:
        rw_refs = {name: stash_refs[k] for k, (name, _, _) in enumerate(RWKV_STASH)}
        hg_refs = {name: stash_refs[len(RWKV_STASH) + k] for k, (name, _, _) in enumerate(HGRN_STASH)}

    c = pl.program_id(2)
    last = pl.num_programs(2) - 1
    H = RWKV_HEAD

    @pl.when(c == 0)
    def _init():
        z = jnp.zeros((H, H), F32)
        for i in range(bb):
            top = jnp.concatenate([srw0_ref[i, 0], z], axis=1)
            bot = jnp.concatenate([z, srw0_ref[i, 1]], axis=1)
            srw_scr[i] = jnp.concatenate([top, bot], axis=0)
            shg_scr[i] = shg0_ref[i, 0].T
        cr_scr[...] = shr_ref[...]
        ck_scr[...] = shk_ref[...]
        cv_scr[...] = shv_ref[...]
        cl_scr[...] = shl_ref[...]
        if pipelined:
            for table in (rw_refs, hg_refs):
                for name, ref in table.items():
                    fill = jnp.ones if name == "dec" else jnp.zeros
                    ref[1] = fill(ref.shape[1:], ref.dtype)

    per = bb // n_groups
    groups = [(gidx * per, per) for gidx in range(n_groups)]

    def rwkv_produce(stash, lo, n):
        return _rwkv_produce(stash, lo, n, C, pr_ref, pk_ref, pv_ref, pl_ref, mur_ref, muk_ref, muv_ref, mul_ref,
                             w0_ref, a0_ref, kk_ref, ka_ref, rk_ref, w1_ref, a1_ref, g1_ref,
                             cr_scr, ck_scr, cv_scr, cl_scr)

    def rwkv_consume(stash, lo, n):
        return _rwkv_consume(stash, lo, n, C, MXU_STAGE_COST[pipelined][0] * n, lnw_ref, lnb_ref, orw_ref, srw_scr)

    def hgrn_produce(stash, lo, n):
        return _hgrn_produce(stash, lo, n, C, q_ref, f_ref, i_ref, lbl_ref)

    def hgrn_consume(stash, lo, n):
        return _hgrn_consume(stash, lo, n, C, MXU_STAGE_COST[pipelined][1] * n, g_ref, hgw_ref, ohg_ref, shg_scr)

    if not pipelined:
        tasks = []
        for lo, n in groups:
            rw, hg = _ValueStash(), _ValueStash()
            tasks.append(_chain(rwkv_produce(rw, lo, n), rwkv_consume(rw, lo, n)))
            tasks.append(_chain(hgrn_produce(hg, lo, n), hgrn_consume(hg, lo, n)))
        _interleave(tasks)
    else:
        def step(write_slot):
            read_slot = 1 - write_slot
            tasks = []
            for lo, n in groups:
                tasks.append(rwkv_consume(_RefStash(rw_refs, read_slot, lo, n), lo, n))
                tasks.append(hgrn_consume(_RefStash(hg_refs, read_slot, lo, n), lo, n))
            for lo, n in groups:
                tasks.append(rwkv_produce(_RefStash(rw_refs, write_slot, lo, n), lo, n))
                tasks.append(hgrn_produce(_RefStash(hg_refs, write_slot, lo, n), lo, n))
            _interleave(tasks)

        parity = lax.rem(c, 2)
        pl.when(parity == 0)(functools.partial(step, 0))
        pl.when(parity == 1)(functools.partial(step, 1))

    @pl.when(c == last)
    def _fin():
        for i in range(bb):
            s_fin = srw_scr[i]
            strw_ref[i, 0] = s_fin[:H, :H]
            strw_ref[i, 1] = s_fin[H:, H:]
            sthg_ref[i, 0] = shg_scr[i].T


def _mixer(proj3, s_shift3, s_rwkv, s_hgrn, p, bb, C, n_groups, pipelined):
    B, T, _ = proj3.shape
    nc = T // C
    grid = (B // bb, PAIRS, nc + 1 if pipelined else nc)
    nb = RWKV_WIDTH // LANES
    hg0 = RWKV_PROJ // LANES
    lora_blk = LORA_COL // (2 * LANES)

    if pipelined:
        def produced(c):
            return jnp.minimum(c, nc - 1)

        def consumed(c):
            return jnp.maximum(c - 1, 0)
    else:
        produced = consumed = lambda c: c

    def tok(col0, chunk=produced):
        return pl.BlockSpec((bb, C, LANES), lambda b, h, c: (b, chunk(c), col0 + h))

    def first(col0):
        return pl.BlockSpec((bb, 1, LANES), lambda b, h, c: (b, 0, col0 + h))

    def vec(col0):
        return pl.BlockSpec((1, LANES), lambda b, h, c: (0, col0 + h))

    in_specs = [
        tok(0), tok(nb), tok(2 * nb),
        pl.BlockSpec((bb, C, 2 * LANES), lambda b, h, c: (b, produced(c), lora_blk)),
        first(0), first(nb), first(2 * nb),
        pl.BlockSpec((bb, 1, 2 * LANES), lambda b, h, c: (b, 0, lora_blk)),
        vec(0), vec(nb), vec(2 * nb),
        pl.BlockSpec((1, 2 * LANES), lambda b, h, c: (0, lora_blk)),
        vec(0), vec(0), vec(0), vec(0), vec(0), vec(0), vec(0),
        pl.BlockSpec((LANES, LANES), lambda b, h, c: (0, h)),
        pl.BlockSpec((LANES, LANES), lambda b, h, c: (0, h)),
        pl.BlockSpec((GATE_RANK, LANES), lambda b, h, c: (0, h)),
        pl.BlockSpec((bb, 2, RWKV_HEAD, RWKV_HEAD), lambda b, h, c: (b, h, 0, 0)),
        tok(hg0), tok(hg0 + nb), tok(hg0 + 2 * nb), tok(hg0 + 3 * nb, consumed),
        pl.BlockSpec((DEPTH + 1, LANES), lambda b, h, c: (0, h)),
        vec(0),
        pl.BlockSpec((bb, 1, HGRN_HEAD, HGRN_HEAD), lambda b, h, c: (b, h, 0, 0)),
    ]
    assert len(in_specs) == N_MIXER_IN
    out_specs = [
        pl.BlockSpec((bb, C, LANES), lambda b, h, c: (b, consumed(c), h)),
        pl.BlockSpec((bb, 2, RWKV_HEAD, RWKV_HEAD), lambda b, h, c: (b, h, 0, 0)),
        pl.BlockSpec((bb, C, LANES), lambda b, h, c: (b, consumed(c), h)),
        pl.BlockSpec((bb, 1, HGRN_HEAD, HGRN_HEAD), lambda b, h, c: (b, h, 0, 0)),
    ]
    out_shape = [jax.ShapeDtypeStruct((B, T, RWKV_WIDTH), F32),
                 jax.ShapeDtypeStruct((B, RWKV_HEADS, RWKV_HEAD, RWKV_HEAD), F32),
                 jax.ShapeDtypeStruct((B, T, HGRN_WIDTH), F32),
                 jax.ShapeDtypeStruct((B, HGRN_HEADS, HGRN_HEAD, HGRN_HEAD), F32)]
    scratch = [pltpu.VMEM((bb, LANES, LANES), F32),
               pltpu.VMEM((bb, 1, LANES), F32), pltpu.VMEM((bb, 1, LANES), F32),
               pltpu.VMEM((bb, 1, LANES), F32), pltpu.VMEM((bb, 1, 2 * LANES), F32),
               pltpu.VMEM((bb, HGRN_HEAD, HGRN_HEAD), F32)]
    assert len(scratch) == N_MIXER_STATE_SCRATCH
    if pipelined:
        scratch += [pltpu.VMEM((2, bb, rows * C if rows else 1, LANES), dt)
                    for _, rows, dt in RWKV_STASH + HGRN_STASH]
    return pl.pallas_call(
        functools.partial(_mixer_kernel, bb=bb, C=C, n_groups=n_groups, pipelined=pipelined),
        grid=grid, in_specs=in_specs, out_specs=out_specs, out_shape=out_shape,
        scratch_shapes=scratch,
        compiler_params=pltpu.CompilerParams(
            dimension_semantics=("arbitrary", "arbitrary", "arbitrary"),
            vmem_limit_bytes=VMEM_LIMIT),
        name="mixer",
    )(proj3, proj3, proj3, proj3, s_shift3, s_shift3, s_shift3, s_shift3,
      p["shift_mu"], p["shift_mu"], p["shift_mu"], p["shift_mu"],
      p["w0"], p["a0"], p["k_k"], p["k_a"], p["r_k"], p["ln_x_w"], p["ln_x_b"],
      p["w1u_pad"], p["a1u_pad"], p["g1u"], s_rwkv,
      proj3, proj3, proj3, proj3, p["lb_logits"], p["hg_norm_w"], s_hgrn)


def _layer_norm(x, g, b):
    mu = jnp.mean(x, axis=-1, keepdims=True)
    d = x - mu
    var = jnp.mean(d * d, axis=-1, keepdims=True)
    return d * lax.rsqrt(var + LN_EPS) * g + b


def _post_kernel(x_ref, orw_ref, ohg_ref, wo1_ref, wo2_ref, g1_ref, b1_ref, wup_ref, wdn_ref,
                 g2_ref, b2_ref, y_ref):
    mix = (jnp.dot(orw_ref[...].astype(BF16), wo1_ref[...], preferred_element_type=F32)
           + jnp.dot(ohg_ref[...].astype(BF16), wo2_ref[...], preferred_element_type=F32))
    h1 = _layer_norm(ALPHA * x_ref[...] + mix, g1_ref[...], b1_ref[...])
    up = jnp.dot(h1.astype(BF16), wup_ref[...], preferred_element_type=F32)
    up = jnp.square(jnp.maximum(up, 0.0))
    ff = jnp.dot(up.astype(BF16), wdn_ref[...], preferred_element_type=F32)
    y_ref[...] = _layer_norm(ALPHA * h1 + ff, g2_ref[...], b2_ref[...])


def _post(x2, orw2, ohg2, p, tm):
    n = x2.shape[0]

    def const(shape):
        return pl.BlockSpec(shape, lambda i: (0, 0), pipeline_mode=pl.Buffered(1))

    return pl.pallas_call(
        _post_kernel,
        grid=(n // tm,),
        in_specs=[pl.BlockSpec((tm, D_MODEL), lambda i: (i, 0)),
                  pl.BlockSpec((tm, RWKV_WIDTH), lambda i: (i, 0)),
                  pl.BlockSpec((tm, HGRN_WIDTH), lambda i: (i, 0)),
                  const((RWKV_WIDTH, D_MODEL)), const((HGRN_WIDTH, D_MODEL)),
                  const((1, D_MODEL)), const((1, D_MODEL)),
                  const((D_MODEL, D_FF)), const((D_FF, D_MODEL)),
                  const((1, D_MODEL)), const((1, D_MODEL))],
        out_specs=pl.BlockSpec((tm, D_MODEL), lambda i: (i, 0)),
        out_shape=jax.ShapeDtypeStruct((n, D_MODEL), F32),
        compiler_params=pltpu.CompilerParams(dimension_semantics=("arbitrary",),
                                             vmem_limit_bytes=VMEM_LIMIT),
        name="post",
    )(x2, orw2, ohg2, p["wo_rw"], p["wo_hg"], p["ln1_g"], p["ln1_b"], p["w_up"], p["w_down"],
      p["ln2_g"], p["ln2_b"])


def _prep_params(w_in, shift_mu, w0, w1u, a0, a1u, g1u, k_k, k_a, r_k, ln_x_w, ln_x_b, lb_logits,
                 hg_norm_w, w_out, ln1_g, ln1_b, w_up, w_down, ln2_g, ln2_b):
    zw = jnp.zeros((LANES - DECAY_RANK, RWKV_WIDTH), F32)
    za = jnp.zeros((LANES - AICL_RANK, RWKV_WIDTH), F32)
    w_out_bf = w_out[0].astype(BF16)
    return {
        "w_in": w_in[0].astype(BF16),
        "shift_mu": shift_mu[0].reshape(1, RWKV_PROJ),
        "w0": w0[0].reshape(1, RWKV_WIDTH), "a0": a0[0].reshape(1, RWKV_WIDTH),
        "k_k": k_k[0].reshape(1, RWKV_WIDTH), "k_a": k_a[0].reshape(1, RWKV_WIDTH),
        "r_k": r_k[0].reshape(1, RWKV_WIDTH),
        "ln_x_w": ln_x_w[0].reshape(1, RWKV_WIDTH), "ln_x_b": ln_x_b[0].reshape(1, RWKV_WIDTH),
        "w1u_pad": jnp.concatenate([w1u[0], zw], axis=0),
        "a1u_pad": jnp.concatenate([za, a1u[0]], axis=0),
        "g1u": g1u[0],
        "lb_logits": lb_logits.astype(F32),
        "hg_norm_w": hg_norm_w[0].reshape(1, HGRN_WIDTH),
        "wo_rw": w_out_bf[:RWKV_WIDTH], "wo_hg": w_out_bf[RWKV_WIDTH:],
        "ln1_g": ln1_g[0].reshape(1, D_MODEL), "ln1_b": ln1_b[0].reshape(1, D_MODEL),
        "w_up": w_up[0].astype(BF16), "w_down": w_down[0].astype(BF16),
        "ln2_g": ln2_g[0].reshape(1, D_MODEL), "ln2_b": ln2_b[0].reshape(1, D_MODEL),
    }


def _run_group(x, s_rwkv, s_hgrn, s_shift, p, *, tm, bb, chunk, n_groups, pipelined):
    B, T, _ = x.shape
    x2 = x.reshape(B * T, D_MODEL)
    proj2 = _proj(x2, p["w_in"], tm)
    proj3 = proj2.reshape(B, T, PROJ)
    o_rw, st_rw, o_hg, st_hg = _mixer(proj3, s_shift.reshape(B, 1, RWKV_PROJ), s_rwkv, s_hgrn, p,
                                      bb, chunk, n_groups, pipelined)
    y2 = _post(x2, o_rw.reshape(B * T, RWKV_WIDTH), o_hg.reshape(B * T, HGRN_WIDTH), p, tm)
    sh = proj3[:, T - 1, :RWKV_PROJ]
    return y2.reshape(B, T, D_MODEL), st_rw[None], st_hg[None], sh[None]


PROMPT_CFG = dict(tm=512, bb=8, chunk=64, n_groups=2, pipelined=True)
SAMPLE_CFG = dict(tm=512, bb=16, chunk=8, n_groups=2, pipelined=False)


def kernel(x_prompt, x_sample, state_rwkv, state_hgrn, state_shift, w_in, shift_mu, w0, w1u, a0, a1u, g1u, k_k, k_a, r_k, ln_x_w, ln_x_b, lb_logits, hg_norm_w, w_out, ln1_g, ln1_b, w_up, w_down, ln2_g, ln2_b):
    assert w_in.shape[0] == DEPTH
    p = _prep_params(w_in, shift_mu, w0, w1u, a0, a1u, g1u, k_k, k_a, r_k, ln_x_w, ln_x_b, lb_logits,
                     hg_norm_w, w_out, ln1_g, ln1_b, w_up, w_down, ln2_g, ln2_b)
    bp = x_prompt.shape[0]
    z_rw = jnp.zeros((bp, RWKV_HEADS, RWKV_HEAD, RWKV_HEAD), F32)
    z_hg = jnp.zeros((bp, HGRN_HEADS, HGRN_HEAD, HGRN_HEAD), F32)
    z_sh = jnp.zeros((bp, RWKV_PROJ), F32)
    y_p, rw_p, hg_p, sh_p = _run_group(x_prompt, z_rw, z_hg, z_sh, p, **PROMPT_CFG)
    y_s, rw_s, hg_s, sh_s = _run_group(x_sample, state_rwkv[0].astype(F32), state_hgrn[0].astype(F32),
                                       state_shift[0].astype(F32), p, **SAMPLE_CFG)
    return (y_p, y_s, rw_p, rw_s, hg_p, hg_s, sh_p, sh_s)
```

```python
import functools
import math

import jax
import jax.numpy as jnp
from jax import lax
from jax.experimental import pallas as pl
from jax.experimental.pallas import tpu as pltpu

F32 = jnp.float32
BF16 = jnp.bfloat16

D_MODEL = 1024
RWKV_WIDTH = 512
RWKV_HEAD = 64
RWKV_HEADS = 8
HGRN_WIDTH = 512
HGRN_HEAD = 128
HGRN_HEADS = 4
DECAY_RANK = 64
AICL_RANK = 64
GATE_RANK = 128
RWKV_PROJ = 3 * RWKV_WIDTH + DECAY_RANK + AICL_RANK + GATE_RANK
HGRN_PROJ = 4 * HGRN_WIDTH
PROJ = RWKV_PROJ + HGRN_PROJ
D_FF = 4 * D_MODEL
DEPTH = 1
ALPHA = (2.0 * DEPTH) ** 0.25
LN_EPS = 1e-5
GN_EPS = RWKV_HEAD * 1e-5
RMS_EPS = 1e-6
DECAY_SCALE = math.exp(-0.5)

LANES = 128
SUBLANES = 8
PAIRS = RWKV_HEADS // 2
LORA_COL = 3 * RWKV_WIDTH
VMEM_LIMIT = 56 * 1024 * 1024

NN = ((1,), (0,))
NT = ((1,), (1,))
TN = ((0,), (0,))

RWKV_STASH = (("lhs", 4, BF16), ("rhs", 2, BF16), ("bt_hat", 2, BF16), ("atrt", 2, BF16), ("vb", 1, BF16),
              ("bkh", 2, BF16), ("dec", 0, F32), ("bonus", 1, F32), ("gate", 1, F32))
HGRN_STASH = (("qs", 1, BF16), ("ks", 1, BF16), ("iv", 1, BF16), ("q", 1, F32), ("kin", 1, F32),
              ("bc", 1, F32), ("dec", 0, F32))


def _dot(a, b, dims):
    return lax.dot_general(a, b, (dims, ((), ())), preferred_element_type=F32)


def _bf(x):
    return x.astype(BF16)


def _cumsum_rows(tri, x):
    p0 = _bf(x)
    r1 = x - p0.astype(F32)
    p1 = _bf(r1)
    p2 = _bf(r1 - p1.astype(F32))
    return _dot(tri, p0, NN) + _dot(tri, p1, NN) + _dot(tri, p2, NN)


def _sigmoid(x):
    return 0.5 * jnp.tanh(0.5 * x) + 0.5


def _interleave(tasks):
    gens = list(tasks)
    nxt = [next(g, None) for g in gens]
    spent = {"M": 0.0, "V": 0.0}
    turn = 0
    while any(k is not None for k in nxt):
        ready = {kind: [j for j, k in enumerate(nxt) if k is not None and k[0] == kind] for kind in spent}
        want = "M" if (ready["M"] and (spent["M"] <= spent["V"] or not ready["V"])) else "V"
        if want == "M":
            cands = ready["M"]
            t = cands[turn % len(cands)]
            turn += 1
        else:
            t = ready["V"][0]
        spent[want] += nxt[t][1]
        nxt[t] = next(gens[t], None)


def _chain(*gens):
    for g in gens:
        yield from g


class _ValueStash:
    def __init__(self):
        self.vals = {}

    def put(self, name, val):
        self.vals[name] = val

    def get(self, name, j=None):
        return self.vals[name] if j is None else self.vals[name][j]


class _RefStash:
    def __init__(self, refs, slot, lo, n):
        self.refs, self.slot, self.lo, self.n = refs, slot, lo, n

    def put(self, name, val):
        self.refs[name][self.slot, self.lo:self.lo + self.n] = val

    def get(self, name, j=None):
        if j is None:
            return self.refs[name][self.slot, self.lo:self.lo + self.n]
        return self.refs[name][self.slot, self.lo + j]


def _proj_kernel(x_ref, w_ref, o_ref):
    o_ref[...] = jnp.dot(x_ref[...].astype(BF16), w_ref[...], preferred_element_type=F32)


def _proj(x2, w_in_bf, tm):
    n = x2.shape[0]
    return pl.pallas_call(
        _proj_kernel,
        grid=(n // tm,),
        in_specs=[pl.BlockSpec((tm, D_MODEL), lambda i: (i, 0)),
                  pl.BlockSpec((D_MODEL, PROJ), lambda i: (0, 0), pipeline_mode=pl.Buffered(1))],
        out_specs=pl.BlockSpec((tm, PROJ), lambda i: (i, 0)),
        out_shape=jax.ShapeDtypeStruct((n, PROJ), F32),
        compiler_params=pltpu.CompilerParams(dimension_semantics=("arbitrary",),
                                             vmem_limit_bytes=VMEM_LIMIT),
        name="proj",
    )(x2, w_in_bf)


def _head_helpers(C):
    lane = lax.broadcasted_iota(jnp.int32, (C, LANES), 1)
    m0 = lane < RWKV_HEAD

    def head_sum(x):
        s0 = jnp.sum(jnp.where(m0, x, 0.0), axis=-1, keepdims=True)
        s1 = jnp.sum(jnp.where(m0, 0.0, x), axis=-1, keepdims=True)
        return jnp.where(m0, s0, s1)

    def stack_heads(x):
        return jnp.concatenate([jnp.where(m0, x, 0.0), jnp.where(m0, 0.0, x)], axis=-2)

    def merge_heads(x):
        return jnp.where(m0, x[:C], x[C:])

    return head_sum, stack_heads, merge_heads


def _rwkv_produce(stash, lo, n, C, pr_ref, pk_ref, pv_ref, pl_ref, mur_ref, muk_ref, muv_ref, mul_ref,
                  w0_ref, a0_ref, kk_ref, ka_ref, rk_ref, w1_ref, a1_ref, g1_ref,
                  cr_scr, ck_scr, cv_scr, cl_scr):
    ids = range(n)
    grp = slice(lo, lo + n)
    head_sum, stack_heads, _ = _head_helpers(C)
    rowc = lax.broadcasted_iota(jnp.int32, (C, C), 0)
    colc = lax.broadcasted_iota(jnp.int32, (C, C), 1)
    tri = jnp.where(rowc >= colc, 1.0, 0.0).astype(BF16)

    def shifted(p_ref, carry_ref, mu_ref):
        p = p_ref[grp]
        row = lax.broadcasted_iota(jnp.int32, p.shape, 1)
        flat = p.reshape(n * C, p.shape[-1])
        prev = jnp.where(row == 0, carry_ref[grp], pltpu.roll(flat, 1, 0).reshape(p.shape))
        carry_ref[grp] = p[:, C - 1:C, :]
        return p + mu_ref[...] * (prev - p)

    w1 = _bf(w1_ref[...])
    a1 = _bf(a1_ref[...])
    g1 = _bf(g1_ref[...])

    yield ("V", 60.0 * n)
    r = shifted(pr_ref, cr_scr, mur_ref)
    xk = shifted(pk_ref, ck_scr, muk_ref)
    v = shifted(pv_ref, cv_scr, muv_ref)
    xl = shifted(pl_ref, cl_scr, mul_ref).reshape(n * C, 2 * LANES)
    xl_lo = xl[:, :LANES]
    dw = jnp.dot(_bf(jnp.tanh(xl_lo)), w1, preferred_element_type=F32)
    da = jnp.dot(_bf(xl_lo), a1, preferred_element_type=F32)
    gate = jnp.dot(_bf(_sigmoid(xl[:, LANES:])), g1, preferred_element_type=F32)
    stash.put("gate", gate.reshape(n, C, LANES))

    yield ("V", 60.0 * n)
    lw = (-DECAY_SCALE * _sigmoid(w0_ref[...] + dw)).reshape(n, C, LANES)
    a_lr = _sigmoid(a0_ref[...] + da).reshape(n, C, LANES)
    kk = xk * kk_ref[...]
    kk = kk * jnp.minimum(lax.rsqrt(head_sum(kk * kk)), 1e12)
    k2 = xk * (1.0 + (a_lr - 1.0) * ka_ref[...])
    b = kk * a_lr
    cum = jnp.stack([_cumsum_rows(tri, lw[j]) for j in ids])

    yield ("V", 80.0 * n)
    cum_last = cum[:, C - 1:C, :]
    e_in = jnp.exp(-cum)
    e_out = jnp.exp(cum_last - cum)
    at = -kk * jnp.exp(cum - lw)
    rt = r * jnp.exp(cum)
    bt = b * e_in
    kt = k2 * e_in
    stash.put("lhs", _bf(jnp.concatenate([stack_heads(at), stack_heads(rt)], axis=1)))
    stash.put("rhs", _bf(jnp.concatenate([bt, kt], axis=1)))
    stash.put("bt_hat", _bf(stack_heads(bt)))
    stash.put("atrt", _bf(jnp.concatenate([at, rt], axis=1)))
    stash.put("vb", _bf(v))
    stash.put("bkh", _bf(jnp.concatenate([b * e_out, k2 * e_out], axis=1)))
    stash.put("dec", jnp.exp(cum_last))
    stash.put("bonus", head_sum(r * k2 * rk_ref[...]) * v)


def _rwkv_consume(stash, lo, n, C, dot_cost, lnw_ref, lnb_ref, o_ref, s_scr):
    H = RWKV_HEAD
    ids = range(n)
    grp = slice(lo, lo + n)
    head_sum, stack_heads, merge_heads = _head_helpers(C)
    r2 = lax.broadcasted_iota(jnp.int32, (2 * C, 2 * C), 0)
    c2 = lax.broadcasted_iota(jnp.int32, (2 * C, 2 * C), 1)
    t2 = jnp.bitwise_and(r2, C - 1)
    s2 = jnp.bitwise_and(c2, C - 1)
    strict2 = t2 > s2
    incl2 = t2 >= s2
    strict_bd = jnp.logical_and(strict2, (r2 >= C) == (c2 >= C))
    eye2 = jnp.where(r2 == c2, 1.0, 0.0).astype(F32)
    bl_r = lax.broadcasted_iota(jnp.int32, (LANES, LANES), 0) < H
    bl_c = lax.broadcasted_iota(jnp.int32, (LANES, LANES), 1) < H
    blockdiag = bl_r == bl_c
    zeros_c = jnp.zeros((C, LANES), BF16)

    yield ("M", dot_cost)
    g_a, g_r = [], []
    for j in ids:
        g = _dot(stash.get("lhs", j), stash.get("rhs", j), NT)
        g_a.append(_bf(jnp.where(strict2, g[:2 * C], 0.0)))
        g_r.append(_bf(jnp.where(incl2, g[2 * C:], 0.0)))
    yield ("M", dot_cost)
    pw, inv = [], []
    for j in ids:
        n_bd = jnp.where(strict_bd, _dot(stash.get("lhs", j)[:2 * C], stash.get("bt_hat", j), NT), 0.0)
        inv.append(eye2 + n_bd)
        pw.append(_bf(n_bd))
    span = 2
    while span < C:
        yield ("M", dot_cost)
        pw = [_bf(_dot(pw[j], pw[j], NN)) for j in ids]
        yield ("M", dot_cost)
        inv = [inv[j] + _dot(_bf(inv[j]), pw[j], NN) for j in ids]
        span *= 2
    yield ("M", dot_cost)
    S = [s_scr[lo + j] for j in ids]
    pq = [_dot(stash.get("atrt", j), _bf(S[j]), NT) for j in ids]
    yield ("M", dot_cost)
    vb = [stash.get("vb", j) for j in ids]
    w_hat = [_bf(stack_heads(pq[j][:C] + merge_heads(_dot(g_a[j], jnp.concatenate([zeros_c, vb[j]], axis=0), NN))))
             for j in ids]
    yield ("M", dot_cost)
    z = []
    for j in ids:
        u2 = _dot(_bf(inv[j]), w_hat[j], NN)
        z.append(jnp.concatenate([_bf(u2[:C] + u2[C:]), vb[j]], axis=0))
    yield ("M", dot_cost)
    y = [pq[j][C:] + merge_heads(_dot(g_r[j], z[j], NN)) for j in ids]
    yield ("M", dot_cost)
    for j in ids:
        upd = _dot(z[j], stash.get("bkh", j), TN)
        s_scr[lo + j] = S[j] * stash.get("dec", j) + jnp.where(blockdiag, upd, 0.0)

    yield ("V", 40.0 * n)
    y = jnp.stack(y)
    mu = head_sum(y) * (1.0 / H)
    dy = y - mu
    var = head_sum(dy * dy) * (1.0 / H)
    yn = dy * lax.rsqrt(var + GN_EPS) * lnw_ref[...] + lnb_ref[...]
    o_ref[grp] = (yn + stash.get("bonus")) * stash.get("gate")


def _mid_rows(x, h):
    R = x.shape[0]
    row = lax.broadcasted_iota(jnp.int32, x.shape, 0)
    if 2 * h >= SUBLANES:
        return jnp.concatenate(
            [jnp.broadcast_to(x[b * 2 * h + h - 1:b * 2 * h + h], (2 * h, LANES)) for b in range(R // (2 * h))],
            axis=0)
    if h == 1:
        return jnp.where(jnp.bitwise_and(row, 1) == 1, pltpu.roll(x, 1, 0), x)
    picks = []
    for half in range(SUBLANES // (2 * h)):
        r0 = half * 2 * h + h - 1
        picks.append(jnp.concatenate(
            [jnp.broadcast_to(x[g * SUBLANES + r0:g * SUBLANES + r0 + 1], (SUBLANES, LANES))
             for g in range(R // SUBLANES)], axis=0))
    out = picks[-1]
    sub = jnp.bitwise_and(row, SUBLANES - 1)
    for half in range(len(picks) - 2, -1, -1):
        out = jnp.where(sub < (half + 1) * 2 * h, picks[half], out)
    return out


def _hgrn_produce(stash, lo, n, C, q_ref, f_ref, i_ref, lbl_ref):
    ids = range(n)
    grp = slice(lo, lo + n)
    logits = lbl_ref[...]
    ex = jnp.exp(logits - jnp.max(logits, axis=0, keepdims=True))
    lb = ex[0:1] / jnp.sum(ex, axis=0, keepdims=True)
    t_i = lax.broadcasted_iota(jnp.int32, (C, C), 0)
    s_i = lax.broadcasted_iota(jnp.int32, (C, C), 1)
    tri = jnp.where(t_i >= s_i, 1.0, 0.0).astype(BF16)

    yield ("V", 40.0 * n)
    q_h = q_ref[grp]
    f_h = f_ref[grp]
    q = q_h * _sigmoid(q_h)
    fg = lb + (1.0 - lb) * _sigmoid(f_h)
    kin = 1.0 - fg
    lf = jnp.log(fg)
    bc = jnp.stack([_cumsum_rows(tri, lf[j]) for j in ids])

    yield ("V", 30.0 * n)
    b_last = bc[:, C - 1:C, :]
    stash.put("qs", _bf(q * jnp.exp(bc)))
    stash.put("ks", _bf(kin * jnp.exp(b_last - bc)))
    stash.put("iv", _bf(i_ref[grp]))
    stash.put("q", q)
    stash.put("kin", kin)
    stash.put("bc", bc)
    stash.put("dec", jnp.exp(b_last))


def _hgrn_consume(stash, lo, n, C, dot_cost, g_ref, hgw_ref, o_ref, s_scr):
    ids = range(n)
    grp = slice(lo, lo + n)
    t_i = lax.broadcasted_iota(jnp.int32, (C, C), 0)
    s_i = lax.broadcasted_iota(jnp.int32, (C, C), 1)
    halves = []
    h = C // 2
    while h >= 1:
        halves.append(h)
        h //= 2

    def level_mask(h):
        same = jnp.bitwise_and(t_i, -2 * h) == jnp.bitwise_and(s_i, -2 * h)
        return jnp.logical_and(same, jnp.logical_and(jnp.bitwise_and(t_i, h) != 0, jnp.bitwise_and(s_i, h) == 0))

    def seq(x, j):
        return _bf(x[j * C:(j + 1) * C])

    yield ("M", dot_cost)
    S = [s_scr[lo + j] for j in ids]
    o_state = [_dot(stash.get("qs", j), _bf(S[j]), NT) for j in ids]
    yield ("M", dot_cost)
    iv = [stash.get("iv", j) for j in ids]
    for j in ids:
        s_scr[lo + j] = S[j] * stash.get("dec", j) + _dot(iv[j], stash.get("ks", j), TN)
    yield ("M", dot_cost)
    q = stash.get("q").reshape(n * C, LANES)
    kin = stash.get("kin").reshape(n * C, LANES)
    bc = stash.get("bc").reshape(n * C, LANES)
    att = [jnp.where(t_i == s_i, _dot(seq(q, j), seq(kin, j), NT), 0.0) for j in ids]
    for h in halves:
        yield ("V", 25.0 * n)
        dmid = jnp.exp(-jnp.abs(bc - _mid_rows(bc, h)))
        ql = q * dmid
        kl = kin * dmid
        yield ("M", dot_cost)
        msk = level_mask(h)
        att = [jnp.where(msk, _dot(seq(ql, j), seq(kl, j), NT), att[j]) for j in ids]
    yield ("M", dot_cost)
    o_intra = [_dot(_bf(att[j]), iv[j], NN) for j in ids]

    yield ("V", 25.0 * n)
    o = jnp.stack([o_intra[j] + o_state[j] for j in ids])
    ms = jnp.mean(o * o, axis=-1, keepdims=True)
    g_h = g_ref[grp]
    o_ref[grp] = o * lax.rsqrt(ms + RMS_EPS) * hgw_ref[...] * (g_h * _sigmoid(g_h))


MXU_STAGE_COST = {True: (20.0, 14.0), False: (30.0, 20.0)}
N_MIXER_IN = 30
N_MIXER_OUT = 4
N_MIXER_STATE_SCRATCH = 6


def _mixer_kernel(*refs, bb, C, n_groups, pipelined):
    (pr_ref, pk_ref, pv_ref, pl_ref, shr_ref, shk_ref, shv_ref, shl_ref,
     mur_ref, muk_ref, muv_ref, mul_ref, w0_ref, a0_ref, kk_ref, ka_ref, rk_ref,
     lnw_ref, lnb_ref, w1_ref, a1_ref, g1_ref, srw0_ref,
     q_ref, f_ref, i_ref, g_ref, lbl_ref, hgw_ref, shg0_ref) = refs[:N_MIXER_IN]
    orw_ref, strw_ref, ohg_ref, sthg_ref = refs[N_MIXER_IN:N_MIXER_IN + N_MIXER_OUT]
    scr = refs[N_MIXER_IN + N_MIXER_OUT:]
    srw_scr, cr_scr, ck_scr, cv_scr, cl_scr, shg_scr = scr[:N_MIXER_STATE_SCRATCH]
    stash_refs = scr[N_MIXER_STATE_SCRATCH:]
    if pipelined:
        rw_refs = {name: stash_refs[k] for k, (name, _, _) in enumerate(RWKV_STASH)}
        hg_refs = {name: stash_refs[len(RWKV_STASH) + k] for k, (name, _, _) in enumerate(HGRN_STASH)}

    c = pl.program_id(2)
    last = pl.num_programs(2) - 1
    H = RWKV_HEAD

    @pl.when(c == 0)
    def _init():
        z = jnp.zeros((H, H), F32)
        for i in range(bb):
            top = jnp.concatenate([srw0_ref[i, 0], z], axis=1)
            bot = jnp.concatenate([z, srw0_ref[i, 1]], axis=1)
            srw_scr[i] = jnp.concatenate([top, bot], axis=0)
            shg_scr[i] = shg0_ref[i, 0].T
        cr_scr[...] = shr_ref[...]
        ck_scr[...] = shk_ref[...]
        cv_scr[...] = shv_ref[...]
        cl_scr[...] = shl_ref[...]
        if pipelined:
            for table in (rw_refs, hg_refs):
                for name, ref in table.items():
                    fill = jnp.ones if name == "dec" else jnp.zeros
                    ref[1] = fill(ref.shape[1:], ref.dtype)

    per = bb // n_groups
    groups = [(gidx * per, per) for gidx in range(n_groups)]

    def rwkv_produce(stash, lo, n):
        return _rwkv_produce(stash, lo, n, C, pr_ref, pk_ref, pv_ref, pl_ref, mur_ref, muk_ref, muv_ref, mul_ref,
                             w0_ref, a0_ref, kk_ref, ka_ref, rk_ref, w1_ref, a1_ref, g1_ref,
                             cr_scr, ck_scr, cv_scr, cl_scr)

    def rwkv_consume(stash, lo, n):
        return _rwkv_consume(stash, lo, n, C, MXU_STAGE_COST[pipelined][0] * n, lnw_ref, lnb_ref, orw_ref, srw_scr)

    def hgrn_produce(stash, lo, n):
        return _hgrn_produce(stash, lo, n, C, q_ref, f_ref, i_ref, lbl_ref)

    def hgrn_consume(stash, lo, n):
        return _hgrn_consume(stash, lo, n, C, MXU_STAGE_COST[pipelined][1] * n, g_ref, hgw_ref, ohg_ref, shg_scr)

    if not pipelined:
        tasks = []
        for lo, n in groups:
            rw, hg = _ValueStash(), _ValueStash()
            tasks.append(_chain(rwkv_produce(rw, lo, n), rwkv_consume(rw, lo, n)))
            tasks.append(_chain(hgrn_produce(hg, lo, n), hgrn_consume(hg, lo, n)))
        _interleave(tasks)
    else:
        def step(write_slot):
            read_slot = 1 - write_slot
            tasks = []
            for lo, n in groups:
                tasks.append(rwkv_consume(_RefStash(rw_refs, read_slot, lo, n), lo, n))
                tasks.append(hgrn_consume(_RefStash(hg_refs, read_slot, lo, n), lo, n))
            for lo, n in groups:
                tasks.append(rwkv_produce(_RefStash(rw_refs, write_slot, lo, n), lo, n))
                tasks.append(hgrn_produce(_RefStash(hg_refs, write_slot, lo, n), lo, n))
            _interleave(tasks)

        parity = lax.rem(c, 2)
        pl.when(parity == 0)(functools.partial(step, 0))
        pl.when(parity == 1)(functools.partial(step, 1))

    @pl.when(c == last)
    def _fin():
        for i in range(bb):
            s_fin = srw_scr[i]
            strw_ref[i, 0] = s_fin[:H, :H]
            strw_ref[i, 1] = s_fin[H:, H:]
            sthg_ref[i, 0] = shg_scr[i].T


def _mixer(proj3, s_shift3, s_rwkv, s_hgrn, p, bb, C, n_groups, pipelined):
    B, T, _ = proj3.shape
    nc = T // C
    grid = (B // bb, PAIRS, nc + 1 if pipelined else nc)
    nb = RWKV_WIDTH // LANES
    hg0 = RWKV_PROJ // LANES
    lora_blk = LORA_COL // (2 * LANES)

    if pipelined:
        def produced(c):
            return jnp.minimum(c, nc - 1)

        def consumed(c):
            return jnp.maximum(c - 1, 0)
    else:
        produced = consumed = lambda c: c

    def tok(col0, chunk=produced):
        return pl.BlockSpec((bb, C, LANES), lambda b, h, c: (b, chunk(c), col0 + h))

    def first(col0):
        return pl.BlockSpec((bb, 1, LANES), lambda b, h, c: (b, 0, col0 + h))

    def vec(col0):
        return pl.BlockSpec((1, LANES), lambda b, h, c: (0, col0 + h))

    in_specs = [
        tok(0), tok(nb), tok(2 * nb),
        pl.BlockSpec((bb, C, 2 * LANES), lambda b, h, c: (b, produced(c), lora_blk)),
        first(0), first(nb), first(2 * nb),
        pl.BlockSpec((bb, 1, 2 * LANES), lambda b, h, c: (b, 0, lora_blk)),
        vec(0), vec(nb), vec(2 * nb),
        pl.BlockSpec((1, 2 * LANES), lambda b, h, c: (0, lora_blk)),
        vec(0), vec(0), vec(0), vec(0), vec(0), vec(0), vec(0),
        pl.BlockSpec((LANES, LANES), lambda b, h, c: (0, h)),
        pl.BlockSpec((LANES, LANES), lambda b, h, c: (0, h)),
        pl.BlockSpec((GATE_RANK, LANES), lambda b, h, c: (0, h)),
        pl.BlockSpec((bb, 2, RWKV_HEAD, RWKV_HEAD), lambda b, h, c: (b, h, 0, 0)),
        tok(hg0), tok(hg0 + nb), tok(hg0 + 2 * nb), tok(hg0 + 3 * nb, consumed),
        pl.BlockSpec((DEPTH + 1, LANES), lambda b, h, c: (0, h)),
        vec(0),
        pl.BlockSpec((bb, 1, HGRN_HEAD, HGRN_HEAD), lambda b, h, c: (b, h, 0, 0)),
    ]
    assert len(in_specs) == N_MIXER_IN
    out_specs = [
        pl.BlockSpec((bb, C, LANES), lambda b, h, c: (b, consumed(c), h)),
        pl.BlockSpec((bb, 2, RWKV_HEAD, RWKV_HEAD), lambda b, h, c: (b, h, 0, 0)),
        pl.BlockSpec((bb, C, LANES), lambda b, h, c: (b, consumed(c), h)),
        pl.BlockSpec((bb, 1, HGRN_HEAD, HGRN_HEAD), lambda b, h, c: (b, h, 0, 0)),
    ]
    out_shape = [jax.ShapeDtypeStruct((B, T, RWKV_WIDTH), F32),
                 jax.ShapeDtypeStruct((B, RWKV_HEADS, RWKV_HEAD, RWKV_HEAD), F32),
                 jax.ShapeDtypeStruct((B, T, HGRN_WIDTH), F32),
                 jax.ShapeDtypeStruct((B, HGRN_HEADS, HGRN_HEAD, HGRN_HEAD), F32)]
    scratch = [pltpu.VMEM((bb, LANES, LANES), F32),
               pltpu.VMEM((bb, 1, LANES), F32), pltpu.VMEM((bb, 1, LANES), F32),
               pltpu.VMEM((bb, 1, LANES), F32), pltpu.VMEM((bb, 1, 2 * LANES), F32),
               pltpu.VMEM((bb, HGRN_HEAD, HGRN_HEAD), F32)]
    assert len(scratch) == N_MIXER_STATE_SCRATCH
    if pipelined:
        scratch += [pltpu.VMEM((2, bb, rows * C if rows else 1, LANES), dt)
                    for _, rows, dt in RWKV_STASH + HGRN_STASH]
    return pl.pallas_call(
        functools.partial(_mixer_kernel, bb=bb, C=C, n_groups=n_groups, pipelined=pipelined),
        grid=grid, in_specs=in_specs, out_specs=out_specs, out_shape=out_shape,
        scratch_shapes=scratch,
        compiler_params=pltpu.CompilerParams(
            dimension_semantics=("arbitrary", "arbitrary", "arbitrary"),
            vmem_limit_bytes=VMEM_LIMIT),
        name="mixer",
    )(proj3, proj3, proj3, proj3, s_shift3, s_shift3, s_shift3, s_shift3,
      p["shift_mu"], p["shift_mu"], p["shift_mu"], p["shift_mu"],
      p["w0"], p["a0"], p["k_k"], p["k_a"], p["r_k"], p["ln_x_w"], p["ln_x_b"],
      p["w1u_pad"], p["a1u_pad"], p["g1u"], s_rwkv,
      proj3, proj3, proj3, proj3, p["lb_logits"], p["hg_norm_w"], s_hgrn)


def _layer_norm(x, g, b):
    mu = jnp.mean(x, axis=-1, keepdims=True)
    d = x - mu
    var = jnp.mean(d * d, axis=-1, keepdims=True)
    return d * lax.rsqrt(var + LN_EPS) * g + b


def _post_kernel(x_ref, orw_ref, ohg_ref, wo1_ref, wo2_ref, g1_ref, b1_ref, wup_ref, wdn_ref,
                 g2_ref, b2_ref, y_ref):
    mix = (jnp.dot(orw_ref[...].astype(BF16), wo1_ref[...], preferred_element_type=F32)
           + jnp.dot(ohg_ref[...].astype(BF16), wo2_ref[...], preferred_element_type=F32))
    h1 = _layer_norm(ALPHA * x_ref[...] + mix, g1_ref[...], b1_ref[...])
    up = jnp.dot(h1.astype(BF16), wup_ref[...], preferred_element_type=F32)
    up = jnp.square(jnp.maximum(up, 0.0))
    ff = jnp.dot(up.astype(BF16), wdn_ref[...], preferred_element_type=F32)
    y_ref[...] = _layer_norm(ALPHA * h1 + ff, g2_ref[...], b2_ref[...])


def _post(x2, orw2, ohg2, p, tm):
    n = x2.shape[0]

    def const(shape):
        return pl.BlockSpec(shape, lambda i: (0, 0), pipeline_mode=pl.Buffered(1))

    return pl.pallas_call(
        _post_kernel,
        grid=(n // tm,),
        in_specs=[pl.BlockSpec((tm, D_MODEL), lambda i: (i, 0)),
                  pl.BlockSpec((tm, RWKV_WIDTH), lambda i: (i, 0)),
                  pl.BlockSpec((tm, HGRN_WIDTH), lambda i: (i, 0)),
                  const((RWKV_WIDTH, D_MODEL)), const((HGRN_WIDTH, D_MODEL)),
                  const((1, D_MODEL)), const((1, D_MODEL)),
                  const((D_MODEL, D_FF)), const((D_FF, D_MODEL)),
                  const((1, D_MODEL)), const((1, D_MODEL))],
        out_specs=pl.BlockSpec((tm, D_MODEL), lambda i: (i, 0)),
        out_shape=jax.ShapeDtypeStruct((n, D_MODEL), F32),
        compiler_params=pltpu.CompilerParams(dimension_semantics=("arbitrary",),
                                             vmem_limit_bytes=VMEM_LIMIT),
        name="post",
    )(x2, orw2, ohg2, p["wo_rw"], p["wo_hg"], p["ln1_g"], p["ln1_b"], p["w_up"], p["w_down"],
      p["ln2_g"], p["ln2_b"])


def _prep_params(w_in, shift_mu, w0, w1u, a0, a1u, g1u, k_k, k_a, r_k, ln_x_w, ln_x_b, lb_logits,
                 hg_norm_w, w_out, ln1_g, ln1_b, w_up, w_down, ln2_g, ln2_b):
    zw = jnp.zeros((LANES - DECAY_RANK, RWKV_WIDTH), F32)
    za = jnp.zeros((LANES - AICL_RANK, RWKV_WIDTH), F32)
    w_out_bf = w_out[0].astype(BF16)
    return {
        "w_in": w_in[0].astype(BF16),
        "shift_mu": shift_mu[0].reshape(1, RWKV_PROJ),
        "w0": w0[0].reshape(1, RWKV_WIDTH), "a0": a0[0].reshape(1, RWKV_WIDTH),
        "k_k": k_k[0].reshape(1, RWKV_WIDTH), "k_a": k_a[0].reshape(1, RWKV_WIDTH),
        "r_k": r_k[0].reshape(1, RWKV_WIDTH),
        "ln_x_w": ln_x_w[0].reshape(1, RWKV_WIDTH), "ln_x_b": ln_x_b[0].reshape(1, RWKV_WIDTH),
        "w1u_pad": jnp.concatenate([w1u[0], zw], axis=0),
        "a1u_pad": jnp.concatenate([za, a1u[0]], axis=0),
        "g1u": g1u[0],
        "lb_logits": lb_logits.astype(F32),
        "hg_norm_w": hg_norm_w[0].reshape(1, HGRN_WIDTH),
        "wo_rw": w_out_bf[:RWKV_WIDTH], "wo_hg": w_out_bf[RWKV_WIDTH:],
        "ln1_g": ln1_g[0].reshape(1, D_MODEL), "ln1_b": ln1_b[0].reshape(1, D_MODEL),
        "w_up": w_up[0].astype(BF16), "w_down": w_down[0].astype(BF16),
        "ln2_g": ln2_g[0].reshape(1, D_MODEL), "ln2_b": ln2_b[0].reshape(1, D_MODEL),
    }


def _run_group(x, s_rwkv, s_hgrn, s_shift, p, *, tm, bb, chunk, n_groups, pipelined):
    B, T, _ = x.shape
    x2 = x.reshape(B * T, D_MODEL)
    proj2 = _proj(x2, p["w_in"], tm)
    proj3 = proj2.reshape(B, T, PROJ)
    o_rw, st_rw, o_hg, st_hg = _mixer(proj3, s_shift.reshape(B, 1, RWKV_PROJ), s_rwkv, s_hgrn, p,
                                      bb, chunk, n_groups, pipelined)
    y2 = _post(x2, o_rw.reshape(B * T, RWKV_WIDTH), o_hg.reshape(B * T, HGRN_WIDTH), p, tm)
    sh = proj3[:, T - 1, :RWKV_PROJ]
    return y2.reshape(B, T, D_MODEL), st_rw[None], st_hg[None], sh[None]


PROMPT_CFG = dict(tm=512, bb=8, chunk=64, n_groups=2, pipelined=True)
SAMPLE_CFG = dict(tm=512, bb=32, chunk=8, n_groups=1, pipelined=False)


def kernel(x_prompt, x_sample, state_rwkv, state_hgrn, state_shift, w_in, shift_mu, w0, w1u, a0, a1u, g1u, k_k, k_a, r_k, ln_x_w, ln_x_b, lb_logits, hg_norm_w, w_out, ln1_g, ln1_b, w_up, w_down, ln2_g, ln2_b):
    assert w_in.shape[0] == DEPTH
    p = _prep_params(w_in, shift_mu, w0, w1u, a0, a1u, g1u, k_k, k_a, r_k, ln_x_w, ln_x_b, lb_logits,
                     hg_norm_w, w_out, ln1_g, ln1_b, w_up, w_down, ln2_g, ln2_b)
    bp = x_prompt.shape[0]
    z_rw = jnp.zeros((bp, RWKV_HEADS, RWKV_HEAD, RWKV_HEAD), F32)
    z_hg = jnp.zeros((bp, HGRN_HEADS, HGRN_HEAD, HGRN_HEAD), F32)
    z_sh = jnp.zeros((bp, RWKV_PROJ), F32)
    y_p, rw_p, hg_p, sh_p = _run_group(x_prompt, z_rw, z_hg, z_sh, p, **PROMPT_CFG)
    y_s, rw_s, hg_s, sh_s = _run_group(x_sample, state_rwkv[0].astype(F32), state_hgrn[0].astype(F32),
                                       state_shift[0].astype(F32), p, **SAMPLE_CFG)
    return (y_p, y_s, rw_p, rw_s, hg_p, hg_s, sh_p, sh_s)
```

```python
import functools
import math

import jax
import jax.numpy as jnp
from jax import lax
from jax.experimental import pallas as pl
from jax.experimental.pallas import tpu as pltpu

F32 = jnp.float32
BF16 = jnp.bfloat16

D_MODEL = 1024
RWKV_WIDTH = 512
RWKV_HEAD = 64
RWKV_HEADS = 8
HGRN_WIDTH = 512
HGRN_HEAD = 128
HGRN_HEADS = 4
DECAY_RANK = 64
AICL_RANK = 64
GATE_RANK = 128
RWKV_PROJ = 3 * RWKV_WIDTH + DECAY_RANK + AICL_RANK + GATE_RANK
HGRN_PROJ = 4 * HGRN_WIDTH
PROJ = RWKV_PROJ + HGRN_PROJ
D_FF = 4 * D_MODEL
DEPTH = 1
ALPHA = (2.0 * DEPTH) ** 0.25
LN_EPS = 1e-5
GN_EPS = RWKV_HEAD * 1e-5
RMS_EPS = 1e-6
DECAY_SCALE = math.exp(-0.5)

LANES = 128
SUBLANES = 8
PAIRS = RWKV_HEADS // 2
LORA_COL = 3 * RWKV_WIDTH
VMEM_LIMIT = 56 * 1024 * 1024

NN = ((1,), (0,))
NT = ((1,), (1,))
TN = ((0,), (0,))

RWKV_STASH = (("lhs", 4, BF16), ("rhs", 2, BF16), ("bt_hat", 2, BF16), ("atrt", 2, BF16), ("vb", 1, BF16),
              ("bkh", 2, BF16), ("dec", 0, F32), ("bonus", 1, F32), ("gate", 1, F32))
HGRN_STASH = (("qs", 1, BF16), ("ks", 1, BF16), ("iv", 1, BF16), ("q", 1, F32), ("kin", 1, F32),
              ("bc", 1, F32), ("dec", 0, F32))


def _dot(a, b, dims):
    return lax.dot_general(a, b, (dims, ((), ())), preferred_element_type=F32)


def _bf(x):
    return x.astype(BF16)


def _cumsum_rows(tri, x):
    p0 = _bf(x)
    r1 = x - p0.astype(F32)
    p1 = _bf(r1)
    p2 = _bf(r1 - p1.astype(F32))
    return _dot(tri, p0, NN) + _dot(tri, p1, NN) + _dot(tri, p2, NN)


def _sigmoid(x):
    return 0.5 * jnp.tanh(0.5 * x) + 0.5


def _interleave(tasks):
    gens = list(tasks)
    nxt = [next(g, None) for g in gens]
    spent = {"M": 0.0, "V": 0.0}
    turn = 0
    while any(k is not None for k in nxt):
        ready = {kind: [j for j, k in enumerate(nxt) if k is not None and k[0] == kind] for kind in spent}
        want = "M" if (ready["M"] and (spent["M"] <= spent["V"] or not ready["V"])) else "V"
        if want == "M":
            cands = ready["M"]
            t = cands[turn % len(cands)]
            turn += 1
        else:
            t = ready["V"][0]
        spent[want] += nxt[t][1]
        nxt[t] = next(gens[t], None)


def _chain(*gens):
    for g in gens:
        yield from g


class _ValueStash:
    def __init__(self):
        self.vals = {}

    def put(self, name, val):
        self.vals[name] = val

    def get(self, name, j=None):
        return self.vals[name] if j is None else self.vals[name][j]


class _RefStash:
    def __init__(self, refs, slot, lo, n):
        self.refs, self.slot, self.lo, self.n = refs, slot, lo, n

    def put(self, name, val):
        self.refs[name][self.slot, self.lo:self.lo + self.n] = val

    def get(self, name, j=None):
        if j is None:
            return self.refs[name][self.slot, self.lo:self.lo + self.n]
        return self.refs[name][self.slot, self.lo + j]


def _proj_kernel(x_ref, w_ref, o_ref, wbf_scr):
    @pl.when(pl.program_id(0) == 0)
    def _cast_weights():
        wbf_scr[...] = w_ref[...].astype(BF16)

    o_ref[...] = jnp.dot(x_ref[...].astype(BF16), wbf_scr[...], preferred_element_type=F32)


def _proj(x2, w_in, tm):
    n = x2.shape[0]
    return pl.pallas_call(
        _proj_kernel,
        grid=(n // tm,),
        in_specs=[pl.BlockSpec((tm, D_MODEL), lambda i: (i, 0)),
                  pl.BlockSpec((D_MODEL, PROJ), lambda i: (0, 0), pipeline_mode=pl.Buffered(1))],
        out_specs=pl.BlockSpec((tm, PROJ), lambda i: (i, 0)),
        out_shape=jax.ShapeDtypeStruct((n, PROJ), F32),
        scratch_shapes=[pltpu.VMEM((D_MODEL, PROJ), BF16)],
        compiler_params=pltpu.CompilerParams(dimension_semantics=("arbitrary",),
                                             vmem_limit_bytes=VMEM_LIMIT),
        name="proj",
    )(x2, w_in)


def _head_helpers(C):
    lane = lax.broadcasted_iota(jnp.int32, (C, LANES), 1)
    m0 = lane < RWKV_HEAD

    def head_sum(x):
        s0 = jnp.sum(jnp.where(m0, x, 0.0), axis=-1, keepdims=True)
        s1 = jnp.sum(jnp.where(m0, 0.0, x), axis=-1, keepdims=True)
        return jnp.where(m0, s0, s1)

    def stack_heads(x):
        return jnp.concatenate([jnp.where(m0, x, 0.0), jnp.where(m0, 0.0, x)], axis=-2)

    def merge_heads(x):
        return jnp.where(m0, x[:C], x[C:])

    return head_sum, stack_heads, merge_heads


def _rwkv_produce(stash, lo, n, C, pr_ref, pk_ref, pv_ref, pl_ref, mur_ref, muk_ref, muv_ref, mul_ref,
                  w0_ref, a0_ref, kk_ref, ka_ref, rk_ref, w1_ref, a1_ref, g1_ref,
                  cr_scr, ck_scr, cv_scr, cl_scr):
    ids = range(n)
    grp = slice(lo, lo + n)
    head_sum, stack_heads, _ = _head_helpers(C)
    rowc = lax.broadcasted_iota(jnp.int32, (C, C), 0)
    colc = lax.broadcasted_iota(jnp.int32, (C, C), 1)
    tri = jnp.where(rowc >= colc, 1.0, 0.0).astype(BF16)

    def shifted(p_ref, carry_ref, mu_ref):
        p = p_ref[grp]
        row = lax.broadcasted_iota(jnp.int32, p.shape, 1)
        flat = p.reshape(n * C, p.shape[-1])
        prev = jnp.where(row == 0, carry_ref[grp], pltpu.roll(flat, 1, 0).reshape(p.shape))
        carry_ref[grp] = p[:, C - 1:C, :]
        return p + mu_ref[...] * (prev - p)

    w1 = _bf(w1_ref[...])
    a1 = _bf(a1_ref[...])
    g1 = _bf(g1_ref[...])

    yield ("V", 60.0 * n)
    r = shifted(pr_ref, cr_scr, mur_ref)
    xk = shifted(pk_ref, ck_scr, muk_ref)
    v = shifted(pv_ref, cv_scr, muv_ref)
    xl = shifted(pl_ref, cl_scr, mul_ref).reshape(n * C, 2 * LANES)
    xl_lo = xl[:, :LANES]
    dw = jnp.dot(_bf(jnp.tanh(xl_lo)), w1, preferred_element_type=F32)
    da = jnp.dot(_bf(xl_lo), a1, preferred_element_type=F32)
    gate = jnp.dot(_bf(_sigmoid(xl[:, LANES:])), g1, preferred_element_type=F32)
    stash.put("gate", gate.reshape(n, C, LANES))

    yield ("V", 60.0 * n)
    lw = (-DECAY_SCALE * _sigmoid(w0_ref[...] + dw)).reshape(n, C, LANES)
    a_lr = _sigmoid(a0_ref[...] + da).reshape(n, C, LANES)
    kk = xk * kk_ref[...]
    kk = kk * jnp.minimum(lax.rsqrt(head_sum(kk * kk)), 1e12)
    k2 = xk * (1.0 + (a_lr - 1.0) * ka_ref[...])
    b = kk * a_lr
    cum = jnp.stack([_cumsum_rows(tri, lw[j]) for j in ids])

    yield ("V", 80.0 * n)
    cum_last = cum[:, C - 1:C, :]
    e_in = jnp.exp(-cum)
    e_out = jnp.exp(cum_last - cum)
    at = -kk * jnp.exp(cum - lw)
    rt = r * jnp.exp(cum)
    bt = b * e_in
    kt = k2 * e_in
    stash.put("lhs", _bf(jnp.concatenate([stack_heads(at), stack_heads(rt)], axis=1)))
    stash.put("rhs", _bf(jnp.concatenate([bt, kt], axis=1)))
    stash.put("bt_hat", _bf(stack_heads(bt)))
    stash.put("atrt", _bf(jnp.concatenate([at, rt], axis=1)))
    stash.put("vb", _bf(v))
    stash.put("bkh", _bf(jnp.concatenate([b * e_out, k2 * e_out], axis=1)))
    stash.put("dec", jnp.exp(cum_last))
    stash.put("bonus", head_sum(r * k2 * rk_ref[...]) * v)


def _rwkv_consume(stash, lo, n, C, dot_cost, lnw_ref, lnb_ref, o_ref, s_scr):
    H = RWKV_HEAD
    ids = range(n)
    grp = slice(lo, lo + n)
    head_sum, stack_heads, merge_heads = _head_helpers(C)
    r2 = lax.broadcasted_iota(jnp.int32, (2 * C, 2 * C), 0)
    c2 = lax.broadcasted_iota(jnp.int32, (2 * C, 2 * C), 1)
    t2 = jnp.bitwise_and(r2, C - 1)
    s2 = jnp.bitwise_and(c2, C - 1)
    strict2 = t2 > s2
    incl2 = t2 >= s2
    strict_bd = jnp.logical_and(strict2, (r2 >= C) == (c2 >= C))
    eye2 = jnp.where(r2 == c2, 1.0, 0.0).astype(F32)
    bl_r = lax.broadcasted_iota(jnp.int32, (LANES, LANES), 0) < H
    bl_c = lax.broadcasted_iota(jnp.int32, (LANES, LANES), 1) < H
    blockdiag = bl_r == bl_c
    zeros_c = jnp.zeros((C, LANES), BF16)

    yield ("M", dot_cost)
    g_a, g_r = [], []
    for j in ids:
        g = _dot(stash.get("lhs", j), stash.get("rhs", j), NT)
        g_a.append(_bf(jnp.where(strict2, g[:2 * C], 0.0)))
        g_r.append(_bf(jnp.where(incl2, g[2 * C:], 0.0)))
    yield ("M", dot_cost)
    pw, inv = [], []
    for j in ids:
        n_bd = jnp.where(strict_bd, _dot(stash.get("lhs", j)[:2 * C], stash.get("bt_hat", j), NT), 0.0)
        inv.append(eye2 + n_bd)
        pw.append(_bf(n_bd))
    span = 2
    while span < C:
        yield ("M", dot_cost)
        pw = [_bf(_dot(pw[j], pw[j], NN)) for j in ids]
        yield ("M", dot_cost)
        inv = [inv[j] + _dot(_bf(inv[j]), pw[j], NN) for j in ids]
        span *= 2
    yield ("M", dot_cost)
    S = [s_scr[lo + j] for j in ids]
    pq = [_dot(stash.get("atrt", j), _bf(S[j]), NT) for j in ids]
    yield ("M", dot_cost)
    vb = [stash.get("vb", j) for j in ids]
    w_hat = [_bf(stack_heads(pq[j][:C] + merge_heads(_dot(g_a[j], jnp.concatenate([zeros_c, vb[j]], axis=0), NN))))
             for j in ids]
    yield ("M", dot_cost)
    z = []
    for j in ids:
        u2 = _dot(_bf(inv[j]), w_hat[j], NN)
        z.append(jnp.concatenate([_bf(u2[:C] + u2[C:]), vb[j]], axis=0))
    yield ("M", dot_cost)
    y = [pq[j][C:] + merge_heads(_dot(g_r[j], z[j], NN)) for j in ids]
    yield ("M", dot_cost)
    for j in ids:
        upd = _dot(z[j], stash.get("bkh", j), TN)
        s_scr[lo + j] = S[j] * stash.get("dec", j) + jnp.where(blockdiag, upd, 0.0)

    yield ("V", 40.0 * n)
    y = jnp.stack(y)
    mu = head_sum(y) * (1.0 / H)
    dy = y - mu
    var = head_sum(dy * dy) * (1.0 / H)
    yn = dy * lax.rsqrt(var + GN_EPS) * lnw_ref[...] + lnb_ref[...]
    o_ref[grp] = (yn + stash.get("bonus")) * stash.get("gate")


def _mid_rows(x, h):
    R = x.shape[0]
    row = lax.broadcasted_iota(jnp.int32, x.shape, 0)
    if 2 * h >= SUBLANES:
        return jnp.concatenate(
            [jnp.broadcast_to(x[b * 2 * h + h - 1:b * 2 * h + h], (2 * h, LANES)) for b in range(R // (2 * h))],
            axis=0)
    if h == 1:
        return jnp.where(jnp.bitwise_and(row, 1) == 1, pltpu.roll(x, 1, 0), x)
    picks = []
    for half in range(SUBLANES // (2 * h)):
        r0 = half * 2 * h + h - 1
        picks.append(jnp.concatenate(
            [jnp.broadcast_to(x[g * SUBLANES + r0:g * SUBLANES + r0 + 1], (SUBLANES, LANES))
             for g in range(R // SUBLANES)], axis=0))
    out = picks[-1]
    sub = jnp.bitwise_and(row, SUBLANES - 1)
    for half in range(len(picks) - 2, -1, -1):
        out = jnp.where(sub < (half + 1) * 2 * h, picks[half], out)
    return out


def _hgrn_produce(stash, lo, n, C, q_ref, f_ref, i_ref, lbl_ref):
    ids = range(n)
    grp = slice(lo, lo + n)
    logits = lbl_ref[...]
    ex = jnp.exp(logits - jnp.max(logits, axis=0, keepdims=True))
    lb = ex[0:1] / jnp.sum(ex, axis=0, keepdims=True)
    t_i = lax.broadcasted_iota(jnp.int32, (C, C), 0)
    s_i = lax.broadcasted_iota(jnp.int32, (C, C), 1)
    tri = jnp.where(t_i >= s_i, 1.0, 0.0).astype(BF16)

    yield ("V", 40.0 * n)
    q_h = q_ref[grp]
    f_h = f_ref[grp]
    q = q_h * _sigmoid(q_h)
    fg = lb + (1.0 - lb) * _sigmoid(f_h)
    kin = 1.0 - fg
    lf = jnp.log(fg)
    bc = jnp.stack([_cumsum_rows(tri, lf[j]) for j in ids])

    yield ("V", 30.0 * n)
    b_last = bc[:, C - 1:C, :]
    stash.put("qs", _bf(q * jnp.exp(bc)))
    stash.put("ks", _bf(kin * jnp.exp(b_last - bc)))
    stash.put("iv", _bf(i_ref[grp]))
    stash.put("q", q)
    stash.put("kin", kin)
    stash.put("bc", bc)
    stash.put("dec", jnp.exp(b_last))


def _hgrn_consume(stash, lo, n, C, dot_cost, g_ref, hgw_ref, o_ref, s_scr):
    ids = range(n)
    grp = slice(lo, lo + n)
    t_i = lax.broadcasted_iota(jnp.int32, (C, C), 0)
    s_i = lax.broadcasted_iota(jnp.int32, (C, C), 1)
    halves = []
    h = C // 2
    while h >= 1:
        halves.append(h)
        h //= 2

    def level_mask(h):
        same = jnp.bitwise_and(t_i, -2 * h) == jnp.bitwise_and(s_i, -2 * h)
        return jnp.logical_and(same, jnp.logical_and(jnp.bitwise_and(t_i, h) != 0, jnp.bitwise_and(s_i, h) == 0))

    def seq(x, j):
        return _bf(x[j * C:(j + 1) * C])

    yield ("M", dot_cost)
    S = [s_scr[lo + j] for j in ids]
    o_state = [_dot(stash.get("qs", j), _bf(S[j]), NT) for j in ids]
    yield ("M", dot_cost)
    iv = [stash.get("iv", j) for j in ids]
    for j in ids:
        s_scr[lo + j] = S[j] * stash.get("dec", j) + _dot(iv[j], stash.get("ks", j), TN)
    yield ("M", dot_cost)
    q = stash.get("q").reshape(n * C, LANES)
    kin = stash.get("kin").reshape(n * C, LANES)
    bc = stash.get("bc").reshape(n * C, LANES)
    att = [jnp.where(t_i == s_i, _dot(seq(q, j), seq(kin, j), NT), 0.0) for j in ids]
    for h in halves:
        yield ("V", 25.0 * n)
        dmid = jnp.exp(-jnp.abs(bc - _mid_rows(bc, h)))
        ql = q * dmid
        kl = kin * dmid
        yield ("M", dot_cost)
        msk = level_mask(h)
        att = [jnp.where(msk, _dot(seq(ql, j), seq(kl, j), NT), att[j]) for j in ids]
    yield ("M", dot_cost)
    o_intra = [_dot(_bf(att[j]), iv[j], NN) for j in ids]

    yield ("V", 25.0 * n)
    o = jnp.stack([o_intra[j] + o_state[j] for j in ids])
    ms = jnp.mean(o * o, axis=-1, keepdims=True)
    g_h = g_ref[grp]
    o_ref[grp] = o * lax.rsqrt(ms + RMS_EPS) * hgw_ref[...] * (g_h * _sigmoid(g_h))


MXU_STAGE_COST = {True: (20.0, 14.0), False: (30.0, 20.0)}
N_MIXER_IN = 30
N_MIXER_OUT = 4
N_MIXER_STATE_SCRATCH = 6


def _mixer_kernel(*refs, bb, C, n_groups, pipelined):
    (pr_ref, pk_ref, pv_ref, pl_ref, shr_ref, shk_ref, shv_ref, shl_ref,
     mur_ref, muk_ref, muv_ref, mul_ref, w0_ref, a0_ref, kk_ref, ka_ref, rk_ref,
     lnw_ref, lnb_ref, w1_ref, a1_ref, g1_ref, srw0_ref,
     q_ref, f_ref, i_ref, g_ref, lbl_ref, hgw_ref, shg0_ref) = refs[:N_MIXER_IN]
    orw_ref, strw_ref, ohg_ref, sthg_ref = refs[N_MIXER_IN:N_MIXER_IN + N_MIXER_OUT]
    scr = refs[N_MIXER_IN + N_MIXER_OUT:]
    srw_scr, cr_scr, ck_scr, cv_scr, cl_scr, shg_scr = scr[:N_MIXER_STATE_SCRATCH]
    stash_refs = scr[N_MIXER_STATE_SCRATCH:]
    if pipelined:
        rw_refs = {name: stash_refs[k] for k, (name, _, _) in enumerate(RWKV_STASH)}
        hg_refs = {name: stash_refs[len(RWKV_STASH) + k] for k, (name, _, _) in enumerate(HGRN_STASH)}

    c = pl.program_id(2)
    last = pl.num_programs(2) - 1
    H = RWKV_HEAD

    @pl.when(c == 0)
    def _init():
        z = jnp.zeros((H, H), F32)
        for i in range(bb):
            top = jnp.concatenate([srw0_ref[i, 0], z], axis=1)
            bot = jnp.concatenate([z, srw0_ref[i, 1]], axis=1)
            srw_scr[i] = jnp.concatenate([top, bot], axis=0)
            shg_scr[i] = shg0_ref[i, 0].T
        cr_scr[...] = shr_ref[...]
        ck_scr[...] = shk_ref[...]
        cv_scr[...] = shv_ref[...]
        cl_scr[...] = shl_ref[...]
        if pipelined:
            for table in (rw_refs, hg_refs):
                for name, ref in table.items():
                    fill = jnp.ones if name == "dec" else jnp.zeros
                    ref[1] = fill(ref.shape[1:], ref.dtype)

    per = bb // n_groups
    groups = [(gidx * per, per) for gidx in range(n_groups)]

    def rwkv_produce(stash, lo, n):
        return _rwkv_produce(stash, lo, n, C, pr_ref, pk_ref, pv_ref, pl_ref, mur_ref, muk_ref, muv_ref, mul_ref,
                             w0_ref, a0_ref, kk_ref, ka_ref, rk_ref, w1_ref, a1_ref, g1_ref,
                             cr_scr, ck_scr, cv_scr, cl_scr)

    def rwkv_consume(stash, lo, n):
        return _rwkv_consume(stash, lo, n, C, MXU_STAGE_COST[pipelined][0] * n, lnw_ref, lnb_ref, orw_ref, srw_scr)

    def hgrn_produce(stash, lo, n):
        return _hgrn_produce(stash, lo, n, C, q_ref, f_ref, i_ref, lbl_ref)

    def hgrn_consume(stash, lo, n):
        return _hgrn_consume(stash, lo, n, C, MXU_STAGE_COST[pipelined][1] * n, g_ref, hgw_ref, ohg_ref, shg_scr)

    if not pipelined:
        tasks = []
        for lo, n in groups:
            rw, hg = _ValueStash(), _ValueStash()
            tasks.append(_chain(rwkv_produce(rw, lo, n), rwkv_consume(rw, lo, n)))
            tasks.append(_chain(hgrn_produce(hg, lo, n), hgrn_consume(hg, lo, n)))
        _interleave(tasks)
    else:
        def step(write_slot):
            read_slot = 1 - write_slot
            tasks = []
            for lo, n in groups:
                tasks.append(rwkv_consume(_RefStash(rw_refs, read_slot, lo, n), lo, n))
                tasks.append(hgrn_consume(_RefStash(hg_refs, read_slot, lo, n), lo, n))
            for lo, n in groups:
                tasks.append(rwkv_produce(_RefStash(rw_refs, write_slot, lo, n), lo, n))
                tasks.append(hgrn_produce(_RefStash(hg_refs, write_slot, lo, n), lo, n))
            _interleave(tasks)

        parity = lax.rem(c, 2)
        pl.when(parity == 0)(functools.partial(step, 0))
        pl.when(parity == 1)(functools.partial(step, 1))

    @pl.when(c == last)
    def _fin():
        for i in range(bb):
            s_fin = srw_scr[i]
            strw_ref[i, 0] = s_fin[:H, :H]
            strw_ref[i, 1] = s_fin[H:, H:]
            sthg_ref[i, 0] = shg_scr[i].T


def _mixer(proj3, s_shift3, s_rwkv, s_hgrn, p, bb, C, n_groups, pipelined):
    B, T, _ = proj3.shape
    nc = T // C
    grid = (B // bb, PAIRS, nc + 1 if pipelined else nc)
    nb = RWKV_WIDTH // LANES
    hg0 = RWKV_PROJ // LANES
    lora_blk = LORA_COL // (2 * LANES)

    if pipelined:
        def produced(c):
            return jnp.minimum(c, nc - 1)

        def consumed(c):
            return jnp.maximum(c - 1, 0)
    else:
        produced = consumed = lambda c: c

    def tok(col0, chunk=produced):
        return pl.BlockSpec((bb, C, LANES), lambda b, h, c: (b, chunk(c), col0 + h))

    def first(col0):
        return pl.BlockSpec((bb, 1, LANES), lambda b, h, c: (b, 0, col0 + h))

    def vec(col0):
        return pl.BlockSpec((1, LANES), lambda b, h, c: (0, col0 + h))

    in_specs = [
        tok(0), tok(nb), tok(2 * nb),
        pl.BlockSpec((bb, C, 2 * LANES), lambda b, h, c: (b, produced(c), lora_blk)),
        first(0), first(nb), first(2 * nb),
        pl.BlockSpec((bb, 1, 2 * LANES), lambda b, h, c: (b, 0, lora_blk)),
        vec(0), vec(nb), vec(2 * nb),
        pl.BlockSpec((1, 2 * LANES), lambda b, h, c: (0, lora_blk)),
        vec(0), vec(0), vec(0), vec(0), vec(0), vec(0), vec(0),
        pl.BlockSpec((LANES, LANES), lambda b, h, c: (0, h)),
        pl.BlockSpec((LANES, LANES), lambda b, h, c: (0, h)),
        pl.BlockSpec((GATE_RANK, LANES), lambda b, h, c: (0, h)),
        pl.BlockSpec((bb, 2, RWKV_HEAD, RWKV_HEAD), lambda b, h, c: (b, h, 0, 0)),
        tok(hg0), tok(hg0 + nb), tok(hg0 + 2 * nb), tok(hg0 + 3 * nb, consumed),
        pl.BlockSpec((DEPTH + 1, LANES), lambda b, h, c: (0, h)),
        vec(0),
        pl.BlockSpec((bb, 1, HGRN_HEAD, HGRN_HEAD), lambda b, h, c: (b, h, 0, 0)),
    ]
    assert len(in_specs) == N_MIXER_IN
    out_specs = [
        pl.BlockSpec((bb, C, LANES), lambda b, h, c: (b, consumed(c), h)),
        pl.BlockSpec((bb, 2, RWKV_HEAD, RWKV_HEAD), lambda b, h, c: (b, h, 0, 0)),
        pl.BlockSpec((bb, C, LANES), lambda b, h, c: (b, consumed(c), h)),
        pl.BlockSpec((bb, 1, HGRN_HEAD, HGRN_HEAD), lambda b, h, c: (b, h, 0, 0)),
    ]
    out_shape = [jax.ShapeDtypeStruct((B, T, RWKV_WIDTH), F32),
                 jax.ShapeDtypeStruct((B, RWKV_HEADS, RWKV_HEAD, RWKV_HEAD), F32),
                 jax.ShapeDtypeStruct((B, T, HGRN_WIDTH), F32),
                 jax.ShapeDtypeStruct((B, HGRN_HEADS, HGRN_HEAD, HGRN_HEAD), F32)]
    scratch = [pltpu.VMEM((bb, LANES, LANES), F32),
               pltpu.VMEM((bb, 1, LANES), F32), pltpu.VMEM((bb, 1, LANES), F32),
               pltpu.VMEM((bb, 1, LANES), F32), pltpu.VMEM((bb, 1, 2 * LANES), F32),
               pltpu.VMEM((bb, HGRN_HEAD, HGRN_HEAD), F32)]
    assert len(scratch) == N_MIXER_STATE_SCRATCH
    if pipelined:
        scratch += [pltpu.VMEM((2, bb, rows * C if rows else 1, LANES), dt)
                    for _, rows, dt in RWKV_STASH + HGRN_STASH]
    return pl.pallas_call(
        functools.partial(_mixer_kernel, bb=bb, C=C, n_groups=n_groups, pipelined=pipelined),
        grid=grid, in_specs=in_specs, out_specs=out_specs, out_shape=out_shape,
        scratch_shapes=scratch,
        compiler_params=pltpu.CompilerParams(
            dimension_semantics=("arbitrary", "arbitrary", "arbitrary"),
            vmem_limit_bytes=VMEM_LIMIT),
        name="mixer",
    )(proj3, proj3, proj3, proj3, s_shift3, s_shift3, s_shift3, s_shift3,
      p["shift_mu"], p["shift_mu"], p["shift_mu"], p["shift_mu"],
      p["w0"], p["a0"], p["k_k"], p["k_a"], p["r_k"], p["ln_x_w"], p["ln_x_b"],
      p["w1u_pad"], p["a1u_pad"], p["g1u"], s_rwkv,
      proj3, proj3, proj3, proj3, p["lb_logits"], p["hg_norm_w"], s_hgrn)


def _layer_norm(x, g, b):
    mu = jnp.mean(x, axis=-1, keepdims=True)
    d = x - mu
    var = jnp.mean(d * d, axis=-1, keepdims=True)
    return d * lax.rsqrt(var + LN_EPS) * g + b


def _post_kernel(x_ref, orw_ref, ohg_ref, wo1_ref, wo2_ref, g1_ref, b1_ref, wup_ref, wdn_ref,
                 g2_ref, b2_ref, y_ref):
    mix = (jnp.dot(orw_ref[...].astype(BF16), wo1_ref[...], preferred_element_type=F32)
           + jnp.dot(ohg_ref[...].astype(BF16), wo2_ref[...], preferred_element_type=F32))
    h1 = _layer_norm(ALPHA * x_ref[...] + mix, g1_ref[...], b1_ref[...])
    up = jnp.dot(h1.astype(BF16), wup_ref[...], preferred_element_type=F32)
    up = jnp.square(jnp.maximum(up, 0.0))
    ff = jnp.dot(up.astype(BF16), wdn_ref[...], preferred_element_type=F32)
    y_ref[...] = _layer_norm(ALPHA * h1 + ff, g2_ref[...], b2_ref[...])


def _post(x2, orw2, ohg2, p, tm):
    n = x2.shape[0]

    def const(shape):
        return pl.BlockSpec(shape, lambda i: (0, 0), pipeline_mode=pl.Buffered(1))

    return pl.pallas_call(
        _post_kernel,
        grid=(n // tm,),
        in_specs=[pl.BlockSpec((tm, D_MODEL), lambda i: (i, 0)),
                  pl.BlockSpec((tm, RWKV_WIDTH), lambda i: (i, 0)),
                  pl.BlockSpec((tm, HGRN_WIDTH), lambda i: (i, 0)),
                  const((RWKV_WIDTH, D_MODEL)), const((HGRN_WIDTH, D_MODEL)),
                  const((1, D_MODEL)), const((1, D_MODEL)),
                  const((D_MODEL, D_FF)), const((D_FF, D_MODEL)),
                  const((1, D_MODEL)), const((1, D_MODEL))],
        out_specs=pl.BlockSpec((tm, D_MODEL), lambda i: (i, 0)),
        out_shape=jax.ShapeDtypeStruct((n, D_MODEL), F32),
        compiler_params=pltpu.CompilerParams(dimension_semantics=("arbitrary",),
                                             vmem_limit_bytes=VMEM_LIMIT),
        name="post",
    )(x2, orw2, ohg2, p["wo_rw"], p["wo_hg"], p["ln1_g"], p["ln1_b"], p["w_up"], p["w_down"],
      p["ln2_g"], p["ln2_b"])


def _prep_params(w_in, shift_mu, w0, w1u, a0, a1u, g1u, k_k, k_a, r_k, ln_x_w, ln_x_b, lb_logits,
                 hg_norm_w, w_out, ln1_g, ln1_b, w_up, w_down, ln2_g, ln2_b):
    zw = jnp.zeros((LANES - DECAY_RANK, RWKV_WIDTH), F32)
    za = jnp.zeros((LANES - AICL_RANK, RWKV_WIDTH), F32)
    w_out_bf = w_out[0].astype(BF16)
    return {
        "w_in": w_in[0],
        "shift_mu": shift_mu[0].reshape(1, RWKV_PROJ),
        "w0": w0[0].reshape(1, RWKV_WIDTH), "a0": a0[0].reshape(1, RWKV_WIDTH),
        "k_k": k_k[0].reshape(1, RWKV_WIDTH), "k_a": k_a[0].reshape(1, RWKV_WIDTH),
        "r_k": r_k[0].reshape(1, RWKV_WIDTH),
        "ln_x_w": ln_x_w[0].reshape(1, RWKV_WIDTH), "ln_x_b": ln_x_b[0].reshape(1, RWKV_WIDTH),
        "w1u_pad": jnp.concatenate([w1u[0], zw], axis=0),
        "a1u_pad": jnp.concatenate([za, a1u[0]], axis=0),
        "g1u": g1u[0],
        "lb_logits": lb_logits.astype(F32),
        "hg_norm_w": hg_norm_w[0].reshape(1, HGRN_WIDTH),
        "wo_rw": w_out_bf[:RWKV_WIDTH], "wo_hg": w_out_bf[RWKV_WIDTH:],
        "ln1_g": ln1_g[0].reshape(1, D_MODEL), "ln1_b": ln1_b[0].reshape(1, D_MODEL),
        "w_up": w_up[0].astype(BF16), "w_down": w_down[0].astype(BF16),
        "ln2_g": ln2_g[0].reshape(1, D_MODEL), "ln2_b": ln2_b[0].reshape(1, D_MODEL),
    }


def _run_group(x, s_rwkv, s_hgrn, s_shift, p, *, tm, bb, chunk, n_groups, pipelined):
    B, T, _ = x.shape
    x2 = x.reshape(B * T, D_MODEL)
    proj2 = _proj(x2, p["w_in"], tm)
    proj3 = proj2.reshape(B, T, PROJ)
    o_rw, st_rw, o_hg, st_hg = _mixer(proj3, s_shift.reshape(B, 1, RWKV_PROJ), s_rwkv, s_hgrn, p,
                                      bb, chunk, n_groups, pipelined)
    y2 = _post(x2, o_rw.reshape(B * T, RWKV_WIDTH), o_hg.reshape(B * T, HGRN_WIDTH), p, tm)
    sh = proj3[:, T - 1, :RWKV_PROJ]
    return y2.reshape(B, T, D_MODEL), st_rw[None], st_hg[None], sh[None]


PROMPT_CFG = dict(tm=512, bb=8, chunk=64, n_groups=1, pipelined=True)
SAMPLE_CFG = dict(tm=512, bb=32, chunk=8, n_groups=1, pipelined=False)


def kernel(x_prompt, x_sample, state_rwkv, state_hgrn, state_shift, w_in, shift_mu, w0, w1u, a0, a1u, g1u, k_k, k_a, r_k, ln_x_w, ln_x_b, lb_logits, hg_norm_w, w_out, ln1_g, ln1_b, w_up, w_down, ln2_g, ln2_b):
    assert w_in.shape[0] == DEPTH
    p = _prep_params(w_in, shift_mu, w0, w1u, a0, a1u, g1u, k_k, k_a, r_k, ln_x_w, ln_x_b, lb_logits,
                     hg_norm_w, w_out, ln1_g, ln1_b, w_up, w_down, ln2_g, ln2_b)
    bp = x_prompt.shape[0]
    z_rw = jnp.zeros((bp, RWKV_HEADS, RWKV_HEAD, RWKV_HEAD), F32)
    z_hg = jnp.zeros((bp, HGRN_HEADS, HGRN_HEAD, HGRN_HEAD), F32)
    z_sh = jnp.zeros((bp, RWKV_PROJ), F32)
    y_p, rw_p, hg_p, sh_p = _run_group(x_prompt, z_rw, z_hg, z_sh, p, **PROMPT_CFG)
    y_s, rw_s, hg_s, sh_s = _run_group(x_sample, state_rwkv[0].astype(F32), state_hgrn[0].astype(F32),
                                       state_shift[0].astype(F32), p, **SAMPLE_CFG)
    return (y_p, y_s, rw_p, rw_s, hg_p, hg_s, sh_p, sh_s)
```

```python
import functools
import math

import jax
import jax.numpy as jnp
from jax import lax
from jax.experimental import pallas as pl
from jax.experimental.pallas import tpu as pltpu

F32 = jnp.float32
BF16 = jnp.bfloat16

D_MODEL = 1024
RWKV_WIDTH = 512
RWKV_HEAD = 64
RWKV_HEADS = 8
HGRN_WIDTH = 512
HGRN_HEAD = 128
HGRN_HEADS = 4
DECAY_RANK = 64
AICL_RANK = 64
GATE_RANK = 128
RWKV_PROJ = 3 * RWKV_WIDTH + DECAY_RANK + AICL_RANK + GATE_RANK
HGRN_PROJ = 4 * HGRN_WIDTH
PROJ = RWKV_PROJ + HGRN_PROJ
D_FF = 4 * D_MODEL
DEPTH = 1
ALPHA = (2.0 * DEPTH) ** 0.25
LN_EPS = 1e-5
GN_EPS = RWKV_HEAD * 1e-5
RMS_EPS = 1e-6
DECAY_SCALE = math.exp(-0.5)

LANES = 128
SUBLANES = 8
PAIRS = RWKV_HEADS // 2
LORA_COL = 3 * RWKV_WIDTH
VMEM_LIMIT = 56 * 1024 * 1024

NN = ((1,), (0,))
NT = ((1,), (1,))
TN = ((0,), (0,))

RWKV_STASH = (("lhs", 4, BF16), ("rhs", 2, BF16), ("bt_hat", 2, BF16), ("atrt", 2, BF16), ("vb", 1, BF16),
              ("bkh", 2, BF16), ("dec", 0, F32), ("bonus", 1, F32), ("gate", 1, F32))
HGRN_STASH = (("qs", 1, BF16), ("ks", 1, BF16), ("iv", 1, BF16), ("q", 1, F32), ("kin", 1, F32),
              ("bc", 1, F32), ("dec", 0, F32))


def _dot(a, b, dims):
    return lax.dot_general(a, b, (dims, ((), ())), preferred_element_type=F32)


def _bf(x):
    return x.astype(BF16)


def _cumsum_rows(tri, x):
    p0 = _bf(x)
    r1 = x - p0.astype(F32)
    p1 = _bf(r1)
    p2 = _bf(r1 - p1.astype(F32))
    return _dot(tri, p0, NN) + _dot(tri, p1, NN) + _dot(tri, p2, NN)


def _sigmoid(x):
    return 0.5 * jnp.tanh(0.5 * x) + 0.5


def _interleave(tasks):
    gens = list(tasks)
    nxt = [next(g, None) for g in gens]
    spent = {"M": 0.0, "V": 0.0}
    turn = 0
    while any(k is not None for k in nxt):
        ready = {kind: [j for j, k in enumerate(nxt) if k is not None and k[0] == kind] for kind in spent}
        want = "M" if (ready["M"] and (spent["M"] <= spent["V"] or not ready["V"])) else "V"
        if want == "M":
            cands = ready["M"]
            t = cands[turn % len(cands)]
            turn += 1
        else:
            t = ready["V"][0]
        spent[want] += nxt[t][1]
        nxt[t] = next(gens[t], None)


def _chain(*gens):
    for g in gens:
        yield from g


class _ValueStash:
    def __init__(self):
        self.vals = {}

    def put(self, name, val):
        self.vals[name] = val

    def get(self, name, j=None):
        return self.vals[name] if j is None else self.vals[name][j]


class _RefStash:
    def __init__(self, refs, slot, lo, n):
        self.refs, self.slot, self.lo, self.n = refs, slot, lo, n

    def put(self, name, val):
        self.refs[name][self.slot, self.lo:self.lo + self.n] = val

    def get(self, name, j=None):
        if j is None:
            return self.refs[name][self.slot, self.lo:self.lo + self.n]
        return self.refs[name][self.slot, self.lo + j]


def _proj_kernel(x_ref, w_ref, o_ref, wbf_scr):
    @pl.when(pl.program_id(0) == 0)
    def _cast_weights():
        wbf_scr[...] = w_ref[...].astype(BF16)

    o_ref[...] = jnp.dot(x_ref[...].astype(BF16), wbf_scr[...], preferred_element_type=F32)


def _proj(x2, w_in, tm):
    n = x2.shape[0]
    return pl.pallas_call(
        _proj_kernel,
        grid=(n // tm,),
        in_specs=[pl.BlockSpec((tm, D_MODEL), lambda i: (i, 0)),
                  pl.BlockSpec((D_MODEL, PROJ), lambda i: (0, 0), pipeline_mode=pl.Buffered(1))],
        out_specs=pl.BlockSpec((tm, PROJ), lambda i: (i, 0)),
        out_shape=jax.ShapeDtypeStruct((n, PROJ), F32),
        scratch_shapes=[pltpu.VMEM((D_MODEL, PROJ), BF16)],
        compiler_params=pltpu.CompilerParams(dimension_semantics=("arbitrary",),
                                             vmem_limit_bytes=VMEM_LIMIT),
        name="proj",
    )(x2, w_in)


def _head_helpers(C):
    lane = lax.broadcasted_iota(jnp.int32, (C, LANES), 1)
    m0 = lane < RWKV_HEAD

    def head_sum(x):
        s0 = jnp.sum(jnp.where(m0, x, 0.0), axis=-1, keepdims=True)
        s1 = jnp.sum(jnp.where(m0, 0.0, x), axis=-1, keepdims=True)
        return jnp.where(m0, s0, s1)

    def stack_heads(x):
        return jnp.concatenate([jnp.where(m0, x, 0.0), jnp.where(m0, 0.0, x)], axis=-2)

    def merge_heads(x):
        return jnp.where(m0, x[:C], x[C:])

    return head_sum, stack_heads, merge_heads


def _rwkv_produce(stash, lo, n, C, pr_ref, pk_ref, pv_ref, pl_ref, mur_ref, muk_ref, muv_ref, mul_ref,
                  w0_ref, a0_ref, kk_ref, ka_ref, rk_ref, w1_ref, a1_ref, g1_ref,
                  cr_scr, ck_scr, cv_scr, cl_scr):
    ids = range(n)
    grp = slice(lo, lo + n)
    head_sum, stack_heads, _ = _head_helpers(C)
    rowc = lax.broadcasted_iota(jnp.int32, (C, C), 0)
    colc = lax.broadcasted_iota(jnp.int32, (C, C), 1)
    tri = jnp.where(rowc >= colc, 1.0, 0.0).astype(BF16)

    def shifted(p_ref, carry_ref, mu_ref):
        p = p_ref[grp]
        row = lax.broadcasted_iota(jnp.int32, p.shape, 1)
        flat = p.reshape(n * C, p.shape[-1])
        prev = jnp.where(row == 0, carry_ref[grp], pltpu.roll(flat, 1, 0).reshape(p.shape))
        carry_ref[grp] = p[:, C - 1:C, :]
        return p + mu_ref[...] * (prev - p)

    w1 = _bf(w1_ref[...])
    a1 = _bf(a1_ref[...])
    g1 = _bf(g1_ref[...])

    yield ("V", 60.0 * n)
    r = shifted(pr_ref, cr_scr, mur_ref)
    xk = shifted(pk_ref, ck_scr, muk_ref)
    v = shifted(pv_ref, cv_scr, muv_ref)
    xl = shifted(pl_ref, cl_scr, mul_ref).reshape(n * C, 2 * LANES)
    xl_lo = xl[:, :LANES]
    dw = jnp.dot(_bf(jnp.tanh(xl_lo)), w1, preferred_element_type=F32)
    da = jnp.dot(_bf(xl_lo), a1, preferred_element_type=F32)
    gate = jnp.dot(_bf(_sigmoid(xl[:, LANES:])), g1, preferred_element_type=F32)
    stash.put("gate", gate.reshape(n, C, LANES))

    yield ("V", 60.0 * n)
    lw = (-DECAY_SCALE * _sigmoid(w0_ref[...] + dw)).reshape(n, C, LANES)
    a_lr = _sigmoid(a0_ref[...] + da).reshape(n, C, LANES)
    kk = xk * kk_ref[...]
    kk = kk * jnp.minimum(lax.rsqrt(head_sum(kk * kk)), 1e12)
    k2 = xk * (1.0 + (a_lr - 1.0) * ka_ref[...])
    b = kk * a_lr
    cum = jnp.stack([_cumsum_rows(tri, lw[j]) for j in ids])

    yield ("V", 80.0 * n)
    cum_last = cum[:, C - 1:C, :]
    e_in = jnp.exp(-cum)
    e_out = jnp.exp(cum_last - cum)
    at = -kk * jnp.exp(cum - lw)
    rt = r * jnp.exp(cum)
    bt = b * e_in
    kt = k2 * e_in
    stash.put("lhs", _bf(jnp.concatenate([stack_heads(at), stack_heads(rt)], axis=1)))
    stash.put("rhs", _bf(jnp.concatenate([bt, kt], axis=1)))
    stash.put("bt_hat", _bf(stack_heads(bt)))
    stash.put("atrt", _bf(jnp.concatenate([at, rt], axis=1)))
    stash.put("vb", _bf(v))
    stash.put("bkh", _bf(jnp.concatenate([b * e_out, k2 * e_out], axis=1)))
    stash.put("dec", jnp.exp(cum_last))
    stash.put("bonus", head_sum(r * k2 * rk_ref[...]) * v)


def _rwkv_consume(stash, lo, n, C, dot_cost, lnw_ref, lnb_ref, o_ref, s_scr):
    H = RWKV_HEAD
    ids = range(n)
    grp = slice(lo, lo + n)
    head_sum, stack_heads, merge_heads = _head_helpers(C)
    r2 = lax.broadcasted_iota(jnp.int32, (2 * C, 2 * C), 0)
    c2 = lax.broadcasted_iota(jnp.int32, (2 * C, 2 * C), 1)
    t2 = jnp.bitwise_and(r2, C - 1)
    s2 = jnp.bitwise_and(c2, C - 1)
    strict2 = t2 > s2
    incl2 = t2 >= s2
    strict_bd = jnp.logical_and(strict2, (r2 >= C) == (c2 >= C))
    eye2 = jnp.where(r2 == c2, 1.0, 0.0).astype(F32)
    bl_r = lax.broadcasted_iota(jnp.int32, (LANES, LANES), 0) < H
    bl_c = lax.broadcasted_iota(jnp.int32, (LANES, LANES), 1) < H
    blockdiag = bl_r == bl_c
    zeros_c = jnp.zeros((C, LANES), BF16)

    yield ("M", dot_cost)
    g_a, g_r = [], []
    for j in ids:
        g = _dot(stash.get("lhs", j), stash.get("rhs", j), NT)
        g_a.append(_bf(jnp.where(strict2, g[:2 * C], 0.0)))
        g_r.append(_bf(jnp.where(incl2, g[2 * C:], 0.0)))
    yield ("M", dot_cost)
    pw, inv = [], []
    for j in ids:
        n_bd = jnp.where(strict_bd, _dot(stash.get("lhs", j)[:2 * C], stash.get("bt_hat", j), NT), 0.0)
        inv.append(eye2 + n_bd)
        pw.append(_bf(n_bd))
    span = 2
    while span < C:
        yield ("M", dot_cost)
        pw = [_bf(_dot(pw[j], pw[j], NN)) for j in ids]
        yield ("M", dot_cost)
        inv = [inv[j] + _dot(_bf(inv[j]), pw[j], NN) for j in ids]
        span *= 2
    yield ("M", dot_cost)
    S = [s_scr[lo + j] for j in ids]
    pq = [_dot(stash.get("atrt", j), _bf(S[j]), NT) for j in ids]
    yield ("M", dot_cost)
    vb = [stash.get("vb", j) for j in ids]
    w_hat = [_bf(stack_heads(pq[j][:C] + merge_heads(_dot(g_a[j], jnp.concatenate([zeros_c, vb[j]], axis=0), NN))))
             for j in ids]
    yield ("M", dot_cost)
    z = []
    for j in ids:
        u2 = _dot(_bf(inv[j]), w_hat[j], NN)
        z.append(jnp.concatenate([_bf(u2[:C] + u2[C:]), vb[j]], axis=0))
    yield ("M", dot_cost)
    y = [pq[j][C:] + merge_heads(_dot(g_r[j], z[j], NN)) for j in ids]
    yield ("M", dot_cost)
    for j in ids:
        upd = _dot(z[j], stash.get("bkh", j), TN)
        s_scr[lo + j] = S[j] * stash.get("dec", j) + jnp.where(blockdiag, upd, 0.0)

    yield ("V", 40.0 * n)
    y = jnp.stack(y)
    mu = head_sum(y) * (1.0 / H)
    dy = y - mu
    var = head_sum(dy * dy) * (1.0 / H)
    yn = dy * lax.rsqrt(var + GN_EPS) * lnw_ref[...] + lnb_ref[...]
    o_ref[grp] = (yn + stash.get("bonus")) * stash.get("gate")


def _mid_rows(x, h):
    R = x.shape[0]
    row = lax.broadcasted_iota(jnp.int32, x.shape, 0)
    if 2 * h >= SUBLANES:
        return jnp.concatenate(
            [jnp.broadcast_to(x[b * 2 * h + h - 1:b * 2 * h + h], (2 * h, LANES)) for b in range(R // (2 * h))],
            axis=0)
    if h == 1:
        return jnp.where(jnp.bitwise_and(row, 1) == 1, pltpu.roll(x, 1, 0), x)
    picks = []
    for half in range(SUBLANES // (2 * h)):
        r0 = half * 2 * h + h - 1
        picks.append(jnp.concatenate(
            [jnp.broadcast_to(x[g * SUBLANES + r0:g * SUBLANES + r0 + 1], (SUBLANES, LANES))
             for g in range(R // SUBLANES)], axis=0))
    out = picks[-1]
    sub = jnp.bitwise_and(row, SUBLANES - 1)
    for half in range(len(picks) - 2, -1, -1):
        out = jnp.where(sub < (half + 1) * 2 * h, picks[half], out)
    return out


def _hgrn_produce(stash, lo, n, C, q_ref, f_ref, i_ref, lbl_ref):
    ids = range(n)
    grp = slice(lo, lo + n)
    logits = lbl_ref[...]
    ex = jnp.exp(logits - jnp.max(logits, axis=0, keepdims=True))
    lb = ex[0:1] / jnp.sum(ex, axis=0, keepdims=True)
    t_i = lax.broadcasted_iota(jnp.int32, (C, C), 0)
    s_i = lax.broadcasted_iota(jnp.int32, (C, C), 1)
    tri = jnp.where(t_i >= s_i, 1.0, 0.0).astype(BF16)

    yield ("V", 40.0 * n)
    q_h = q_ref[grp]
    f_h = f_ref[grp]
    q = q_h * _sigmoid(q_h)
    fg = lb + (1.0 - lb) * _sigmoid(f_h)
    kin = 1.0 - fg
    lf = jnp.log(fg)
    bc = jnp.stack([_cumsum_rows(tri, lf[j]) for j in ids])

    yield ("V", 30.0 * n)
    b_last = bc[:, C - 1:C, :]
    stash.put("qs", _bf(q * jnp.exp(bc)))
    stash.put("ks", _bf(kin * jnp.exp(b_last - bc)))
    stash.put("iv", _bf(i_ref[grp]))
    stash.put("q", q)
    stash.put("kin", kin)
    stash.put("bc", bc)
    stash.put("dec", jnp.exp(b_last))


def _hgrn_consume(stash, lo, n, C, dot_cost, g_ref, hgw_ref, o_ref, s_scr):
    ids = range(n)
    grp = slice(lo, lo + n)
    t_i = lax.broadcasted_iota(jnp.int32, (C, C), 0)
    s_i = lax.broadcasted_iota(jnp.int32, (C, C), 1)
    halves = []
    h = C // 2
    while h >= 1:
        halves.append(h)
        h //= 2

    def level_mask(h):
        same = jnp.bitwise_and(t_i, -2 * h) == jnp.bitwise_and(s_i, -2 * h)
        return jnp.logical_and(same, jnp.logical_and(jnp.bitwise_and(t_i, h) != 0, jnp.bitwise_and(s_i, h) == 0))

    def seq(x, j):
        return _bf(x[j * C:(j + 1) * C])

    yield ("M", dot_cost)
    S = [s_scr[lo + j] for j in ids]
    o_state = [_dot(stash.get("qs", j), _bf(S[j]), NT) for j in ids]
    yield ("M", dot_cost)
    iv = [stash.get("iv", j) for j in ids]
    for j in ids:
        s_scr[lo + j] = S[j] * stash.get("dec", j) + _dot(iv[j], stash.get("ks", j), TN)
    yield ("M", dot_cost)
    q = stash.get("q").reshape(n * C, LANES)
    kin = stash.get("kin").reshape(n * C, LANES)
    bc = stash.get("bc").reshape(n * C, LANES)
    att = [jnp.where(t_i == s_i, _dot(seq(q, j), seq(kin, j), NT), 0.0) for j in ids]
    for h in halves:
        yield ("V", 25.0 * n)
        dmid = jnp.exp(-jnp.abs(bc - _mid_rows(bc, h)))
        ql = q * dmid
        kl = kin * dmid
        yield ("M", dot_cost)
        msk = level_mask(h)
        att = [jnp.where(msk, _dot(seq(ql, j), seq(kl, j), NT), att[j]) for j in ids]
    yield ("M", dot_cost)
    o_intra = [_dot(_bf(att[j]), iv[j], NN) for j in ids]

    yield ("V", 25.0 * n)
    o = jnp.stack([o_intra[j] + o_state[j] for j in ids])
    ms = jnp.mean(o * o, axis=-1, keepdims=True)
    g_h = g_ref[grp]
    o_ref[grp] = o * lax.rsqrt(ms + RMS_EPS) * hgw_ref[...] * (g_h * _sigmoid(g_h))


MXU_STAGE_COST = {True: (45.0, 30.0), False: (30.0, 20.0)}
N_MIXER_IN = 30
N_MIXER_OUT = 4
N_MIXER_STATE_SCRATCH = 6


def _mixer_kernel(*refs, bb, C, n_groups, pipelined):
    (pr_ref, pk_ref, pv_ref, pl_ref, shr_ref, shk_ref, shv_ref, shl_ref,
     mur_ref, muk_ref, muv_ref, mul_ref, w0_ref, a0_ref, kk_ref, ka_ref, rk_ref,
     lnw_ref, lnb_ref, w1_ref, a1_ref, g1_ref, srw0_ref,
     q_ref, f_ref, i_ref, g_ref, lbl_ref, hgw_ref, shg0_ref) = refs[:N_MIXER_IN]
    orw_ref, strw_ref, ohg_ref, sthg_ref = refs[N_MIXER_IN:N_MIXER_IN + N_MIXER_OUT]
    scr = refs[N_MIXER_IN + N_MIXER_OUT:]
    srw_scr, cr_scr, ck_scr, cv_scr, cl_scr, shg_scr = scr[:N_MIXER_STATE_SCRATCH]
    stash_refs = scr[N_MIXER_STATE_SCRATCH:]
    if pipelined:
        rw_refs = {name: stash_refs[k] for k, (name, _, _) in enumerate(RWKV_STASH)}
        hg_refs = {name: stash_refs[len(RWKV_STASH) + k] for k, (name, _, _) in enumerate(HGRN_STASH)}

    c = pl.program_id(2)
    last = pl.num_programs(2) - 1
    H = RWKV_HEAD

    @pl.when(c == 0)
    def _init():
        z = jnp.zeros((H, H), F32)
        for i in range(bb):
            top = jnp.concatenate([srw0_ref[i, 0], z], axis=1)
            bot = jnp.concatenate([z, srw0_ref[i, 1]], axis=1)
            srw_scr[i] = jnp.concatenate([top, bot], axis=0)
            shg_scr[i] = shg0_ref[i, 0].T
        cr_scr[...] = shr_ref[...]
        ck_scr[...] = shk_ref[...]
        cv_scr[...] = shv_ref[...]
        cl_scr[...] = shl_ref[...]
        if pipelined:
            for table in (rw_refs, hg_refs):
                for name, ref in table.items():
                    fill = jnp.ones if name == "dec" else jnp.zeros
                    ref[1] = fill(ref.shape[1:], ref.dtype)

    per = bb // n_groups
    groups = [(gidx * per, per) for gidx in range(n_groups)]

    def rwkv_produce(stash, lo, n):
        return _rwkv_produce(stash, lo, n, C, pr_ref, pk_ref, pv_ref, pl_ref, mur_ref, muk_ref, muv_ref, mul_ref,
                             w0_ref, a0_ref, kk_ref, ka_ref, rk_ref, w1_ref, a1_ref, g1_ref,
                             cr_scr, ck_scr, cv_scr, cl_scr)

    def rwkv_consume(stash, lo, n):
        return _rwkv_consume(stash, lo, n, C, MXU_STAGE_COST[pipelined][0] * n, lnw_ref, lnb_ref, orw_ref, srw_scr)

    def hgrn_produce(stash, lo, n):
        return _hgrn_produce(stash, lo, n, C, q_ref, f_ref, i_ref, lbl_ref)

    def hgrn_consume(stash, lo, n):
        return _hgrn_consume(stash, lo, n, C, MXU_STAGE_COST[pipelined][1] * n, g_ref, hgw_ref, ohg_ref, shg_scr)

    if not pipelined:
        tasks = []
        for lo, n in groups:
            rw, hg = _ValueStash(), _ValueStash()
            tasks.append(_chain(rwkv_produce(rw, lo, n), rwkv_consume(rw, lo, n)))
            tasks.append(_chain(hgrn_produce(hg, lo, n), hgrn_consume(hg, lo, n)))
        _interleave(tasks)
    else:
        def step(write_slot):
            read_slot = 1 - write_slot
            tasks = []
            for lo, n in groups:
                tasks.append(rwkv_consume(_RefStash(rw_refs, read_slot, lo, n), lo, n))
                tasks.append(hgrn_consume(_RefStash(hg_refs, read_slot, lo, n), lo, n))
            for lo, n in groups:
                tasks.append(rwkv_produce(_RefStash(rw_refs, write_slot, lo, n), lo, n))
                tasks.append(hgrn_produce(_RefStash(hg_refs, write_slot, lo, n), lo, n))
            _interleave(tasks)

        parity = lax.rem(c, 2)
        pl.when(parity == 0)(functools.partial(step, 0))
        pl.when(parity == 1)(functools.partial(step, 1))

    @pl.when(c == last)
    def _fin():
        for i in range(bb):
            s_fin = srw_scr[i]
            strw_ref[i, 0] = s_fin[:H, :H]
            strw_ref[i, 1] = s_fin[H:, H:]
            sthg_ref[i, 0] = shg_scr[i].T


def _mixer(proj3, s_shift3, s_rwkv, s_hgrn, p, bb, C, n_groups, pipelined):
    B, T, _ = proj3.shape
    nc = T // C
    grid = (B // bb, PAIRS, nc + 1 if pipelined else nc)
    nb = RWKV_WIDTH // LANES
    hg0 = RWKV_PROJ // LANES
    lora_blk = LORA_COL // (2 * LANES)

    if pipelined:
        def produced(c):
            return jnp.minimum(c, nc - 1)

        def consumed(c):
            return jnp.maximum(c - 1, 0)
    else:
        produced = consumed = lambda c: c

    def tok(col0, chunk=produced):
        return pl.BlockSpec((bb, C, LANES), lambda b, h, c: (b, chunk(c), col0 + h))

    def first(col0):
        return pl.BlockSpec((bb, 1, LANES), lambda b, h, c: (b, 0, col0 + h))

    def vec(col0):
        return pl.BlockSpec((1, LANES), lambda b, h, c: (0, col0 + h))

    in_specs = [
        tok(0), tok(nb), tok(2 * nb),
        pl.BlockSpec((bb, C, 2 * LANES), lambda b, h, c: (b, produced(c), lora_blk)),
        first(0), first(nb), first(2 * nb),
        pl.BlockSpec((bb, 1, 2 * LANES), lambda b, h, c: (b, 0, lora_blk)),
        vec(0), vec(nb), vec(2 * nb),
        pl.BlockSpec((1, 2 * LANES), lambda b, h, c: (0, lora_blk)),
        vec(0), vec(0), vec(0), vec(0), vec(0), vec(0), vec(0),
        pl.BlockSpec((LANES, LANES), lambda b, h, c: (0, h)),
        pl.BlockSpec((LANES, LANES), lambda b, h, c: (0, h)),
        pl.BlockSpec((GATE_RANK, LANES), lambda b, h, c: (0, h)),
        pl.BlockSpec((bb, 2, RWKV_HEAD, RWKV_HEAD), lambda b, h, c: (b, h, 0, 0)),
        tok(hg0), tok(hg0 + nb), tok(hg0 + 2 * nb), tok(hg0 + 3 * nb, consumed),
        pl.BlockSpec((DEPTH + 1, LANES), lambda b, h, c: (0, h)),
        vec(0),
        pl.BlockSpec((bb, 1, HGRN_HEAD, HGRN_HEAD), lambda b, h, c: (b, h, 0, 0)),
    ]
    assert len(in_specs) == N_MIXER_IN
    out_specs = [
        pl.BlockSpec((bb, C, LANES), lambda b, h, c: (b, consumed(c), h)),
        pl.BlockSpec((bb, 2, RWKV_HEAD, RWKV_HEAD), lambda b, h, c: (b, h, 0, 0)),
        pl.BlockSpec((bb, C, LANES), lambda b, h, c: (b, consumed(c), h)),
        pl.BlockSpec((bb, 1, HGRN_HEAD, HGRN_HEAD), lambda b, h, c: (b, h, 0, 0)),
    ]
    out_shape = [jax.ShapeDtypeStruct((B, T, RWKV_WIDTH), F32),
                 jax.ShapeDtypeStruct((B, RWKV_HEADS, RWKV_HEAD, RWKV_HEAD), F32),
                 jax.ShapeDtypeStruct((B, T, HGRN_WIDTH), F32),
                 jax.ShapeDtypeStruct((B, HGRN_HEADS, HGRN_HEAD, HGRN_HEAD), F32)]
    scratch = [pltpu.VMEM((bb, LANES, LANES), F32),
               pltpu.VMEM((bb, 1, LANES), F32), pltpu.VMEM((bb, 1, LANES), F32),
               pltpu.VMEM((bb, 1, LANES), F32), pltpu.VMEM((bb, 1, 2 * LANES), F32),
               pltpu.VMEM((bb, HGRN_HEAD, HGRN_HEAD), F32)]
    assert len(scratch) == N_MIXER_STATE_SCRATCH
    if pipelined:
        scratch += [pltpu.VMEM((2, bb, rows * C if rows else 1, LANES), dt)
                    for _, rows, dt in RWKV_STASH + HGRN_STASH]
    return pl.pallas_call(
        functools.partial(_mixer_kernel, bb=bb, C=C, n_groups=n_groups, pipelined=pipelined),
        grid=grid, in_specs=in_specs, out_specs=out_specs, out_shape=out_shape,
        scratch_shapes=scratch,
        compiler_params=pltpu.CompilerParams(
            dimension_semantics=("arbitrary", "arbitrary", "arbitrary"),
            vmem_limit_bytes=VMEM_LIMIT),
        name="mixer",
    )(proj3, proj3, proj3, proj3, s_shift3, s_shift3, s_shift3, s_shift3,
      p["shift_mu"], p["shift_mu"], p["shift_mu"], p["shift_mu"],
      p["w0"], p["a0"], p["k_k"], p["k_a"], p["r_k"], p["ln_x_w"], p["ln_x_b"],
      p["w1u_pad"], p["a1u_pad"], p["g1u"], s_rwkv,
      proj3, proj3, proj3, proj3, p["lb_logits"], p["hg_norm_w"], s_hgrn)


def _layer_norm(x, g, b):
    mu = jnp.mean(x, axis=-1, keepdims=True)
    d = x - mu
    var = jnp.mean(d * d, axis=-1, keepdims=True)
    return d * lax.rsqrt(var + LN_EPS) * g + b


def _post_kernel(x_ref, orw_ref, ohg_ref, wo1_ref, wo2_ref, g1_ref, b1_ref, wup_ref, wdn_ref,
                 g2_ref, b2_ref, y_ref):
    mix = (jnp.dot(orw_ref[...].astype(BF16), wo1_ref[...], preferred_element_type=F32)
           + jnp.dot(ohg_ref[...].astype(BF16), wo2_ref[...], preferred_element_type=F32))
    h1 = _layer_norm(ALPHA * x_ref[...] + mix, g1_ref[...], b1_ref[...])
    up = jnp.dot(h1.astype(BF16), wup_ref[...], preferred_element_type=F32)
    up = jnp.square(jnp.maximum(up, 0.0))
    ff = jnp.dot(up.astype(BF16), wdn_ref[...], preferred_element_type=F32)
    y_ref[...] = _layer_norm(ALPHA * h1 + ff, g2_ref[...], b2_ref[...])


def _post(x2, orw2, ohg2, p, tm):
    n = x2.shape[0]

    def const(shape):
        return pl.BlockSpec(shape, lambda i: (0, 0), pipeline_mode=pl.Buffered(1))

    return pl.pallas_call(
        _post_kernel,
        grid=(n // tm,),
        in_specs=[pl.BlockSpec((tm, D_MODEL), lambda i: (i, 0)),
                  pl.BlockSpec((tm, RWKV_WIDTH), lambda i: (i, 0)),
                  pl.BlockSpec((tm, HGRN_WIDTH), lambda i: (i, 0)),
                  const((RWKV_WIDTH, D_MODEL)), const((HGRN_WIDTH, D_MODEL)),
                  const((1, D_MODEL)), const((1, D_MODEL)),
                  const((D_MODEL, D_FF)), const((D_FF, D_MODEL)),
                  const((1, D_MODEL)), const((1, D_MODEL))],
        out_specs=pl.BlockSpec((tm, D_MODEL), lambda i: (i, 0)),
        out_shape=jax.ShapeDtypeStruct((n, D_MODEL), F32),
        compiler_params=pltpu.CompilerParams(dimension_semantics=("arbitrary",),
                                             vmem_limit_bytes=VMEM_LIMIT),
        name="post",
    )(x2, orw2, ohg2, p["wo_rw"], p["wo_hg"], p["ln1_g"], p["ln1_b"], p["w_up"], p["w_down"],
      p["ln2_g"], p["ln2_b"])


def _prep_params(w_in, shift_mu, w0, w1u, a0, a1u, g1u, k_k, k_a, r_k, ln_x_w, ln_x_b, lb_logits,
                 hg_norm_w, w_out, ln1_g, ln1_b, w_up, w_down, ln2_g, ln2_b):
    zw = jnp.zeros((LANES - DECAY_RANK, RWKV_WIDTH), F32)
    za = jnp.zeros((LANES - AICL_RANK, RWKV_WIDTH), F32)
    w_out_bf = w_out[0].astype(BF16)
    return {
        "w_in": w_in[0],
        "shift_mu": shift_mu[0].reshape(1, RWKV_PROJ),
        "w0": w0[0].reshape(1, RWKV_WIDTH), "a0": a0[0].reshape(1, RWKV_WIDTH),
        "k_k": k_k[0].reshape(1, RWKV_WIDTH), "k_a": k_a[0].reshape(1, RWKV_WIDTH),
        "r_k": r_k[0].reshape(1, RWKV_WIDTH),
        "ln_x_w": ln_x_w[0].reshape(1, RWKV_WIDTH), "ln_x_b": ln_x_b[0].reshape(1, RWKV_WIDTH),
        "w1u_pad": jnp.concatenate([w1u[0], zw], axis=0),
        "a1u_pad": jnp.concatenate([za, a1u[0]], axis=0),
        "g1u": g1u[0],
        "lb_logits": lb_logits.astype(F32),
        "hg_norm_w": hg_norm_w[0].reshape(1, HGRN_WIDTH),
        "wo_rw": w_out_bf[:RWKV_WIDTH], "wo_hg": w_out_bf[RWKV_WIDTH:],
        "ln1_g": ln1_g[0].reshape(1, D_MODEL), "ln1_b": ln1_b[0].reshape(1, D_MODEL),
        "w_up": w_up[0].astype(BF16), "w_down": w_down[0].astype(BF16),
        "ln2_g": ln2_g[0].reshape(1, D_MODEL), "ln2_b": ln2_b[0].reshape(1, D_MODEL),
    }


def _run_group(x, s_rwkv, s_hgrn, s_shift, p, *, tm, bb, chunk, n_groups, pipelined):
    B, T, _ = x.shape
    x2 = x.reshape(B * T, D_MODEL)
    proj2 = _proj(x2, p["w_in"], tm)
    proj3 = proj2.reshape(B, T, PROJ)
    o_rw, st_rw, o_hg, st_hg = _mixer(proj3, s_shift.reshape(B, 1, RWKV_PROJ), s_rwkv, s_hgrn, p,
                                      bb, chunk, n_groups, pipelined)
    y2 = _post(x2, o_rw.reshape(B * T, RWKV_WIDTH), o_hg.reshape(B * T, HGRN_WIDTH), p, tm)
    sh = proj3[:, T - 1, :RWKV_PROJ]
    return y2.reshape(B, T, D_MODEL), st_rw[None], st_hg[None], sh[None]


PROMPT_CFG = dict(tm=512, bb=8, chunk=64, n_groups=1, pipelined=True)
SAMPLE_CFG = dict(tm=512, bb=32, chunk=8, n_groups=1, pipelined=False)


def kernel(x_prompt, x_sample, state_rwkv, state_hgrn, state_shift, w_in, shift_mu, w0, w1u, a0, a1u, g1u, k_k, k_a, r_k, ln_x_w, ln_x_b, lb_logits, hg_norm_w, w_out, ln1_g, ln1_b, w_up, w_down, ln2_g, ln2_b):
    assert w_in.shape[0] == DEPTH
    p = _prep_params(w_in, shift_mu, w0, w1u, a0, a1u, g1u, k_k, k_a, r_k, ln_x_w, ln_x_b, lb_logits,
                     hg_norm_w, w_out, ln1_g, ln1_b, w_up, w_down, ln2_g, ln2_b)
    bp = x_prompt.shape[0]
    z_rw = jnp.zeros((bp, RWKV_HEADS, RWKV_HEAD, RWKV_HEAD), F32)
    z_hg = jnp.zeros((bp, HGRN_HEADS, HGRN_HEAD, HGRN_HEAD), F32)
    z_sh = jnp.zeros((bp, RWKV_PROJ), F32)
    y_p, rw_p, hg_p, sh_p = _run_group(x_prompt, z_rw, z_hg, z_sh, p, **PROMPT_CFG)
    y_s, rw_s, hg_s, sh_s = _run_group(x_sample, state_rwkv[0].astype(F32), state_hgrn[0].astype(F32),
                                       state_shift[0].astype(F32), p, **SAMPLE_CFG)
    return (y_p, y_s, rw_p, rw_s, hg_p, hg_s, sh_p, sh_s)
```

```python
import functools
import math

import jax
import jax.numpy as jnp
from jax import lax
from jax.experimental import pallas as pl
from jax.experimental.pallas import tpu as pltpu

F32 = jnp.float32
BF16 = jnp.bfloat16

D_MODEL = 1024
RWKV_WIDTH = 512
RWKV_HEAD = 64
RWKV_HEADS = 8
HGRN_WIDTH = 512
HGRN_HEAD = 128
HGRN_HEADS = 4
DECAY_RANK = 64
AICL_RANK = 64
GATE_RANK = 128
RWKV_PROJ = 3 * RWKV_WIDTH + DECAY_RANK + AICL_RANK + GATE_RANK
HGRN_PROJ = 4 * HGRN_WIDTH
PROJ = RWKV_PROJ + HGRN_PROJ
D_FF = 4 * D_MODEL
DEPTH = 1
ALPHA = (2.0 * DEPTH) ** 0.25
LN_EPS = 1e-5
GN_EPS = RWKV_HEAD * 1e-5
RMS_EPS = 1e-6
DECAY_SCALE = math.exp(-0.5)

LANES = 128
SUBLANES = 8
PAIRS = RWKV_HEADS // 2
LORA_COL = 3 * RWKV_WIDTH
VMEM_LIMIT = 56 * 1024 * 1024
POST_SPLIT = 2

NN = ((1,), (0,))
NT = ((1,), (1,))
TN = ((0,), (0,))

RWKV_STASH = (("lhs", 4, BF16), ("rhs", 2, BF16), ("bt_hat", 2, BF16), ("atrt", 2, BF16), ("vb", 1, BF16),
              ("bkh", 2, BF16), ("dec", 0, F32), ("bonus", 1, F32), ("gate", 1, F32))
HGRN_STASH = (("qs", 1, BF16), ("ks", 1, BF16), ("iv", 1, BF16), ("q", 1, F32), ("kin", 1, F32),
              ("bc", 1, F32), ("dec", 0, F32))


def _dot(a, b, dims):
    return lax.dot_general(a, b, (dims, ((), ())), preferred_element_type=F32)


def _bf(x):
    return x.astype(BF16)


def _cumsum_rows(tri, x):
    p0 = _bf(x)
    r1 = x - p0.astype(F32)
    p1 = _bf(r1)
    p2 = _bf(r1 - p1.astype(F32))
    return _dot(tri, p0, NN) + _dot(tri, p1, NN) + _dot(tri, p2, NN)


def _sigmoid(x):
    return 0.5 * jnp.tanh(0.5 * x) + 0.5


def _interleave(tasks):
    gens = list(tasks)
    nxt = [next(g, None) for g in gens]
    spent = {"M": 0.0, "V": 0.0}
    turn = 0
    while any(k is not None for k in nxt):
        ready = {kind: [j for j, k in enumerate(nxt) if k is not None and k[0] == kind] for kind in spent}
        want = "M" if (ready["M"] and (spent["M"] <= spent["V"] or not ready["V"])) else "V"
        if want == "M":
            cands = ready["M"]
            t = cands[turn % len(cands)]
            turn += 1
        else:
            t = ready["V"][0]
        spent[want] += nxt[t][1]
        nxt[t] = next(gens[t], None)


def _chain(*gens):
    for g in gens:
        yield from g


class _ValueStash:
    def __init__(self):
        self.vals = {}

    def put(self, name, val):
        self.vals[name] = val

    def get(self, name, j=None):
        return self.vals[name] if j is None else self.vals[name][j]


class _RefStash:
    def __init__(self, refs, slot, lo, n):
        self.refs, self.slot, self.lo, self.n = refs, slot, lo, n

    def put(self, name, val):
        self.refs[name][self.slot, self.lo:self.lo + self.n] = val

    def get(self, name, j=None):
        if j is None:
            return self.refs[name][self.slot, self.lo:self.lo + self.n]
        return self.refs[name][self.slot, self.lo + j]


def _proj_kernel(x_ref, w_ref, o_ref, wbf_scr):
    @pl.when(pl.program_id(0) == 0)
    def _cast_weights():
        wbf_scr[...] = w_ref[...].astype(BF16)

    o_ref[...] = jnp.dot(x_ref[...].astype(BF16), wbf_scr[...], preferred_element_type=F32)


def _proj(x2, w_in, tm):
    n = x2.shape[0]
    return pl.pallas_call(
        _proj_kernel,
        grid=(n // tm,),
        in_specs=[pl.BlockSpec((tm, D_MODEL), lambda i: (i, 0)),
                  pl.BlockSpec((D_MODEL, PROJ), lambda i: (0, 0), pipeline_mode=pl.Buffered(1))],
        out_specs=pl.BlockSpec((tm, PROJ), lambda i: (i, 0)),
        out_shape=jax.ShapeDtypeStruct((n, PROJ), F32),
        scratch_shapes=[pltpu.VMEM((D_MODEL, PROJ), BF16)],
        compiler_params=pltpu.CompilerParams(dimension_semantics=("arbitrary",),
                                             vmem_limit_bytes=VMEM_LIMIT),
        name="proj",
    )(x2, w_in)


def _head_helpers(C):
    lane = lax.broadcasted_iota(jnp.int32, (C, LANES), 1)
    m0 = lane < RWKV_HEAD

    def head_sum(x):
        s0 = jnp.sum(jnp.where(m0, x, 0.0), axis=-1, keepdims=True)
        s1 = jnp.sum(jnp.where(m0, 0.0, x), axis=-1, keepdims=True)
        return jnp.where(m0, s0, s1)

    def stack_heads(x):
        return jnp.concatenate([jnp.where(m0, x, 0.0), jnp.where(m0, 0.0, x)], axis=-2)

    def merge_heads(x):
        return jnp.where(m0, x[:C], x[C:])

    return head_sum, stack_heads, merge_heads


def _rwkv_produce(stash, lo, n, C, pr_ref, pk_ref, pv_ref, pl_ref, mur_ref, muk_ref, muv_ref, mul_ref,
                  w0_ref, a0_ref, kk_ref, ka_ref, rk_ref, w1_ref, a1_ref, g1_ref,
                  cr_scr, ck_scr, cv_scr, cl_scr):
    ids = range(n)
    grp = slice(lo, lo + n)
    head_sum, stack_heads, _ = _head_helpers(C)
    rowc = lax.broadcasted_iota(jnp.int32, (C, C), 0)
    colc = lax.broadcasted_iota(jnp.int32, (C, C), 1)
    tri = jnp.where(rowc >= colc, 1.0, 0.0).astype(BF16)

    def shifted(p_ref, carry_ref, mu_ref):
        p = p_ref[grp]
        row = lax.broadcasted_iota(jnp.int32, p.shape, 1)
        flat = p.reshape(n * C, p.shape[-1])
        prev = jnp.where(row == 0, carry_ref[grp], pltpu.roll(flat, 1, 0).reshape(p.shape))
        carry_ref[grp] = p[:, C - 1:C, :]
        return p + mu_ref[...] * (prev - p)

    w1 = _bf(w1_ref[...])
    a1 = _bf(a1_ref[...])
    g1 = _bf(g1_ref[...])

    yield ("V", 60.0 * n)
    r = shifted(pr_ref, cr_scr, mur_ref)
    xk = shifted(pk_ref, ck_scr, muk_ref)
    v = shifted(pv_ref, cv_scr, muv_ref)
    xl = shifted(pl_ref, cl_scr, mul_ref).reshape(n * C, 2 * LANES)
    xl_lo = xl[:, :LANES]
    dw = jnp.dot(_bf(jnp.tanh(xl_lo)), w1, preferred_element_type=F32)
    da = jnp.dot(_bf(xl_lo), a1, preferred_element_type=F32)
    gate = jnp.dot(_bf(_sigmoid(xl[:, LANES:])), g1, preferred_element_type=F32)
    stash.put("gate", gate.reshape(n, C, LANES))

    yield ("V", 60.0 * n)
    lw = (-DECAY_SCALE * _sigmoid(w0_ref[...] + dw)).reshape(n, C, LANES)
    a_lr = _sigmoid(a0_ref[...] + da).reshape(n, C, LANES)
    kk = xk * kk_ref[...]
    kk = kk * jnp.minimum(lax.rsqrt(head_sum(kk * kk)), 1e12)
    k2 = xk * (1.0 + (a_lr - 1.0) * ka_ref[...])
    b = kk * a_lr
    cum = jnp.stack([_cumsum_rows(tri, lw[j]) for j in ids])

    yield ("V", 80.0 * n)
    cum_last = cum[:, C - 1:C, :]
    e_in = jnp.exp(-cum)
    e_out = jnp.exp(cum_last - cum)
    at = -kk * jnp.exp(cum - lw)
    rt = r * jnp.exp(cum)
    bt = b * e_in
    kt = k2 * e_in
    stash.put("lhs", _bf(jnp.concatenate([stack_heads(at), stack_heads(rt)], axis=1)))
    stash.put("rhs", _bf(jnp.concatenate([bt, kt], axis=1)))
    stash.put("bt_hat", _bf(stack_heads(bt)))
    stash.put("atrt", _bf(jnp.concatenate([at, rt], axis=1)))
    stash.put("vb", _bf(v))
    stash.put("bkh", _bf(jnp.concatenate([b * e_out, k2 * e_out], axis=1)))
    stash.put("dec", jnp.exp(cum_last))
    stash.put("bonus", head_sum(r * k2 * rk_ref[...]) * v)


def _rwkv_consume(stash, lo, n, C, dot_cost, lnw_ref, lnb_ref, o_ref, s_scr):
    H = RWKV_HEAD
    ids = range(n)
    grp = slice(lo, lo + n)
    head_sum, stack_heads, merge_heads = _head_helpers(C)
    r2 = lax.broadcasted_iota(jnp.int32, (2 * C, 2 * C), 0)
    c2 = lax.broadcasted_iota(jnp.int32, (2 * C, 2 * C), 1)
    t2 = jnp.bitwise_and(r2, C - 1)
    s2 = jnp.bitwise_and(c2, C - 1)
    strict2 = t2 > s2
    incl2 = t2 >= s2
    strict_bd = jnp.logical_and(strict2, (r2 >= C) == (c2 >= C))
    eye2 = jnp.where(r2 == c2, 1.0, 0.0).astype(F32)
    bl_r = lax.broadcasted_iota(jnp.int32, (LANES, LANES), 0) < H
    bl_c = lax.broadcasted_iota(jnp.int32, (LANES, LANES), 1) < H
    blockdiag = bl_r == bl_c
    zeros_c = jnp.zeros((C, LANES), BF16)

    yield ("M", dot_cost)
    g_a, g_r = [], []
    for j in ids:
        g = _dot(stash.get("lhs", j), stash.get("rhs", j), NT)
        g_a.append(_bf(jnp.where(strict2, g[:2 * C], 0.0)))
        g_r.append(_bf(jnp.where(incl2, g[2 * C:], 0.0)))
    yield ("M", dot_cost)
    pw, inv = [], []
    for j in ids:
        n_bd = jnp.where(strict_bd, _dot(stash.get("lhs", j)[:2 * C], stash.get("bt_hat", j), NT), 0.0)
        inv.append(eye2 + n_bd)
        pw.append(_bf(n_bd))
    span = 2
    while span < C:
        yield ("M", dot_cost)
        pw = [_bf(_dot(pw[j], pw[j], NN)) for j in ids]
        yield ("M", dot_cost)
        inv = [inv[j] + _dot(_bf(inv[j]), pw[j], NN) for j in ids]
        span *= 2
    yield ("M", dot_cost)
    S = [s_scr[lo + j] for j in ids]
    pq = [_dot(stash.get("atrt", j), _bf(S[j]), NT) for j in ids]
    yield ("M", dot_cost)
    vb = [stash.get("vb", j) for j in ids]
    w_hat = [_bf(stack_heads(pq[j][:C] + merge_heads(_dot(g_a[j], jnp.concatenate([zeros_c, vb[j]], axis=0), NN))))
             for j in ids]
    yield ("M", dot_cost)
    z = []
    for j in ids:
        u2 = _dot(_bf(inv[j]), w_hat[j], NN)
        z.append(jnp.concatenate([_bf(u2[:C] + u2[C:]), vb[j]], axis=0))
    yield ("M", dot_cost)
    y = [pq[j][C:] + merge_heads(_dot(g_r[j], z[j], NN)) for j in ids]
    yield ("M", dot_cost)
    for j in ids:
        upd = _dot(z[j], stash.get("bkh", j), TN)
        s_scr[lo + j] = S[j] * stash.get("dec", j) + jnp.where(blockdiag, upd, 0.0)

    yield ("V", 40.0 * n)
    y = jnp.stack(y)
    mu = head_sum(y) * (1.0 / H)
    dy = y - mu
    var = head_sum(dy * dy) * (1.0 / H)
    yn = dy * lax.rsqrt(var + GN_EPS) * lnw_ref[...] + lnb_ref[...]
    o_ref[grp] = (yn + stash.get("bonus")) * stash.get("gate")


def _mid_rows(x, h):
    R = x.shape[0]
    row = lax.broadcasted_iota(jnp.int32, x.shape, 0)
    if 2 * h >= SUBLANES:
        return jnp.concatenate(
            [jnp.broadcast_to(x[b * 2 * h + h - 1:b * 2 * h + h], (2 * h, LANES)) for b in range(R // (2 * h))],
            axis=0)
    if h == 1:
        return jnp.where(jnp.bitwise_and(row, 1) == 1, pltpu.roll(x, 1, 0), x)
    picks = []
    for half in range(SUBLANES // (2 * h)):
        r0 = half * 2 * h + h - 1
        picks.append(jnp.concatenate(
            [jnp.broadcast_to(x[g * SUBLANES + r0:g * SUBLANES + r0 + 1], (SUBLANES, LANES))
             for g in range(R // SUBLANES)], axis=0))
    out = picks[-1]
    sub = jnp.bitwise_and(row, SUBLANES - 1)
    for half in range(len(picks) - 2, -1, -1):
        out = jnp.where(sub < (half + 1) * 2 * h, picks[half], out)
    return out


def _hgrn_produce(stash, lo, n, C, q_ref, f_ref, i_ref, lbl_ref):
    ids = range(n)
    grp = slice(lo, lo + n)
    logits = lbl_ref[...]
    ex = jnp.exp(logits - jnp.max(logits, axis=0, keepdims=True))
    lb = ex[0:1] / jnp.sum(ex, axis=0, keepdims=True)
    t_i = lax.broadcasted_iota(jnp.int32, (C, C), 0)
    s_i = lax.broadcasted_iota(jnp.int32, (C, C), 1)
    tri = jnp.where(t_i >= s_i, 1.0, 0.0).astype(BF16)

    yield ("V", 40.0 * n)
    q_h = q_ref[grp]
    f_h = f_ref[grp]
    q = q_h * _sigmoid(q_h)
    fg = lb + (1.0 - lb) * _sigmoid(f_h)
    kin = 1.0 - fg
    lf = jnp.log(fg)
    bc = jnp.stack([_cumsum_rows(tri, lf[j]) for j in ids])

    yield ("V", 30.0 * n)
    b_last = bc[:, C - 1:C, :]
    stash.put("qs", _bf(q * jnp.exp(bc)))
    stash.put("ks", _bf(kin * jnp.exp(b_last - bc)))
    stash.put("iv", _bf(i_ref[grp]))
    stash.put("q", q)
    stash.put("kin", kin)
    stash.put("bc", bc)
    stash.put("dec", jnp.exp(b_last))


def _hgrn_consume(stash, lo, n, C, dot_cost, g_ref, hgw_ref, o_ref, s_scr):
    ids = range(n)
    grp = slice(lo, lo + n)
    t_i = lax.broadcasted_iota(jnp.int32, (C, C), 0)
    s_i = lax.broadcasted_iota(jnp.int32, (C, C), 1)
    halves = []
    h = C // 2
    while h >= 1:
        halves.append(h)
        h //= 2

    def level_mask(h):
        same = jnp.bitwise_and(t_i, -2 * h) == jnp.bitwise_and(s_i, -2 * h)
        return jnp.logical_and(same, jnp.logical_and(jnp.bitwise_and(t_i, h) != 0, jnp.bitwise_and(s_i, h) == 0))

    def seq(x, j):
        return _bf(x[j * C:(j + 1) * C])

    yield ("M", dot_cost)
    S = [s_scr[lo + j] for j in ids]
    o_state = [_dot(stash.get("qs", j), _bf(S[j]), NT) for j in ids]
    yield ("M", dot_cost)
    iv = [stash.get("iv", j) for j in ids]
    for j in ids:
        s_scr[lo + j] = S[j] * stash.get("dec", j) + _dot(iv[j], stash.get("ks", j), TN)
    yield ("M", dot_cost)
    q = stash.get("q").reshape(n * C, LANES)
    kin = stash.get("kin").reshape(n * C, LANES)
    bc = stash.get("bc").reshape(n * C, LANES)
    att = [jnp.where(t_i == s_i, _dot(seq(q, j), seq(kin, j), NT), 0.0) for j in ids]
    for h in halves:
        yield ("V", 25.0 * n)
        dmid = jnp.exp(-jnp.abs(bc - _mid_rows(bc, h)))
        ql = q * dmid
        kl = kin * dmid
        yield ("M", dot_cost)
        msk = level_mask(h)
        att = [jnp.where(msk, _dot(seq(ql, j), seq(kl, j), NT), att[j]) for j in ids]
    yield ("M", dot_cost)
    o_intra = [_dot(_bf(att[j]), iv[j], NN) for j in ids]

    yield ("V", 25.0 * n)
    o = jnp.stack([o_intra[j] + o_state[j] for j in ids])
    ms = jnp.mean(o * o, axis=-1, keepdims=True)
    g_h = g_ref[grp]
    o_ref[grp] = o * lax.rsqrt(ms + RMS_EPS) * hgw_ref[...] * (g_h * _sigmoid(g_h))


MXU_STAGE_COST = {True: (45.0, 30.0), False: (30.0, 20.0)}
N_MIXER_IN = 30
N_MIXER_OUT = 4
N_MIXER_STATE_SCRATCH = 6


def _mixer_kernel(*refs, bb, C, n_groups, pipelined):
    (pr_ref, pk_ref, pv_ref, pl_ref, shr_ref, shk_ref, shv_ref, shl_ref,
     mur_ref, muk_ref, muv_ref, mul_ref, w0_ref, a0_ref, kk_ref, ka_ref, rk_ref,
     lnw_ref, lnb_ref, w1_ref, a1_ref, g1_ref, srw0_ref,
     q_ref, f_ref, i_ref, g_ref, lbl_ref, hgw_ref, shg0_ref) = refs[:N_MIXER_IN]
    orw_ref, strw_ref, ohg_ref, sthg_ref = refs[N_MIXER_IN:N_MIXER_IN + N_MIXER_OUT]
    scr = refs[N_MIXER_IN + N_MIXER_OUT:]
    srw_scr, cr_scr, ck_scr, cv_scr, cl_scr, shg_scr = scr[:N_MIXER_STATE_SCRATCH]
    stash_refs = scr[N_MIXER_STATE_SCRATCH:]
    if pipelined:
        rw_refs = {name: stash_refs[k] for k, (name, _, _) in enumerate(RWKV_STASH)}
        hg_refs = {name: stash_refs[len(RWKV_STASH) + k] for k, (name, _, _) in enumerate(HGRN_STASH)}

    c = pl.program_id(2)
    last = pl.num_programs(2) - 1
    H = RWKV_HEAD

    @pl.when(c == 0)
    def _init():
        z = jnp.zeros((H, H), F32)
        for i in range(bb):
            top = jnp.concatenate([srw0_ref[i, 0], z], axis=1)
            bot = jnp.concatenate([z, srw0_ref[i, 1]], axis=1)
            srw_scr[i] = jnp.concatenate([top, bot], axis=0)
            shg_scr[i] = shg0_ref[i, 0].T
        cr_scr[...] = shr_ref[...]
        ck_scr[...] = shk_ref[...]
        cv_scr[...] = shv_ref[...]
        cl_scr[...] = shl_ref[...]
        if pipelined:
            for table in (rw_refs, hg_refs):
                for name, ref in table.items():
                    fill = jnp.ones if name == "dec" else jnp.zeros
                    ref[1] = fill(ref.shape[1:], ref.dtype)

    per = bb // n_groups
    groups = [(gidx * per, per) for gidx in range(n_groups)]

    def rwkv_produce(stash, lo, n):
        return _rwkv_produce(stash, lo, n, C, pr_ref, pk_ref, pv_ref, pl_ref, mur_ref, muk_ref, muv_ref, mul_ref,
                             w0_ref, a0_ref, kk_ref, ka_ref, rk_ref, w1_ref, a1_ref, g1_ref,
                             cr_scr, ck_scr, cv_scr, cl_scr)

    def rwkv_consume(stash, lo, n):
        return _rwkv_consume(stash, lo, n, C, MXU_STAGE_COST[pipelined][0] * n, lnw_ref, lnb_ref, orw_ref, srw_scr)

    def hgrn_produce(stash, lo, n):
        return _hgrn_produce(stash, lo, n, C, q_ref, f_ref, i_ref, lbl_ref)

    def hgrn_consume(stash, lo, n):
        return _hgrn_consume(stash, lo, n, C, MXU_STAGE_COST[pipelined][1] * n, g_ref, hgw_ref, ohg_ref, shg_scr)

    if not pipelined:
        tasks = []
        for lo, n in groups:
            rw, hg = _ValueStash(), _ValueStash()
            tasks.append(_chain(rwkv_produce(rw, lo, n), rwkv_consume(rw, lo, n)))
            tasks.append(_chain(hgrn_produce(hg, lo, n), hgrn_consume(hg, lo, n)))
        _interleave(tasks)
    else:
        def step(write_slot):
            read_slot = 1 - write_slot
            tasks = []
            for lo, n in groups:
                tasks.append(rwkv_consume(_RefStash(rw_refs, read_slot, lo, n), lo, n))
                tasks.append(hgrn_consume(_RefStash(hg_refs, read_slot, lo, n), lo, n))
            for lo, n in groups:
                tasks.append(rwkv_produce(_RefStash(rw_refs, write_slot, lo, n), lo, n))
                tasks.append(hgrn_produce(_RefStash(hg_refs, write_slot, lo, n), lo, n))
            _interleave(tasks)

        parity = lax.rem(c, 2)
        pl.when(parity == 0)(functools.partial(step, 0))
        pl.when(parity == 1)(functools.partial(step, 1))

    @pl.when(c == last)
    def _fin():
        for i in range(bb):
            s_fin = srw_scr[i]
            strw_ref[i, 0] = s_fin[:H, :H]
            strw_ref[i, 1] = s_fin[H:, H:]
            sthg_ref[i, 0] = shg_scr[i].T


def _mixer(proj3, s_shift3, s_rwkv, s_hgrn, p, bb, C, n_groups, pipelined):
    B, T, _ = proj3.shape
    nc = T // C
    grid = (B // bb, PAIRS, nc + 1 if pipelined else nc)
    nb = RWKV_WIDTH // LANES
    hg0 = RWKV_PROJ // LANES
    lora_blk = LORA_COL // (2 * LANES)

    if pipelined:
        def produced(c):
            return jnp.minimum(c, nc - 1)

        def consumed(c):
            return jnp.maximum(c - 1, 0)
    else:
        produced = consumed = lambda c: c

    def tok(col0, chunk=produced):
        return pl.BlockSpec((bb, C, LANES), lambda b, h, c: (b, chunk(c), col0 + h))

    def first(col0):
        return pl.BlockSpec((bb, 1, LANES), lambda b, h, c: (b, 0, col0 + h))

    def vec(col0):
        return pl.BlockSpec((1, LANES), lambda b, h, c: (0, col0 + h))

    in_specs = [
        tok(0), tok(nb), tok(2 * nb),
        pl.BlockSpec((bb, C, 2 * LANES), lambda b, h, c: (b, produced(c), lora_blk)),
        first(0), first(nb), first(2 * nb),
        pl.BlockSpec((bb, 1, 2 * LANES), lambda b, h, c: (b, 0, lora_blk)),
        vec(0), vec(nb), vec(2 * nb),
        pl.BlockSpec((1, 2 * LANES), lambda b, h, c: (0, lora_blk)),
        vec(0), vec(0), vec(0), vec(0), vec(0), vec(0), vec(0),
        pl.BlockSpec((LANES, LANES), lambda b, h, c: (0, h)),
        pl.BlockSpec((LANES, LANES), lambda b, h, c: (0, h)),
        pl.BlockSpec((GATE_RANK, LANES), lambda b, h, c: (0, h)),
        pl.BlockSpec((bb, 2, RWKV_HEAD, RWKV_HEAD), lambda b, h, c: (b, h, 0, 0)),
        tok(hg0), tok(hg0 + nb), tok(hg0 + 2 * nb), tok(hg0 + 3 * nb, consumed),
        pl.BlockSpec((DEPTH + 1, LANES), lambda b, h, c: (0, h)),
        vec(0),
        pl.BlockSpec((bb, 1, HGRN_HEAD, HGRN_HEAD), lambda b, h, c: (b, h, 0, 0)),
    ]
    assert len(in_specs) == N_MIXER_IN
    out_specs = [
        pl.BlockSpec((bb, C, LANES), lambda b, h, c: (b, consumed(c), h)),
        pl.BlockSpec((bb, 2, RWKV_HEAD, RWKV_HEAD), lambda b, h, c: (b, h, 0, 0)),
        pl.BlockSpec((bb, C, LANES), lambda b, h, c: (b, consumed(c), h)),
        pl.BlockSpec((bb, 1, HGRN_HEAD, HGRN_HEAD), lambda b, h, c: (b, h, 0, 0)),
    ]
    out_shape = [jax.ShapeDtypeStruct((B, T, RWKV_WIDTH), F32),
                 jax.ShapeDtypeStruct((B, RWKV_HEADS, RWKV_HEAD, RWKV_HEAD), F32),
                 jax.ShapeDtypeStruct((B, T, HGRN_WIDTH), F32),
                 jax.ShapeDtypeStruct((B, HGRN_HEADS, HGRN_HEAD, HGRN_HEAD), F32)]
    scratch = [pltpu.VMEM((bb, LANES, LANES), F32),
               pltpu.VMEM((bb, 1, LANES), F32), pltpu.VMEM((bb, 1, LANES), F32),
               pltpu.VMEM((bb, 1, LANES), F32), pltpu.VMEM((bb, 1, 2 * LANES), F32),
               pltpu.VMEM((bb, HGRN_HEAD, HGRN_HEAD), F32)]
    assert len(scratch) == N_MIXER_STATE_SCRATCH
    if pipelined:
        scratch += [pltpu.VMEM((2, bb, rows * C if rows else 1, LANES), dt)
                    for _, rows, dt in RWKV_STASH + HGRN_STASH]
    return pl.pallas_call(
        functools.partial(_mixer_kernel, bb=bb, C=C, n_groups=n_groups, pipelined=pipelined),
        grid=grid, in_specs=in_specs, out_specs=out_specs, out_shape=out_shape,
        scratch_shapes=scratch,
        compiler_params=pltpu.CompilerParams(
            dimension_semantics=("arbitrary", "arbitrary", "arbitrary"),
            vmem_limit_bytes=VMEM_LIMIT),
        name="mixer",
    )(proj3, proj3, proj3, proj3, s_shift3, s_shift3, s_shift3, s_shift3,
      p["shift_mu"], p["shift_mu"], p["shift_mu"], p["shift_mu"],
      p["w0"], p["a0"], p["k_k"], p["k_a"], p["r_k"], p["ln_x_w"], p["ln_x_b"],
      p["w1u_pad"], p["a1u_pad"], p["g1u"], s_rwkv,
      proj3, proj3, proj3, proj3, p["lb_logits"], p["hg_norm_w"], s_hgrn)


def _layer_norm(x, g, b):
    mu = jnp.mean(x, axis=-1, keepdims=True)
    d = x - mu
    var = jnp.mean(d * d, axis=-1, keepdims=True)
    return d * lax.rsqrt(var + LN_EPS) * g + b


def _post_kernel(x_ref, orw_ref, ohg_ref, wo1_ref, wo2_ref, g1_ref, b1_ref, wup_ref, wdn_ref,
                 g2_ref, b2_ref, y_ref):
    tm = x_ref.shape[0]
    rows = [slice(k * tm // POST_SPLIT, (k + 1) * tm // POST_SPLIT) for k in range(POST_SPLIT)]

    def mix_stage(r):
        return (jnp.dot(orw_ref[r, :].astype(BF16), wo1_ref[...], preferred_element_type=F32)
                + jnp.dot(ohg_ref[r, :].astype(BF16), wo2_ref[...], preferred_element_type=F32))

    def up_stage(r, mix):
        h1 = _layer_norm(ALPHA * x_ref[r, :] + mix, g1_ref[...], b1_ref[...])
        return h1, jnp.dot(h1.astype(BF16), wup_ref[...], preferred_element_type=F32)

    def down_stage(up):
        up = jnp.square(jnp.maximum(up, 0.0))
        return jnp.dot(up.astype(BF16), wdn_ref[...], preferred_element_type=F32)

    def out_stage(r, h1, ff):
        y_ref[r, :] = _layer_norm(ALPHA * h1 + ff, g2_ref[...], b2_ref[...])

    mix, h1, up, ff = {}, {}, {}, {}
    for step in range(POST_SPLIT + 3):
        if step < POST_SPLIT:
            mix[step] = mix_stage(rows[step])
        k = step - 1
        if 0 <= k < POST_SPLIT:
            h1[k], up[k] = up_stage(rows[k], mix.pop(k))
        k = step - 2
        if 0 <= k < POST_SPLIT:
            ff[k] = down_stage(up.pop(k))
        k = step - 3
        if 0 <= k < POST_SPLIT:
            out_stage(rows[k], h1.pop(k), ff.pop(k))


def _post(x2, orw2, ohg2, p, tm):
    n = x2.shape[0]

    def const(shape):
        return pl.BlockSpec(shape, lambda i: (0, 0), pipeline_mode=pl.Buffered(1))

    return pl.pallas_call(
        _post_kernel,
        grid=(n // tm,),
        in_specs=[pl.BlockSpec((tm, D_MODEL), lambda i: (i, 0)),
                  pl.BlockSpec((tm, RWKV_WIDTH), lambda i: (i, 0)),
                  pl.BlockSpec((tm, HGRN_WIDTH), lambda i: (i, 0)),
                  const((RWKV_WIDTH, D_MODEL)), const((HGRN_WIDTH, D_MODEL)),
                  const((1, D_MODEL)), const((1, D_MODEL)),
                  const((D_MODEL, D_FF)), const((D_FF, D_MODEL)),
                  const((1, D_MODEL)), const((1, D_MODEL))],
        out_specs=pl.BlockSpec((tm, D_MODEL), lambda i: (i, 0)),
        out_shape=jax.ShapeDtypeStruct((n, D_MODEL), F32),
        compiler_params=pltpu.CompilerParams(dimension_semantics=("arbitrary",),
                                             vmem_limit_bytes=VMEM_LIMIT),
        name="post",
    )(x2, orw2, ohg2, p["wo_rw"], p["wo_hg"], p["ln1_g"], p["ln1_b"], p["w_up"], p["w_down"],
      p["ln2_g"], p["ln2_b"])


def _prep_params(w_in, shift_mu, w0, w1u, a0, a1u, g1u, k_k, k_a, r_k, ln_x_w, ln_x_b, lb_logits,
                 hg_norm_w, w_out, ln1_g, ln1_b, w_up, w_down, ln2_g, ln2_b):
    zw = jnp.zeros((LANES - DECAY_RANK, RWKV_WIDTH), F32)
    za = jnp.zeros((LANES - AICL_RANK, RWKV_WIDTH), F32)
    w_out_bf = w_out[0].astype(BF16)
    return {
        "w_in": w_in[0],
        "shift_mu": shift_mu[0].reshape(1, RWKV_PROJ),
        "w0": w0[0].reshape(1, RWKV_WIDTH), "a0": a0[0].reshape(1, RWKV_WIDTH),
        "k_k": k_k[0].reshape(1, RWKV_WIDTH), "k_a": k_a[0].reshape(1, RWKV_WIDTH),
        "r_k": r_k[0].reshape(1, RWKV_WIDTH),
        "ln_x_w": ln_x_w[0].reshape(1, RWKV_WIDTH), "ln_x_b": ln_x_b[0].reshape(1, RWKV_WIDTH),
        "w1u_pad": jnp.concatenate([w1u[0], zw], axis=0),
        "a1u_pad": jnp.concatenate([za, a1u[0]], axis=0),
        "g1u": g1u[0],
        "lb_logits": lb_logits.astype(F32),
        "hg_norm_w": hg_norm_w[0].reshape(1, HGRN_WIDTH),
        "wo_rw": w_out_bf[:RWKV_WIDTH], "wo_hg": w_out_bf[RWKV_WIDTH:],
        "ln1_g": ln1_g[0].reshape(1, D_MODEL), "ln1_b": ln1_b[0].reshape(1, D_MODEL),
        "w_up": w_up[0].astype(BF16), "w_down": w_down[0].astype(BF16),
        "ln2_g": ln2_g[0].reshape(1, D_MODEL), "ln2_b": ln2_b[0].reshape(1, D_MODEL),
    }


def _run_group(x, s_rwkv, s_hgrn, s_shift, p, *, tm, bb, chunk, n_groups, pipelined):
    B, T, _ = x.shape
    x2 = x.reshape(B * T, D_MODEL)
    proj2 = _proj(x2, p["w_in"], tm)
    proj3 = proj2.reshape(B, T, PROJ)
    o_rw, st_rw, o_hg, st_hg = _mixer(proj3, s_shift.reshape(B, 1, RWKV_PROJ), s_rwkv, s_hgrn, p,
                                      bb, chunk, n_groups, pipelined)
    y2 = _post(x2, o_rw.reshape(B * T, RWKV_WIDTH), o_hg.reshape(B * T, HGRN_WIDTH), p, tm)
    sh = proj3[:, T - 1, :RWKV_PROJ]
    return y2.reshape(B, T, D_MODEL), st_rw[None], st_hg[None], sh[None]


PROMPT_CFG = dict(tm=512, bb=8, chunk=64, n_groups=1, pipelined=True)
SAMPLE_CFG = dict(tm=512, bb=32, chunk=8, n_groups=1, pipelined=False)


def kernel(x_prompt, x_sample, state_rwkv, state_hgrn, state_shift, w_in, shift_mu, w0, w1u, a0, a1u, g1u, k_k, k_a, r_k, ln_x_w, ln_x_b, lb_logits, hg_norm_w, w_out, ln1_g, ln1_b, w_up, w_down, ln2_g, ln2_b):
    assert w_in.shape[0] == DEPTH
    p = _prep_params(w_in, shift_mu, w0, w1u, a0, a1u, g1u, k_k, k_a, r_k, ln_x_w, ln_x_b, lb_logits,
                     hg_norm_w, w_out, ln1_g, ln1_b, w_up, w_down, ln2_g, ln2_b)
    bp = x_prompt.shape[0]
    z_rw = jnp.zeros((bp, RWKV_HEADS, RWKV_HEAD, RWKV_HEAD), F32)
    z_hg = jnp.zeros((bp, HGRN_HEADS, HGRN_HEAD, HGRN_HEAD), F32)
    z_sh = jnp.zeros((bp, RWKV_PROJ), F32)
    y_p, rw_p, hg_p, sh_p = _run_group(x_prompt, z_rw, z_hg, z_sh, p, **PROMPT_CFG)
    y_s, rw_s, hg_s, sh_s = _run_group(x_sample, state_rwkv[0].astype(F32), state_hgrn[0].astype(F32),
                                       state_shift[0].astype(F32), p, **SAMPLE_CFG)
    return (y_p, y_s, rw_p, rw_s, hg_p, hg_s, sh_p, sh_s)
```

```python
import functools
import math

import jax
import jax.numpy as jnp
from jax import lax
from jax.experimental import pallas as pl
from jax.experimental.pallas import tpu as pltpu

F32 = jnp.float32
BF16 = jnp.bfloat16

D_MODEL = 1024
RWKV_WIDTH = 512
RWKV_HEAD = 64
RWKV_HEADS = 8
HGRN_WIDTH = 512
HGRN_HEAD = 128
HGRN_HEADS = 4
DECAY_RANK = 64
AICL_RANK = 64
GATE_RANK = 128
RWKV_PROJ = 3 * RWKV_WIDTH + DECAY_RANK + AICL_RANK + GATE_RANK
HGRN_PROJ = 4 * HGRN_WIDTH
PROJ = RWKV_PROJ + HGRN_PROJ
D_FF = 4 * D_MODEL
DEPTH = 1
ALPHA = (2.0 * DEPTH) ** 0.25
LN_EPS = 1e-5
GN_EPS = RWKV_HEAD * 1e-5
RMS_EPS = 1e-6
DECAY_SCALE = math.exp(-0.5)

LANES = 128
SUBLANES = 8
PAIRS = RWKV_HEADS // 2
LORA_COL = 3 * RWKV_WIDTH
VMEM_LIMIT = 56 * 1024 * 1024
POST_SPLIT = 2

NN = ((1,), (0,))
NT = ((1,), (1,))
TN = ((0,), (0,))

RWKV_STASH = (("lhs", 4, BF16), ("rhs", 2, BF16), ("bt_hat", 2, BF16), ("atrt", 2, BF16), ("vb", 1, BF16),
              ("bkh", 2, BF16), ("dec", 0, F32), ("bonus", 1, F32), ("gate", 1, F32))
HGRN_STASH = (("qs", 1, BF16), ("ks", 1, BF16), ("iv", 1, BF16), ("q", 1, F32), ("kin", 1, F32),
              ("bc", 1, F32), ("dec", 0, F32))


def _dot(a, b, dims):
    return lax.dot_general(a, b, (dims, ((), ())), preferred_element_type=F32)


def _bf(x):
    return x.astype(BF16)


def _cumsum_rows(tri, x):
    p0 = _bf(x)
    r1 = x - p0.astype(F32)
    p1 = _bf(r1)
    p2 = _bf(r1 - p1.astype(F32))
    return _dot(tri, p0, NN) + _dot(tri, p1, NN) + _dot(tri, p2, NN)


def _sigmoid(x):
    return 0.5 * jnp.tanh(0.5 * x) + 0.5


def _interleave(tasks):
    gens = list(tasks)
    nxt = [next(g, None) for g in gens]
    spent = {"M": 0.0, "V": 0.0}
    turn = 0
    while any(k is not None for k in nxt):
        ready = {kind: [j for j, k in enumerate(nxt) if k is not None and k[0] == kind] for kind in spent}
        want = "M" if (ready["M"] and (spent["M"] <= spent["V"] or not ready["V"])) else "V"
        if want == "M":
            cands = ready["M"]
            t = cands[turn % len(cands)]
            turn += 1
        else:
            t = ready["V"][0]
        spent[want] += nxt[t][1]
        nxt[t] = next(gens[t], None)


def _chain(*gens):
    for g in gens:
        yield from g


class _ValueStash:
    def __init__(self):
        self.vals = {}

    def put(self, name, val):
        self.vals[name] = val

    def get(self, name, j=None):
        return self.vals[name] if j is None else self.vals[name][j]


class _RefStash:
    def __init__(self, refs, slot, lo, n):
        self.refs, self.slot, self.lo, self.n = refs, slot, lo, n

    def put(self, name, val):
        self.refs[name][self.slot, self.lo:self.lo + self.n] = val

    def get(self, name, j=None):
        if j is None:
            return self.refs[name][self.slot, self.lo:self.lo + self.n]
        return self.refs[name][self.slot, self.lo + j]


def _proj_kernel(x_ref, w_ref, first_ref, mu_ref, lbl_ref, o_ref, last_ref, wbf_scr, carry_scr, *, tiles_per_seq):
    nb, tt, _ = x_ref.shape
    tile = pl.program_id(0)

    @pl.when(tile == 0)
    def _cast_weights():
        wbf_scr[...] = w_ref[...].astype(BF16)

    xb = x_ref[...].reshape(nb * tt, D_MODEL).astype(BF16)

    def cols(lo, width):
        return jnp.dot(xb, wbf_scr[:, lo:lo + width], preferred_element_type=F32)

    first = first_ref[...]
    if tiles_per_seq > 1:
        first = jnp.where(lax.rem(tile, tiles_per_seq) == 0, first, carry_scr[...])
    last_rows = []
    for lo in range(0, RWKV_PROJ, RWKV_WIDTH):
        width = min(RWKV_WIDTH, RWKV_PROJ - lo)
        p = cols(lo, width)
        p3 = p.reshape(nb, tt, width)
        row = lax.broadcasted_iota(jnp.int32, p3.shape, 1)
        prev = jnp.where(row == 0, first[:, :, lo:lo + width], pltpu.roll(p, 1, 0).reshape(p3.shape))
        o_ref[:, :, lo:lo + width] = p3 + mu_ref[:, lo:lo + width] * (prev - p3)
        last_rows.append(p3[:, tt - 1:tt, :])
    last = jnp.concatenate(last_rows, axis=-1)
    last_ref[...] = last
    if tiles_per_seq > 1:
        carry_scr[...] = last

    logits = lbl_ref[...]
    ex = jnp.exp(logits - jnp.max(logits, axis=0, keepdims=True))
    lb = ex[0:1] / jnp.sum(ex, axis=0, keepdims=True)
    q_h = cols(RWKV_PROJ, HGRN_WIDTH)
    o_ref[:, :, RWKV_PROJ:RWKV_PROJ + HGRN_WIDTH] = (q_h * _sigmoid(q_h)).reshape(nb, tt, HGRN_WIDTH)
    f_h = cols(RWKV_PROJ + HGRN_WIDTH, HGRN_WIDTH)
    lf = jnp.log(lb + (1.0 - lb) * _sigmoid(f_h))
    o_ref[:, :, RWKV_PROJ + HGRN_WIDTH:RWKV_PROJ + 2 * HGRN_WIDTH] = lf.reshape(nb, tt, HGRN_WIDTH)
    i_h = cols(RWKV_PROJ + 2 * HGRN_WIDTH, HGRN_WIDTH)
    o_ref[:, :, RWKV_PROJ + 2 * HGRN_WIDTH:RWKV_PROJ + 3 * HGRN_WIDTH] = i_h.reshape(nb, tt, HGRN_WIDTH)
    g_h = cols(RWKV_PROJ + 3 * HGRN_WIDTH, HGRN_WIDTH)
    o_ref[:, :, RWKV_PROJ + 3 * HGRN_WIDTH:] = (g_h * _sigmoid(g_h)).reshape(nb, tt, HGRN_WIDTH)


def _proj(x3, s_shift3, p, tm):
    B, T, _ = x3.shape
    tt = min(T, tm)
    nb = tm // tt
    tiles_per_seq = T // tt

    def tile_map(i):
        return (i // tiles_per_seq, lax.rem(i, tiles_per_seq), 0)

    def seq_map(i):
        return (i // tiles_per_seq, 0, 0)

    def const(shape):
        return pl.BlockSpec(shape, lambda i: (0, 0), pipeline_mode=pl.Buffered(1))

    return pl.pallas_call(
        functools.partial(_proj_kernel, tiles_per_seq=tiles_per_seq),
        grid=(B * T // tm,),
        in_specs=[pl.BlockSpec((nb, tt, D_MODEL), tile_map),
                  const((D_MODEL, PROJ)),
                  pl.BlockSpec((nb, 1, RWKV_PROJ), seq_map),
                  const((1, RWKV_PROJ)), const((DEPTH + 1, HGRN_WIDTH))],
        out_specs=[pl.BlockSpec((nb, tt, PROJ), tile_map),
                   pl.BlockSpec((nb, 1, RWKV_PROJ), seq_map)],
        out_shape=[jax.ShapeDtypeStruct((B, T, PROJ), F32),
                   jax.ShapeDtypeStruct((B, 1, RWKV_PROJ), F32)],
        scratch_shapes=[pltpu.VMEM((D_MODEL, PROJ), BF16), pltpu.VMEM((nb, 1, RWKV_PROJ), F32)],
        compiler_params=pltpu.CompilerParams(dimension_semantics=("arbitrary",),
                                             vmem_limit_bytes=VMEM_LIMIT),
        name="proj",
    )(x3, p["w_in"], s_shift3, p["shift_mu"], p["lb_logits"])


def _head_helpers(C):
    lane = lax.broadcasted_iota(jnp.int32, (C, LANES), 1)
    m0 = lane < RWKV_HEAD

    def head_sum(x):
        s0 = jnp.sum(jnp.where(m0, x, 0.0), axis=-1, keepdims=True)
        s1 = jnp.sum(jnp.where(m0, 0.0, x), axis=-1, keepdims=True)
        return jnp.where(m0, s0, s1)

    def stack_heads(x):
        return jnp.concatenate([jnp.where(m0, x, 0.0), jnp.where(m0, 0.0, x)], axis=-2)

    def merge_heads(x):
        return jnp.where(m0, x[:C], x[C:])

    return head_sum, stack_heads, merge_heads


def _rwkv_produce(stash, lo, n, C, r_ref, k_ref, v_ref, l_ref,
                  w0_ref, a0_ref, kk_ref, ka_ref, rk_ref, w1_ref, a1_ref, g1_ref):
    ids = range(n)
    grp = slice(lo, lo + n)
    head_sum, stack_heads, _ = _head_helpers(C)
    rowc = lax.broadcasted_iota(jnp.int32, (C, C), 0)
    colc = lax.broadcasted_iota(jnp.int32, (C, C), 1)
    tri = jnp.where(rowc >= colc, 1.0, 0.0).astype(BF16)

    w1 = _bf(w1_ref[...])
    a1 = _bf(a1_ref[...])
    g1 = _bf(g1_ref[...])

    yield ("V", 30.0 * n)
    xl = l_ref[grp].reshape(n * C, 2 * LANES)
    xl_lo = xl[:, :LANES]
    dw = jnp.dot(_bf(jnp.tanh(xl_lo)), w1, preferred_element_type=F32)
    da = jnp.dot(_bf(xl_lo), a1, preferred_element_type=F32)
    gate = jnp.dot(_bf(_sigmoid(xl[:, LANES:])), g1, preferred_element_type=F32)
    stash.put("gate", gate.reshape(n, C, LANES))

    yield ("V", 60.0 * n)
    lw = (-DECAY_SCALE * _sigmoid(w0_ref[...] + dw)).reshape(n, C, LANES)
    a_lr = _sigmoid(a0_ref[...] + da).reshape(n, C, LANES)
    xk = k_ref[grp]
    kk = xk * kk_ref[...]
    kk = kk * jnp.minimum(lax.rsqrt(head_sum(kk * kk)), 1e12)
    k2 = xk * (1.0 + (a_lr - 1.0) * ka_ref[...])
    b = kk * a_lr
    cum = jnp.stack([_cumsum_rows(tri, lw[j]) for j in ids])

    yield ("V", 80.0 * n)
    r = r_ref[grp]
    v = v_ref[grp]
    cum_last = cum[:, C - 1:C, :]
    e_in = jnp.exp(-cum)
    e_out = jnp.exp(cum_last - cum)
    at = -kk * jnp.exp(cum - lw)
    rt = r * jnp.exp(cum)
    bt = b * e_in
    kt = k2 * e_in
    stash.put("lhs", _bf(jnp.concatenate([stack_heads(at), stack_heads(rt)], axis=1)))
    stash.put("rhs", _bf(jnp.concatenate([bt, kt], axis=1)))
    stash.put("bt_hat", _bf(stack_heads(bt)))
    stash.put("atrt", _bf(jnp.concatenate([at, rt], axis=1)))
    stash.put("vb", _bf(v))
    stash.put("bkh", _bf(jnp.concatenate([b * e_out, k2 * e_out], axis=1)))
    stash.put("dec", jnp.exp(cum_last))
    stash.put("bonus", head_sum(r * k2 * rk_ref[...]) * v)


def _rwkv_consume(stash, lo, n, C, dot_cost, lnw_ref, lnb_ref, o_ref, s_scr):
    H = RWKV_HEAD
    ids = range(n)
    grp = slice(lo, lo + n)
    head_sum, stack_heads, merge_heads = _head_helpers(C)
    r2 = lax.broadcasted_iota(jnp.int32, (2 * C, 2 * C), 0)
    c2 = lax.broadcasted_iota(jnp.int32, (2 * C, 2 * C), 1)
    t2 = jnp.bitwise_and(r2, C - 1)
    s2 = jnp.bitwise_and(c2, C - 1)
    strict2 = t2 > s2
    incl2 = t2 >= s2
    strict_bd = jnp.logical_and(strict2, (r2 >= C) == (c2 >= C))
    eye2 = jnp.where(r2 == c2, 1.0, 0.0).astype(F32)
    bl_r = lax.broadcasted_iota(jnp.int32, (LANES, LANES), 0) < H
    bl_c = lax.broadcasted_iota(jnp.int32, (LANES, LANES), 1) < H
    blockdiag = bl_r == bl_c
    zeros_c = jnp.zeros((C, LANES), BF16)

    yield ("M", dot_cost)
    g_a, g_r = [], []
    for j in ids:
        g = _dot(stash.get("lhs", j), stash.get("rhs", j), NT)
        g_a.append(_bf(jnp.where(strict2, g[:2 * C], 0.0)))
        g_r.append(_bf(jnp.where(incl2, g[2 * C:], 0.0)))
    yield ("M", dot_cost)
    pw, inv = [], []
    for j in ids:
        n_bd = jnp.where(strict_bd, _dot(stash.get("lhs", j)[:2 * C], stash.get("bt_hat", j), NT), 0.0)
        inv.append(eye2 + n_bd)
        pw.append(_bf(n_bd))
    span = 2
    while span < C:
        yield ("M", dot_cost)
        pw = [_bf(_dot(pw[j], pw[j], NN)) for j in ids]
        yield ("M", dot_cost)
        inv = [inv[j] + _dot(_bf(inv[j]), pw[j], NN) for j in ids]
        span *= 2
    yield ("M", dot_cost)
    S = [s_scr[lo + j] for j in ids]
    pq = [_dot(stash.get("atrt", j), _bf(S[j]), NT) for j in ids]
    yield ("M", dot_cost)
    vb = [stash.get("vb", j) for j in ids]
    w_hat = [_bf(stack_heads(pq[j][:C] + merge_heads(_dot(g_a[j], jnp.concatenate([zeros_c, vb[j]], axis=0), NN))))
             for j in ids]
    yield ("M", dot_cost)
    z = []
    for j in ids:
        u2 = _dot(_bf(inv[j]), w_hat[j], NN)
        z.append(jnp.concatenate([_bf(u2[:C] + u2[C:]), vb[j]], axis=0))
    yield ("M", dot_cost)
    y = [pq[j][C:] + merge_heads(_dot(g_r[j], z[j], NN)) for j in ids]
    yield ("M", dot_cost)
    for j in ids:
        upd = _dot(z[j], stash.get("bkh", j), TN)
        s_scr[lo + j] = S[j] * stash.get("dec", j) + jnp.where(blockdiag, upd, 0.0)

    yield ("V", 40.0 * n)
    y = jnp.stack(y)
    mu = head_sum(y) * (1.0 / H)
    dy = y - mu
    var = head_sum(dy * dy) * (1.0 / H)
    yn = dy * lax.rsqrt(var + GN_EPS) * lnw_ref[...] + lnb_ref[...]
    o_ref[grp] = (yn + stash.get("bonus")) * stash.get("gate")


def _mid_rows(x, h):
    R = x.shape[0]
    row = lax.broadcasted_iota(jnp.int32, x.shape, 0)
    if 2 * h >= SUBLANES:
        return jnp.concatenate(
            [jnp.broadcast_to(x[b * 2 * h + h - 1:b * 2 * h + h], (2 * h, LANES)) for b in range(R // (2 * h))],
            axis=0)
    if h == 1:
        return jnp.where(jnp.bitwise_and(row, 1) == 1, pltpu.roll(x, 1, 0), x)
    picks = []
    for half in range(SUBLANES // (2 * h)):
        r0 = half * 2 * h + h - 1
        picks.append(jnp.concatenate(
            [jnp.broadcast_to(x[g * SUBLANES + r0:g * SUBLANES + r0 + 1], (SUBLANES, LANES))
             for g in range(R // SUBLANES)], axis=0))
    out = picks[-1]
    sub = jnp.bitwise_and(row, SUBLANES - 1)
    for half in range(len(picks) - 2, -1, -1):
        out = jnp.where(sub < (half + 1) * 2 * h, picks[half], out)
    return out


def _hgrn_produce(stash, lo, n, C, q_ref, lf_ref, i_ref):
    ids = range(n)
    grp = slice(lo, lo + n)
    t_i = lax.broadcasted_iota(jnp.int32, (C, C), 0)
    s_i = lax.broadcasted_iota(jnp.int32, (C, C), 1)
    tri = jnp.where(t_i >= s_i, 1.0, 0.0).astype(BF16)

    yield ("V", 20.0 * n)
    q = q_ref[grp]
    lf = lf_ref[grp]
    kin = 1.0 - jnp.exp(lf)
    bc = jnp.stack([_cumsum_rows(tri, lf[j]) for j in ids])

    yield ("V", 30.0 * n)
    b_last = bc[:, C - 1:C, :]
    stash.put("qs", _bf(q * jnp.exp(bc)))
    stash.put("ks", _bf(kin * jnp.exp(b_last - bc)))
    stash.put("iv", _bf(i_ref[grp]))
    stash.put("q", q)
    stash.put("kin", kin)
    stash.put("bc", bc)
    stash.put("dec", jnp.exp(b_last))


def _hgrn_consume(stash, lo, n, C, dot_cost, g_ref, hgw_ref, o_ref, s_scr):
    ids = range(n)
    grp = slice(lo, lo + n)
    t_i = lax.broadcasted_iota(jnp.int32, (C, C), 0)
    s_i = lax.broadcasted_iota(jnp.int32, (C, C), 1)
    halves = []
    h = C // 2
    while h >= 1:
        halves.append(h)
        h //= 2

    def level_mask(h):
        same = jnp.bitwise_and(t_i, -2 * h) == jnp.bitwise_and(s_i, -2 * h)
        return jnp.logical_and(same, jnp.logical_and(jnp.bitwise_and(t_i, h) != 0, jnp.bitwise_and(s_i, h) == 0))

    def seq(x, j):
        return _bf(x[j * C:(j + 1) * C])

    yield ("M", dot_cost)
    S = [s_scr[lo + j] for j in ids]
    o_state = [_dot(stash.get("qs", j), _bf(S[j]), NT) for j in ids]
    yield ("M", dot_cost)
    iv = [stash.get("iv", j) for j in ids]
    for j in ids:
        s_scr[lo + j] = S[j] * stash.get("dec", j) + _dot(iv[j], stash.get("ks", j), TN)
    yield ("M", dot_cost)
    q = stash.get("q").reshape(n * C, LANES)
    kin = stash.get("kin").reshape(n * C, LANES)
    bc = stash.get("bc").reshape(n * C, LANES)
    att = [jnp.where(t_i == s_i, _dot(seq(q, j), seq(kin, j), NT), 0.0) for j in ids]
    for h in halves:
        yield ("V", 25.0 * n)
        dmid = jnp.exp(-jnp.abs(bc - _mid_rows(bc, h)))
        ql = q * dmid
        kl = kin * dmid
        yield ("M", dot_cost)
        msk = level_mask(h)
        att = [jnp.where(msk, _dot(seq(ql, j), seq(kl, j), NT), att[j]) for j in ids]
    yield ("M", dot_cost)
    o_intra = [_dot(_bf(att[j]), iv[j], NN) for j in ids]

    yield ("V", 25.0 * n)
    o = jnp.stack([o_intra[j] + o_state[j] for j in ids])
    ms = jnp.mean(o * o, axis=-1, keepdims=True)
    o_ref[grp] = o * lax.rsqrt(ms + RMS_EPS) * hgw_ref[...] * g_ref[grp]


MXU_STAGE_COST = {True: (45.0, 30.0), False: (30.0, 20.0)}
N_MIXER_IN = 21
N_MIXER_OUT = 4
N_MIXER_STATE_SCRATCH = 2


def _mixer_kernel(*refs, bb, C, n_groups, pipelined):
    (r_ref, k_ref, v_ref, l_ref, w0_ref, a0_ref, kk_ref, ka_ref, rk_ref,
     lnw_ref, lnb_ref, w1_ref, a1_ref, g1_ref, srw0_ref,
     q_ref, lf_ref, i_ref, g_ref, hgw_ref, shg0_ref) = refs[:N_MIXER_IN]
    orw_ref, strw_ref, ohg_ref, sthg_ref = refs[N_MIXER_IN:N_MIXER_IN + N_MIXER_OUT]
    scr = refs[N_MIXER_IN + N_MIXER_OUT:]
    srw_scr, shg_scr = scr[:N_MIXER_STATE_SCRATCH]
    stash_refs = scr[N_MIXER_STATE_SCRATCH:]
    if pipelined:
        rw_refs = {name: stash_refs[k] for k, (name, _, _) in enumerate(RWKV_STASH)}
        hg_refs = {name: stash_refs[len(RWKV_STASH) + k] for k, (name, _, _) in enumerate(HGRN_STASH)}

    c = pl.program_id(2)
    last = pl.num_programs(2) - 1
    H = RWKV_HEAD

    @pl.when(c == 0)
    def _init():
        z = jnp.zeros((H, H), F32)
        for i in range(bb):
            top = jnp.concatenate([srw0_ref[i, 0], z], axis=1)
            bot = jnp.concatenate([z, srw0_ref[i, 1]], axis=1)
            srw_scr[i] = jnp.concatenate([top, bot], axis=0)
            shg_scr[i] = shg0_ref[i, 0].T
        if pipelined:
            for table in (rw_refs, hg_refs):
                for name, ref in table.items():
                    fill = jnp.ones if name == "dec" else jnp.zeros
                    ref[1] = fill(ref.shape[1:], ref.dtype)

    per = bb // n_groups
    groups = [(gidx * per, per) for gidx in range(n_groups)]

    def rwkv_produce(stash, lo, n):
        return _rwkv_produce(stash, lo, n, C, r_ref, k_ref, v_ref, l_ref,
                             w0_ref, a0_ref, kk_ref, ka_ref, rk_ref, w1_ref, a1_ref, g1_ref)

    def rwkv_consume(stash, lo, n):
        return _rwkv_consume(stash, lo, n, C, MXU_STAGE_COST[pipelined][0] * n, lnw_ref, lnb_ref, orw_ref, srw_scr)

    def hgrn_produce(stash, lo, n):
        return _hgrn_produce(stash, lo, n, C, q_ref, lf_ref, i_ref)

    def hgrn_consume(stash, lo, n):
        return _hgrn_consume(stash, lo, n, C, MXU_STAGE_COST[pipelined][1] * n, g_ref, hgw_ref, ohg_ref, shg_scr)

    if not pipelined:
        tasks = []
        for lo, n in groups:
            rw, hg = _ValueStash(), _ValueStash()
            tasks.append(_chain(rwkv_produce(rw, lo, n), rwkv_consume(rw, lo, n)))
            tasks.append(_chain(hgrn_produce(hg, lo, n), hgrn_consume(hg, lo, n)))
        _interleave(tasks)
    else:
        def step(write_slot):
            read_slot = 1 - write_slot
            tasks = []
            for lo, n in groups:
                tasks.append(rwkv_consume(_RefStash(rw_refs, read_slot, lo, n), lo, n))
                tasks.append(hgrn_consume(_RefStash(hg_refs, read_slot, lo, n), lo, n))
            for lo, n in groups:
                tasks.append(rwkv_produce(_RefStash(rw_refs, write_slot, lo, n), lo, n))
                tasks.append(hgrn_produce(_RefStash(hg_refs, write_slot, lo, n), lo, n))
            _interleave(tasks)

        parity = lax.rem(c, 2)
        pl.when(parity == 0)(functools.partial(step, 0))
        pl.when(parity == 1)(functools.partial(step, 1))

    @pl.when(c == last)
    def _fin():
        for i in range(bb):
            s_fin = srw_scr[i]
            strw_ref[i, 0] = s_fin[:H, :H]
            strw_ref[i, 1] = s_fin[H:, H:]
            sthg_ref[i, 0] = shg_scr[i].T


def _mixer(proj3, s_rwkv, s_hgrn, p, bb, C, n_groups, pipelined):
    B, T, _ = proj3.shape
    nc = T // C
    grid = (B // bb, PAIRS, nc + 1 if pipelined else nc)
    nb = RWKV_WIDTH // LANES
    hg0 = RWKV_PROJ // LANES
    lora_blk = LORA_COL // (2 * LANES)

    if pipelined:
        def produced(c):
            return jnp.minimum(c, nc - 1)

        def consumed(c):
            return jnp.maximum(c - 1, 0)
    else:
        produced = consumed = lambda c: c

    def tok(col0, chunk=produced):
        return pl.BlockSpec((bb, C, LANES), lambda b, h, c: (b, chunk(c), col0 + h))

    def vec(col0):
        return pl.BlockSpec((1, LANES), lambda b, h, c: (0, col0 + h))

    in_specs = [
        tok(0), tok(nb), tok(2 * nb),
        pl.BlockSpec((bb, C, 2 * LANES), lambda b, h, c: (b, produced(c), lora_blk)),
        vec(0), vec(0), vec(0), vec(0), vec(0), vec(0), vec(0),
        pl.BlockSpec((LANES, LANES), lambda b, h, c: (0, h)),
        pl.BlockSpec((LANES, LANES), lambda b, h, c: (0, h)),
        pl.BlockSpec((GATE_RANK, LANES), lambda b, h, c: (0, h)),
        pl.BlockSpec((bb, 2, RWKV_HEAD, RWKV_HEAD), lambda b, h, c: (b, h, 0, 0)),
        tok(hg0), tok(hg0 + nb), tok(hg0 + 2 * nb), tok(hg0 + 3 * nb, consumed),
        vec(0),
        pl.BlockSpec((bb, 1, HGRN_HEAD, HGRN_HEAD), lambda b, h, c: (b, h, 0, 0)),
    ]
    assert len(in_specs) == N_MIXER_IN
    out_specs = [
        pl.BlockSpec((bb, C, LANES), lambda b, h, c: (b, consumed(c), h)),
        pl.BlockSpec((bb, 2, RWKV_HEAD, RWKV_HEAD), lambda b, h, c: (b, h, 0, 0)),
        pl.BlockSpec((bb, C, LANES), lambda b, h, c: (b, consumed(c), h)),
        pl.BlockSpec((bb, 1, HGRN_HEAD, HGRN_HEAD), lambda b, h, c: (b, h, 0, 0)),
    ]
    out_shape = [jax.ShapeDtypeStruct((B, T, RWKV_WIDTH), F32),
                 jax.ShapeDtypeStruct((B, RWKV_HEADS, RWKV_HEAD, RWKV_HEAD), F32),
                 jax.ShapeDtypeStruct((B, T, HGRN_WIDTH), F32),
                 jax.ShapeDtypeStruct((B, HGRN_HEADS, HGRN_HEAD, HGRN_HEAD), F32)]
    scratch = [pltpu.VMEM((bb, LANES, LANES), F32), pltpu.VMEM((bb, HGRN_HEAD, HGRN_HEAD), F32)]
    assert len(scratch) == N_MIXER_STATE_SCRATCH
    if pipelined:
        scratch += [pltpu.VMEM((2, bb, rows * C if rows else 1, LANES), dt)
                    for _, rows, dt in RWKV_STASH + HGRN_STASH]
    return pl.pallas_call(
        functools.partial(_mixer_kernel, bb=bb, C=C, n_groups=n_groups, pipelined=pipelined),
        grid=grid, in_specs=in_specs, out_specs=out_specs, out_shape=out_shape,
        scratch_shapes=scratch,
        compiler_params=pltpu.CompilerParams(
            dimension_semantics=("arbitrary", "arbitrary", "arbitrary"),
            vmem_limit_bytes=VMEM_LIMIT),
        name="mixer",
    )(proj3, proj3, proj3, proj3,
      p["w0"], p["a0"], p["k_k"], p["k_a"], p["r_k"], p["ln_x_w"], p["ln_x_b"],
      p["w1u_pad"], p["a1u_pad"], p["g1u"], s_rwkv,
      proj3, proj3, proj3, proj3, p["hg_norm_w"], s_hgrn)


def _layer_norm(x, g, b):
    mu = jnp.mean(x, axis=-1, keepdims=True)
    d = x - mu
    var = jnp.mean(d * d, axis=-1, keepdims=True)
    return d * lax.rsqrt(var + LN_EPS) * g + b


def _post_kernel(x_ref, orw_ref, ohg_ref, wo1_ref, wo2_ref, g1_ref, b1_ref, wup_ref, wdn_ref,
                 g2_ref, b2_ref, y_ref):
    tm = x_ref.shape[0]
    rows = [slice(k * tm // POST_SPLIT, (k + 1) * tm // POST_SPLIT) for k in range(POST_SPLIT)]

    def mix_stage(r):
        return (jnp.dot(orw_ref[r, :].astype(BF16), wo1_ref[...], preferred_element_type=F32)
                + jnp.dot(ohg_ref[r, :].astype(BF16), wo2_ref[...], preferred_element_type=F32))

    def up_stage(r, mix):
        h1 = _layer_norm(ALPHA * x_ref[r, :] + mix, g1_ref[...], b1_ref[...])
        return h1, jnp.dot(h1.astype(BF16), wup_ref[...], preferred_element_type=F32)

    def down_stage(up):
        up = jnp.square(jnp.maximum(up, 0.0))
        return jnp.dot(up.astype(BF16), wdn_ref[...], preferred_element_type=F32)

    def out_stage(r, h1, ff):
        y_ref[r, :] = _layer_norm(ALPHA * h1 + ff, g2_ref[...], b2_ref[...])

    mix, h1, up, ff = {}, {}, {}, {}
    for step in range(POST_SPLIT + 3):
        if step < POST_SPLIT:
            mix[step] = mix_stage(rows[step])
        k = step - 1
        if 0 <= k < POST_SPLIT:
            h1[k], up[k] = up_stage(rows[k], mix.pop(k))
        k = step - 2
        if 0 <= k < POST_SPLIT:
            ff[k] = down_stage(up.pop(k))
        k = step - 3
        if 0 <= k < POST_SPLIT:
            out_stage(rows[k], h1.pop(k), ff.pop(k))


def _post(x2, orw2, ohg2, p, tm):
    n = x2.shape[0]

    def const(shape):
        return pl.BlockSpec(shape, lambda i: (0, 0), pipeline_mode=pl.Buffered(1))

    return pl.pallas_call(
        _post_kernel,
        grid=(n // tm,),
        in_specs=[pl.BlockSpec((tm, D_MODEL), lambda i: (i, 0)),
                  pl.BlockSpec((tm, RWKV_WIDTH), lambda i: (i, 0)),
                  pl.BlockSpec((tm, HGRN_WIDTH), lambda i: (i, 0)),
                  const((RWKV_WIDTH, D_MODEL)), const((HGRN_WIDTH, D_MODEL)),
                  const((1, D_MODEL)), const((1, D_MODEL)),
                  const((D_MODEL, D_FF)), const((D_FF, D_MODEL)),
                  const((1, D_MODEL)), const((1, D_MODEL))],
        out_specs=pl.BlockSpec((tm, D_MODEL), lambda i: (i, 0)),
        out_shape=jax.ShapeDtypeStruct((n, D_MODEL), F32),
        compiler_params=pltpu.CompilerParams(dimension_semantics=("arbitrary",),
                                             vmem_limit_bytes=VMEM_LIMIT),
        name="post",
    )(x2, orw2, ohg2, p["wo_rw"], p["wo_hg"], p["ln1_g"], p["ln1_b"], p["w_up"], p["w_down"],
      p["ln2_g"], p["ln2_b"])


def _prep_params(w_in, shift_mu, w0, w1u, a0, a1u, g1u, k_k, k_a, r_k, ln_x_w, ln_x_b, lb_logits,
                 hg_norm_w, w_out, ln1_g, ln1_b, w_up, w_down, ln2_g, ln2_b):
    zw = jnp.zeros((LANES - DECAY_RANK, RWKV_WIDTH), F32)
    za = jnp.zeros((LANES - AICL_RANK, RWKV_WIDTH), F32)
    w_out_bf = w_out[0].astype(BF16)
    return {
        "w_in": w_in[0],
        "shift_mu": shift_mu[0].reshape(1, RWKV_PROJ),
        "w0": w0[0].reshape(1, RWKV_WIDTH), "a0": a0[0].reshape(1, RWKV_WIDTH),
        "k_k": k_k[0].reshape(1, RWKV_WIDTH), "k_a": k_a[0].reshape(1, RWKV_WIDTH),
        "r_k": r_k[0].reshape(1, RWKV_WIDTH),
        "ln_x_w": ln_x_w[0].reshape(1, RWKV_WIDTH), "ln_x_b": ln_x_b[0].reshape(1, RWKV_WIDTH),
        "w1u_pad": jnp.concatenate([w1u[0], zw], axis=0),
        "a1u_pad": jnp.concatenate([za, a1u[0]], axis=0),
        "g1u": g1u[0],
        "lb_logits": lb_logits.astype(F32),
        "hg_norm_w": hg_norm_w[0].reshape(1, HGRN_WIDTH),
        "wo_rw": w_out_bf[:RWKV_WIDTH], "wo_hg": w_out_bf[RWKV_WIDTH:],
        "ln1_g": ln1_g[0].reshape(1, D_MODEL), "ln1_b": ln1_b[0].reshape(1, D_MODEL),
        "w_up": w_up[0].astype(BF16), "w_down": w_down[0].astype(BF16),
        "ln2_g": ln2_g[0].reshape(1, D_MODEL), "ln2_b": ln2_b[0].reshape(1, D_MODEL),
    }


def _run_group(x, s_rwkv, s_hgrn, s_shift, p, *, tm, bb, chunk, n_groups, pipelined):
    B, T, _ = x.shape
    proj3, last = _proj(x, s_shift.reshape(B, 1, RWKV_PROJ), p, tm)
    o_rw, st_rw, o_hg, st_hg = _mixer(proj3, s_rwkv, s_hgrn, p, bb, chunk, n_groups, pipelined)
    y2 = _post(x.reshape(B * T, D_MODEL), o_rw.reshape(B * T, RWKV_WIDTH), o_hg.reshape(B * T, HGRN_WIDTH),
               p, tm)
    return y2.reshape(B, T, D_MODEL), st_rw[None], st_hg[None], last.reshape(1, B, RWKV_PROJ)


PROMPT_CFG = dict(tm=512, bb=8, chunk=64, n_groups=1, pipelined=True)
SAMPLE_CFG = dict(tm=512, bb=32, chunk=8, n_groups=1, pipelined=False)


def kernel(x_prompt, x_sample, state_rwkv, state_hgrn, state_shift, w_in, shift_mu, w0, w1u, a0, a1u, g1u, k_k, k_a, r_k, ln_x_w, ln_x_b, lb_logits, hg_norm_w, w_out, ln1_g, ln1_b, w_up, w_down, ln2_g, ln2_b):
    assert w_in.shape[0] == DEPTH
    p = _prep_params(w_in, shift_mu, w0, w1u, a0, a1u, g1u, k_k, k_a, r_k, ln_x_w, ln_x_b, lb_logits,
                     hg_norm_w, w_out, ln1_g, ln1_b, w_up, w_down, ln2_g, ln2_b)
    bp = x_prompt.shape[0]
    z_rw = jnp.zeros((bp, RWKV_HEADS, RWKV_HEAD, RWKV_HEAD), F32)
    z_hg = jnp.zeros((bp, HGRN_HEADS, HGRN_HEAD, HGRN_HEAD), F32)
    z_sh = jnp.zeros((bp, RWKV_PROJ), F32)
    y_p, rw_p, hg_p, sh_p = _run_group(x_prompt, z_rw, z_hg, z_sh, p, **PROMPT_CFG)
    y_s, rw_s, hg_s, sh_s = _run_group(x_sample, state_rwkv[0].astype(F32), state_hgrn[0].astype(F32),
                                       state_shift[0].astype(F32), p, **SAMPLE_CFG)
    return (y_p, y_s, rw_p, rw_s, hg_p, hg_s, sh_p, sh_s)
```

```python
import functools
import math

import jax
import jax.numpy as jnp
from jax import lax
from jax.experimental import pallas as pl
from jax.experimental.pallas import tpu as pltpu

F32 = jnp.float32
BF16 = jnp.bfloat16

D_MODEL = 1024
RWKV_WIDTH = 512
RWKV_HEAD = 64
RWKV_HEADS = 8
HGRN_WIDTH = 512
HGRN_HEAD = 128
HGRN_HEADS = 4
DECAY_RANK = 64
AICL_RANK = 64
GATE_RANK = 128
RWKV_PROJ = 3 * RWKV_WIDTH + DECAY_RANK + AICL_RANK + GATE_RANK
HGRN_PROJ = 4 * HGRN_WIDTH
PROJ = RWKV_PROJ + HGRN_PROJ
D_FF = 4 * D_MODEL
DEPTH = 1
ALPHA = (2.0 * DEPTH) ** 0.25
LN_EPS = 1e-5
GN_EPS = RWKV_HEAD * 1e-5
RMS_EPS = 1e-6
DECAY_SCALE = math.exp(-0.5)

LANES = 128
SUBLANES = 8
PAIRS = RWKV_HEADS // 2
LORA_COL = 3 * RWKV_WIDTH
VMEM_LIMIT = 56 * 1024 * 1024
POST_SPLIT = 2

NN = ((1,), (0,))
NT = ((1,), (1,))
TN = ((0,), (0,))

RWKV_STASH = (("lhs", 4, BF16), ("rhs", 2, BF16), ("bt_hat", 2, BF16), ("atrt", 2, BF16), ("vb", 1, BF16),
              ("bkh", 2, BF16), ("dec", 0, F32), ("bonus", 1, F32), ("gate", 1, F32))
HGRN_STASH = (("qs", 1, BF16), ("ks", 1, BF16), ("iv", 1, BF16), ("q", 1, F32), ("kin", 1, F32),
              ("bc", 1, F32), ("dec", 0, F32))


def _dot(a, b, dims):
    return lax.dot_general(a, b, (dims, ((), ())), preferred_element_type=F32)


def _bf(x):
    return x.astype(BF16)


def _cumsum_rows(tri, x):
    p0 = _bf(x)
    r1 = x - p0.astype(F32)
    p1 = _bf(r1)
    p2 = _bf(r1 - p1.astype(F32))
    return _dot(tri, p0, NN) + _dot(tri, p1, NN) + _dot(tri, p2, NN)


def _sigmoid(x):
    return 0.5 * jnp.tanh(0.5 * x) + 0.5


def _interleave(tasks):
    gens = list(tasks)
    nxt = [next(g, None) for g in gens]
    spent = {"M": 0.0, "V": 0.0}
    turn = 0
    while any(k is not None for k in nxt):
        ready = {kind: [j for j, k in enumerate(nxt) if k is not None and k[0] == kind] for kind in spent}
        want = "M" if (ready["M"] and (spent["M"] <= spent["V"] or not ready["V"])) else "V"
        if want == "M":
            cands = ready["M"]
            t = cands[turn % len(cands)]
            turn += 1
        else:
            t = ready["V"][0]
        spent[want] += nxt[t][1]
        nxt[t] = next(gens[t], None)


def _chain(*gens):
    for g in gens:
        yield from g


class _ValueStash:
    def __init__(self):
        self.vals = {}

    def put(self, name, val):
        self.vals[name] = val

    def get(self, name, j=None):
        return self.vals[name] if j is None else self.vals[name][j]


class _RefStash:
    def __init__(self, refs, slot, lo, n):
        self.refs, self.slot, self.lo, self.n = refs, slot, lo, n

    def put(self, name, val):
        self.refs[name][self.slot, self.lo:self.lo + self.n] = val

    def get(self, name, j=None):
        if j is None:
            return self.refs[name][self.slot, self.lo:self.lo + self.n]
        return self.refs[name][self.slot, self.lo + j]


def _proj_kernel(x_ref, w_ref, first_ref, mu_ref, lbl_ref, o_ref, last_ref, wbf_scr, carry_scr, *, tiles_per_seq):
    nb, tt, _ = x_ref.shape
    tile = pl.program_id(0)

    @pl.when(tile == 0)
    def _cast_weights():
        wbf_scr[...] = w_ref[...].astype(BF16)

    xb = x_ref[...].reshape(nb * tt, D_MODEL).astype(BF16)

    def cols(lo, width):
        return jnp.dot(xb, wbf_scr[:, lo:lo + width], preferred_element_type=F32)

    first = first_ref[...]
    if tiles_per_seq > 1:
        first = jnp.where(lax.rem(tile, tiles_per_seq) == 0, first, carry_scr[...])
    last_rows = []
    for lo in range(0, RWKV_PROJ, RWKV_WIDTH):
        width = min(RWKV_WIDTH, RWKV_PROJ - lo)
        p = cols(lo, width)
        p3 = p.reshape(nb, tt, width)
        row = lax.broadcasted_iota(jnp.int32, p3.shape, 1)
        prev = jnp.where(row == 0, first[:, :, lo:lo + width], pltpu.roll(p, 1, 0).reshape(p3.shape))
        o_ref[:, :, lo:lo + width] = p3 + mu_ref[:, lo:lo + width] * (prev - p3)
        last_rows.append(p3[:, tt - 1:tt, :])
    last = jnp.concatenate(last_rows, axis=-1)
    last_ref[...] = last
    if tiles_per_seq > 1:
        carry_scr[...] = last

    logits = lbl_ref[...]
    ex = jnp.exp(logits - jnp.max(logits, axis=0, keepdims=True))
    lb = ex[0:1] / jnp.sum(ex, axis=0, keepdims=True)
    q_h = cols(RWKV_PROJ, HGRN_WIDTH)
    o_ref[:, :, RWKV_PROJ:RWKV_PROJ + HGRN_WIDTH] = (q_h * _sigmoid(q_h)).reshape(nb, tt, HGRN_WIDTH)
    f_h = cols(RWKV_PROJ + HGRN_WIDTH, HGRN_WIDTH)
    lf = jnp.log(lb + (1.0 - lb) * _sigmoid(f_h))
    o_ref[:, :, RWKV_PROJ + HGRN_WIDTH:RWKV_PROJ + 2 * HGRN_WIDTH] = lf.reshape(nb, tt, HGRN_WIDTH)
    i_h = cols(RWKV_PROJ + 2 * HGRN_WIDTH, HGRN_WIDTH)
    o_ref[:, :, RWKV_PROJ + 2 * HGRN_WIDTH:RWKV_PROJ + 3 * HGRN_WIDTH] = i_h.reshape(nb, tt, HGRN_WIDTH)
    g_h = cols(RWKV_PROJ + 3 * HGRN_WIDTH, HGRN_WIDTH)
    o_ref[:, :, RWKV_PROJ + 3 * HGRN_WIDTH:] = (g_h * _sigmoid(g_h)).reshape(nb, tt, HGRN_WIDTH)


def _proj(x3, s_shift3, p, tm):
    B, T, _ = x3.shape
    tt = min(T, tm)
    nb = tm // tt
    tiles_per_seq = T // tt

    def tile_map(i):
        return (i // tiles_per_seq, lax.rem(i, tiles_per_seq), 0)

    def seq_map(i):
        return (i // tiles_per_seq, 0, 0)

    def const(shape):
        return pl.BlockSpec(shape, lambda i: (0, 0), pipeline_mode=pl.Buffered(1))

    return pl.pallas_call(
        functools.partial(_proj_kernel, tiles_per_seq=tiles_per_seq),
        grid=(B * T // tm,),
        in_specs=[pl.BlockSpec((nb, tt, D_MODEL), tile_map),
                  const((D_MODEL, PROJ)),
                  pl.BlockSpec((nb, 1, RWKV_PROJ), seq_map),
                  const((1, RWKV_PROJ)), const((DEPTH + 1, HGRN_WIDTH))],
        out_specs=[pl.BlockSpec((nb, tt, PROJ), tile_map),
                   pl.BlockSpec((nb, 1, RWKV_PROJ), seq_map)],
        out_shape=[jax.ShapeDtypeStruct((B, T, PROJ), F32),
                   jax.ShapeDtypeStruct((B, 1, RWKV_PROJ), F32)],
        scratch_shapes=[pltpu.VMEM((D_MODEL, PROJ), BF16), pltpu.VMEM((nb, 1, RWKV_PROJ), F32)],
        compiler_params=pltpu.CompilerParams(dimension_semantics=("arbitrary",),
                                             vmem_limit_bytes=VMEM_LIMIT),
        name="proj",
    )(x3, p["w_in"], s_shift3, p["shift_mu"], p["lb_logits"])


def _head_helpers(C):
    lane = lax.broadcasted_iota(jnp.int32, (C, LANES), 1)
    m0 = lane < RWKV_HEAD

    def head_sum(x):
        s0 = jnp.sum(jnp.where(m0, x, 0.0), axis=-1, keepdims=True)
        s1 = jnp.sum(jnp.where(m0, 0.0, x), axis=-1, keepdims=True)
        return jnp.where(m0, s0, s1)

    def stack_heads(x):
        return jnp.concatenate([jnp.where(m0, x, 0.0), jnp.where(m0, 0.0, x)], axis=-2)

    def merge_heads(x):
        return jnp.where(m0, x[:C], x[C:])

    return head_sum, stack_heads, merge_heads


def _rwkv_produce(stash, lo, n, C, r_ref, k_ref, v_ref, l_ref,
                  w0_ref, a0_ref, kk_ref, ka_ref, rk_ref, w1_ref, a1_ref, g1_ref):
    ids = range(n)
    grp = slice(lo, lo + n)
    head_sum, stack_heads, _ = _head_helpers(C)
    rowc = lax.broadcasted_iota(jnp.int32, (C, C), 0)
    colc = lax.broadcasted_iota(jnp.int32, (C, C), 1)
    tri = jnp.where(rowc >= colc, 1.0, 0.0).astype(BF16)

    w1 = _bf(w1_ref[...])
    a1 = _bf(a1_ref[...])
    g1 = _bf(g1_ref[...])

    yield ("V", 30.0 * n)
    xl = l_ref[grp].reshape(n * C, 2 * LANES)
    xl_lo = xl[:, :LANES]
    dw = jnp.dot(_bf(jnp.tanh(xl_lo)), w1, preferred_element_type=F32)
    da = jnp.dot(_bf(xl_lo), a1, preferred_element_type=F32)
    gate = jnp.dot(_bf(_sigmoid(xl[:, LANES:])), g1, preferred_element_type=F32)
    stash.put("gate", gate.reshape(n, C, LANES))

    yield ("V", 60.0 * n)
    lw = (-DECAY_SCALE * _sigmoid(w0_ref[...] + dw)).reshape(n, C, LANES)
    a_lr = _sigmoid(a0_ref[...] + da).reshape(n, C, LANES)
    xk = k_ref[grp]
    kk = xk * kk_ref[...]
    kk = kk * jnp.minimum(lax.rsqrt(head_sum(kk * kk)), 1e12)
    k2 = xk * (1.0 + (a_lr - 1.0) * ka_ref[...])
    b = kk * a_lr
    cum = jnp.stack([_cumsum_rows(tri, lw[j]) for j in ids])

    yield ("V", 80.0 * n)
    r = r_ref[grp]
    v = v_ref[grp]
    cum_last = cum[:, C - 1:C, :]
    e_in = jnp.exp(-cum)
    e_out = jnp.exp(cum_last - cum)
    at = -kk * jnp.exp(cum - lw)
    rt = r * jnp.exp(cum)
    bt = b * e_in
    kt = k2 * e_in
    stash.put("lhs", _bf(jnp.concatenate([stack_heads(at), stack_heads(rt)], axis=1)))
    stash.put("rhs", _bf(jnp.concatenate([bt, kt], axis=1)))
    stash.put("bt_hat", _bf(stack_heads(bt)))
    stash.put("atrt", _bf(jnp.concatenate([at, rt], axis=1)))
    stash.put("vb", _bf(v))
    stash.put("bkh", _bf(jnp.concatenate([b * e_out, k2 * e_out], axis=1)))
    stash.put("dec", jnp.exp(cum_last))
    stash.put("bonus", head_sum(r * k2 * rk_ref[...]) * v)


def _rwkv_consume(stash, lo, n, C, dot_cost, lnw_ref, lnb_ref, o_ref, s_scr):
    H = RWKV_HEAD
    ids = range(n)
    grp = slice(lo, lo + n)
    head_sum, stack_heads, merge_heads = _head_helpers(C)
    r2 = lax.broadcasted_iota(jnp.int32, (2 * C, 2 * C), 0)
    c2 = lax.broadcasted_iota(jnp.int32, (2 * C, 2 * C), 1)
    t2 = jnp.bitwise_and(r2, C - 1)
    s2 = jnp.bitwise_and(c2, C - 1)
    strict2 = t2 > s2
    incl2 = t2 >= s2
    strict_bd = jnp.logical_and(strict2, (r2 >= C) == (c2 >= C))
    eye2 = jnp.where(r2 == c2, 1.0, 0.0).astype(F32)
    bl_r = lax.broadcasted_iota(jnp.int32, (LANES, LANES), 0) < H
    bl_c = lax.broadcasted_iota(jnp.int32, (LANES, LANES), 1) < H
    blockdiag = bl_r == bl_c
    zeros_c = jnp.zeros((C, LANES), BF16)

    yield ("M", dot_cost)
    g_a, g_r = [], []
    for j in ids:
        g = _dot(stash.get("lhs", j), stash.get("rhs", j), NT)
        g_a.append(_bf(jnp.where(strict2, g[:2 * C], 0.0)))
        g_r.append(_bf(jnp.where(incl2, g[2 * C:], 0.0)))
    yield ("M", dot_cost)
    pw, inv = [], []
    for j in ids:
        n_bd = jnp.where(strict_bd, _dot(stash.get("lhs", j)[:2 * C], stash.get("bt_hat", j), NT), 0.0)
        inv.append(eye2 + n_bd)
        pw.append(_bf(n_bd))
    span = 2
    while span < C:
        yield ("M", dot_cost)
        pw = [_bf(_dot(pw[j], pw[j], NN)) for j in ids]
        yield ("M", dot_cost)
        inv = [inv[j] + _dot(_bf(inv[j]), pw[j], NN) for j in ids]
        span *= 2
    yield ("M", dot_cost)
    S = [s_scr[lo + j] for j in ids]
    pq = [_dot(stash.get("atrt", j), _bf(S[j]), NT) for j in ids]
    yield ("M", dot_cost)
    vb = [stash.get("vb", j) for j in ids]
    w_hat = [_bf(stack_heads(pq[j][:C] + merge_heads(_dot(g_a[j], jnp.concatenate([zeros_c, vb[j]], axis=0), NN))))
             for j in ids]
    yield ("M", dot_cost)
    z = []
    for j in ids:
        u2 = _dot(_bf(inv[j]), w_hat[j], NN)
        z.append(jnp.concatenate([_bf(u2[:C] + u2[C:]), vb[j]], axis=0))
    yield ("M", dot_cost)
    y = [pq[j][C:] + merge_heads(_dot(g_r[j], z[j], NN)) for j in ids]
    yield ("M", dot_cost)
    for j in ids:
        upd = _dot(z[j], stash.get("bkh", j), TN)
        s_scr[lo + j] = S[j] * stash.get("dec", j) + jnp.where(blockdiag, upd, 0.0)

    yield ("V", 40.0 * n)
    y = jnp.stack(y)
    mu = head_sum(y) * (1.0 / H)
    dy = y - mu
    var = head_sum(dy * dy) * (1.0 / H)
    yn = dy * lax.rsqrt(var + GN_EPS) * lnw_ref[...] + lnb_ref[...]
    o_ref[grp] = (yn + stash.get("bonus")) * stash.get("gate")


def _mid_rows(x, h):
    R = x.shape[0]
    row = lax.broadcasted_iota(jnp.int32, x.shape, 0)
    if 2 * h >= SUBLANES:
        return jnp.concatenate(
            [jnp.broadcast_to(x[b * 2 * h + h - 1:b * 2 * h + h], (2 * h, LANES)) for b in range(R // (2 * h))],
            axis=0)
    if h == 1:
        return jnp.where(jnp.bitwise_and(row, 1) == 1, pltpu.roll(x, 1, 0), x)
    picks = []
    for half in range(SUBLANES // (2 * h)):
        r0 = half * 2 * h + h - 1
        picks.append(jnp.concatenate(
            [jnp.broadcast_to(x[g * SUBLANES + r0:g * SUBLANES + r0 + 1], (SUBLANES, LANES))
             for g in range(R // SUBLANES)], axis=0))
    out = picks[-1]
    sub = jnp.bitwise_and(row, SUBLANES - 1)
    for half in range(len(picks) - 2, -1, -1):
        out = jnp.where(sub < (half + 1) * 2 * h, picks[half], out)
    return out


def _hgrn_produce(stash, lo, n, C, q_ref, lf_ref, i_ref):
    ids = range(n)
    grp = slice(lo, lo + n)
    t_i = lax.broadcasted_iota(jnp.int32, (C, C), 0)
    s_i = lax.broadcasted_iota(jnp.int32, (C, C), 1)
    tri = jnp.where(t_i >= s_i, 1.0, 0.0).astype(BF16)

    yield ("V", 20.0 * n)
    q = q_ref[grp]
    lf = lf_ref[grp]
    kin = 1.0 - jnp.exp(lf)
    bc = jnp.stack([_cumsum_rows(tri, lf[j]) for j in ids])

    yield ("V", 30.0 * n)
    b_last = bc[:, C - 1:C, :]
    stash.put("qs", _bf(q * jnp.exp(bc)))
    stash.put("ks", _bf(kin * jnp.exp(b_last - bc)))
    stash.put("iv", _bf(i_ref[grp]))
    stash.put("q", q)
    stash.put("kin", kin)
    stash.put("bc", bc)
    stash.put("dec", jnp.exp(b_last))


def _hgrn_consume(stash, lo, n, C, dot_cost, g_ref, hgw_ref, o_ref, s_scr):
    ids = range(n)
    grp = slice(lo, lo + n)
    t_i = lax.broadcasted_iota(jnp.int32, (C, C), 0)
    s_i = lax.broadcasted_iota(jnp.int32, (C, C), 1)
    halves = []
    h = C // 2
    while h >= 1:
        halves.append(h)
        h //= 2

    def level_mask(h):
        same = jnp.bitwise_and(t_i, -2 * h) == jnp.bitwise_and(s_i, -2 * h)
        return jnp.logical_and(same, jnp.logical_and(jnp.bitwise_and(t_i, h) != 0, jnp.bitwise_and(s_i, h) == 0))

    def seq(x, j):
        return _bf(x[j * C:(j + 1) * C])

    yield ("M", dot_cost)
    S = [s_scr[lo + j] for j in ids]
    o_state = [_dot(stash.get("qs", j), _bf(S[j]), NT) for j in ids]
    yield ("M", dot_cost)
    iv = [stash.get("iv", j) for j in ids]
    for j in ids:
        s_scr[lo + j] = S[j] * stash.get("dec", j) + _dot(iv[j], stash.get("ks", j), TN)
    yield ("M", dot_cost)
    q = stash.get("q").reshape(n * C, LANES)
    kin = stash.get("kin").reshape(n * C, LANES)
    bc = stash.get("bc").reshape(n * C, LANES)
    att = [jnp.where(t_i == s_i, _dot(seq(q, j), seq(kin, j), NT), 0.0) for j in ids]
    for h in halves:
        yield ("V", 25.0 * n)
        dmid = jnp.exp(-jnp.abs(bc - _mid_rows(bc, h)))
        ql = q * dmid
        kl = kin * dmid
        yield ("M", dot_cost)
        msk = level_mask(h)
        att = [jnp.where(msk, _dot(seq(ql, j), seq(kl, j), NT), att[j]) for j in ids]
    yield ("M", dot_cost)
    o_intra = [_dot(_bf(att[j]), iv[j], NN) for j in ids]

    yield ("V", 25.0 * n)
    o = jnp.stack([o_intra[j] + o_state[j] for j in ids])
    ms = jnp.mean(o * o, axis=-1, keepdims=True)
    o_ref[grp] = o * lax.rsqrt(ms + RMS_EPS) * hgw_ref[...] * g_ref[grp]


MXU_STAGE_COST = {True: (28.0, 19.0), False: (30.0, 20.0)}
N_MIXER_IN = 21
N_MIXER_OUT = 4
N_MIXER_STATE_SCRATCH = 2


def _mixer_kernel(*refs, bb, C, n_groups, pipelined):
    (r_ref, k_ref, v_ref, l_ref, w0_ref, a0_ref, kk_ref, ka_ref, rk_ref,
     lnw_ref, lnb_ref, w1_ref, a1_ref, g1_ref, srw0_ref,
     q_ref, lf_ref, i_ref, g_ref, hgw_ref, shg0_ref) = refs[:N_MIXER_IN]
    orw_ref, strw_ref, ohg_ref, sthg_ref = refs[N_MIXER_IN:N_MIXER_IN + N_MIXER_OUT]
    scr = refs[N_MIXER_IN + N_MIXER_OUT:]
    srw_scr, shg_scr = scr[:N_MIXER_STATE_SCRATCH]
    stash_refs = scr[N_MIXER_STATE_SCRATCH:]
    if pipelined:
        rw_refs = {name: stash_refs[k] for k, (name, _, _) in enumerate(RWKV_STASH)}
        hg_refs = {name: stash_refs[len(RWKV_STASH) + k] for k, (name, _, _) in enumerate(HGRN_STASH)}

    c = pl.program_id(2)
    last = pl.num_programs(2) - 1
    H = RWKV_HEAD

    @pl.when(c == 0)
    def _init():
        z = jnp.zeros((H, H), F32)
        for i in range(bb):
            top = jnp.concatenate([srw0_ref[i, 0], z], axis=1)
            bot = jnp.concatenate([z, srw0_ref[i, 1]], axis=1)
            srw_scr[i] = jnp.concatenate([top, bot], axis=0)
            shg_scr[i] = shg0_ref[i, 0].T
        if pipelined:
            for table in (rw_refs, hg_refs):
                for name, ref in table.items():
                    fill = jnp.ones if name == "dec" else jnp.zeros
                    ref[1] = fill(ref.shape[1:], ref.dtype)

    per = bb // n_groups
    groups = [(gidx * per, per) for gidx in range(n_groups)]

    def rwkv_produce(stash, lo, n):
        return _rwkv_produce(stash, lo, n, C, r_ref, k_ref, v_ref, l_ref,
                             w0_ref, a0_ref, kk_ref, ka_ref, rk_ref, w1_ref, a1_ref, g1_ref)

    def rwkv_consume(stash, lo, n):
        return _rwkv_consume(stash, lo, n, C, MXU_STAGE_COST[pipelined][0] * n, lnw_ref, lnb_ref, orw_ref, srw_scr)

    def hgrn_produce(stash, lo, n):
        return _hgrn_produce(stash, lo, n, C, q_ref, lf_ref, i_ref)

    def hgrn_consume(stash, lo, n):
        return _hgrn_consume(stash, lo, n, C, MXU_STAGE_COST[pipelined][1] * n, g_ref, hgw_ref, ohg_ref, shg_scr)

    if not pipelined:
        tasks = []
        for lo, n in groups:
            rw, hg = _ValueStash(), _ValueStash()
            tasks.append(_chain(rwkv_produce(rw, lo, n), rwkv_consume(rw, lo, n)))
            tasks.append(_chain(hgrn_produce(hg, lo, n), hgrn_consume(hg, lo, n)))
        _interleave(tasks)
    else:
        def step(write_slot):
            read_slot = 1 - write_slot
            tasks = []
            for lo, n in groups:
                tasks.append(rwkv_consume(_RefStash(rw_refs, read_slot, lo, n), lo, n))
                tasks.append(hgrn_consume(_RefStash(hg_refs, read_slot, lo, n), lo, n))
            for lo, n in groups:
                tasks.append(rwkv_produce(_RefStash(rw_refs, write_slot, lo, n), lo, n))
                tasks.append(hgrn_produce(_RefStash(hg_refs, write_slot, lo, n), lo, n))
            _interleave(tasks)

        parity = lax.rem(c, 2)
        pl.when(parity == 0)(functools.partial(step, 0))
        pl.when(parity == 1)(functools.partial(step, 1))

    @pl.when(c == last)
    def _fin():
        for i in range(bb):
            s_fin = srw_scr[i]
            strw_ref[i, 0] = s_fin[:H, :H]
            strw_ref[i, 1] = s_fin[H:, H:]
            sthg_ref[i, 0] = shg_scr[i].T


def _mixer(proj3, s_rwkv, s_hgrn, p, bb, C, n_groups, pipelined):
    B, T, _ = proj3.shape
    nc = T // C
    grid = (B // bb, PAIRS, nc + 1 if pipelined else nc)
    nb = RWKV_WIDTH // LANES
    hg0 = RWKV_PROJ // LANES
    lora_blk = LORA_COL // (2 * LANES)

    if pipelined:
        def produced(c):
            return jnp.minimum(c, nc - 1)

        def consumed(c):
            return jnp.maximum(c - 1, 0)
    else:
        produced = consumed = lambda c: c

    def tok(col0, chunk=produced):
        return pl.BlockSpec((bb, C, LANES), lambda b, h, c: (b, chunk(c), col0 + h))

    def vec(col0):
        return pl.BlockSpec((1, LANES), lambda b, h, c: (0, col0 + h))

    in_specs = [
        tok(0), tok(nb), tok(2 * nb),
        pl.BlockSpec((bb, C, 2 * LANES), lambda b, h, c: (b, produced(c), lora_blk)),
        vec(0), vec(0), vec(0), vec(0), vec(0), vec(0), vec(0),
        pl.BlockSpec((LANES, LANES), lambda b, h, c: (0, h)),
        pl.BlockSpec((LANES, LANES), lambda b, h, c: (0, h)),
        pl.BlockSpec((GATE_RANK, LANES), lambda b, h, c: (0, h)),
        pl.BlockSpec((bb, 2, RWKV_HEAD, RWKV_HEAD), lambda b, h, c: (b, h, 0, 0)),
        tok(hg0), tok(hg0 + nb), tok(hg0 + 2 * nb), tok(hg0 + 3 * nb, consumed),
        vec(0),
        pl.BlockSpec((bb, 1, HGRN_HEAD, HGRN_HEAD), lambda b, h, c: (b, h, 0, 0)),
    ]
    assert len(in_specs) == N_MIXER_IN
    out_specs = [
        pl.BlockSpec((bb, C, LANES), lambda b, h, c: (b, consumed(c), h)),
        pl.BlockSpec((bb, 2, RWKV_HEAD, RWKV_HEAD), lambda b, h, c: (b, h, 0, 0)),
        pl.BlockSpec((bb, C, LANES), lambda b, h, c: (b, consumed(c), h)),
        pl.BlockSpec((bb, 1, HGRN_HEAD, HGRN_HEAD), lambda b, h, c: (b, h, 0, 0)),
    ]
    out_shape = [jax.ShapeDtypeStruct((B, T, RWKV_WIDTH), F32),
                 jax.ShapeDtypeStruct((B, RWKV_HEADS, RWKV_HEAD, RWKV_HEAD), F32),
                 jax.ShapeDtypeStruct((B, T, HGRN_WIDTH), F32),
                 jax.ShapeDtypeStruct((B, HGRN_HEADS, HGRN_HEAD, HGRN_HEAD), F32)]
    scratch = [pltpu.VMEM((bb, LANES, LANES), F32), pltpu.VMEM((bb, HGRN_HEAD, HGRN_HEAD), F32)]
    assert len(scratch) == N_MIXER_STATE_SCRATCH
    if pipelined:
        scratch += [pltpu.VMEM((2, bb, rows * C if rows else 1, LANES), dt)
                    for _, rows, dt in RWKV_STASH + HGRN_STASH]
    return pl.pallas_call(
        functools.partial(_mixer_kernel, bb=bb, C=C, n_groups=n_groups, pipelined=pipelined),
        grid=grid, in_specs=in_specs, out_specs=out_specs, out_shape=out_shape,
        scratch_shapes=scratch,
        compiler_params=pltpu.CompilerParams(
            dimension_semantics=("arbitrary", "arbitrary", "arbitrary"),
            vmem_limit_bytes=VMEM_LIMIT),
        name="mixer",
    )(proj3, proj3, proj3, proj3,
      p["w0"], p["a0"], p["k_k"], p["k_a"], p["r_k"], p["ln_x_w"], p["ln_x_b"],
      p["w1u_pad"], p["a1u_pad"], p["g1u"], s_rwkv,
      proj3, proj3, proj3, proj3, p["hg_norm_w"], s_hgrn)


def _layer_norm(x, g, b):
    mu = jnp.mean(x, axis=-1, keepdims=True)
    d = x - mu
    var = jnp.mean(d * d, axis=-1, keepdims=True)
    return d * lax.rsqrt(var + LN_EPS) * g + b


def _post_kernel(x_ref, orw_ref, ohg_ref, wo1_ref, wo2_ref, g1_ref, b1_ref, wup_ref, wdn_ref,
                 g2_ref, b2_ref, y_ref):
    tm = x_ref.shape[0]
    rows = [slice(k * tm // POST_SPLIT, (k + 1) * tm // POST_SPLIT) for k in range(POST_SPLIT)]

    def mix_stage(r):
        return (jnp.dot(orw_ref[r, :].astype(BF16), wo1_ref[...], preferred_element_type=F32)
                + jnp.dot(ohg_ref[r, :].astype(BF16), wo2_ref[...], preferred_element_type=F32))

    def up_stage(r, mix):
        h1 = _layer_norm(ALPHA * x_ref[r, :] + mix, g1_ref[...], b1_ref[...])
        return h1, jnp.dot(h1.astype(BF16), wup_ref[...], preferred_element_type=F32)

    def down_stage(up):
        up = jnp.square(jnp.maximum(up, 0.0))
        return jnp.dot(up.astype(BF16), wdn_ref[...], preferred_element_type=F32)

    def out_stage(r, h1, ff):
        y_ref[r, :] = _layer_norm(ALPHA * h1 + ff, g2_ref[...], b2_ref[...])

    mix, h1, up, ff = {}, {}, {}, {}
    for step in range(POST_SPLIT + 3):
        if step < POST_SPLIT:
            mix[step] = mix_stage(rows[step])
        k = step - 1
        if 0 <= k < POST_SPLIT:
            h1[k], up[k] = up_stage(rows[k], mix.pop(k))
        k = step - 2
        if 0 <= k < POST_SPLIT:
            ff[k] = down_stage(up.pop(k))
        k = step - 3
        if 0 <= k < POST_SPLIT:
            out_stage(rows[k], h1.pop(k), ff.pop(k))


def _post(x2, orw2, ohg2, p, tm):
    n = x2.shape[0]

    def const(shape):
        return pl.BlockSpec(shape, lambda i: (0, 0), pipeline_mode=pl.Buffered(1))

    return pl.pallas_call(
        _post_kernel,
        grid=(n // tm,),
        in_specs=[pl.BlockSpec((tm, D_MODEL), lambda i: (i, 0)),
                  pl.BlockSpec((tm, RWKV_WIDTH), lambda i: (i, 0)),
                  pl.BlockSpec((tm, HGRN_WIDTH), lambda i: (i, 0)),
                  const((RWKV_WIDTH, D_MODEL)), const((HGRN_WIDTH, D_MODEL)),
                  const((1, D_MODEL)), const((1, D_MODEL)),
                  const((D_MODEL, D_FF)), const((D_FF, D_MODEL)),
                  const((1, D_MODEL)), const((1, D_MODEL))],
        out_specs=pl.BlockSpec((tm, D_MODEL), lambda i: (i, 0)),
        out_shape=jax.ShapeDtypeStruct((n, D_MODEL), F32),
        compiler_params=pltpu.CompilerParams(dimension_semantics=("arbitrary",),
                                             vmem_limit_bytes=VMEM_LIMIT),
        name="post",
    )(x2, orw2, ohg2, p["wo_rw"], p["wo_hg"], p["ln1_g"], p["ln1_b"], p["w_up"], p["w_down"],
      p["ln2_g"], p["ln2_b"])


def _prep_params(w_in, shift_mu, w0, w1u, a0, a1u, g1u, k_k, k_a, r_k, ln_x_w, ln_x_b, lb_logits,
                 hg_norm_w, w_out, ln1_g, ln1_b, w_up, w_down, ln2_g, ln2_b):
    zw = jnp.zeros((LANES - DECAY_RANK, RWKV_WIDTH), F32)
    za = jnp.zeros((LANES - AICL_RANK, RWKV_WIDTH), F32)
    w_out_bf = w_out[0].astype(BF16)
    return {
        "w_in": w_in[0],
        "shift_mu": shift_mu[0].reshape(1, RWKV_PROJ),
        "w0": w0[0].reshape(1, RWKV_WIDTH), "a0": a0[0].reshape(1, RWKV_WIDTH),
        "k_k": k_k[0].reshape(1, RWKV_WIDTH), "k_a": k_a[0].reshape(1, RWKV_WIDTH),
        "r_k": r_k[0].reshape(1, RWKV_WIDTH),
        "ln_x_w": ln_x_w[0].reshape(1, RWKV_WIDTH), "ln_x_b": ln_x_b[0].reshape(1, RWKV_WIDTH),
        "w1u_pad": jnp.concatenate([w1u[0], zw], axis=0),
        "a1u_pad": jnp.concatenate([za, a1u[0]], axis=0),
        "g1u": g1u[0],
        "lb_logits": lb_logits.astype(F32),
        "hg_norm_w": hg_norm_w[0].reshape(1, HGRN_WIDTH),
        "wo_rw": w_out_bf[:RWKV_WIDTH], "wo_hg": w_out_bf[RWKV_WIDTH:],
        "ln1_g": ln1_g[0].reshape(1, D_MODEL), "ln1_b": ln1_b[0].reshape(1, D_MODEL),
        "w_up": w_up[0].astype(BF16), "w_down": w_down[0].astype(BF16),
        "ln2_g": ln2_g[0].reshape(1, D_MODEL), "ln2_b": ln2_b[0].reshape(1, D_MODEL),
    }


def _run_group(x, s_rwkv, s_hgrn, s_shift, p, *, tm, bb, chunk, n_groups, pipelined):
    B, T, _ = x.shape
    proj3, last = _proj(x, s_shift.reshape(B, 1, RWKV_PROJ), p, tm)
    o_rw, st_rw, o_hg, st_hg = _mixer(proj3, s_rwkv, s_hgrn, p, bb, chunk, n_groups, pipelined)
    y2 = _post(x.reshape(B * T, D_MODEL), o_rw.reshape(B * T, RWKV_WIDTH), o_hg.reshape(B * T, HGRN_WIDTH),
               p, tm)
    return y2.reshape(B, T, D_MODEL), st_rw[None], st_hg[None], last.reshape(1, B, RWKV_PROJ)


PROMPT_CFG = dict(tm=512, bb=8, chunk=64, n_groups=1, pipelined=True)
SAMPLE_CFG = dict(tm=512, bb=32, chunk=8, n_groups=1, pipelined=False)


def kernel(x_prompt, x_sample, state_rwkv, state_hgrn, state_shift, w_in, shift_mu, w0, w1u, a0, a1u, g1u, k_k, k_a, r_k, ln_x_w, ln_x_b, lb_logits, hg_norm_w, w_out, ln1_g, ln1_b, w_up, w_down, ln2_g, ln2_b):
    assert w_in.shape[0] == DEPTH
    p = _prep_params(w_in, shift_mu, w0, w1u, a0, a1u, g1u, k_k, k_a, r_k, ln_x_w, ln_x_b, lb_logits,
                     hg_norm_w, w_out, ln1_g, ln1_b, w_up, w_down, ln2_g, ln2_b)
    bp = x_prompt.shape[0]
    z_rw = jnp.zeros((bp, RWKV_HEADS, RWKV_HEAD, RWKV_HEAD), F32)
    z_hg = jnp.zeros((bp, HGRN_HEADS, HGRN_HEAD, HGRN_HEAD), F32)
    z_sh = jnp.zeros((bp, RWKV_PROJ), F32)
    y_p, rw_p, hg_p, sh_p = _run_group(x_prompt, z_rw, z_hg, z_sh, p, **PROMPT_CFG)
    y_s, rw_s, hg_s, sh_s = _run_group(x_sample, state_rwkv[0].astype(F32), state_hgrn[0].astype(F32),
                                       state_shift[0].astype(F32), p, **SAMPLE_CFG)
    return (y_p, y_s, rw_p, rw_s, hg_p, hg_s, sh_p, sh_s)
```

```python
import functools
import math

import jax
import jax.numpy as jnp
from jax import lax
from jax.experimental import pallas as pl
from jax.experimental.pallas import tpu as pltpu

F32 = jnp.float32
BF16 = jnp.bfloat16

D_MODEL = 1024
RWKV_WIDTH = 512
RWKV_HEAD = 64
RWKV_HEADS = 8
HGRN_WIDTH = 512
HGRN_HEAD = 128
HGRN_HEADS = 4
DECAY_RANK = 64
AICL_RANK = 64
GATE_RANK = 128
RWKV_PROJ = 3 * RWKV_WIDTH + DECAY_RANK + AICL_RANK + GATE_RANK
HGRN_PROJ = 4 * HGRN_WIDTH
PROJ = RWKV_PROJ + HGRN_PROJ
D_FF = 4 * D_MODEL
DEPTH = 1
ALPHA = (2.0 * DEPTH) ** 0.25
LN_EPS = 1e-5
GN_EPS = RWKV_HEAD * 1e-5
RMS_EPS = 1e-6
KK_NORM_FLOOR = 1e-12
DECAY_SCALE = math.exp(-0.5)

LANES = 128
SUBLANES = 8
PAIRS = RWKV_HEADS // 2
LORA_COL = 3 * RWKV_WIDTH
VMEM_LIMIT = 56 * 1024 * 1024
PROJ_SECTION = 256
POST_SPLIT = 2

NN = ((1,), (0,))
NT = ((1,), (1,))
TN = ((0,), (0,))

RWKV_STASH = (("lhs", 4, BF16), ("rhs", 2, BF16), ("bt_hat", 2, BF16), ("atrt", 2, BF16), ("vb", 1, BF16),
              ("bkh", 2, BF16), ("dec", 0, F32), ("bonus", 1, F32), ("gate", 1, F32))
HGRN_STASH = (("qs", 1, BF16), ("ks", 1, BF16), ("iv", 1, BF16), ("q", 1, F32), ("kin", 1, F32),
              ("bc", 1, F32), ("dec", 0, F32))


def _dot(a, b, dims):
    return lax.dot_general(a, b, (dims, ((), ())), preferred_element_type=F32)


def _bf(x):
    return x.astype(BF16)


def _cumsum_rows(tri, x):
    p0 = _bf(x)
    r1 = x - p0.astype(F32)
    p1 = _bf(r1)
    p2 = _bf(r1 - p1.astype(F32))
    return _dot(tri, p0, NN) + _dot(tri, p1, NN) + _dot(tri, p2, NN)


def _sigmoid(x):
    return 0.5 * jnp.tanh(0.5 * x) + 0.5


def _interleave(tasks):
    gens = list(tasks)
    nxt = [next(g, None) for g in gens]
    spent = {"M": 0.0, "V": 0.0}
    turn = 0
    while any(k is not None for k in nxt):
        ready = {kind: [j for j, k in enumerate(nxt) if k is not None and k[0] == kind] for kind in spent}
        want = "M" if (ready["M"] and (spent["M"] <= spent["V"] or not ready["V"])) else "V"
        if want == "M":
            cands = ready["M"]
            t = cands[turn % len(cands)]
            turn += 1
        else:
            t = ready["V"][0]
        spent[want] += nxt[t][1]
        nxt[t] = next(gens[t], None)


def _chain(*gens):
    for g in gens:
        yield from g


class _ValueStash:
    def __init__(self):
        self.vals = {}

    def put(self, name, val):
        self.vals[name] = val

    def get(self, name, j=None):
        return self.vals[name] if j is None else self.vals[name][j]


class _RefStash:
    def __init__(self, refs, slot, lo, n):
        self.refs, self.slot, self.lo, self.n = refs, slot, lo, n

    def put(self, name, val):
        self.refs[name][self.slot, self.lo:self.lo + self.n] = val

    def get(self, name, j=None):
        if j is None:
            return self.refs[name][self.slot, self.lo:self.lo + self.n]
        return self.refs[name][self.slot, self.lo + j]


def _proj_kernel(x_ref, w_ref, first_ref, mu_ref, lbl_ref, o_ref, last_ref, wbf_scr, carry_scr, *, tiles_per_seq):
    nb, tt, _ = x_ref.shape
    tile = pl.program_id(0)

    @pl.when(tile == 0)
    def _cast_weights():
        wbf_scr[...] = w_ref[...].astype(BF16)

    xb = x_ref[...].reshape(nb * tt, D_MODEL).astype(BF16)

    def cols(lo, width):
        return jnp.dot(xb, wbf_scr[:, lo:lo + width], preferred_element_type=F32)

    first = first_ref[...]
    if tiles_per_seq > 1:
        first = jnp.where(lax.rem(tile, tiles_per_seq) == 0, first, carry_scr[...])
    last_rows = []
    for lo in range(0, RWKV_PROJ, PROJ_SECTION):
        width = PROJ_SECTION
        p = cols(lo, width)
        p3 = p.reshape(nb, tt, width)
        row = lax.broadcasted_iota(jnp.int32, p3.shape, 1)
        prev = jnp.where(row == 0, first[:, :, lo:lo + width], pltpu.roll(p, 1, 0).reshape(p3.shape))
        o_ref[:, :, lo:lo + width] = p3 + mu_ref[:, lo:lo + width] * (prev - p3)
        last_rows.append(p3[:, tt - 1:tt, :])
    last = jnp.concatenate(last_rows, axis=-1)
    last_ref[...] = last
    if tiles_per_seq > 1:
        carry_scr[...] = last

    logits = lbl_ref[...]
    ex = jnp.exp(logits - jnp.max(logits, axis=0, keepdims=True))
    lb = ex[0:1] / jnp.sum(ex, axis=0, keepdims=True)
    for part, act in enumerate(("silu", "log_gate", "copy", "silu")):
        for lo in range(0, HGRN_WIDTH, PROJ_SECTION):
            col = RWKV_PROJ + part * HGRN_WIDTH + lo
            h = cols(col, PROJ_SECTION)
            if act == "silu":
                h = h * _sigmoid(h)
            elif act == "log_gate":
                lb_s = lb[:, lo:lo + PROJ_SECTION]
                h = jnp.log(lb_s + (1.0 - lb_s) * _sigmoid(h))
            o_ref[:, :, col:col + PROJ_SECTION] = h.reshape(nb, tt, PROJ_SECTION)


def _proj(x3, s_shift3, p, tm):
    B, T, _ = x3.shape
    tt = min(T, tm)
    nb = tm // tt
    tiles_per_seq = T // tt

    def tile_map(i):
        return (i // tiles_per_seq, lax.rem(i, tiles_per_seq), 0)

    def seq_map(i):
        return (i // tiles_per_seq, 0, 0)

    def const(shape):
        return pl.BlockSpec(shape, lambda i: (0, 0), pipeline_mode=pl.Buffered(1))

    return pl.pallas_call(
        functools.partial(_proj_kernel, tiles_per_seq=tiles_per_seq),
        grid=(B * T // tm,),
        in_specs=[pl.BlockSpec((nb, tt, D_MODEL), tile_map),
                  const((D_MODEL, PROJ)),
                  pl.BlockSpec((nb, 1, RWKV_PROJ), seq_map),
                  const((1, RWKV_PROJ)), const((DEPTH + 1, HGRN_WIDTH))],
        out_specs=[pl.BlockSpec((nb, tt, PROJ), tile_map),
                   pl.BlockSpec((nb, 1, RWKV_PROJ), seq_map)],
        out_shape=[jax.ShapeDtypeStruct((B, T, PROJ), F32),
                   jax.ShapeDtypeStruct((B, 1, RWKV_PROJ), F32)],
        scratch_shapes=[pltpu.VMEM((D_MODEL, PROJ), BF16), pltpu.VMEM((nb, 1, RWKV_PROJ), F32)],
        compiler_params=pltpu.CompilerParams(dimension_semantics=("arbitrary",),
                                             vmem_limit_bytes=VMEM_LIMIT),
        name="proj",
    )(x3, p["w_in"], s_shift3, p["shift_mu"], p["lb_logits"])


def _head_helpers(C):
    lane = lax.broadcasted_iota(jnp.int32, (C, LANES), 1)
    m0 = lane < RWKV_HEAD

    def head_sum(x):
        s0 = jnp.sum(jnp.where(m0, x, 0.0), axis=-1, keepdims=True)
        s1 = jnp.sum(jnp.where(m0, 0.0, x), axis=-1, keepdims=True)
        return jnp.where(m0, s0, s1)

    def stack_heads(x):
        return jnp.concatenate([jnp.where(m0, x, 0.0), jnp.where(m0, 0.0, x)], axis=-2)

    def merge_heads(x):
        return jnp.where(m0, x[:C], x[C:])

    return head_sum, stack_heads, merge_heads


def _rwkv_produce(stash, lo, n, C, r_ref, k_ref, v_ref, l_ref,
                  w0_ref, a0_ref, kk_ref, ka_ref, rk_ref, w1_ref, a1_ref, g1_ref):
    ids = range(n)
    grp = slice(lo, lo + n)
    head_sum, stack_heads, _ = _head_helpers(C)
    rowc = lax.broadcasted_iota(jnp.int32, (C, C), 0)
    colc = lax.broadcasted_iota(jnp.int32, (C, C), 1)
    tri = jnp.where(rowc >= colc, 1.0, 0.0).astype(BF16)

    w1 = _bf(w1_ref[...])
    a1 = _bf(a1_ref[...])
    g1 = _bf(g1_ref[...])

    yield ("V", 30.0 * n)
    xl = l_ref[grp].reshape(n * C, 2 * LANES)
    xl_lo = xl[:, :LANES]
    dw = jnp.dot(_bf(jnp.tanh(xl_lo)), w1, preferred_element_type=F32)
    da = jnp.dot(_bf(xl_lo), a1, preferred_element_type=F32)
    gate = jnp.dot(_bf(_sigmoid(xl[:, LANES:])), g1, preferred_element_type=F32)
    stash.put("gate", gate.reshape(n, C, LANES))

    yield ("V", 60.0 * n)
    lw = (-DECAY_SCALE * _sigmoid(w0_ref[...] + dw)).reshape(n, C, LANES)
    a_lr = _sigmoid(a0_ref[...] + da).reshape(n, C, LANES)
    xk = k_ref[grp]
    kk = xk * kk_ref[...]
    kk = kk * jnp.minimum(lax.rsqrt(head_sum(kk * kk)), 1.0 / KK_NORM_FLOOR)
    k2 = xk * (1.0 + (a_lr - 1.0) * ka_ref[...])
    b = kk * a_lr
    cum = jnp.stack([_cumsum_rows(tri, lw[j]) for j in ids])

    yield ("V", 80.0 * n)
    r = r_ref[grp]
    v = v_ref[grp]
    cum_last = cum[:, C - 1:C, :]
    e_in = jnp.exp(-cum)
    e_out = jnp.exp(cum_last - cum)
    at = -kk * jnp.exp(cum - lw)
    rt = r * jnp.exp(cum)
    bt = b * e_in
    kt = k2 * e_in
    stash.put("lhs", _bf(jnp.concatenate([stack_heads(at), stack_heads(rt)], axis=1)))
    stash.put("rhs", _bf(jnp.concatenate([bt, kt], axis=1)))
    stash.put("bt_hat", _bf(stack_heads(bt)))
    stash.put("atrt", _bf(jnp.concatenate([at, rt], axis=1)))
    stash.put("vb", _bf(v))
    stash.put("bkh", _bf(jnp.concatenate([b * e_out, k2 * e_out], axis=1)))
    stash.put("dec", jnp.exp(cum_last))
    stash.put("bonus", head_sum(r * k2 * rk_ref[...]) * v)


def _rwkv_consume(stash, lo, n, C, dot_cost, lnw_ref, lnb_ref, o_ref, s_scr):
    H = RWKV_HEAD
    ids = range(n)
    grp = slice(lo, lo + n)
    head_sum, stack_heads, merge_heads = _head_helpers(C)
    r2 = lax.broadcasted_iota(jnp.int32, (2 * C, 2 * C), 0)
    c2 = lax.broadcasted_iota(jnp.int32, (2 * C, 2 * C), 1)
    t2 = jnp.bitwise_and(r2, C - 1)
    s2 = jnp.bitwise_and(c2, C - 1)
    strict2 = t2 > s2
    incl2 = t2 >= s2
    strict_bd = jnp.logical_and(strict2, (r2 >= C) == (c2 >= C))
    eye2 = jnp.where(r2 == c2, 1.0, 0.0).astype(F32)
    bl_r = lax.broadcasted_iota(jnp.int32, (LANES, LANES), 0) < H
    bl_c = lax.broadcasted_iota(jnp.int32, (LANES, LANES), 1) < H
    blockdiag = bl_r == bl_c
    zeros_c = jnp.zeros((C, LANES), BF16)

    yield ("M", dot_cost)
    g_a, g_r = [], []
    for j in ids:
        g = _dot(stash.get("lhs", j), stash.get("rhs", j), NT)
        g_a.append(_bf(jnp.where(strict2, g[:2 * C], 0.0)))
        g_r.append(_bf(jnp.where(incl2, g[2 * C:], 0.0)))
    yield ("M", dot_cost)
    pw, inv = [], []
    for j in ids:
        n_bd = jnp.where(strict_bd, _dot(stash.get("lhs", j)[:2 * C], stash.get("bt_hat", j), NT), 0.0)
        inv.append(eye2 + n_bd)
        pw.append(_bf(n_bd))
    span = 2
    while span < C:
        yield ("M", dot_cost)
        pw = [_bf(_dot(pw[j], pw[j], NN)) for j in ids]
        yield ("M", dot_cost)
        inv = [inv[j] + _dot(_bf(inv[j]), pw[j], NN) for j in ids]
        span *= 2
    yield ("M", dot_cost)
    S = [s_scr[lo + j] for j in ids]
    pq = [_dot(stash.get("atrt", j), _bf(S[j]), NT) for j in ids]
    yield ("M", dot_cost)
    vb = [stash.get("vb", j) for j in ids]
    w_hat = [_bf(stack_heads(pq[j][:C] + merge_heads(_dot(g_a[j], jnp.concatenate([zeros_c, vb[j]], axis=0), NN))))
             for j in ids]
    yield ("M", dot_cost)
    z = []
    for j in ids:
        u2 = _dot(_bf(inv[j]), w_hat[j], NN)
        z.append(jnp.concatenate([_bf(u2[:C] + u2[C:]), vb[j]], axis=0))
    yield ("M", dot_cost)
    y = [pq[j][C:] + merge_heads(_dot(g_r[j], z[j], NN)) for j in ids]
    yield ("M", dot_cost)
    for j in ids:
        upd = _dot(z[j], stash.get("bkh", j), TN)
        s_scr[lo + j] = S[j] * stash.get("dec", j) + jnp.where(blockdiag, upd, 0.0)

    yield ("V", 40.0 * n)
    y = jnp.stack(y)
    mu = head_sum(y) * (1.0 / H)
    dy = y - mu
    var = head_sum(dy * dy) * (1.0 / H)
    yn = dy * lax.rsqrt(var + GN_EPS) * lnw_ref[...] + lnb_ref[...]
    o_ref[grp] = (yn + stash.get("bonus")) * stash.get("gate")


def _mid_rows(x, h):
    R = x.shape[0]
    row = lax.broadcasted_iota(jnp.int32, x.shape, 0)
    if 2 * h >= SUBLANES:
        return jnp.concatenate(
            [jnp.broadcast_to(x[b * 2 * h + h - 1:b * 2 * h + h], (2 * h, LANES)) for b in range(R // (2 * h))],
            axis=0)
    if h == 1:
        return jnp.where(jnp.bitwise_and(row, 1) == 1, pltpu.roll(x, 1, 0), x)
    picks = []
    for half in range(SUBLANES // (2 * h)):
        r0 = half * 2 * h + h - 1
        picks.append(jnp.concatenate(
            [jnp.broadcast_to(x[g * SUBLANES + r0:g * SUBLANES + r0 + 1], (SUBLANES, LANES))
             for g in range(R // SUBLANES)], axis=0))
    out = picks[-1]
    sub = jnp.bitwise_and(row, SUBLANES - 1)
    for half in range(len(picks) - 2, -1, -1):
        out = jnp.where(sub < (half + 1) * 2 * h, picks[half], out)
    return out


def _hgrn_produce(stash, lo, n, C, q_ref, lf_ref, i_ref):
    ids = range(n)
    grp = slice(lo, lo + n)
    t_i = lax.broadcasted_iota(jnp.int32, (C, C), 0)
    s_i = lax.broadcasted_iota(jnp.int32, (C, C), 1)
    tri = jnp.where(t_i >= s_i, 1.0, 0.0).astype(BF16)

    yield ("V", 20.0 * n)
    q = q_ref[grp]
    lf = lf_ref[grp]
    kin = 1.0 - jnp.exp(lf)
    bc = jnp.stack([_cumsum_rows(tri, lf[j]) for j in ids])

    yield ("V", 30.0 * n)
    b_last = bc[:, C - 1:C, :]
    stash.put("qs", _bf(q * jnp.exp(bc)))
    stash.put("ks", _bf(kin * jnp.exp(b_last - bc)))
    stash.put("iv", _bf(i_ref[grp]))
    stash.put("q", q)
    stash.put("kin", kin)
    stash.put("bc", bc)
    stash.put("dec", jnp.exp(b_last))


def _hgrn_consume(stash, lo, n, C, dot_cost, g_ref, hgw_ref, o_ref, s_scr):
    ids = range(n)
    grp = slice(lo, lo + n)
    t_i = lax.broadcasted_iota(jnp.int32, (C, C), 0)
    s_i = lax.broadcasted_iota(jnp.int32, (C, C), 1)
    halves = []
    h = C // 2
    while h >= 1:
        halves.append(h)
        h //= 2

    def level_mask(h):
        same = jnp.bitwise_and(t_i, -2 * h) == jnp.bitwise_and(s_i, -2 * h)
        return jnp.logical_and(same, jnp.logical_and(jnp.bitwise_and(t_i, h) != 0, jnp.bitwise_and(s_i, h) == 0))

    def seq(x, j):
        return _bf(x[j * C:(j + 1) * C])

    yield ("M", dot_cost)
    S = [s_scr[lo + j] for j in ids]
    o_state = [_dot(stash.get("qs", j), _bf(S[j]), NT) for j in ids]
    yield ("M", dot_cost)
    iv = [stash.get("iv", j) for j in ids]
    for j in ids:
        s_scr[lo + j] = S[j] * stash.get("dec", j) + _dot(iv[j], stash.get("ks", j), TN)
    yield ("M", dot_cost)
    q = stash.get("q").reshape(n * C, LANES)
    kin = stash.get("kin").reshape(n * C, LANES)
    bc = stash.get("bc").reshape(n * C, LANES)
    att = [jnp.where(t_i == s_i, _dot(seq(q, j), seq(kin, j), NT), 0.0) for j in ids]
    for h in halves:
        yield ("V", 25.0 * n)
        dmid = jnp.exp(-jnp.abs(bc - _mid_rows(bc, h)))
        ql = q * dmid
        kl = kin * dmid
        yield ("M", dot_cost)
        msk = level_mask(h)
        att = [jnp.where(msk, _dot(seq(ql, j), seq(kl, j), NT), att[j]) for j in ids]
    yield ("M", dot_cost)
    o_intra = [_dot(_bf(att[j]), iv[j], NN) for j in ids]

    yield ("V", 25.0 * n)
    o = jnp.stack([o_intra[j] + o_state[j] for j in ids])
    ms = jnp.mean(o * o, axis=-1, keepdims=True)
    o_ref[grp] = o * lax.rsqrt(ms + RMS_EPS) * hgw_ref[...] * g_ref[grp]


MXU_STAGE_COST = {True: (28.0, 19.0), False: (30.0, 20.0)}
N_MIXER_IN = 21
N_MIXER_OUT = 4
N_MIXER_STATE_SCRATCH = 2


def _mixer_kernel(*refs, bb, C, n_groups, pipelined):
    (r_ref, k_ref, v_ref, l_ref, w0_ref, a0_ref, kk_ref, ka_ref, rk_ref,
     lnw_ref, lnb_ref, w1_ref, a1_ref, g1_ref, srw0_ref,
     q_ref, lf_ref, i_ref, g_ref, hgw_ref, shg0_ref) = refs[:N_MIXER_IN]
    orw_ref, strw_ref, ohg_ref, sthg_ref = refs[N_MIXER_IN:N_MIXER_IN + N_MIXER_OUT]
    scr = refs[N_MIXER_IN + N_MIXER_OUT:]
    srw_scr, shg_scr = scr[:N_MIXER_STATE_SCRATCH]
    stash_refs = scr[N_MIXER_STATE_SCRATCH:]
    if pipelined:
        rw_refs = {name: stash_refs[k] for k, (name, _, _) in enumerate(RWKV_STASH)}
        hg_refs = {name: stash_refs[len(RWKV_STASH) + k] for k, (name, _, _) in enumerate(HGRN_STASH)}

    c = pl.program_id(2)
    last = pl.num_programs(2) - 1
    H = RWKV_HEAD

    @pl.when(c == 0)
    def _init():
        z = jnp.zeros((H, H), F32)
        for i in range(bb):
            top = jnp.concatenate([srw0_ref[i, 0], z], axis=1)
            bot = jnp.concatenate([z, srw0_ref[i, 1]], axis=1)
            srw_scr[i] = jnp.concatenate([top, bot], axis=0)
            shg_scr[i] = shg0_ref[i, 0].T
        if pipelined:
            for table in (rw_refs, hg_refs):
                for name, ref in table.items():
                    fill = jnp.ones if name == "dec" else jnp.zeros
                    ref[1] = fill(ref.shape[1:], ref.dtype)

    per = bb // n_groups
    groups = [(gidx * per, per) for gidx in range(n_groups)]

    def rwkv_produce(stash, lo, n):
        return _rwkv_produce(stash, lo, n, C, r_ref, k_ref, v_ref, l_ref,
                             w0_ref, a0_ref, kk_ref, ka_ref, rk_ref, w1_ref, a1_ref, g1_ref)

    def rwkv_consume(stash, lo, n):
        return _rwkv_consume(stash, lo, n, C, MXU_STAGE_COST[pipelined][0] * n, lnw_ref, lnb_ref, orw_ref, srw_scr)

    def hgrn_produce(stash, lo, n):
        return _hgrn_produce(stash, lo, n, C, q_ref, lf_ref, i_ref)

    def hgrn_consume(stash, lo, n):
        return _hgrn_consume(stash, lo, n, C, MXU_STAGE_COST[pipelined][1] * n, g_ref, hgw_ref, ohg_ref, shg_scr)

    if not pipelined:
        tasks = []
        for lo, n in groups:
            rw, hg = _ValueStash(), _ValueStash()
            tasks.append(_chain(rwkv_produce(rw, lo, n), rwkv_consume(rw, lo, n)))
            tasks.append(_chain(hgrn_produce(hg, lo, n), hgrn_consume(hg, lo, n)))
        _interleave(tasks)
    else:
        def step(write_slot):
            read_slot = 1 - write_slot
            tasks = []
            for lo, n in groups:
                tasks.append(rwkv_consume(_RefStash(rw_refs, read_slot, lo, n), lo, n))
                tasks.append(hgrn_consume(_RefStash(hg_refs, read_slot, lo, n), lo, n))
            for lo, n in groups:
                tasks.append(rwkv_produce(_RefStash(rw_refs, write_slot, lo, n), lo, n))
                tasks.append(hgrn_produce(_RefStash(hg_refs, write_slot, lo, n), lo, n))
            _interleave(tasks)

        parity = lax.rem(c, 2)
        pl.when(parity == 0)(functools.partial(step, 0))
        pl.when(parity == 1)(functools.partial(step, 1))

    @pl.when(c == last)
    def _fin():
        for i in range(bb):
            s_fin = srw_scr[i]
            strw_ref[i, 0] = s_fin[:H, :H]
            strw_ref[i, 1] = s_fin[H:, H:]
            sthg_ref[i, 0] = shg_scr[i].T


def _mixer(proj3, s_rwkv, s_hgrn, p, bb, C, n_groups, pipelined):
    B, T, _ = proj3.shape
    nc = T // C
    grid = (B // bb, PAIRS, nc + 1 if pipelined else nc)
    nb = RWKV_WIDTH // LANES
    hg0 = RWKV_PROJ // LANES
    lora_blk = LORA_COL // (2 * LANES)

    if pipelined:
        def produced(c):
            return jnp.minimum(c, nc - 1)

        def consumed(c):
            return jnp.maximum(c - 1, 0)
    else:
        produced = consumed = lambda c: c

    def tok(col0, chunk=produced):
        return pl.BlockSpec((bb, C, LANES), lambda b, h, c: (b, chunk(c), col0 + h))

    def vec(col0):
        return pl.BlockSpec((1, LANES), lambda b, h, c: (0, col0 + h))

    in_specs = [
        tok(0), tok(nb), tok(2 * nb),
        pl.BlockSpec((bb, C, 2 * LANES), lambda b, h, c: (b, produced(c), lora_blk)),
        vec(0), vec(0), vec(0), vec(0), vec(0), vec(0), vec(0),
        pl.BlockSpec((LANES, LANES), lambda b, h, c: (0, h)),
        pl.BlockSpec((LANES, LANES), lambda b, h, c: (0, h)),
        pl.BlockSpec((GATE_RANK, LANES), lambda b, h, c: (0, h)),
        pl.BlockSpec((bb, 2, RWKV_HEAD, RWKV_HEAD), lambda b, h, c: (b, h, 0, 0)),
        tok(hg0), tok(hg0 + nb), tok(hg0 + 2 * nb), tok(hg0 + 3 * nb, consumed),
        vec(0),
        pl.BlockSpec((bb, 1, HGRN_HEAD, HGRN_HEAD), lambda b, h, c: (b, h, 0, 0)),
    ]
    assert len(in_specs) == N_MIXER_IN
    out_specs = [
        pl.BlockSpec((bb, C, LANES), lambda b, h, c: (b, consumed(c), h)),
        pl.BlockSpec((bb, 2, RWKV_HEAD, RWKV_HEAD), lambda b, h, c: (b, h, 0, 0)),
        pl.BlockSpec((bb, C, LANES), lambda b, h, c: (b, consumed(c), h)),
        pl.BlockSpec((bb, 1, HGRN_HEAD, HGRN_HEAD), lambda b, h, c: (b, h, 0, 0)),
    ]
    out_shape = [jax.ShapeDtypeStruct((B, T, RWKV_WIDTH), F32),
                 jax.ShapeDtypeStruct((B, RWKV_HEADS, RWKV_HEAD, RWKV_HEAD), F32),
                 jax.ShapeDtypeStruct((B, T, HGRN_WIDTH), F32),
                 jax.ShapeDtypeStruct((B, HGRN_HEADS, HGRN_HEAD, HGRN_HEAD), F32)]
    scratch = [pltpu.VMEM((bb, LANES, LANES), F32), pltpu.VMEM((bb, HGRN_HEAD, HGRN_HEAD), F32)]
    assert len(scratch) == N_MIXER_STATE_SCRATCH
    if pipelined:
        scratch += [pltpu.VMEM((2, bb, rows * C if rows else 1, LANES), dt)
                    for _, rows, dt in RWKV_STASH + HGRN_STASH]
    return pl.pallas_call(
        functools.partial(_mixer_kernel, bb=bb, C=C, n_groups=n_groups, pipelined=pipelined),
        grid=grid, in_specs=in_specs, out_specs=out_specs, out_shape=out_shape,
        scratch_shapes=scratch,
        compiler_params=pltpu.CompilerParams(
            dimension_semantics=("arbitrary", "arbitrary", "arbitrary"),
            vmem_limit_bytes=VMEM_LIMIT),
        name="mixer",
    )(proj3, proj3, proj3, proj3,
      p["w0"], p["a0"], p["k_k"], p["k_a"], p["r_k"], p["ln_x_w"], p["ln_x_b"],
      p["w1u_pad"], p["a1u_pad"], p["g1u"], s_rwkv,
      proj3, proj3, proj3, proj3, p["hg_norm_w"], s_hgrn)


def _layer_norm(x, g, b):
    mu = jnp.mean(x, axis=-1, keepdims=True)
    d = x - mu
    var = jnp.mean(d * d, axis=-1, keepdims=True)
    return d * lax.rsqrt(var + LN_EPS) * g + b


def _post_kernel(x_ref, orw_ref, ohg_ref, wo1_ref, wo2_ref, g1_ref, b1_ref, wup_ref, wdn_ref,
                 g2_ref, b2_ref, y_ref):
    tm = x_ref.shape[0]
    rows = [slice(k * tm // POST_SPLIT, (k + 1) * tm // POST_SPLIT) for k in range(POST_SPLIT)]

    def mix_stage(r):
        return (jnp.dot(orw_ref[r, :].astype(BF16), wo1_ref[...], preferred_element_type=F32)
                + jnp.dot(ohg_ref[r, :].astype(BF16), wo2_ref[...], preferred_element_type=F32))

    def up_stage(r, mix):
        h1 = _layer_norm(ALPHA * x_ref[r, :] + mix, g1_ref[...], b1_ref[...])
        return h1, jnp.dot(h1.astype(BF16), wup_ref[...], preferred_element_type=F32)

    def down_stage(up):
        up = jnp.square(jnp.maximum(up, 0.0))
        return jnp.dot(up.astype(BF16), wdn_ref[...], preferred_element_type=F32)

    def out_stage(r, h1, ff):
        y_ref[r, :] = _layer_norm(ALPHA * h1 + ff, g2_ref[...], b2_ref[...])

    mix, h1, up, ff = {}, {}, {}, {}
    for step in range(POST_SPLIT + 3):
        if step < POST_SPLIT:
            mix[step] = mix_stage(rows[step])
        k = step - 1
        if 0 <= k < POST_SPLIT:
            h1[k], up[k] = up_stage(rows[k], mix.pop(k))
        k = step - 2
        if 0 <= k < POST_SPLIT:
            ff[k] = down_stage(up.pop(k))
        k = step - 3
        if 0 <= k < POST_SPLIT:
            out_stage(rows[k], h1.pop(k), ff.pop(k))


def _post(x2, orw2, ohg2, p, tm):
    n = x2.shape[0]

    def const(shape):
        return pl.BlockSpec(shape, lambda i: (0, 0), pipeline_mode=pl.Buffered(1))

    return pl.pallas_call(
        _post_kernel,
        grid=(n // tm,),
        in_specs=[pl.BlockSpec((tm, D_MODEL), lambda i: (i, 0)),
                  pl.BlockSpec((tm, RWKV_WIDTH), lambda i: (i, 0)),
                  pl.BlockSpec((tm, HGRN_WIDTH), lambda i: (i, 0)),
                  const((RWKV_WIDTH, D_MODEL)), const((HGRN_WIDTH, D_MODEL)),
                  const((1, D_MODEL)), const((1, D_MODEL)),
                  const((D_MODEL, D_FF)), const((D_FF, D_MODEL)),
                  const((1, D_MODEL)), const((1, D_MODEL))],
        out_specs=pl.BlockSpec((tm, D_MODEL), lambda i: (i, 0)),
        out_shape=jax.ShapeDtypeStruct((n, D_MODEL), F32),
        compiler_params=pltpu.CompilerParams(dimension_semantics=("arbitrary",),
                                             vmem_limit_bytes=VMEM_LIMIT),
        name="post",
    )(x2, orw2, ohg2, p["wo_rw"], p["wo_hg"], p["ln1_g"], p["ln1_b"], p["w_up"], p["w_down"],
      p["ln2_g"], p["ln2_b"])


def _prep_params(w_in, shift_mu, w0, w1u, a0, a1u, g1u, k_k, k_a, r_k, ln_x_w, ln_x_b, lb_logits,
                 hg_norm_w, w_out, ln1_g, ln1_b, w_up, w_down, ln2_g, ln2_b):
    zw = jnp.zeros((LANES - DECAY_RANK, RWKV_WIDTH), F32)
    za = jnp.zeros((LANES - AICL_RANK, RWKV_WIDTH), F32)
    w_out_bf = w_out[0].astype(BF16)
    return {
        "w_in": w_in[0],
        "shift_mu": shift_mu[0].reshape(1, RWKV_PROJ),
        "w0": w0[0].reshape(1, RWKV_WIDTH), "a0": a0[0].reshape(1, RWKV_WIDTH),
        "k_k": k_k[0].reshape(1, RWKV_WIDTH), "k_a": k_a[0].reshape(1, RWKV_WIDTH),
        "r_k": r_k[0].reshape(1, RWKV_WIDTH),
        "ln_x_w": ln_x_w[0].reshape(1, RWKV_WIDTH), "ln_x_b": ln_x_b[0].reshape(1, RWKV_WIDTH),
        "w1u_pad": jnp.concatenate([w1u[0], zw], axis=0),
        "a1u_pad": jnp.concatenate([za, a1u[0]], axis=0),
        "g1u": g1u[0],
        "lb_logits": lb_logits.astype(F32),
        "hg_norm_w": hg_norm_w[0].reshape(1, HGRN_WIDTH),
        "wo_rw": w_out_bf[:RWKV_WIDTH], "wo_hg": w_out_bf[RWKV_WIDTH:],
        "ln1_g": ln1_g[0].reshape(1, D_MODEL), "ln1_b": ln1_b[0].reshape(1, D_MODEL),
        "w_up": w_up[0].astype(BF16), "w_down": w_down[0].astype(BF16),
        "ln2_g": ln2_g[0].reshape(1, D_MODEL), "ln2_b": ln2_b[0].reshape(1, D_MODEL),
    }


def _run_group(x, s_rwkv, s_hgrn, s_shift, p, *, tm, bb, chunk, n_groups, pipelined):
    B, T, _ = x.shape
    proj3, last = _proj(x, s_shift.reshape(B, 1, RWKV_PROJ), p, tm)
    o_rw, st_rw, o_hg, st_hg = _mixer(proj3, s_rwkv, s_hgrn, p, bb, chunk, n_groups, pipelined)
    y2 = _post(x.reshape(B * T, D_MODEL), o_rw.reshape(B * T, RWKV_WIDTH), o_hg.reshape(B * T, HGRN_WIDTH),
               p, tm)
    return y2.reshape(B, T, D_MODEL), st_rw[None], st_hg[None], last.reshape(1, B, RWKV_PROJ)


PROMPT_CFG = dict(tm=512, bb=8, chunk=64, n_groups=1, pipelined=True)
SAMPLE_CFG = dict(tm=512, bb=32, chunk=8, n_groups=1, pipelined=False)


def kernel(x_prompt, x_sample, state_rwkv, state_hgrn, state_shift, w_in, shift_mu, w0, w1u, a0, a1u, g1u, k_k, k_a, r_k, ln_x_w, ln_x_b, lb_logits, hg_norm_w, w_out, ln1_g, ln1_b, w_up, w_down, ln2_g, ln2_b):
    assert w_in.shape[0] == DEPTH
    p = _prep_params(w_in, shift_mu, w0, w1u, a0, a1u, g1u, k_k, k_a, r_k, ln_x_w, ln_x_b, lb_logits,
                     hg_norm_w, w_out, ln1_g, ln1_b, w_up, w_down, ln2_g, ln2_b)
    bp = x_prompt.shape[0]
    z_rw = jnp.zeros((bp, RWKV_HEADS, RWKV_HEAD, RWKV_HEAD), F32)
    z_hg = jnp.zeros((bp, HGRN_HEADS, HGRN_HEAD, HGRN_HEAD), F32)
    z_sh = jnp.zeros((bp, RWKV_PROJ), F32)
    y_p, rw_p, hg_p, sh_p = _run_group(x_prompt, z_rw, z_hg, z_sh, p, **PROMPT_CFG)
    y_s, rw_s, hg_s, sh_s = _run_group(x_sample, state_rwkv[0].astype(F32), state_hgrn[0].astype(F32),
                                       state_shift[0].astype(F32), p, **SAMPLE_CFG)
    return (y_p, y_s, rw_p, rw_s, hg_p, hg_s, sh_p, sh_s)
```

```python
import functools
import math

import jax
import jax.numpy as jnp
from jax import lax
from jax.experimental import pallas as pl
from jax.experimental.pallas import tpu as pltpu

F32 = jnp.float32
BF16 = jnp.bfloat16

D_MODEL = 1024
RWKV_WIDTH = 512
RWKV_HEAD = 64
RWKV_HEADS = 8
HGRN_WIDTH = 512
HGRN_HEAD = 128
HGRN_HEADS = 4
DECAY_RANK = 64
AICL_RANK = 64
GATE_RANK = 128
RWKV_PROJ = 3 * RWKV_WIDTH + DECAY_RANK + AICL_RANK + GATE_RANK
HGRN_PROJ = 4 * HGRN_WIDTH
PROJ = RWKV_PROJ + HGRN_PROJ
D_FF = 4 * D_MODEL
DEPTH = 1
ALPHA = (2.0 * DEPTH) ** 0.25
LN_EPS = 1e-5
GN_EPS = RWKV_HEAD * 1e-5
RMS_EPS = 1e-6
KK_NORM_FLOOR = 1e-12
DECAY_SCALE = math.exp(-0.5)

LANES = 128
SUBLANES = 8
PAIRS = RWKV_HEADS // 2
LORA_COL = 3 * RWKV_WIDTH
VMEM_LIMIT = 56 * 1024 * 1024
PROJ_SECTION = 256
POST_SPLIT = 2

NN = ((1,), (0,))
NT = ((1,), (1,))
TN = ((0,), (0,))

RWKV_STASH = (("lhs", 4, BF16), ("rhs", 2, BF16), ("bt_hat", 2, BF16), ("atrt", 2, BF16), ("vb", 1, BF16),
              ("bkh", 2, BF16), ("dec", 0, F32), ("bonus", 1, F32), ("gate", 1, F32))
HGRN_STASH = (("qs", 1, BF16), ("ks", 1, BF16), ("iv", 1, BF16), ("q", 1, F32), ("kin", 1, F32),
              ("bc", 1, F32), ("dec", 0, F32))


def _dot(a, b, dims):
    return lax.dot_general(a, b, (dims, ((), ())), preferred_element_type=F32)


def _bf(x):
    return x.astype(BF16)


def _cumsum_rows(tri, x):
    p0 = _bf(x)
    r1 = x - p0.astype(F32)
    p1 = _bf(r1)
    p2 = _bf(r1 - p1.astype(F32))
    return _dot(tri, p0, NN) + _dot(tri, p1, NN) + _dot(tri, p2, NN)


def _sigmoid(x):
    return 0.5 * jnp.tanh(0.5 * x) + 0.5


def _interleave(tasks):
    gens = list(tasks)
    nxt = [next(g, None) for g in gens]
    spent = {"M": 0.0, "V": 0.0}
    turn = 0
    while any(k is not None for k in nxt):
        ready = {kind: [j for j, k in enumerate(nxt) if k is not None and k[0] == kind] for kind in spent}
        want = "M" if (ready["M"] and (spent["M"] <= spent["V"] or not ready["V"])) else "V"
        if want == "M":
            cands = ready["M"]
            t = cands[turn % len(cands)]
            turn += 1
        else:
            t = ready["V"][0]
        spent[want] += nxt[t][1]
        nxt[t] = next(gens[t], None)


def _chain(*gens):
    for g in gens:
        yield from g


class _ValueStash:
    def __init__(self):
        self.vals = {}

    def put(self, name, val):
        self.vals[name] = val

    def get(self, name, j=None):
        return self.vals[name] if j is None else self.vals[name][j]


class _RefStash:
    def __init__(self, refs, slot, lo, n):
        self.refs, self.slot, self.lo, self.n = refs, slot, lo, n

    def put(self, name, val):
        self.refs[name][self.slot, self.lo:self.lo + self.n] = val

    def get(self, name, j=None):
        if j is None:
            return self.refs[name][self.slot, self.lo:self.lo + self.n]
        return self.refs[name][self.slot, self.lo + j]


def _proj_kernel(x_ref, w_ref, first_ref, mu_ref, lbl_ref, o_ref, last_ref, wbf_scr, carry_scr, *, tiles_per_seq):
    nb, tt, _ = x_ref.shape
    tile = pl.program_id(0)

    @pl.when(tile == 0)
    def _cast_weights():
        wbf_scr[...] = w_ref[...].astype(BF16)

    xb = x_ref[...].reshape(nb * tt, D_MODEL).astype(BF16)

    def cols(lo, width):
        return jnp.dot(xb, wbf_scr[:, lo:lo + width], preferred_element_type=F32)

    first = first_ref[...]
    if tiles_per_seq > 1:
        first = jnp.where(lax.rem(tile, tiles_per_seq) == 0, first, carry_scr[...])
    last_rows = []
    for lo in range(0, RWKV_PROJ, PROJ_SECTION):
        width = PROJ_SECTION
        p = cols(lo, width)
        p3 = p.reshape(nb, tt, width)
        row = lax.broadcasted_iota(jnp.int32, p3.shape, 1)
        prev = jnp.where(row == 0, first[:, :, lo:lo + width], pltpu.roll(p, 1, 0).reshape(p3.shape))
        o_ref[:, :, lo:lo + width] = p3 + mu_ref[:, lo:lo + width] * (prev - p3)
        last_rows.append(p3[:, tt - 1:tt, :])
    last = jnp.concatenate(last_rows, axis=-1)
    last_ref[...] = last
    if tiles_per_seq > 1:
        carry_scr[...] = last

    logits = lbl_ref[...]
    ex = jnp.exp(logits - jnp.max(logits, axis=0, keepdims=True))
    lb = ex[0:1] / jnp.sum(ex, axis=0, keepdims=True)
    for part, act in enumerate(("silu", "log_gate", "copy", "silu")):
        for lo in range(0, HGRN_WIDTH, PROJ_SECTION):
            col = RWKV_PROJ + part * HGRN_WIDTH + lo
            h = cols(col, PROJ_SECTION)
            if act == "silu":
                h = h * _sigmoid(h)
            elif act == "log_gate":
                lb_s = lb[:, lo:lo + PROJ_SECTION]
                h = jnp.log(lb_s + (1.0 - lb_s) * _sigmoid(h))
            o_ref[:, :, col:col + PROJ_SECTION] = h.reshape(nb, tt, PROJ_SECTION)


def _proj(x3, s_shift3, p, tm):
    B, T, _ = x3.shape
    tt = min(T, tm)
    nb = tm // tt
    tiles_per_seq = T // tt

    def tile_map(i):
        return (i // tiles_per_seq, lax.rem(i, tiles_per_seq), 0)

    def seq_map(i):
        return (i // tiles_per_seq, 0, 0)

    def const(shape):
        return pl.BlockSpec(shape, lambda i: (0, 0), pipeline_mode=pl.Buffered(1))

    return pl.pallas_call(
        functools.partial(_proj_kernel, tiles_per_seq=tiles_per_seq),
        grid=(B * T // tm,),
        in_specs=[pl.BlockSpec((nb, tt, D_MODEL), tile_map),
                  const((D_MODEL, PROJ)),
                  pl.BlockSpec((nb, 1, RWKV_PROJ), seq_map),
                  const((1, RWKV_PROJ)), const((DEPTH + 1, HGRN_WIDTH))],
        out_specs=[pl.BlockSpec((nb, tt, PROJ), tile_map),
                   pl.BlockSpec((nb, 1, RWKV_PROJ), seq_map)],
        out_shape=[jax.ShapeDtypeStruct((B, T, PROJ), F32),
                   jax.ShapeDtypeStruct((B, 1, RWKV_PROJ), F32)],
        scratch_shapes=[pltpu.VMEM((D_MODEL, PROJ), BF16), pltpu.VMEM((nb, 1, RWKV_PROJ), F32)],
        compiler_params=pltpu.CompilerParams(dimension_semantics=("arbitrary",),
                                             vmem_limit_bytes=VMEM_LIMIT),
        name="proj",
    )(x3, p["w_in"], s_shift3, p["shift_mu"], p["lb_logits"])


def _head_helpers(C):
    lane = lax.broadcasted_iota(jnp.int32, (C, LANES), 1)
    m0 = lane < RWKV_HEAD

    def head_sum(x):
        s0 = jnp.sum(jnp.where(m0, x, 0.0), axis=-1, keepdims=True)
        s1 = jnp.sum(jnp.where(m0, 0.0, x), axis=-1, keepdims=True)
        return jnp.where(m0, s0, s1)

    def stack_heads(x):
        return jnp.concatenate([jnp.where(m0, x, 0.0), jnp.where(m0, 0.0, x)], axis=-2)

    def merge_heads(x):
        return jnp.where(m0, x[:C], x[C:])

    return head_sum, stack_heads, merge_heads


def _rwkv_produce(stash, lo, n, C, r_ref, k_ref, v_ref, l_ref,
                  w0_ref, a0_ref, kk_ref, ka_ref, rk_ref, w1_ref, a1_ref, g1_ref):
    ids = range(n)
    grp = slice(lo, lo + n)
    head_sum, stack_heads, _ = _head_helpers(C)
    rowc = lax.broadcasted_iota(jnp.int32, (C, C), 0)
    colc = lax.broadcasted_iota(jnp.int32, (C, C), 1)
    tri = jnp.where(rowc >= colc, 1.0, 0.0).astype(BF16)

    w1 = _bf(w1_ref[...])
    a1 = _bf(a1_ref[...])
    g1 = _bf(g1_ref[...])

    yield ("V", 30.0 * n)
    xl = l_ref[grp].reshape(n * C, 2 * LANES)
    xl_lo = xl[:, :LANES]
    dw = jnp.dot(_bf(jnp.tanh(xl_lo)), w1, preferred_element_type=F32)
    da = jnp.dot(_bf(xl_lo), a1, preferred_element_type=F32)
    gate = jnp.dot(_bf(_sigmoid(xl[:, LANES:])), g1, preferred_element_type=F32)
    stash.put("gate", gate.reshape(n, C, LANES))

    yield ("V", 60.0 * n)
    lw = (-DECAY_SCALE * _sigmoid(w0_ref[...] + dw)).reshape(n, C, LANES)
    a_lr = _sigmoid(a0_ref[...] + da).reshape(n, C, LANES)
    xk = k_ref[grp]
    kk = xk * kk_ref[...]
    kk = kk * jnp.minimum(lax.rsqrt(head_sum(kk * kk)), 1.0 / KK_NORM_FLOOR)
    k2 = xk * (1.0 + (a_lr - 1.0) * ka_ref[...])
    b = kk * a_lr
    cum = jnp.stack([_cumsum_rows(tri, lw[j]) for j in ids])

    yield ("V", 80.0 * n)
    r = r_ref[grp]
    v = v_ref[grp]
    cum_last = cum[:, C - 1:C, :]
    e_in = jnp.exp(-cum)
    e_out = jnp.exp(cum_last - cum)
    at = -kk * jnp.exp(cum - lw)
    rt = r * jnp.exp(cum)
    bt = b * e_in
    kt = k2 * e_in
    stash.put("lhs", _bf(jnp.concatenate([stack_heads(at), stack_heads(rt)], axis=1)))
    stash.put("rhs", _bf(jnp.concatenate([bt, kt], axis=1)))
    stash.put("bt_hat", _bf(stack_heads(bt)))
    stash.put("atrt", _bf(jnp.concatenate([at, rt], axis=1)))
    stash.put("vb", _bf(v))
    stash.put("bkh", _bf(jnp.concatenate([b * e_out, k2 * e_out], axis=1)))
    stash.put("dec", jnp.exp(cum_last))
    stash.put("bonus", head_sum(r * k2 * rk_ref[...]) * v)


def _rwkv_consume(stash, lo, n, C, dot_cost, lnw_ref, lnb_ref, o_ref, s_scr):
    H = RWKV_HEAD
    ids = range(n)
    grp = slice(lo, lo + n)
    head_sum, stack_heads, merge_heads = _head_helpers(C)
    r2 = lax.broadcasted_iota(jnp.int32, (2 * C, 2 * C), 0)
    c2 = lax.broadcasted_iota(jnp.int32, (2 * C, 2 * C), 1)
    t2 = jnp.bitwise_and(r2, C - 1)
    s2 = jnp.bitwise_and(c2, C - 1)
    strict2 = t2 > s2
    incl2 = t2 >= s2
    strict_bd = jnp.logical_and(strict2, (r2 >= C) == (c2 >= C))
    eye2 = jnp.where(r2 == c2, 1.0, 0.0).astype(F32)
    bl_r = lax.broadcasted_iota(jnp.int32, (LANES, LANES), 0) < H
    bl_c = lax.broadcasted_iota(jnp.int32, (LANES, LANES), 1) < H
    blockdiag = bl_r == bl_c
    zeros_c = jnp.zeros((C, LANES), BF16)

    yield ("M", 2 * dot_cost)
    g_a, g_r, pw, inv = [], [], [], []
    for j in ids:
        lhs = stash.get("lhs", j)
        g = _dot(lhs, stash.get("rhs", j), NT)
        g_a.append(_bf(jnp.where(strict2, g[:2 * C], 0.0)))
        g_r.append(_bf(jnp.where(incl2, g[2 * C:], 0.0)))
        n_bd = jnp.where(strict_bd, _dot(lhs[:2 * C], stash.get("bt_hat", j), NT), 0.0)
        inv.append(eye2 + n_bd)
        pw.append(_bf(n_bd))
    yield ("M", 2 * dot_cost)
    S = [s_scr[lo + j] for j in ids]
    pq = [_dot(stash.get("atrt", j), _bf(S[j]), NT) for j in ids]
    pw = [_bf(_dot(pw[j], pw[j], NN)) for j in ids]
    span = 4
    while span < C:
        yield ("M", 2 * dot_cost)
        nxt = [_bf(_dot(pw[j], pw[j], NN)) for j in ids]
        inv = [inv[j] + _dot(_bf(inv[j]), pw[j], NN) for j in ids]
        pw = nxt
        span *= 2
    yield ("M", dot_cost)
    inv = [inv[j] + _dot(_bf(inv[j]), pw[j], NN) for j in ids]
    yield ("M", dot_cost)
    vb = [stash.get("vb", j) for j in ids]
    w_hat = [_bf(stack_heads(pq[j][:C] + merge_heads(_dot(g_a[j], jnp.concatenate([zeros_c, vb[j]], axis=0), NN))))
             for j in ids]
    yield ("M", dot_cost)
    z = []
    for j in ids:
        u2 = _dot(_bf(inv[j]), w_hat[j], NN)
        z.append(jnp.concatenate([_bf(u2[:C] + u2[C:]), vb[j]], axis=0))
    yield ("M", dot_cost)
    y = [pq[j][C:] + merge_heads(_dot(g_r[j], z[j], NN)) for j in ids]
    yield ("M", dot_cost)
    for j in ids:
        upd = _dot(z[j], stash.get("bkh", j), TN)
        s_scr[lo + j] = S[j] * stash.get("dec", j) + jnp.where(blockdiag, upd, 0.0)

    yield ("V", 40.0 * n)
    y = jnp.stack(y)
    mu = head_sum(y) * (1.0 / H)
    dy = y - mu
    var = head_sum(dy * dy) * (1.0 / H)
    yn = dy * lax.rsqrt(var + GN_EPS) * lnw_ref[...] + lnb_ref[...]
    o_ref[grp] = (yn + stash.get("bonus")) * stash.get("gate")


def _mid_rows(x, h):
    R = x.shape[0]
    row = lax.broadcasted_iota(jnp.int32, x.shape, 0)
    if 2 * h >= SUBLANES:
        return jnp.concatenate(
            [jnp.broadcast_to(x[b * 2 * h + h - 1:b * 2 * h + h], (2 * h, LANES)) for b in range(R // (2 * h))],
            axis=0)
    if h == 1:
        return jnp.where(jnp.bitwise_and(row, 1) == 1, pltpu.roll(x, 1, 0), x)
    picks = []
    for half in range(SUBLANES // (2 * h)):
        r0 = half * 2 * h + h - 1
        picks.append(jnp.concatenate(
            [jnp.broadcast_to(x[g * SUBLANES + r0:g * SUBLANES + r0 + 1], (SUBLANES, LANES))
             for g in range(R // SUBLANES)], axis=0))
    out = picks[-1]
    sub = jnp.bitwise_and(row, SUBLANES - 1)
    for half in range(len(picks) - 2, -1, -1):
        out = jnp.where(sub < (half + 1) * 2 * h, picks[half], out)
    return out


def _hgrn_produce(stash, lo, n, C, q_ref, lf_ref, i_ref):
    ids = range(n)
    grp = slice(lo, lo + n)
    t_i = lax.broadcasted_iota(jnp.int32, (C, C), 0)
    s_i = lax.broadcasted_iota(jnp.int32, (C, C), 1)
    tri = jnp.where(t_i >= s_i, 1.0, 0.0).astype(BF16)

    yield ("V", 20.0 * n)
    q = q_ref[grp]
    lf = lf_ref[grp]
    kin = 1.0 - jnp.exp(lf)
    bc = jnp.stack([_cumsum_rows(tri, lf[j]) for j in ids])

    yield ("V", 30.0 * n)
    b_last = bc[:, C - 1:C, :]
    stash.put("qs", _bf(q * jnp.exp(bc)))
    stash.put("ks", _bf(kin * jnp.exp(b_last - bc)))
    stash.put("iv", _bf(i_ref[grp]))
    stash.put("q", q)
    stash.put("kin", kin)
    stash.put("bc", bc)
    stash.put("dec", jnp.exp(b_last))


def _hgrn_consume(stash, lo, n, C, dot_cost, g_ref, hgw_ref, o_ref, s_scr):
    ids = range(n)
    grp = slice(lo, lo + n)
    t_i = lax.broadcasted_iota(jnp.int32, (C, C), 0)
    s_i = lax.broadcasted_iota(jnp.int32, (C, C), 1)
    halves = []
    h = C // 2
    while h >= 1:
        halves.append(h)
        h //= 2

    def level_mask(h):
        same = jnp.bitwise_and(t_i, -2 * h) == jnp.bitwise_and(s_i, -2 * h)
        return jnp.logical_and(same, jnp.logical_and(jnp.bitwise_and(t_i, h) != 0, jnp.bitwise_and(s_i, h) == 0))

    def seq(x, j):
        return _bf(x[j * C:(j + 1) * C])

    yield ("M", dot_cost)
    S = [s_scr[lo + j] for j in ids]
    o_state = [_dot(stash.get("qs", j), _bf(S[j]), NT) for j in ids]
    yield ("M", dot_cost)
    iv = [stash.get("iv", j) for j in ids]
    for j in ids:
        s_scr[lo + j] = S[j] * stash.get("dec", j) + _dot(iv[j], stash.get("ks", j), TN)
    yield ("M", dot_cost)
    q = stash.get("q").reshape(n * C, LANES)
    kin = stash.get("kin").reshape(n * C, LANES)
    bc = stash.get("bc").reshape(n * C, LANES)
    att = [jnp.where(t_i == s_i, _dot(seq(q, j), seq(kin, j), NT), 0.0) for j in ids]
    for h in halves:
        yield ("V", 25.0 * n)
        dmid = jnp.exp(-jnp.abs(bc - _mid_rows(bc, h)))
        ql = q * dmid
        kl = kin * dmid
        yield ("M", dot_cost)
        msk = level_mask(h)
        att = [jnp.where(msk, _dot(seq(ql, j), seq(kl, j), NT), att[j]) for j in ids]
    yield ("M", dot_cost)
    o_intra = [_dot(_bf(att[j]), iv[j], NN) for j in ids]

    yield ("V", 25.0 * n)
    o = jnp.stack([o_intra[j] + o_state[j] for j in ids])
    ms = jnp.mean(o * o, axis=-1, keepdims=True)
    o_ref[grp] = o * lax.rsqrt(ms + RMS_EPS) * hgw_ref[...] * g_ref[grp]


MXU_STAGE_COST = {True: (28.0, 19.0), False: (30.0, 20.0)}
N_MIXER_IN = 21
N_MIXER_OUT = 4
N_MIXER_STATE_SCRATCH = 2


def _mixer_kernel(*refs, bb, C, n_groups, pipelined):
    (r_ref, k_ref, v_ref, l_ref, w0_ref, a0_ref, kk_ref, ka_ref, rk_ref,
     lnw_ref, lnb_ref, w1_ref, a1_ref, g1_ref, srw0_ref,
     q_ref, lf_ref, i_ref, g_ref, hgw_ref, shg0_ref) = refs[:N_MIXER_IN]
    orw_ref, strw_ref, ohg_ref, sthg_ref = refs[N_MIXER_IN:N_MIXER_IN + N_MIXER_OUT]
    scr = refs[N_MIXER_IN + N_MIXER_OUT:]
    srw_scr, shg_scr = scr[:N_MIXER_STATE_SCRATCH]
    stash_refs = scr[N_MIXER_STATE_SCRATCH:]
    if pipelined:
        rw_refs = {name: stash_refs[k] for k, (name, _, _) in enumerate(RWKV_STASH)}
        hg_refs = {name: stash_refs[len(RWKV_STASH) + k] for k, (name, _, _) in enumerate(HGRN_STASH)}

    c = pl.program_id(2)
    last = pl.num_programs(2) - 1
    H = RWKV_HEAD

    @pl.when(c == 0)
    def _init():
        z = jnp.zeros((H, H), F32)
        for i in range(bb):
            top = jnp.concatenate([srw0_ref[i, 0], z], axis=1)
            bot = jnp.concatenate([z, srw0_ref[i, 1]], axis=1)
            srw_scr[i] = jnp.concatenate([top, bot], axis=0)
            shg_scr[i] = shg0_ref[i, 0].T
        if pipelined:
            for table in (rw_refs, hg_refs):
                for name, ref in table.items():
                    fill = jnp.ones if name == "dec" else jnp.zeros
                    ref[1] = fill(ref.shape[1:], ref.dtype)

    per = bb // n_groups
    groups = [(gidx * per, per) for gidx in range(n_groups)]

    def rwkv_produce(stash, lo, n):
        return _rwkv_produce(stash, lo, n, C, r_ref, k_ref, v_ref, l_ref,
                             w0_ref, a0_ref, kk_ref, ka_ref, rk_ref, w1_ref, a1_ref, g1_ref)

    def rwkv_consume(stash, lo, n):
        return _rwkv_consume(stash, lo, n, C, MXU_STAGE_COST[pipelined][0] * n, lnw_ref, lnb_ref, orw_ref, srw_scr)

    def hgrn_produce(stash, lo, n):
        return _hgrn_produce(stash, lo, n, C, q_ref, lf_ref, i_ref)

    def hgrn_consume(stash, lo, n):
        return _hgrn_consume(stash, lo, n, C, MXU_STAGE_COST[pipelined][1] * n, g_ref, hgw_ref, ohg_ref, shg_scr)

    if not pipelined:
        tasks = []
        for lo, n in groups:
            rw, hg = _ValueStash(), _ValueStash()
            tasks.append(_chain(rwkv_produce(rw, lo, n), rwkv_consume(rw, lo, n)))
            tasks.append(_chain(hgrn_produce(hg, lo, n), hgrn_consume(hg, lo, n)))
        _interleave(tasks)
    else:
        def step(write_slot):
            read_slot = 1 - write_slot
            tasks = []
            for lo, n in groups:
                tasks.append(rwkv_consume(_RefStash(rw_refs, read_slot, lo, n), lo, n))
                tasks.append(hgrn_consume(_RefStash(hg_refs, read_slot, lo, n), lo, n))
            for lo, n in groups:
                tasks.append(rwkv_produce(_RefStash(rw_refs, write_slot, lo, n), lo, n))
                tasks.append(hgrn_produce(_RefStash(hg_refs, write_slot, lo, n), lo, n))
            _interleave(tasks)

        parity = lax.rem(c, 2)
        pl.when(parity == 0)(functools.partial(step, 0))
        pl.when(parity == 1)(functools.partial(step, 1))

    @pl.when(c == last)
    def _fin():
        for i in range(bb):
            s_fin = srw_scr[i]
            strw_ref[i, 0] = s_fin[:H, :H]
            strw_ref[i, 1] = s_fin[H:, H:]
            sthg_ref[i, 0] = shg_scr[i].T


def _mixer(proj3, s_rwkv, s_hgrn, p, bb, C, n_groups, pipelined):
    B, T, _ = proj3.shape
    nc = T // C
    grid = (B // bb, PAIRS, nc + 1 if pipelined else nc)
    nb = RWKV_WIDTH // LANES
    hg0 = RWKV_PROJ // LANES
    lora_blk = LORA_COL // (2 * LANES)

    if pipelined:
        def produced(c):
            return jnp.minimum(c, nc - 1)

        def consumed(c):
            return jnp.maximum(c - 1, 0)
    else:
        produced = consumed = lambda c: c

    def tok(col0, chunk=produced):
        return pl.BlockSpec((bb, C, LANES), lambda b, h, c: (b, chunk(c), col0 + h))

    def vec(col0):
        return pl.BlockSpec((1, LANES), lambda b, h, c: (0, col0 + h))

    in_specs = [
        tok(0), tok(nb), tok(2 * nb),
        pl.BlockSpec((bb, C, 2 * LANES), lambda b, h, c: (b, produced(c), lora_blk)),
        vec(0), vec(0), vec(0), vec(0), vec(0), vec(0), vec(0),
        pl.BlockSpec((LANES, LANES), lambda b, h, c: (0, h)),
        pl.BlockSpec((LANES, LANES), lambda b, h, c: (0, h)),
        pl.BlockSpec((GATE_RANK, LANES), lambda b, h, c: (0, h)),
        pl.BlockSpec((bb, 2, RWKV_HEAD, RWKV_HEAD), lambda b, h, c: (b, h, 0, 0)),
        tok(hg0), tok(hg0 + nb), tok(hg0 + 2 * nb), tok(hg0 + 3 * nb, consumed),
        vec(0),
        pl.BlockSpec((bb, 1, HGRN_HEAD, HGRN_HEAD), lambda b, h, c: (b, h, 0, 0)),
    ]
    assert len(in_specs) == N_MIXER_IN
    out_specs = [
        pl.BlockSpec((bb, C, LANES), lambda b, h, c: (b, consumed(c), h)),
        pl.BlockSpec((bb, 2, RWKV_HEAD, RWKV_HEAD), lambda b, h, c: (b, h, 0, 0)),
        pl.BlockSpec((bb, C, LANES), lambda b, h, c: (b, consumed(c), h)),
        pl.BlockSpec((bb, 1, HGRN_HEAD, HGRN_HEAD), lambda b, h, c: (b, h, 0, 0)),
    ]
    out_shape = [jax.ShapeDtypeStruct((B, T, RWKV_WIDTH), F32),
                 jax.ShapeDtypeStruct((B, RWKV_HEADS, RWKV_HEAD, RWKV_HEAD), F32),
                 jax.ShapeDtypeStruct((B, T, HGRN_WIDTH), F32),
                 jax.ShapeDtypeStruct((B, HGRN_HEADS, HGRN_HEAD, HGRN_HEAD), F32)]
    scratch = [pltpu.VMEM((bb, LANES, LANES), F32), pltpu.VMEM((bb, HGRN_HEAD, HGRN_HEAD), F32)]
    assert len(scratch) == N_MIXER_STATE_SCRATCH
    if pipelined:
        scratch += [pltpu.VMEM((2, bb, rows * C if rows else 1, LANES), dt)
                    for _, rows, dt in RWKV_STASH + HGRN_STASH]
    return pl.pallas_call(
        functools.partial(_mixer_kernel, bb=bb, C=C, n_groups=n_groups, pipelined=pipelined),
        grid=grid, in_specs=in_specs, out_specs=out_specs, out_shape=out_shape,
        scratch_shapes=scratch,
        compiler_params=pltpu.CompilerParams(
            dimension_semantics=("arbitrary", "arbitrary", "arbitrary"),
            vmem_limit_bytes=VMEM_LIMIT),
        name="mixer",
    )(proj3, proj3, proj3, proj3,
      p["w0"], p["a0"], p["k_k"], p["k_a"], p["r_k"], p["ln_x_w"], p["ln_x_b"],
      p["w1u_pad"], p["a1u_pad"], p["g1u"], s_rwkv,
      proj3, proj3, proj3, proj3, p["hg_norm_w"], s_hgrn)


def _layer_norm(x, g, b):
    mu = jnp.mean(x, axis=-1, keepdims=True)
    d = x - mu
    var = jnp.mean(d * d, axis=-1, keepdims=True)
    return d * lax.rsqrt(var + LN_EPS) * g + b


def _post_kernel(x_ref, orw_ref, ohg_ref, wo1_ref, wo2_ref, g1_ref, b1_ref, wup_ref, wdn_ref,
                 g2_ref, b2_ref, y_ref):
    tm = x_ref.shape[0]
    rows = [slice(k * tm // POST_SPLIT, (k + 1) * tm // POST_SPLIT) for k in range(POST_SPLIT)]

    def mix_stage(r):
        return (jnp.dot(orw_ref[r, :].astype(BF16), wo1_ref[...], preferred_element_type=F32)
                + jnp.dot(ohg_ref[r, :].astype(BF16), wo2_ref[...], preferred_element_type=F32))

    def up_stage(r, mix):
        h1 = _layer_norm(ALPHA * x_ref[r, :] + mix, g1_ref[...], b1_ref[...])
        return h1, jnp.dot(h1.astype(BF16), wup_ref[...], preferred_element_type=F32)

    def down_stage(up):
        up = jnp.square(jnp.maximum(up, 0.0))
        return jnp.dot(up.astype(BF16), wdn_ref[...], preferred_element_type=F32)

    def out_stage(r, h1, ff):
        y_ref[r, :] = _layer_norm(ALPHA * h1 + ff, g2_ref[...], b2_ref[...])

    mix, h1, up, ff = {}, {}, {}, {}
    for step in range(POST_SPLIT + 3):
        if step < POST_SPLIT:
            mix[step] = mix_stage(rows[step])
        k = step - 1
        if 0 <= k < POST_SPLIT:
            h1[k], up[k] = up_stage(rows[k], mix.pop(k))
        k = step - 2
        if 0 <= k < POST_SPLIT:
            ff[k] = down_stage(up.pop(k))
        k = step - 3
        if 0 <= k < POST_SPLIT:
            out_stage(rows[k], h1.pop(k), ff.pop(k))


def _post(x2, orw2, ohg2, p, tm):
    n = x2.shape[0]

    def const(shape):
        return pl.BlockSpec(shape, lambda i: (0, 0), pipeline_mode=pl.Buffered(1))

    return pl.pallas_call(
        _post_kernel,
        grid=(n // tm,),
        in_specs=[pl.BlockSpec((tm, D_MODEL), lambda i: (i, 0)),
                  pl.BlockSpec((tm, RWKV_WIDTH), lambda i: (i, 0)),
                  pl.BlockSpec((tm, HGRN_WIDTH), lambda i: (i, 0)),
                  const((RWKV_WIDTH, D_MODEL)), const((HGRN_WIDTH, D_MODEL)),
                  const((1, D_MODEL)), const((1, D_MODEL)),
                  const((D_MODEL, D_FF)), const((D_FF, D_MODEL)),
                  const((1, D_MODEL)), const((1, D_MODEL))],
        out_specs=pl.BlockSpec((tm, D_MODEL), lambda i: (i, 0)),
        out_shape=jax.ShapeDtypeStruct((n, D_MODEL), F32),
        compiler_params=pltpu.CompilerParams(dimension_semantics=("arbitrary",),
                                             vmem_limit_bytes=VMEM_LIMIT),
        name="post",
    )(x2, orw2, ohg2, p["wo_rw"], p["wo_hg"], p["ln1_g"], p["ln1_b"], p["w_up"], p["w_down"],
      p["ln2_g"], p["ln2_b"])


def _prep_params(w_in, shift_mu, w0, w1u, a0, a1u, g1u, k_k, k_a, r_k, ln_x_w, ln_x_b, lb_logits,
                 hg_norm_w, w_out, ln1_g, ln1_b, w_up, w_down, ln2_g, ln2_b):
    zw = jnp.zeros((LANES - DECAY_RANK, RWKV_WIDTH), F32)
    za = jnp.zeros((LANES - AICL_RANK, RWKV_WIDTH), F32)
    w_out_bf = w_out[0].astype(BF16)
    return {
        "w_in": w_in[0],
        "shift_mu": shift_mu[0].reshape(1, RWKV_PROJ),
        "w0": w0[0].reshape(1, RWKV_WIDTH), "a0": a0[0].reshape(1, RWKV_WIDTH),
        "k_k": k_k[0].reshape(1, RWKV_WIDTH), "k_a": k_a[0].reshape(1, RWKV_WIDTH),
        "r_k": r_k[0].reshape(1, RWKV_WIDTH),
        "ln_x_w": ln_x_w[0].reshape(1, RWKV_WIDTH), "ln_x_b": ln_x_b[0].reshape(1, RWKV_WIDTH),
        "w1u_pad": jnp.concatenate([w1u[0], zw], axis=0),
        "a1u_pad": jnp.concatenate([za, a1u[0]], axis=0),
        "g1u": g1u[0],
        "lb_logits": lb_logits.astype(F32),
        "hg_norm_w": hg_norm_w[0].reshape(1, HGRN_WIDTH),
        "wo_rw": w_out_bf[:RWKV_WIDTH], "wo_hg": w_out_bf[RWKV_WIDTH:],
        "ln1_g": ln1_g[0].reshape(1, D_MODEL), "ln1_b": ln1_b[0].reshape(1, D_MODEL),
        "w_up": w_up[0].astype(BF16), "w_down": w_down[0].astype(BF16),
        "ln2_g": ln2_g[0].reshape(1, D_MODEL), "ln2_b": ln2_b[0].reshape(1, D_MODEL),
    }


def _run_group(x, s_rwkv, s_hgrn, s_shift, p, *, tm, bb, chunk, n_groups, pipelined):
    B, T, _ = x.shape
    proj3, last = _proj(x, s_shift.reshape(B, 1, RWKV_PROJ), p, tm)
    o_rw, st_rw, o_hg, st_hg = _mixer(proj3, s_rwkv, s_hgrn, p, bb, chunk, n_groups, pipelined)
    y2 = _post(x.reshape(B * T, D_MODEL), o_rw.reshape(B * T, RWKV_WIDTH), o_hg.reshape(B * T, HGRN_WIDTH),
               p, tm)
    return y2.reshape(B, T, D_MODEL), st_rw[None], st_hg[None], last.reshape(1, B, RWKV_PROJ)


PROMPT_CFG = dict(tm=512, bb=8, chunk=64, n_groups=1, pipelined=True)
SAMPLE_CFG = dict(tm=512, bb=32, chunk=8, n_groups=1, pipelined=False)


def kernel(x_prompt, x_sample, state_rwkv, state_hgrn, state_shift, w_in, shift_mu, w0, w1u, a0, a1u, g1u, k_k, k_a, r_k, ln_x_w, ln_x_b, lb_logits, hg_norm_w, w_out, ln1_g, ln1_b, w_up, w_down, ln2_g, ln2_b):
    assert w_in.shape[0] == DEPTH
    p = _prep_params(w_in, shift_mu, w0, w1u, a0, a1u, g1u, k_k, k_a, r_k, ln_x_w, ln_x_b, lb_logits,
                     hg_norm_w, w_out, ln1_g, ln1_b, w_up, w_down, ln2_g, ln2_b)
    bp = x_prompt.shape[0]
    z_rw = jnp.zeros((bp, RWKV_HEADS, RWKV_HEAD, RWKV_HEAD), F32)
    z_hg = jnp.zeros((bp, HGRN_HEADS, HGRN_HEAD, HGRN_HEAD), F32)
    z_sh = jnp.zeros((bp, RWKV_PROJ), F32)
    y_p, rw_p, hg_p, sh_p = _run_group(x_prompt, z_rw, z_hg, z_sh, p, **PROMPT_CFG)
    y_s, rw_s, hg_s, sh_s = _run_group(x_sample, state_rwkv[0].astype(F32), state_hgrn[0].astype(F32),
                                       state_shift[0].astype(F32), p, **SAMPLE_CFG)
    return (y_p, y_s, rw_p, rw_s, hg_p, hg_s, sh_p, sh_s)
```

```python
import functools
import math

import jax
import jax.numpy as jnp
from jax import lax
from jax.experimental import pallas as pl
from jax.experimental.pallas import tpu as pltpu

F32 = jnp.float32
BF16 = jnp.bfloat16

D_MODEL = 1024
RWKV_WIDTH = 512
RWKV_HEAD = 64
RWKV_HEADS = 8
HGRN_WIDTH = 512
HGRN_HEAD = 128
HGRN_HEADS = 4
DECAY_RANK = 64
AICL_RANK = 64
GATE_RANK = 128
RWKV_PROJ = 3 * RWKV_WIDTH + DECAY_RANK + AICL_RANK + GATE_RANK
HGRN_PROJ = 4 * HGRN_WIDTH
PROJ = RWKV_PROJ + HGRN_PROJ
D_FF = 4 * D_MODEL
DEPTH = 1
ALPHA = (2.0 * DEPTH) ** 0.25
LN_EPS = 1e-5
GN_EPS = RWKV_HEAD * 1e-5
RMS_EPS = 1e-6
KK_NORM_FLOOR = 1e-12
DECAY_SCALE = math.exp(-0.5)

LANES = 128
SUBLANES = 8
PAIRS = RWKV_HEADS // 2
LORA_COL = 3 * RWKV_WIDTH
VMEM_LIMIT = 56 * 1024 * 1024
PROJ_SECTION = 256
POST_SPLIT = 2

NN = ((1,), (0,))
NT = ((1,), (1,))
TN = ((0,), (0,))

RWKV_STASH = (("lhs", 4, BF16), ("rhs", 2, BF16), ("bt_hat", 2, BF16), ("atrt", 2, BF16), ("vb", 1, BF16),
              ("bkh", 2, BF16), ("dec", 0, F32), ("bonus", 1, F32), ("gate", 1, F32))
HGRN_STASH = (("qs", 1, BF16), ("ks", 1, BF16), ("iv", 1, BF16), ("q", 1, F32), ("kin", 1, F32),
              ("bc", 1, F32), ("dec", 0, F32))


def _dot(a, b, dims):
    return lax.dot_general(a, b, (dims, ((), ())), preferred_element_type=F32)


def _bf(x):
    return x.astype(BF16)


def _cumsum_rows(tri, x):
    p0 = _bf(x)
    r1 = x - p0.astype(F32)
    p1 = _bf(r1)
    p2 = _bf(r1 - p1.astype(F32))
    return _dot(tri, p0, NN) + _dot(tri, p1, NN) + _dot(tri, p2, NN)


def _sigmoid(x):
    return 0.5 * jnp.tanh(0.5 * x) + 0.5


def _interleave(tasks):
    gens = list(tasks)
    nxt = [next(g, None) for g in gens]
    spent = {"M": 0.0, "V": 0.0}
    turn = 0
    while any(k is not None for k in nxt):
        ready = {kind: [j for j, k in enumerate(nxt) if k is not None and k[0] == kind] for kind in spent}
        want = "M" if (ready["M"] and (spent["M"] <= spent["V"] or not ready["V"])) else "V"
        if want == "M":
            cands = ready["M"]
            t = cands[turn % len(cands)]
            turn += 1
        else:
            t = ready["V"][0]
        spent[want] += nxt[t][1]
        nxt[t] = next(gens[t], None)


def _chain(*gens):
    for g in gens:
        yield from g


class _ValueStash:
    def __init__(self):
        self.vals = {}

    def put(self, name, val):
        self.vals[name] = val

    def get(self, name, j=None):
        return self.vals[name] if j is None else self.vals[name][j]


class _RefStash:
    def __init__(self, refs, slot, lo, n):
        self.refs, self.slot, self.lo, self.n = refs, slot, lo, n

    def put(self, name, val):
        self.refs[name][self.slot, self.lo:self.lo + self.n] = val

    def get(self, name, j=None):
        if j is None:
            return self.refs[name][self.slot, self.lo:self.lo + self.n]
        return self.refs[name][self.slot, self.lo + j]


def _proj_kernel(x_ref, w_ref, first_ref, mu_ref, lbl_ref, o_ref, last_ref, wbf_scr, carry_scr, *, tiles_per_seq):
    nb, tt, _ = x_ref.shape
    tile = pl.program_id(0)

    @pl.when(tile == 0)
    def _cast_weights():
        wbf_scr[...] = w_ref[...].astype(BF16)

    xb = x_ref[...].reshape(nb * tt, D_MODEL).astype(BF16)

    def cols(lo, width):
        return jnp.dot(xb, wbf_scr[:, lo:lo + width], preferred_element_type=F32)

    first = first_ref[...]
    if tiles_per_seq > 1:
        first = jnp.where(lax.rem(tile, tiles_per_seq) == 0, first, carry_scr[...])
    last_rows = []
    for lo in range(0, RWKV_PROJ, PROJ_SECTION):
        width = PROJ_SECTION
        p = cols(lo, width)
        p3 = p.reshape(nb, tt, width)
        row = lax.broadcasted_iota(jnp.int32, p3.shape, 1)
        prev = jnp.where(row == 0, first[:, :, lo:lo + width], pltpu.roll(p, 1, 0).reshape(p3.shape))
        o_ref[:, :, lo:lo + width] = p3 + mu_ref[:, lo:lo + width] * (prev - p3)
        last_rows.append(p3[:, tt - 1:tt, :])
    last = jnp.concatenate(last_rows, axis=-1)
    last_ref[...] = last
    if tiles_per_seq > 1:
        carry_scr[...] = last

    logits = lbl_ref[...]
    ex = jnp.exp(logits - jnp.max(logits, axis=0, keepdims=True))
    lb = ex[0:1] / jnp.sum(ex, axis=0, keepdims=True)
    for part, act in enumerate(("silu", "log_gate", "copy", "silu")):
        for lo in range(0, HGRN_WIDTH, PROJ_SECTION):
            col = RWKV_PROJ + part * HGRN_WIDTH + lo
            h = cols(col, PROJ_SECTION)
            if act == "silu":
                h = h * _sigmoid(h)
            elif act == "log_gate":
                lb_s = lb[:, lo:lo + PROJ_SECTION]
                h = jnp.log(lb_s + (1.0 - lb_s) * _sigmoid(h))
            o_ref[:, :, col:col + PROJ_SECTION] = h.reshape(nb, tt, PROJ_SECTION)


def _proj(x3, s_shift3, p, tm):
    B, T, _ = x3.shape
    tt = min(T, tm)
    nb = tm // tt
    tiles_per_seq = T // tt

    def tile_map(i):
        return (i // tiles_per_seq, lax.rem(i, tiles_per_seq), 0)

    def seq_map(i):
        return (i // tiles_per_seq, 0, 0)

    def const(shape):
        return pl.BlockSpec(shape, lambda i: (0, 0), pipeline_mode=pl.Buffered(1))

    return pl.pallas_call(
        functools.partial(_proj_kernel, tiles_per_seq=tiles_per_seq),
        grid=(B * T // tm,),
        in_specs=[pl.BlockSpec((nb, tt, D_MODEL), tile_map),
                  const((D_MODEL, PROJ)),
                  pl.BlockSpec((nb, 1, RWKV_PROJ), seq_map),
                  const((1, RWKV_PROJ)), const((DEPTH + 1, HGRN_WIDTH))],
        out_specs=[pl.BlockSpec((nb, tt, PROJ), tile_map),
                   pl.BlockSpec((nb, 1, RWKV_PROJ), seq_map)],
        out_shape=[jax.ShapeDtypeStruct((B, T, PROJ), F32),
                   jax.ShapeDtypeStruct((B, 1, RWKV_PROJ), F32)],
        scratch_shapes=[pltpu.VMEM((D_MODEL, PROJ), BF16), pltpu.VMEM((nb, 1, RWKV_PROJ), F32)],
        compiler_params=pltpu.CompilerParams(dimension_semantics=("arbitrary",),
                                             vmem_limit_bytes=VMEM_LIMIT),
        name="proj",
    )(x3, p["w_in"], s_shift3, p["shift_mu"], p["lb_logits"])


def _head_helpers(C):
    lane = lax.broadcasted_iota(jnp.int32, (C, LANES), 1)
    m0 = lane < RWKV_HEAD

    def head_sum(x):
        s0 = jnp.sum(jnp.where(m0, x, 0.0), axis=-1, keepdims=True)
        s1 = jnp.sum(jnp.where(m0, 0.0, x), axis=-1, keepdims=True)
        return jnp.where(m0, s0, s1)

    def stack_heads(x):
        return jnp.concatenate([jnp.where(m0, x, 0.0), jnp.where(m0, 0.0, x)], axis=-2)

    def merge_heads(x):
        return jnp.where(m0, x[:C], x[C:])

    return head_sum, stack_heads, merge_heads


def _rwkv_produce(stash, lo, n, C, r_ref, k_ref, v_ref, l_ref,
                  w0_ref, a0_ref, kk_ref, ka_ref, rk_ref, w1_ref, a1_ref, g1_ref):
    ids = range(n)
    grp = slice(lo, lo + n)
    head_sum, stack_heads, _ = _head_helpers(C)
    rowc = lax.broadcasted_iota(jnp.int32, (C, C), 0)
    colc = lax.broadcasted_iota(jnp.int32, (C, C), 1)
    tri = jnp.where(rowc >= colc, 1.0, 0.0).astype(BF16)

    w1 = _bf(w1_ref[...])
    a1 = _bf(a1_ref[...])
    g1 = _bf(g1_ref[...])

    yield ("V", 30.0 * n)
    xl = l_ref[grp].reshape(n * C, 2 * LANES)
    xl_lo = xl[:, :LANES]
    dw = jnp.dot(_bf(jnp.tanh(xl_lo)), w1, preferred_element_type=F32)
    da = jnp.dot(_bf(xl_lo), a1, preferred_element_type=F32)
    gate = jnp.dot(_bf(_sigmoid(xl[:, LANES:])), g1, preferred_element_type=F32)
    stash.put("gate", gate.reshape(n, C, LANES))

    yield ("V", 60.0 * n)
    lw = (-DECAY_SCALE * _sigmoid(w0_ref[...] + dw)).reshape(n, C, LANES)
    a_lr = _sigmoid(a0_ref[...] + da).reshape(n, C, LANES)
    xk = k_ref[grp]
    kk = xk * kk_ref[...]
    kk = kk * jnp.minimum(lax.rsqrt(head_sum(kk * kk)), 1.0 / KK_NORM_FLOOR)
    k2 = xk * (1.0 + (a_lr - 1.0) * ka_ref[...])
    b = kk * a_lr
    cum = jnp.stack([_cumsum_rows(tri, lw[j]) for j in ids])

    yield ("V", 80.0 * n)
    r = r_ref[grp]
    v = v_ref[grp]
    cum_last = cum[:, C - 1:C, :]
    e_in = jnp.exp(-cum)
    e_out = jnp.exp(cum_last - cum)
    at = -kk * jnp.exp(cum - lw)
    rt = r * jnp.exp(cum)
    bt = b * e_in
    kt = k2 * e_in
    stash.put("lhs", _bf(jnp.concatenate([stack_heads(at), stack_heads(rt)], axis=1)))
    stash.put("rhs", _bf(jnp.concatenate([bt, kt], axis=1)))
    stash.put("bt_hat", _bf(stack_heads(bt)))
    stash.put("atrt", _bf(jnp.concatenate([at, rt], axis=1)))
    stash.put("vb", _bf(v))
    stash.put("bkh", _bf(jnp.concatenate([b * e_out, k2 * e_out], axis=1)))
    stash.put("dec", jnp.exp(cum_last))
    stash.put("bonus", head_sum(r * k2 * rk_ref[...]) * v)


def _rwkv_consume(stash, lo, n, C, dot_cost, lnw_ref, lnb_ref, o_ref, s_scr):
    H = RWKV_HEAD
    ids = range(n)
    grp = slice(lo, lo + n)
    head_sum, stack_heads, merge_heads = _head_helpers(C)
    r2 = lax.broadcasted_iota(jnp.int32, (2 * C, 2 * C), 0)
    c2 = lax.broadcasted_iota(jnp.int32, (2 * C, 2 * C), 1)
    t2 = jnp.bitwise_and(r2, C - 1)
    s2 = jnp.bitwise_and(c2, C - 1)
    strict2 = t2 > s2
    incl2 = t2 >= s2
    strict_bd = jnp.logical_and(strict2, (r2 >= C) == (c2 >= C))
    eye2 = jnp.where(r2 == c2, 1.0, 0.0).astype(F32)
    bl_r = lax.broadcasted_iota(jnp.int32, (LANES, LANES), 0) < H
    bl_c = lax.broadcasted_iota(jnp.int32, (LANES, LANES), 1) < H
    blockdiag = bl_r == bl_c
    zeros_c = jnp.zeros((C, LANES), BF16)

    yield ("M", 2 * dot_cost)
    g_a, g_r, pw, inv = [], [], [], []
    for j in ids:
        lhs = stash.get("lhs", j)
        g = _dot(lhs, stash.get("rhs", j), NT)
        g_a.append(_bf(jnp.where(strict2, g[:2 * C], 0.0)))
        g_r.append(_bf(jnp.where(incl2, g[2 * C:], 0.0)))
        n_bd = jnp.where(strict_bd, _dot(lhs[:2 * C], stash.get("bt_hat", j), NT), 0.0)
        inv.append(eye2 + n_bd)
        pw.append(_bf(n_bd))
    yield ("M", 2 * dot_cost)
    S = [s_scr[lo + j] for j in ids]
    pq = [_dot(stash.get("atrt", j), _bf(S[j]), NT) for j in ids]
    pw = [_bf(_dot(pw[j], pw[j], NN)) for j in ids]
    span = 4
    while span < C:
        yield ("M", 2 * dot_cost)
        nxt = [_bf(_dot(pw[j], pw[j], NN)) for j in ids]
        inv = [inv[j] + _dot(_bf(inv[j]), pw[j], NN) for j in ids]
        pw = nxt
        span *= 2
    yield ("M", 2 * dot_cost)
    inv = [inv[j] + _dot(_bf(inv[j]), pw[j], NN) for j in ids]
    vb = [stash.get("vb", j) for j in ids]
    w_hat = [_bf(stack_heads(pq[j][:C] + merge_heads(_dot(g_a[j], jnp.concatenate([zeros_c, vb[j]], axis=0), NN))))
             for j in ids]
    yield ("M", dot_cost)
    z = []
    for j in ids:
        u2 = _dot(_bf(inv[j]), w_hat[j], NN)
        z.append(jnp.concatenate([_bf(u2[:C] + u2[C:]), vb[j]], axis=0))
    yield ("M", 2 * dot_cost)
    y = [pq[j][C:] + merge_heads(_dot(g_r[j], z[j], NN)) for j in ids]
    for j in ids:
        upd = _dot(z[j], stash.get("bkh", j), TN)
        s_scr[lo + j] = S[j] * stash.get("dec", j) + jnp.where(blockdiag, upd, 0.0)

    yield ("V", 40.0 * n)
    y = jnp.stack(y)
    mu = head_sum(y) * (1.0 / H)
    dy = y - mu
    var = head_sum(dy * dy) * (1.0 / H)
    yn = dy * lax.rsqrt(var + GN_EPS) * lnw_ref[...] + lnb_ref[...]
    o_ref[grp] = (yn + stash.get("bonus")) * stash.get("gate")


def _mid_rows(x, h):
    R = x.shape[0]
    row = lax.broadcasted_iota(jnp.int32, x.shape, 0)
    if 2 * h >= SUBLANES:
        return jnp.concatenate(
            [jnp.broadcast_to(x[b * 2 * h + h - 1:b * 2 * h + h], (2 * h, LANES)) for b in range(R // (2 * h))],
            axis=0)
    if h == 1:
        return jnp.where(jnp.bitwise_and(row, 1) == 1, pltpu.roll(x, 1, 0), x)
    picks = []
    for half in range(SUBLANES // (2 * h)):
        r0 = half * 2 * h + h - 1
        picks.append(jnp.concatenate(
            [jnp.broadcast_to(x[g * SUBLANES + r0:g * SUBLANES + r0 + 1], (SUBLANES, LANES))
             for g in range(R // SUBLANES)], axis=0))
    out = picks[-1]
    sub = jnp.bitwise_and(row, SUBLANES - 1)
    for half in range(len(picks) - 2, -1, -1):
        out = jnp.where(sub < (half + 1) * 2 * h, picks[half], out)
    return out


def _hgrn_produce(stash, lo, n, C, q_ref, lf_ref, i_ref):
    ids = range(n)
    grp = slice(lo, lo + n)
    t_i = lax.broadcasted_iota(jnp.int32, (C, C), 0)
    s_i = lax.broadcasted_iota(jnp.int32, (C, C), 1)
    tri = jnp.where(t_i >= s_i, 1.0, 0.0).astype(BF16)

    yield ("V", 20.0 * n)
    q = q_ref[grp]
    lf = lf_ref[grp]
    kin = 1.0 - jnp.exp(lf)
    bc = jnp.stack([_cumsum_rows(tri, lf[j]) for j in ids])

    yield ("V", 30.0 * n)
    b_last = bc[:, C - 1:C, :]
    stash.put("qs", _bf(q * jnp.exp(bc)))
    stash.put("ks", _bf(kin * jnp.exp(b_last - bc)))
    stash.put("iv", _bf(i_ref[grp]))
    stash.put("q", q)
    stash.put("kin", kin)
    stash.put("bc", bc)
    stash.put("dec", jnp.exp(b_last))


def _hgrn_consume(stash, lo, n, C, dot_cost, g_ref, hgw_ref, o_ref, s_scr):
    ids = range(n)
    grp = slice(lo, lo + n)
    t_i = lax.broadcasted_iota(jnp.int32, (C, C), 0)
    s_i = lax.broadcasted_iota(jnp.int32, (C, C), 1)
    halves = []
    h = C // 2
    while h >= 1:
        halves.append(h)
        h //= 2

    def level_mask(h):
        same = jnp.bitwise_and(t_i, -2 * h) == jnp.bitwise_and(s_i, -2 * h)
        return jnp.logical_and(same, jnp.logical_and(jnp.bitwise_and(t_i, h) != 0, jnp.bitwise_and(s_i, h) == 0))

    def seq(x, j):
        return _bf(x[j * C:(j + 1) * C])

    yield ("M", dot_cost)
    S = [s_scr[lo + j] for j in ids]
    o_state = [_dot(stash.get("qs", j), _bf(S[j]), NT) for j in ids]
    yield ("M", dot_cost)
    iv = [stash.get("iv", j) for j in ids]
    for j in ids:
        s_scr[lo + j] = S[j] * stash.get("dec", j) + _dot(iv[j], stash.get("ks", j), TN)
    yield ("M", dot_cost)
    q = stash.get("q").reshape(n * C, LANES)
    kin = stash.get("kin").reshape(n * C, LANES)
    bc = stash.get("bc").reshape(n * C, LANES)
    att = [jnp.where(t_i == s_i, _dot(seq(q, j), seq(kin, j), NT), 0.0) for j in ids]
    for h in halves:
        yield ("V", 25.0 * n)
        dmid = jnp.exp(-jnp.abs(bc - _mid_rows(bc, h)))
        ql = q * dmid
        kl = kin * dmid
        yield ("M", dot_cost)
        msk = level_mask(h)
        att = [jnp.where(msk, _dot(seq(ql, j), seq(kl, j), NT), att[j]) for j in ids]
    yield ("M", dot_cost)
    o_intra = [_dot(_bf(att[j]), iv[j], NN) for j in ids]

    yield ("V", 25.0 * n)
    o = jnp.stack([o_intra[j] + o_state[j] for j in ids])
    ms = jnp.mean(o * o, axis=-1, keepdims=True)
    o_ref[grp] = o * lax.rsqrt(ms + RMS_EPS) * hgw_ref[...] * g_ref[grp]


MXU_STAGE_COST = {True: (28.0, 19.0), False: (30.0, 20.0)}
N_MIXER_IN = 21
N_MIXER_OUT = 4
N_MIXER_STATE_SCRATCH = 2


def _mixer_kernel(*refs, bb, C, n_groups, pipelined):
    (r_ref, k_ref, v_ref, l_ref, w0_ref, a0_ref, kk_ref, ka_ref, rk_ref,
     lnw_ref, lnb_ref, w1_ref, a1_ref, g1_ref, srw0_ref,
     q_ref, lf_ref, i_ref, g_ref, hgw_ref, shg0_ref) = refs[:N_MIXER_IN]
    orw_ref, strw_ref, ohg_ref, sthg_ref = refs[N_MIXER_IN:N_MIXER_IN + N_MIXER_OUT]
    scr = refs[N_MIXER_IN + N_MIXER_OUT:]
    srw_scr, shg_scr = scr[:N_MIXER_STATE_SCRATCH]
    stash_refs = scr[N_MIXER_STATE_SCRATCH:]
    if pipelined:
        rw_refs = {name: stash_refs[k] for k, (name, _, _) in enumerate(RWKV_STASH)}
        hg_refs = {name: stash_refs[len(RWKV_STASH) + k] for k, (name, _, _) in enumerate(HGRN_STASH)}

    c = pl.program_id(2)
    last = pl.num_programs(2) - 1
    H = RWKV_HEAD

    @pl.when(c == 0)
    def _init():
        z = jnp.zeros((H, H), F32)
        for i in range(bb):
            top = jnp.concatenate([srw0_ref[i, 0], z], axis=1)
            bot = jnp.concatenate([z, srw0_ref[i, 1]], axis=1)
            srw_scr[i] = jnp.concatenate([top, bot], axis=0)
            shg_scr[i] = shg0_ref[i, 0].T
        if pipelined:
            for table in (rw_refs, hg_refs):
                for name, ref in table.items():
                    fill = jnp.ones if name == "dec" else jnp.zeros
                    ref[1] = fill(ref.shape[1:], ref.dtype)

    per = bb // n_groups
    groups = [(gidx * per, per) for gidx in range(n_groups)]

    def rwkv_produce(stash, lo, n):
        return _rwkv_produce(stash, lo, n, C, r_ref, k_ref, v_ref, l_ref,
                             w0_ref, a0_ref, kk_ref, ka_ref, rk_ref, w1_ref, a1_ref, g1_ref)

    def rwkv_consume(stash, lo, n):
        return _rwkv_consume(stash, lo, n, C, MXU_STAGE_COST[pipelined][0] * n, lnw_ref, lnb_ref, orw_ref, srw_scr)

    def hgrn_produce(stash, lo, n):
        return _hgrn_produce(stash, lo, n, C, q_ref, lf_ref, i_ref)

    def hgrn_consume(stash, lo, n):
        return _hgrn_consume(stash, lo, n, C, MXU_STAGE_COST[pipelined][1] * n, g_ref, hgw_ref, ohg_ref, shg_scr)

    if not pipelined:
        tasks = []
        for lo, n in groups:
            rw, hg = _ValueStash(), _ValueStash()
            tasks.append(_chain(rwkv_produce(rw, lo, n), rwkv_consume(rw, lo, n)))
            tasks.append(_chain(hgrn_produce(hg, lo, n), hgrn_consume(hg, lo, n)))
        _interleave(tasks)
    else:
        def step(write_slot):
            read_slot = 1 - write_slot
            tasks = []
            for lo, n in groups:
                tasks.append(rwkv_consume(_RefStash(rw_refs, read_slot, lo, n), lo, n))
                tasks.append(hgrn_consume(_RefStash(hg_refs, read_slot, lo, n), lo, n))
            for lo, n in groups:
                tasks.append(rwkv_produce(_RefStash(rw_refs, write_slot, lo, n), lo, n))
                tasks.append(hgrn_produce(_RefStash(hg_refs, write_slot, lo, n), lo, n))
            _interleave(tasks)

        parity = lax.rem(c, 2)
        pl.when(parity == 0)(functools.partial(step, 0))
        pl.when(parity == 1)(functools.partial(step, 1))

    @pl.when(c == last)
    def _fin():
        for i in range(bb):
            s_fin = srw_scr[i]
            strw_ref[i, 0] = s_fin[:H, :H]
            strw_ref[i, 1] = s_fin[H:, H:]
            sthg_ref[i, 0] = shg_scr[i].T


def _mixer(proj3, s_rwkv, s_hgrn, p, bb, C, n_groups, pipelined):
    B, T, _ = proj3.shape
    nc = T // C
    grid = (B // bb, PAIRS, nc + 1 if pipelined else nc)
    nb = RWKV_WIDTH // LANES
    hg0 = RWKV_PROJ // LANES
    lora_blk = LORA_COL // (2 * LANES)

    if pipelined:
        def produced(c):
            return jnp.minimum(c, nc - 1)

        def consumed(c):
            return jnp.maximum(c - 1, 0)
    else:
        produced = consumed = lambda c: c

    def tok(col0, chunk=produced):
        return pl.BlockSpec((bb, C, LANES), lambda b, h, c: (b, chunk(c), col0 + h))

    def vec(col0):
        return pl.BlockSpec((1, LANES), lambda b, h, c: (0, col0 + h))

    in_specs = [
        tok(0), tok(nb), tok(2 * nb),
        pl.BlockSpec((bb, C, 2 * LANES), lambda b, h, c: (b, produced(c), lora_blk)),
        vec(0), vec(0), vec(0), vec(0), vec(0), vec(0), vec(0),
        pl.BlockSpec((LANES, LANES), lambda b, h, c: (0, h)),
        pl.BlockSpec((LANES, LANES), lambda b, h, c: (0, h)),
        pl.BlockSpec((GATE_RANK, LANES), lambda b, h, c: (0, h)),
        pl.BlockSpec((bb, 2, RWKV_HEAD, RWKV_HEAD), lambda b, h, c: (b, h, 0, 0)),
        tok(hg0), tok(hg0 + nb), tok(hg0 + 2 * nb), tok(hg0 + 3 * nb, consumed),
        vec(0),
        pl.BlockSpec((bb, 1, HGRN_HEAD, HGRN_HEAD), lambda b, h, c: (b, h, 0, 0)),
    ]
    assert len(in_specs) == N_MIXER_IN
    out_specs = [
        pl.BlockSpec((bb, C, LANES), lambda b, h, c: (b, consumed(c), h)),
        pl.BlockSpec((bb, 2, RWKV_HEAD, RWKV_HEAD), lambda b, h, c: (b, h, 0, 0)),
        pl.BlockSpec((bb, C, LANES), lambda b, h, c: (b, consumed(c), h)),
        pl.BlockSpec((bb, 1, HGRN_HEAD, HGRN_HEAD), lambda b, h, c: (b, h, 0, 0)),
    ]
    out_shape = [jax.ShapeDtypeStruct((B, T, RWKV_WIDTH), F32),
                 jax.ShapeDtypeStruct((B, RWKV_HEADS, RWKV_HEAD, RWKV_HEAD), F32),
                 jax.ShapeDtypeStruct((B, T, HGRN_WIDTH), F32),
                 jax.ShapeDtypeStruct((B, HGRN_HEADS, HGRN_HEAD, HGRN_HEAD), F32)]
    scratch = [pltpu.VMEM((bb, LANES, LANES), F32), pltpu.VMEM((bb, HGRN_HEAD, HGRN_HEAD), F32)]
    assert len(scratch) == N_MIXER_STATE_SCRATCH
    if pipelined:
        scratch += [pltpu.VMEM((2, bb, rows * C if rows else 1, LANES), dt)
                    for _, rows, dt in RWKV_STASH + HGRN_STASH]
    return pl.pallas_call(
        functools.partial(_mixer_kernel, bb=bb, C=C, n_groups=n_groups, pipelined=pipelined),
        grid=grid, in_specs=in_specs, out_specs=out_specs, out_shape=out_shape,
        scratch_shapes=scratch,
        compiler_params=pltpu.CompilerParams(
            dimension_semantics=("arbitrary", "arbitrary", "arbitrary"),
            vmem_limit_bytes=VMEM_LIMIT),
        name="mixer",
    )(proj3, proj3, proj3, proj3,
      p["w0"], p["a0"], p["k_k"], p["k_a"], p["r_k"], p["ln_x_w"], p["ln_x_b"],
      p["w1u_pad"], p["a1u_pad"], p["g1u"], s_rwkv,
      proj3, proj3, proj3, proj3, p["hg_norm_w"], s_hgrn)


def _layer_norm(x, g, b):
    mu = jnp.mean(x, axis=-1, keepdims=True)
    d = x - mu
    var = jnp.mean(d * d, axis=-1, keepdims=True)
    return d * lax.rsqrt(var + LN_EPS) * g + b


def _post_kernel(x_ref, orw_ref, ohg_ref, wo1_ref, wo2_ref, g1_ref, b1_ref, wup_ref, wdn_ref,
                 g2_ref, b2_ref, y_ref):
    tm = x_ref.shape[0]
    rows = [slice(k * tm // POST_SPLIT, (k + 1) * tm // POST_SPLIT) for k in range(POST_SPLIT)]

    def mix_stage(r):
        return (jnp.dot(orw_ref[r, :].astype(BF16), wo1_ref[...], preferred_element_type=F32)
                + jnp.dot(ohg_ref[r, :].astype(BF16), wo2_ref[...], preferred_element_type=F32))

    def up_stage(r, mix):
        h1 = _layer_norm(ALPHA * x_ref[r, :] + mix, g1_ref[...], b1_ref[...])
        return h1, jnp.dot(h1.astype(BF16), wup_ref[...], preferred_element_type=F32)

    def down_stage(up):
        up = jnp.square(jnp.maximum(up, 0.0))
        return jnp.dot(up.astype(BF16), wdn_ref[...], preferred_element_type=F32)

    def out_stage(r, h1, ff):
        y_ref[r, :] = _layer_norm(ALPHA * h1 + ff, g2_ref[...], b2_ref[...])

    mix, h1, up, ff = {}, {}, {}, {}
    for step in range(POST_SPLIT + 3):
        if step < POST_SPLIT:
            mix[step] = mix_stage(rows[step])
        k = step - 1
        if 0 <= k < POST_SPLIT:
            h1[k], up[k] = up_stage(rows[k], mix.pop(k))
        k = step - 2
        if 0 <= k < POST_SPLIT:
            ff[k] = down_stage(up.pop(k))
        k = step - 3
        if 0 <= k < POST_SPLIT:
            out_stage(rows[k], h1.pop(k), ff.pop(k))


def _post(x2, orw2, ohg2, p, tm):
    n = x2.shape[0]

    def const(shape):
        return pl.BlockSpec(shape, lambda i: (0, 0), pipeline_mode=pl.Buffered(1))

    return pl.pallas_call(
        _post_kernel,
        grid=(n // tm,),
        in_specs=[pl.BlockSpec((tm, D_MODEL), lambda i: (i, 0)),
                  pl.BlockSpec((tm, RWKV_WIDTH), lambda i: (i, 0)),
                  pl.BlockSpec((tm, HGRN_WIDTH), lambda i: (i, 0)),
                  const((RWKV_WIDTH, D_MODEL)), const((HGRN_WIDTH, D_MODEL)),
                  const((1, D_MODEL)), const((1, D_MODEL)),
                  const((D_MODEL, D_FF)), const((D_FF, D_MODEL)),
                  const((1, D_MODEL)), const((1, D_MODEL))],
        out_specs=pl.BlockSpec((tm, D_MODEL), lambda i: (i, 0)),
        out_shape=jax.ShapeDtypeStruct((n, D_MODEL), F32),
        compiler_params=pltpu.CompilerParams(dimension_semantics=("arbitrary",),
                                             vmem_limit_bytes=VMEM_LIMIT),
        name="post",
    )(x2, orw2, ohg2, p["wo_rw"], p["wo_hg"], p["ln1_g"], p["ln1_b"], p["w_up"], p["w_down"],
      p["ln2_g"], p["ln2_b"])


def _prep_params(w_in, shift_mu, w0, w1u, a0, a1u, g1u, k_k, k_a, r_k, ln_x_w, ln_x_b, lb_logits,
                 hg_norm_w, w_out, ln1_g, ln1_b, w_up, w_down, ln2_g, ln2_b):
    zw = jnp.zeros((LANES - DECAY_RANK, RWKV_WIDTH), F32)
    za = jnp.zeros((LANES - AICL_RANK, RWKV_WIDTH), F32)
    w_out_bf = w_out[0].astype(BF16)
    return {
        "w_in": w_in[0],
        "shift_mu": shift_mu[0].reshape(1, RWKV_PROJ),
        "w0": w0[0].reshape(1, RWKV_WIDTH), "a0": a0[0].reshape(1, RWKV_WIDTH),
        "k_k": k_k[0].reshape(1, RWKV_WIDTH), "k_a": k_a[0].reshape(1, RWKV_WIDTH),
        "r_k": r_k[0].reshape(1, RWKV_WIDTH),
        "ln_x_w": ln_x_w[0].reshape(1, RWKV_WIDTH), "ln_x_b": ln_x_b[0].reshape(1, RWKV_WIDTH),
        "w1u_pad": jnp.concatenate([w1u[0], zw], axis=0),
        "a1u_pad": jnp.concatenate([za, a1u[0]], axis=0),
        "g1u": g1u[0],
        "lb_logits": lb_logits.astype(F32),
        "hg_norm_w": hg_norm_w[0].reshape(1, HGRN_WIDTH),
        "wo_rw": w_out_bf[:RWKV_WIDTH], "wo_hg": w_out_bf[RWKV_WIDTH:],
        "ln1_g": ln1_g[0].reshape(1, D_MODEL), "ln1_b": ln1_b[0].reshape(1, D_MODEL),
        "w_up": w_up[0].astype(BF16), "w_down": w_down[0].astype(BF16),
        "ln2_g": ln2_g[0].reshape(1, D_MODEL), "ln2_b": ln2_b[0].reshape(1, D_MODEL),
    }


def _run_group(x, s_rwkv, s_hgrn, s_shift, p, *, tm, bb, chunk, n_groups, pipelined):
    B, T, _ = x.shape
    proj3, last = _proj(x, s_shift.reshape(B, 1, RWKV_PROJ), p, tm)
    o_rw, st_rw, o_hg, st_hg = _mixer(proj3, s_rwkv, s_hgrn, p, bb, chunk, n_groups, pipelined)
    y2 = _post(x.reshape(B * T, D_MODEL), o_rw.reshape(B * T, RWKV_WIDTH), o_hg.reshape(B * T, HGRN_WIDTH),
               p, tm)
    return y2.reshape(B, T, D_MODEL), st_rw[None], st_hg[None], last.reshape(1, B, RWKV_PROJ)


PROMPT_CFG = dict(tm=512, bb=8, chunk=64, n_groups=1, pipelined=True)
SAMPLE_CFG = dict(tm=512, bb=32, chunk=8, n_groups=1, pipelined=False)


def kernel(x_prompt, x_sample, state_rwkv, state_hgrn, state_shift, w_in, shift_mu, w0, w1u, a0, a1u, g1u, k_k, k_a, r_k, ln_x_w, ln_x_b, lb_logits, hg_norm_w, w_out, ln1_g, ln1_b, w_up, w_down, ln2_g, ln2_b):
    assert w_in.shape[0] == DEPTH
    p = _prep_params(w_in, shift_mu, w0, w1u, a0, a1u, g1u, k_k, k_a, r_k, ln_x_w, ln_x_b, lb_logits,
                     hg_norm_w, w_out, ln1_g, ln1_b, w_up, w_down, ln2_g, ln2_b)
    bp = x_prompt.shape[0]
    z_rw = jnp.zeros((bp, RWKV_HEADS, RWKV_HEAD, RWKV_HEAD), F32)
    z_hg = jnp.zeros((bp, HGRN_HEADS, HGRN_HEAD, HGRN_HEAD), F32)
    z_sh = jnp.zeros((bp, RWKV_PROJ), F32)
    y_p, rw_p, hg_p, sh_p = _run_group(x_prompt, z_rw, z_hg, z_sh, p, **PROMPT_CFG)
    y_s, rw_s, hg_s, sh_s = _run_group(x_sample, state_rwkv[0].astype(F32), state_hgrn[0].astype(F32),
                                       state_shift[0].astype(F32), p, **SAMPLE_CFG)
    return (y_p, y_s, rw_p, rw_s, hg_p, hg_s, sh_p, sh_s)
```

```python
import functools
import math

import jax
import jax.numpy as jnp
from jax import lax
from jax.experimental import pallas as pl
from jax.experimental.pallas import tpu as pltpu

F32 = jnp.float32
BF16 = jnp.bfloat16

D_MODEL = 1024
RWKV_WIDTH = 512
RWKV_HEAD = 64
RWKV_HEADS = 8
HGRN_WIDTH = 512
HGRN_HEAD = 128
HGRN_HEADS = 4
DECAY_RANK = 64
AICL_RANK = 64
GATE_RANK = 128
RWKV_PROJ = 3 * RWKV_WIDTH + DECAY_RANK + AICL_RANK + GATE_RANK
HGRN_PROJ = 4 * HGRN_WIDTH
PROJ = RWKV_PROJ + HGRN_PROJ
D_FF = 4 * D_MODEL
DEPTH = 1
ALPHA = (2.0 * DEPTH) ** 0.25
LN_EPS = 1e-5
GN_EPS = RWKV_HEAD * 1e-5
RMS_EPS = 1e-6
KK_NORM_FLOOR = 1e-12
DECAY_SCALE = math.exp(-0.5)

LANES = 128
SUBLANES = 8
PAIRS = RWKV_HEADS // 2
LORA_COL = 3 * RWKV_WIDTH
VMEM_LIMIT = 56 * 1024 * 1024
PROJ_SECTION = 256
POST_SPLIT = 2

NN = ((1,), (0,))
NT = ((1,), (1,))
TN = ((0,), (0,))

RWKV_STASH = (("lhs", 4, BF16), ("rhs", 2, BF16), ("bt_hat", 2, BF16), ("atrt", 2, BF16), ("vb", 1, BF16),
              ("bkh", 2, BF16), ("dec", 0, F32), ("bonus", 1, F32), ("gate", 1, F32))
HGRN_STASH = (("qs", 1, BF16), ("ks", 1, BF16), ("iv", 1, BF16), ("q", 1, F32), ("kin", 1, F32),
              ("bc", 1, F32), ("dec", 0, F32))


def _dot(a, b, dims):
    return lax.dot_general(a, b, (dims, ((), ())), preferred_element_type=F32)


def _bf(x):
    return x.astype(BF16)


def _cumsum_rows(tri, x):
    p0 = _bf(x)
    r1 = x - p0.astype(F32)
    p1 = _bf(r1)
    p2 = _bf(r1 - p1.astype(F32))
    return _dot(tri, p0, NN) + _dot(tri, p1, NN) + _dot(tri, p2, NN)


def _sigmoid(x):
    return 0.5 * jnp.tanh(0.5 * x) + 0.5


def _interleave(tasks):
    gens = list(tasks)
    nxt = [next(g, None) for g in gens]
    spent = {"M": 0.0, "V": 0.0}
    turn = 0
    while any(k is not None for k in nxt):
        ready = {kind: [j for j, k in enumerate(nxt) if k is not None and k[0] == kind] for kind in spent}
        want = "M" if (ready["M"] and (spent["M"] <= spent["V"] or not ready["V"])) else "V"
        if want == "M":
            cands = ready["M"]
            t = cands[turn % len(cands)]
            turn += 1
        else:
            t = ready["V"][0]
        spent[want] += nxt[t][1]
        nxt[t] = next(gens[t], None)


def _chain(*gens):
    for g in gens:
        yield from g


class _ValueStash:
    def __init__(self):
        self.vals = {}

    def put(self, name, val):
        self.vals[name] = val

    def get(self, name, j=None):
        return self.vals[name] if j is None else self.vals[name][j]


class _RefStash:
    def __init__(self, refs, slot, lo, n):
        self.refs, self.slot, self.lo, self.n = refs, slot, lo, n

    def put(self, name, val):
        self.refs[name][self.slot, self.lo:self.lo + self.n] = val

    def get(self, name, j=None):
        if j is None:
            return self.refs[name][self.slot, self.lo:self.lo + self.n]
        return self.refs[name][self.slot, self.lo + j]


def _proj_kernel(x_ref, w_ref, first_ref, mu_ref, lbl_ref, o_ref, last_ref, wbf_scr, carry_scr, *, tiles_per_seq):
    nb, tt, _ = x_ref.shape
    tile = pl.program_id(0)

    @pl.when(tile == 0)
    def _cast_weights():
        wbf_scr[...] = w_ref[...].astype(BF16)

    xb = x_ref[...].reshape(nb * tt, D_MODEL).astype(BF16)

    def cols(lo, width):
        return jnp.dot(xb, wbf_scr[:, lo:lo + width], preferred_element_type=F32)

    first = first_ref[...]
    if tiles_per_seq > 1:
        first = jnp.where(lax.rem(tile, tiles_per_seq) == 0, first, carry_scr[...])
    last_rows = []
    for lo in range(0, RWKV_PROJ, PROJ_SECTION):
        width = PROJ_SECTION
        p = cols(lo, width)
        p3 = p.reshape(nb, tt, width)
        row = lax.broadcasted_iota(jnp.int32, p3.shape, 1)
        prev = jnp.where(row == 0, first[:, :, lo:lo + width], pltpu.roll(p, 1, 0).reshape(p3.shape))
        o_ref[:, :, lo:lo + width] = p3 + mu_ref[:, lo:lo + width] * (prev - p3)
        last_rows.append(p3[:, tt - 1:tt, :])
    last = jnp.concatenate(last_rows, axis=-1)
    last_ref[...] = last
    if tiles_per_seq > 1:
        carry_scr[...] = last

    logits = lbl_ref[...]
    ex = jnp.exp(logits - jnp.max(logits, axis=0, keepdims=True))
    lb = ex[0:1] / jnp.sum(ex, axis=0, keepdims=True)
    for part, act in enumerate(("silu", "log_gate", "copy", "silu")):
        for lo in range(0, HGRN_WIDTH, PROJ_SECTION):
            col = RWKV_PROJ + part * HGRN_WIDTH + lo
            h = cols(col, PROJ_SECTION)
            if act == "silu":
                h = h * _sigmoid(h)
            elif act == "log_gate":
                lb_s = lb[:, lo:lo + PROJ_SECTION]
                h = jnp.log(lb_s + (1.0 - lb_s) * _sigmoid(h))
            o_ref[:, :, col:col + PROJ_SECTION] = h.reshape(nb, tt, PROJ_SECTION)


def _proj(x3, s_shift3, p, tm):
    B, T, _ = x3.shape
    tt = min(T, tm)
    nb = tm // tt
    tiles_per_seq = T // tt

    def tile_map(i):
        return (i // tiles_per_seq, lax.rem(i, tiles_per_seq), 0)

    def seq_map(i):
        return (i // tiles_per_seq, 0, 0)

    def const(shape):
        return pl.BlockSpec(shape, lambda i: (0, 0), pipeline_mode=pl.Buffered(1))

    return pl.pallas_call(
        functools.partial(_proj_kernel, tiles_per_seq=tiles_per_seq),
        grid=(B * T // tm,),
        in_specs=[pl.BlockSpec((nb, tt, D_MODEL), tile_map),
                  const((D_MODEL, PROJ)),
                  pl.BlockSpec((nb, 1, RWKV_PROJ), seq_map),
                  const((1, RWKV_PROJ)), const((DEPTH + 1, HGRN_WIDTH))],
        out_specs=[pl.BlockSpec((nb, tt, PROJ), tile_map),
                   pl.BlockSpec((nb, 1, RWKV_PROJ), seq_map)],
        out_shape=[jax.ShapeDtypeStruct((B, T, PROJ), F32),
                   jax.ShapeDtypeStruct((B, 1, RWKV_PROJ), F32)],
        scratch_shapes=[pltpu.VMEM((D_MODEL, PROJ), BF16), pltpu.VMEM((nb, 1, RWKV_PROJ), F32)],
        compiler_params=pltpu.CompilerParams(dimension_semantics=("arbitrary",),
                                             vmem_limit_bytes=VMEM_LIMIT),
        name="proj",
    )(x3, p["w_in"], s_shift3, p["shift_mu"], p["lb_logits"])


def _head_helpers(C):
    lane = lax.broadcasted_iota(jnp.int32, (C, LANES), 1)
    m0 = lane < RWKV_HEAD

    def head_sum(x):
        s0 = jnp.sum(jnp.where(m0, x, 0.0), axis=-1, keepdims=True)
        s1 = jnp.sum(jnp.where(m0, 0.0, x), axis=-1, keepdims=True)
        return jnp.where(m0, s0, s1)

    def stack_heads(x):
        return jnp.concatenate([jnp.where(m0, x, 0.0), jnp.where(m0, 0.0, x)], axis=-2)

    def merge_heads(x):
        return jnp.where(m0, x[:C], x[C:])

    return head_sum, stack_heads, merge_heads


def _rwkv_produce(stash, lo, n, C, r_ref, k_ref, v_ref, l_ref,
                  w0_ref, a0_ref, kk_ref, ka_ref, rk_ref, w1_ref, a1_ref, g1_ref):
    ids = range(n)
    grp = slice(lo, lo + n)
    head_sum, stack_heads, _ = _head_helpers(C)
    rowc = lax.broadcasted_iota(jnp.int32, (C, C), 0)
    colc = lax.broadcasted_iota(jnp.int32, (C, C), 1)
    tri = jnp.where(rowc >= colc, 1.0, 0.0).astype(BF16)

    w1 = _bf(w1_ref[...])
    a1 = _bf(a1_ref[...])
    g1 = _bf(g1_ref[...])

    yield ("V", 30.0 * n)
    xl = l_ref[grp].reshape(n * C, 2 * LANES)
    xl_lo = xl[:, :LANES]
    dw = jnp.dot(_bf(jnp.tanh(xl_lo)), w1, preferred_element_type=F32)
    da = jnp.dot(_bf(xl_lo), a1, preferred_element_type=F32)
    gate = jnp.dot(_bf(_sigmoid(xl[:, LANES:])), g1, preferred_element_type=F32)
    stash.put("gate", gate.reshape(n, C, LANES))

    yield ("V", 60.0 * n)
    lw = (-DECAY_SCALE * _sigmoid(w0_ref[...] + dw)).reshape(n, C, LANES)
    a_lr = _sigmoid(a0_ref[...] + da).reshape(n, C, LANES)
    xk = k_ref[grp]
    kk = xk * kk_ref[...]
    kk = kk * jnp.minimum(lax.rsqrt(head_sum(kk * kk)), 1.0 / KK_NORM_FLOOR)
    k2 = xk * (1.0 + (a_lr - 1.0) * ka_ref[...])
    b = kk * a_lr
    cum = jnp.stack([_cumsum_rows(tri, lw[j]) for j in ids])

    yield ("V", 80.0 * n)
    r = r_ref[grp]
    v = v_ref[grp]
    cum_last = cum[:, C - 1:C, :]
    e_in = jnp.exp(-cum)
    e_out = jnp.exp(cum_last - cum)
    at = -kk * jnp.exp(cum - lw)
    rt = r * jnp.exp(cum)
    bt = b * e_in
    kt = k2 * e_in
    stash.put("lhs", _bf(jnp.concatenate([stack_heads(at), stack_heads(rt)], axis=1)))
    stash.put("rhs", _bf(jnp.concatenate([bt, kt], axis=1)))
    stash.put("bt_hat", _bf(stack_heads(bt)))
    stash.put("atrt", _bf(jnp.concatenate([at, rt], axis=1)))
    stash.put("vb", _bf(v))
    stash.put("bkh", _bf(jnp.concatenate([b * e_out, k2 * e_out], axis=1)))
    stash.put("dec", jnp.exp(cum_last))
    stash.put("bonus", head_sum(r * k2 * rk_ref[...]) * v)


def _rwkv_consume(stash, lo, n, C, dot_cost, lnw_ref, lnb_ref, o_ref, s_scr):
    H = RWKV_HEAD
    ids = range(n)
    grp = slice(lo, lo + n)
    head_sum, stack_heads, merge_heads = _head_helpers(C)
    r2 = lax.broadcasted_iota(jnp.int32, (2 * C, 2 * C), 0)
    c2 = lax.broadcasted_iota(jnp.int32, (2 * C, 2 * C), 1)
    t2 = jnp.bitwise_and(r2, C - 1)
    s2 = jnp.bitwise_and(c2, C - 1)
    strict2 = t2 > s2
    incl2 = t2 >= s2
    strict_bd = jnp.logical_and(strict2, (r2 >= C) == (c2 >= C))
    eye2 = jnp.where(r2 == c2, 1.0, 0.0).astype(F32)
    bl_r = lax.broadcasted_iota(jnp.int32, (LANES, LANES), 0) < H
    bl_c = lax.broadcasted_iota(jnp.int32, (LANES, LANES), 1) < H
    blockdiag = bl_r == bl_c
    zeros_c = jnp.zeros((C, LANES), BF16)

    yield ("M", 2 * dot_cost)
    g_a, g_r, pw, inv = [], [], [], []
    for j in ids:
        lhs = stash.get("lhs", j)
        g = _dot(lhs, stash.get("rhs", j), NT)
        g_a.append(_bf(jnp.where(strict2, g[:2 * C], 0.0)))
        g_r.append(_bf(jnp.where(incl2, g[2 * C:], 0.0)))
        n_bd = jnp.where(strict_bd, _dot(lhs[:2 * C], stash.get("bt_hat", j), NT), 0.0)
        inv.append(eye2 + n_bd)
        pw.append(_bf(n_bd))
    yield ("M", 2 * dot_cost)
    S = [s_scr[lo + j] for j in ids]
    pq = [_dot(stash.get("atrt", j), _bf(S[j]), NT) for j in ids]
    pw = [_bf(_dot(pw[j], pw[j], NN)) for j in ids]
    span = 4
    while span < C:
        yield ("M", 2 * dot_cost)
        nxt = [_bf(_dot(pw[j], pw[j], NN)) for j in ids]
        inv = [inv[j] + _dot(_bf(inv[j]), pw[j], NN) for j in ids]
        pw = nxt
        span *= 2
    yield ("M", 2 * dot_cost)
    inv = [inv[j] + _dot(_bf(inv[j]), pw[j], NN) for j in ids]
    vb = [stash.get("vb", j) for j in ids]
    w_hat = [_bf(stack_heads(pq[j][:C] + merge_heads(_dot(g_a[j], jnp.concatenate([zeros_c, vb[j]], axis=0), NN))))
             for j in ids]
    yield ("M", dot_cost)
    z = []
    for j in ids:
        u2 = _dot(_bf(inv[j]), w_hat[j], NN)
        z.append(jnp.concatenate([_bf(u2[:C] + u2[C:]), vb[j]], axis=0))
    yield ("M", 2 * dot_cost)
    y = [pq[j][C:] + merge_heads(_dot(g_r[j], z[j], NN)) for j in ids]
    for j in ids:
        upd = _dot(z[j], stash.get("bkh", j), TN)
        s_scr[lo + j] = S[j] * stash.get("dec", j) + jnp.where(blockdiag, upd, 0.0)

    yield ("V", 40.0 * n)
    y = jnp.stack(y)
    mu = head_sum(y) * (1.0 / H)
    dy = y - mu
    var = head_sum(dy * dy) * (1.0 / H)
    yn = dy * lax.rsqrt(var + GN_EPS) * lnw_ref[...] + lnb_ref[...]
    o_ref[grp] = (yn + stash.get("bonus")) * stash.get("gate")


def _mid_rows(x, h):
    R = x.shape[0]
    row = lax.broadcasted_iota(jnp.int32, x.shape, 0)
    if 2 * h >= SUBLANES:
        return jnp.concatenate(
            [jnp.broadcast_to(x[b * 2 * h + h - 1:b * 2 * h + h], (2 * h, LANES)) for b in range(R // (2 * h))],
            axis=0)
    if h == 1:
        return jnp.where(jnp.bitwise_and(row, 1) == 1, pltpu.roll(x, 1, 0), x)
    picks = []
    for half in range(SUBLANES // (2 * h)):
        r0 = half * 2 * h + h - 1
        picks.append(jnp.concatenate(
            [jnp.broadcast_to(x[g * SUBLANES + r0:g * SUBLANES + r0 + 1], (SUBLANES, LANES))
             for g in range(R // SUBLANES)], axis=0))
    out = picks[-1]
    sub = jnp.bitwise_and(row, SUBLANES - 1)
    for half in range(len(picks) - 2, -1, -1):
        out = jnp.where(sub < (half + 1) * 2 * h, picks[half], out)
    return out


def _hgrn_produce(stash, lo, n, C, q_ref, lf_ref, i_ref):
    ids = range(n)
    grp = slice(lo, lo + n)
    t_i = lax.broadcasted_iota(jnp.int32, (C, C), 0)
    s_i = lax.broadcasted_iota(jnp.int32, (C, C), 1)
    tri = jnp.where(t_i >= s_i, 1.0, 0.0).astype(BF16)

    yield ("V", 20.0 * n)
    q = q_ref[grp]
    lf = lf_ref[grp]
    kin = 1.0 - jnp.exp(lf)
    bc = jnp.stack([_cumsum_rows(tri, lf[j]) for j in ids])

    yield ("V", 30.0 * n)
    b_last = bc[:, C - 1:C, :]
    stash.put("qs", _bf(q * jnp.exp(bc)))
    stash.put("ks", _bf(kin * jnp.exp(b_last - bc)))
    stash.put("iv", _bf(i_ref[grp]))
    stash.put("q", q)
    stash.put("kin", kin)
    stash.put("bc", bc)
    stash.put("dec", jnp.exp(b_last))


def _hgrn_consume(stash, lo, n, C, dot_cost, g_ref, hgw_ref, o_ref, s_scr):
    ids = range(n)
    grp = slice(lo, lo + n)
    t_i = lax.broadcasted_iota(jnp.int32, (C, C), 0)
    s_i = lax.broadcasted_iota(jnp.int32, (C, C), 1)
    halves = []
    h = C // 2
    while h >= 1:
        halves.append(h)
        h //= 2

    def level_mask(h):
        same = jnp.bitwise_and(t_i, -2 * h) == jnp.bitwise_and(s_i, -2 * h)
        return jnp.logical_and(same, jnp.logical_and(jnp.bitwise_and(t_i, h) != 0, jnp.bitwise_and(s_i, h) == 0))

    def seq(x, j):
        return _bf(x[j * C:(j + 1) * C])

    yield ("M", dot_cost)
    S = [s_scr[lo + j] for j in ids]
    o_state = [_dot(stash.get("qs", j), _bf(S[j]), NT) for j in ids]
    yield ("M", dot_cost)
    iv = [stash.get("iv", j) for j in ids]
    for j in ids:
        s_scr[lo + j] = S[j] * stash.get("dec", j) + _dot(iv[j], stash.get("ks", j), TN)
    yield ("M", dot_cost)
    q = stash.get("q").reshape(n * C, LANES)
    kin = stash.get("kin").reshape(n * C, LANES)
    bc = stash.get("bc").reshape(n * C, LANES)
    att = [jnp.where(t_i == s_i, _dot(seq(q, j), seq(kin, j), NT), 0.0) for j in ids]
    for h in halves:
        yield ("V", 25.0 * n)
        dmid = jnp.exp(-jnp.abs(bc - _mid_rows(bc, h)))
        ql = q * dmid
        kl = kin * dmid
        yield ("M", dot_cost)
        msk = level_mask(h)
        att = [jnp.where(msk, _dot(seq(ql, j), seq(kl, j), NT), att[j]) for j in ids]
    yield ("M", dot_cost)
    o_intra = [_dot(_bf(att[j]), iv[j], NN) for j in ids]

    yield ("V", 25.0 * n)
    o = jnp.stack([o_intra[j] + o_state[j] for j in ids])
    ms = jnp.mean(o * o, axis=-1, keepdims=True)
    o_ref[grp] = o * lax.rsqrt(ms + RMS_EPS) * hgw_ref[...] * g_ref[grp]


MXU_STAGE_COST = {True: (22.0, 15.0), False: (30.0, 20.0)}
N_MIXER_IN = 21
N_MIXER_OUT = 4
N_MIXER_STATE_SCRATCH = 2


def _mixer_kernel(*refs, bb, C, n_groups, pipelined):
    (r_ref, k_ref, v_ref, l_ref, w0_ref, a0_ref, kk_ref, ka_ref, rk_ref,
     lnw_ref, lnb_ref, w1_ref, a1_ref, g1_ref, srw0_ref,
     q_ref, lf_ref, i_ref, g_ref, hgw_ref, shg0_ref) = refs[:N_MIXER_IN]
    orw_ref, strw_ref, ohg_ref, sthg_ref = refs[N_MIXER_IN:N_MIXER_IN + N_MIXER_OUT]
    scr = refs[N_MIXER_IN + N_MIXER_OUT:]
    srw_scr, shg_scr = scr[:N_MIXER_STATE_SCRATCH]
    stash_refs = scr[N_MIXER_STATE_SCRATCH:]
    if pipelined:
        rw_refs = {name: stash_refs[k] for k, (name, _, _) in enumerate(RWKV_STASH)}
        hg_refs = {name: stash_refs[len(RWKV_STASH) + k] for k, (name, _, _) in enumerate(HGRN_STASH)}

    c = pl.program_id(2)
    last = pl.num_programs(2) - 1
    H = RWKV_HEAD

    @pl.when(c == 0)
    def _init():
        z = jnp.zeros((H, H), F32)
        for i in range(bb):
            top = jnp.concatenate([srw0_ref[i, 0], z], axis=1)
            bot = jnp.concatenate([z, srw0_ref[i, 1]], axis=1)
            srw_scr[i] = jnp.concatenate([top, bot], axis=0)
            shg_scr[i] = shg0_ref[i, 0].T
        if pipelined:
            for table in (rw_refs, hg_refs):
                for name, ref in table.items():
                    fill = jnp.ones if name == "dec" else jnp.zeros
                    ref[1] = fill(ref.shape[1:], ref.dtype)

    per = bb // n_groups
    groups = [(gidx * per, per) for gidx in range(n_groups)]

    def rwkv_produce(stash, lo, n):
        return _rwkv_produce(stash, lo, n, C, r_ref, k_ref, v_ref, l_ref,
                             w0_ref, a0_ref, kk_ref, ka_ref, rk_ref, w1_ref, a1_ref, g1_ref)

    def rwkv_consume(stash, lo, n):
        return _rwkv_consume(stash, lo, n, C, MXU_STAGE_COST[pipelined][0] * n, lnw_ref, lnb_ref, orw_ref, srw_scr)

    def hgrn_produce(stash, lo, n):
        return _hgrn_produce(stash, lo, n, C, q_ref, lf_ref, i_ref)

    def hgrn_consume(stash, lo, n):
        return _hgrn_consume(stash, lo, n, C, MXU_STAGE_COST[pipelined][1] * n, g_ref, hgw_ref, ohg_ref, shg_scr)

    if not pipelined:
        tasks = []
        for lo, n in groups:
            rw, hg = _ValueStash(), _ValueStash()
            tasks.append(_chain(rwkv_produce(rw, lo, n), rwkv_consume(rw, lo, n)))
            tasks.append(_chain(hgrn_produce(hg, lo, n), hgrn_consume(hg, lo, n)))
        _interleave(tasks)
    else:
        def step(write_slot):
            read_slot = 1 - write_slot
            tasks = []
            for lo, n in groups:
                tasks.append(rwkv_consume(_RefStash(rw_refs, read_slot, lo, n), lo, n))
                tasks.append(hgrn_consume(_RefStash(hg_refs, read_slot, lo, n), lo, n))
            for lo, n in groups:
                tasks.append(rwkv_produce(_RefStash(rw_refs, write_slot, lo, n), lo, n))
                tasks.append(hgrn_produce(_RefStash(hg_refs, write_slot, lo, n), lo, n))
            _interleave(tasks)

        parity = lax.rem(c, 2)
        pl.when(parity == 0)(functools.partial(step, 0))
        pl.when(parity == 1)(functools.partial(step, 1))

    @pl.when(c == last)
    def _fin():
        for i in range(bb):
            s_fin = srw_scr[i]
            strw_ref[i, 0] = s_fin[:H, :H]
            strw_ref[i, 1] = s_fin[H:, H:]
            sthg_ref[i, 0] = shg_scr[i].T


def _mixer(proj3, s_rwkv, s_hgrn, p, bb, C, n_groups, pipelined):
    B, T, _ = proj3.shape
    nc = T // C
    grid = (B // bb, PAIRS, nc + 1 if pipelined else nc)
    nb = RWKV_WIDTH // LANES
    hg0 = RWKV_PROJ // LANES
    lora_blk = LORA_COL // (2 * LANES)

    if pipelined:
        def produced(c):
            return jnp.minimum(c, nc - 1)

        def consumed(c):
            return jnp.maximum(c - 1, 0)
    else:
        produced = consumed = lambda c: c

    def tok(col0, chunk=produced):
        return pl.BlockSpec((bb, C, LANES), lambda b, h, c: (b, chunk(c), col0 + h))

    def vec(col0):
        return pl.BlockSpec((1, LANES), lambda b, h, c: (0, col0 + h))

    in_specs = [
        tok(0), tok(nb), tok(2 * nb),
        pl.BlockSpec((bb, C, 2 * LANES), lambda b, h, c: (b, produced(c), lora_blk)),
        vec(0), vec(0), vec(0), vec(0), vec(0), vec(0), vec(0),
        pl.BlockSpec((LANES, LANES), lambda b, h, c: (0, h)),
        pl.BlockSpec((LANES, LANES), lambda b, h, c: (0, h)),
        pl.BlockSpec((GATE_RANK, LANES), lambda b, h, c: (0, h)),
        pl.BlockSpec((bb, 2, RWKV_HEAD, RWKV_HEAD), lambda b, h, c: (b, h, 0, 0)),
        tok(hg0), tok(hg0 + nb), tok(hg0 + 2 * nb), tok(hg0 + 3 * nb, consumed),
        vec(0),
        pl.BlockSpec((bb, 1, HGRN_HEAD, HGRN_HEAD), lambda b, h, c: (b, h, 0, 0)),
    ]
    assert len(in_specs) == N_MIXER_IN
    out_specs = [
        pl.BlockSpec((bb, C, LANES), lambda b, h, c: (b, consumed(c), h)),
        pl.BlockSpec((bb, 2, RWKV_HEAD, RWKV_HEAD), lambda b, h, c: (b, h, 0, 0)),
        pl.BlockSpec((bb, C, LANES), lambda b, h, c: (b, consumed(c), h)),
        pl.BlockSpec((bb, 1, HGRN_HEAD, HGRN_HEAD), lambda b, h, c: (b, h, 0, 0)),
    ]
    out_shape = [jax.ShapeDtypeStruct((B, T, RWKV_WIDTH), F32),
                 jax.ShapeDtypeStruct((B, RWKV_HEADS, RWKV_HEAD, RWKV_HEAD), F32),
                 jax.ShapeDtypeStruct((B, T, HGRN_WIDTH), F32),
                 jax.ShapeDtypeStruct((B, HGRN_HEADS, HGRN_HEAD, HGRN_HEAD), F32)]
    scratch = [pltpu.VMEM((bb, LANES, LANES), F32), pltpu.VMEM((bb, HGRN_HEAD, HGRN_HEAD), F32)]
    assert len(scratch) == N_MIXER_STATE_SCRATCH
    if pipelined:
        scratch += [pltpu.VMEM((2, bb, rows * C if rows else 1, LANES), dt)
                    for _, rows, dt in RWKV_STASH + HGRN_STASH]
    return pl.pallas_call(
        functools.partial(_mixer_kernel, bb=bb, C=C, n_groups=n_groups, pipelined=pipelined),
        grid=grid, in_specs=in_specs, out_specs=out_specs, out_shape=out_shape,
        scratch_shapes=scratch,
        compiler_params=pltpu.CompilerParams(
            dimension_semantics=("arbitrary", "arbitrary", "arbitrary"),
            vmem_limit_bytes=VMEM_LIMIT),
        name="mixer",
    )(proj3, proj3, proj3, proj3,
      p["w0"], p["a0"], p["k_k"], p["k_a"], p["r_k"], p["ln_x_w"], p["ln_x_b"],
      p["w1u_pad"], p["a1u_pad"], p["g1u"], s_rwkv,
      proj3, proj3, proj3, proj3, p["hg_norm_w"], s_hgrn)


def _layer_norm(x, g, b):
    mu = jnp.mean(x, axis=-1, keepdims=True)
    d = x - mu
    var = jnp.mean(d * d, axis=-1, keepdims=True)
    return d * lax.rsqrt(var + LN_EPS) * g + b


def _post_kernel(x_ref, orw_ref, ohg_ref, wo1_ref, wo2_ref, g1_ref, b1_ref, wup_ref, wdn_ref,
                 g2_ref, b2_ref, y_ref):
    tm = x_ref.shape[0]
    rows = [slice(k * tm // POST_SPLIT, (k + 1) * tm // POST_SPLIT) for k in range(POST_SPLIT)]

    def mix_stage(r):
        return (jnp.dot(orw_ref[r, :].astype(BF16), wo1_ref[...], preferred_element_type=F32)
                + jnp.dot(ohg_ref[r, :].astype(BF16), wo2_ref[...], preferred_element_type=F32))

    def up_stage(r, mix):
        h1 = _layer_norm(ALPHA * x_ref[r, :] + mix, g1_ref[...], b1_ref[...])
        return h1, jnp.dot(h1.astype(BF16), wup_ref[...], preferred_element_type=F32)

    def down_stage(up):
        up = jnp.square(jnp.maximum(up, 0.0))
        return jnp.dot(up.astype(BF16), wdn_ref[...], preferred_element_type=F32)

    def out_stage(r, h1, ff):
        y_ref[r, :] = _layer_norm(ALPHA * h1 + ff, g2_ref[...], b2_ref[...])

    mix, h1, up, ff = {}, {}, {}, {}
    for step in range(POST_SPLIT + 3):
        if step < POST_SPLIT:
            mix[step] = mix_stage(rows[step])
        k = step - 1
        if 0 <= k < POST_SPLIT:
            h1[k], up[k] = up_stage(rows[k], mix.pop(k))
        k = step - 2
        if 0 <= k < POST_SPLIT:
            ff[k] = down_stage(up.pop(k))
        k = step - 3
        if 0 <= k < POST_SPLIT:
            out_stage(rows[k], h1.pop(k), ff.pop(k))


def _post(x2, orw2, ohg2, p, tm):
    n = x2.shape[0]

    def const(shape):
        return pl.BlockSpec(shape, lambda i: (0, 0), pipeline_mode=pl.Buffered(1))

    return pl.pallas_call(
        _post_kernel,
        grid=(n // tm,),
        in_specs=[pl.BlockSpec((tm, D_MODEL), lambda i: (i, 0)),
                  pl.BlockSpec((tm, RWKV_WIDTH), lambda i: (i, 0)),
                  pl.BlockSpec((tm, HGRN_WIDTH), lambda i: (i, 0)),
                  const((RWKV_WIDTH, D_MODEL)), const((HGRN_WIDTH, D_MODEL)),
                  const((1, D_MODEL)), const((1, D_MODEL)),
                  const((D_MODEL, D_FF)), const((D_FF, D_MODEL)),
                  const((1, D_MODEL)), const((1, D_MODEL))],
        out_specs=pl.BlockSpec((tm, D_MODEL), lambda i: (i, 0)),
        out_shape=jax.ShapeDtypeStruct((n, D_MODEL), F32),
        compiler_params=pltpu.CompilerParams(dimension_semantics=("arbitrary",),
                                             vmem_limit_bytes=VMEM_LIMIT),
        name="post",
    )(x2, orw2, ohg2, p["wo_rw"], p["wo_hg"], p["ln1_g"], p["ln1_b"], p["w_up"], p["w_down"],
      p["ln2_g"], p["ln2_b"])


def _prep_params(w_in, shift_mu, w0, w1u, a0, a1u, g1u, k_k, k_a, r_k, ln_x_w, ln_x_b, lb_logits,
                 hg_norm_w, w_out, ln1_g, ln1_b, w_up, w_down, ln2_g, ln2_b):
    zw = jnp.zeros((LANES - DECAY_RANK, RWKV_WIDTH), F32)
    za = jnp.zeros((LANES - AICL_RANK, RWKV_WIDTH), F32)
    w_out_bf = w_out[0].astype(BF16)
    return {
        "w_in": w_in[0],
        "shift_mu": shift_mu[0].reshape(1, RWKV_PROJ),
        "w0": w0[0].reshape(1, RWKV_WIDTH), "a0": a0[0].reshape(1, RWKV_WIDTH),
        "k_k": k_k[0].reshape(1, RWKV_WIDTH), "k_a": k_a[0].reshape(1, RWKV_WIDTH),
        "r_k": r_k[0].reshape(1, RWKV_WIDTH),
        "ln_x_w": ln_x_w[0].reshape(1, RWKV_WIDTH), "ln_x_b": ln_x_b[0].reshape(1, RWKV_WIDTH),
        "w1u_pad": jnp.concatenate([w1u[0], zw], axis=0),
        "a1u_pad": jnp.concatenate([za, a1u[0]], axis=0),
        "g1u": g1u[0],
        "lb_logits": lb_logits.astype(F32),
        "hg_norm_w": hg_norm_w[0].reshape(1, HGRN_WIDTH),
        "wo_rw": w_out_bf[:RWKV_WIDTH], "wo_hg": w_out_bf[RWKV_WIDTH:],
        "ln1_g": ln1_g[0].reshape(1, D_MODEL), "ln1_b": ln1_b[0].reshape(1, D_MODEL),
        "w_up": w_up[0].astype(BF16), "w_down": w_down[0].astype(BF16),
        "ln2_g": ln2_g[0].reshape(1, D_MODEL), "ln2_b": ln2_b[0].reshape(1, D_MODEL),
    }


def _run_group(x, s_rwkv, s_hgrn, s_shift, p, *, tm, bb, chunk, n_groups, pipelined):
    B, T, _ = x.shape
    proj3, last = _proj(x, s_shift.reshape(B, 1, RWKV_PROJ), p, tm)
    o_rw, st_rw, o_hg, st_hg = _mixer(proj3, s_rwkv, s_hgrn, p, bb, chunk, n_groups, pipelined)
    y2 = _post(x.reshape(B * T, D_MODEL), o_rw.reshape(B * T, RWKV_WIDTH), o_hg.reshape(B * T, HGRN_WIDTH),
               p, tm)
    return y2.reshape(B, T, D_MODEL), st_rw[None], st_hg[None], last.reshape(1, B, RWKV_PROJ)


PROMPT_CFG = dict(tm=512, bb=8, chunk=64, n_groups=1, pipelined=True)
SAMPLE_CFG = dict(tm=512, bb=32, chunk=8, n_groups=1, pipelined=False)


def kernel(x_prompt, x_sample, state_rwkv, state_hgrn, state_shift, w_in, shift_mu, w0, w1u, a0, a1u, g1u, k_k, k_a, r_k, ln_x_w, ln_x_b, lb_logits, hg_norm_w, w_out, ln1_g, ln1_b, w_up, w_down, ln2_g, ln2_b):
    assert w_in.shape[0] == DEPTH
    p = _prep_params(w_in, shift_mu, w0, w1u, a0, a1u, g1u, k_k, k_a, r_k, ln_x_w, ln_x_b, lb_logits,
                     hg_norm_w, w_out, ln1_g, ln1_b, w_up, w_down, ln2_g, ln2_b)
    bp = x_prompt.shape[0]
    z_rw = jnp.zeros((bp, RWKV_HEADS, RWKV_HEAD, RWKV_HEAD), F32)
    z_hg = jnp.zeros((bp, HGRN_HEADS, HGRN_HEAD, HGRN_HEAD), F32)
    z_sh = jnp.zeros((bp, RWKV_PROJ), F32)
    y_p, rw_p, hg_p, sh_p = _run_group(x_prompt, z_rw, z_hg, z_sh, p, **PROMPT_CFG)
    y_s, rw_s, hg_s, sh_s = _run_group(x_sample, state_rwkv[0].astype(F32), state_hgrn[0].astype(F32),
                                       state_shift[0].astype(F32), p, **SAMPLE_CFG)
    return (y_p, y_s, rw_p, rw_s, hg_p, hg_s, sh_p, sh_s)
```

```python
import functools
import math

import jax
import jax.numpy as jnp
from jax import lax
from jax.experimental import pallas as pl
from jax.experimental.pallas import tpu as pltpu

F32 = jnp.float32
BF16 = jnp.bfloat16

D_MODEL = 1024
RWKV_WIDTH = 512
RWKV_HEAD = 64
RWKV_HEADS = 8
HGRN_WIDTH = 512
HGRN_HEAD = 128
HGRN_HEADS = 4
DECAY_RANK = 64
AICL_RANK = 64
GATE_RANK = 128
RWKV_PROJ = 3 * RWKV_WIDTH + DECAY_RANK + AICL_RANK + GATE_RANK
HGRN_PROJ = 4 * HGRN_WIDTH
PROJ = RWKV_PROJ + HGRN_PROJ
D_FF = 4 * D_MODEL
DEPTH = 1
ALPHA = (2.0 * DEPTH) ** 0.25
LN_EPS = 1e-5
GN_EPS = RWKV_HEAD * 1e-5
RMS_EPS = 1e-6
KK_NORM_FLOOR = 1e-12
DECAY_SCALE = math.exp(-0.5)

LANES = 128
SUBLANES = 8
PAIRS = RWKV_HEADS // 2
LORA_COL = 3 * RWKV_WIDTH
VMEM_LIMIT = 56 * 1024 * 1024
PROJ_SECTION = 256
POST_SPLIT = 2

NN = ((1,), (0,))
NT = ((1,), (1,))
TN = ((0,), (0,))

RWKV_STASH = (("lhs", 4, BF16), ("rhs", 2, BF16), ("bt_hat", 2, BF16), ("atrt", 2, BF16), ("vb", 1, BF16),
              ("bkh", 2, BF16), ("dec", 0, F32), ("bonus", 1, F32), ("gate", 1, F32))
HGRN_STASH = (("qs", 1, BF16), ("ks", 1, BF16), ("iv", 1, BF16), ("q", 1, F32), ("kin", 1, F32),
              ("bc", 1, F32), ("dec", 0, F32))


def _dot(a, b, dims):
    return lax.dot_general(a, b, (dims, ((), ())), preferred_element_type=F32)


def _bf(x):
    return x.astype(BF16)


def _cumsum_rows(tri, x):
    p0 = _bf(x)
    r1 = x - p0.astype(F32)
    p1 = _bf(r1)
    p2 = _bf(r1 - p1.astype(F32))
    return _dot(tri, p0, NN) + _dot(tri, p1, NN) + _dot(tri, p2, NN)


def _sigmoid(x):
    return 0.5 * jnp.tanh(0.5 * x) + 0.5


def _interleave(tasks):
    gens = list(tasks)
    nxt = [next(g, None) for g in gens]
    spent = {"M": 0.0, "V": 0.0}
    turn = 0
    while any(k is not None for k in nxt):
        ready = {kind: [j for j, k in enumerate(nxt) if k is not None and k[0] == kind] for kind in spent}
        want = "M" if (ready["M"] and (spent["M"] <= spent["V"] or not ready["V"])) else "V"
        if want == "M":
            cands = ready["M"]
            t = cands[turn % len(cands)]
            turn += 1
        else:
            t = ready["V"][0]
        spent[want] += nxt[t][1]
        nxt[t] = next(gens[t], None)


def _chain(*gens):
    for g in gens:
        yield from g


class _ValueStash:
    def __init__(self):
        self.vals = {}

    def put(self, name, val):
        self.vals[name] = val

    def get(self, name, j=None):
        return self.vals[name] if j is None else self.vals[name][j]


class _RefStash:
    def __init__(self, refs, slot, lo, n):
        self.refs, self.slot, self.lo, self.n = refs, slot, lo, n

    def put(self, name, val):
        self.refs[name][self.slot, self.lo:self.lo + self.n] = val

    def get(self, name, j=None):
        if j is None:
            return self.refs[name][self.slot, self.lo:self.lo + self.n]
        return self.refs[name][self.slot, self.lo + j]


def _proj_kernel(x_ref, w_ref, first_ref, mu_ref, lbl_ref, o_ref, last_ref, wbf_scr, carry_scr, *, tiles_per_seq):
    nb, tt, _ = x_ref.shape
    tile = pl.program_id(0)

    @pl.when(tile == 0)
    def _cast_weights():
        wbf_scr[...] = w_ref[...].astype(BF16)

    xb = x_ref[...].reshape(nb * tt, D_MODEL).astype(BF16)

    def cols(lo, width):
        return jnp.dot(xb, wbf_scr[:, lo:lo + width], preferred_element_type=F32)

    first = first_ref[...]
    if tiles_per_seq > 1:
        first = jnp.where(lax.rem(tile, tiles_per_seq) == 0, first, carry_scr[...])
    last_rows = []
    for lo in range(0, RWKV_PROJ, PROJ_SECTION):
        width = PROJ_SECTION
        p = cols(lo, width)
        p3 = p.reshape(nb, tt, width)
        row = lax.broadcasted_iota(jnp.int32, p3.shape, 1)
        prev = jnp.where(row == 0, first[:, :, lo:lo + width], pltpu.roll(p, 1, 0).reshape(p3.shape))
        o_ref[:, :, lo:lo + width] = p3 + mu_ref[:, lo:lo + width] * (prev - p3)
        last_rows.append(p3[:, tt - 1:tt, :])
    last = jnp.concatenate(last_rows, axis=-1)
    last_ref[...] = last
    if tiles_per_seq > 1:
        carry_scr[...] = last

    logits = lbl_ref[...]
    ex = jnp.exp(logits - jnp.max(logits, axis=0, keepdims=True))
    lb = ex[0:1] / jnp.sum(ex, axis=0, keepdims=True)
    for part, act in enumerate(("silu", "log_gate", "copy", "silu")):
        for lo in range(0, HGRN_WIDTH, PROJ_SECTION):
            col = RWKV_PROJ + part * HGRN_WIDTH + lo
            h = cols(col, PROJ_SECTION)
            if act == "silu":
                h = h * _sigmoid(h)
            elif act == "log_gate":
                lb_s = lb[:, lo:lo + PROJ_SECTION]
                h = jnp.log(lb_s + (1.0 - lb_s) * _sigmoid(h))
            o_ref[:, :, col:col + PROJ_SECTION] = h.reshape(nb, tt, PROJ_SECTION)


def _proj(x3, s_shift3, p, tm):
    B, T, _ = x3.shape
    tt = min(T, tm)
    nb = tm // tt
    tiles_per_seq = T // tt

    def tile_map(i):
        return (i // tiles_per_seq, lax.rem(i, tiles_per_seq), 0)

    def seq_map(i):
        return (i // tiles_per_seq, 0, 0)

    def const(shape):
        return pl.BlockSpec(shape, lambda i: (0, 0), pipeline_mode=pl.Buffered(1))

    return pl.pallas_call(
        functools.partial(_proj_kernel, tiles_per_seq=tiles_per_seq),
        grid=(B * T // tm,),
        in_specs=[pl.BlockSpec((nb, tt, D_MODEL), tile_map),
                  const((D_MODEL, PROJ)),
                  pl.BlockSpec((nb, 1, RWKV_PROJ), seq_map),
                  const((1, RWKV_PROJ)), const((DEPTH + 1, HGRN_WIDTH))],
        out_specs=[pl.BlockSpec((nb, tt, PROJ), tile_map),
                   pl.BlockSpec((nb, 1, RWKV_PROJ), seq_map)],
        out_shape=[jax.ShapeDtypeStruct((B, T, PROJ), F32),
                   jax.ShapeDtypeStruct((B, 1, RWKV_PROJ), F32)],
        scratch_shapes=[pltpu.VMEM((D_MODEL, PROJ), BF16), pltpu.VMEM((nb, 1, RWKV_PROJ), F32)],
        compiler_params=pltpu.CompilerParams(dimension_semantics=("arbitrary",),
                                             vmem_limit_bytes=VMEM_LIMIT),
        name="proj",
    )(x3, p["w_in"], s_shift3, p["shift_mu"], p["lb_logits"])


def _head_helpers(C):
    lane = lax.broadcasted_iota(jnp.int32, (C, LANES), 1)
    m0 = lane < RWKV_HEAD

    def head_sum(x):
        s0 = jnp.sum(jnp.where(m0, x, 0.0), axis=-1, keepdims=True)
        s1 = jnp.sum(jnp.where(m0, 0.0, x), axis=-1, keepdims=True)
        return jnp.where(m0, s0, s1)

    def stack_heads(x):
        return jnp.concatenate([jnp.where(m0, x, 0.0), jnp.where(m0, 0.0, x)], axis=-2)

    def merge_heads(x):
        return jnp.where(m0, x[:C], x[C:])

    return head_sum, stack_heads, merge_heads


def _rwkv_produce(stash, lo, n, C, r_ref, k_ref, v_ref, l_ref,
                  w0_ref, a0_ref, kk_ref, ka_ref, rk_ref, w1_ref, a1_ref, g1_ref):
    ids = range(n)
    grp = slice(lo, lo + n)
    head_sum, stack_heads, _ = _head_helpers(C)
    rowc = lax.broadcasted_iota(jnp.int32, (C, C), 0)
    colc = lax.broadcasted_iota(jnp.int32, (C, C), 1)
    tri = jnp.where(rowc >= colc, 1.0, 0.0).astype(BF16)

    w1 = _bf(w1_ref[...])
    a1 = _bf(a1_ref[...])
    g1 = _bf(g1_ref[...])

    yield ("V", 30.0 * n)
    xl = l_ref[grp].reshape(n * C, 2 * LANES)
    xl_lo = xl[:, :LANES]
    dw = jnp.dot(_bf(jnp.tanh(xl_lo)), w1, preferred_element_type=F32)
    da = jnp.dot(_bf(xl_lo), a1, preferred_element_type=F32)
    gate = jnp.dot(_bf(_sigmoid(xl[:, LANES:])), g1, preferred_element_type=F32)
    stash.put("gate", gate.reshape(n, C, LANES))

    yield ("V", 60.0 * n)
    lw = (-DECAY_SCALE * _sigmoid(w0_ref[...] + dw)).reshape(n, C, LANES)
    a_lr = _sigmoid(a0_ref[...] + da).reshape(n, C, LANES)
    xk = k_ref[grp]
    kk = xk * kk_ref[...]
    kk = kk * jnp.minimum(lax.rsqrt(head_sum(kk * kk)), 1.0 / KK_NORM_FLOOR)
    k2 = xk * (1.0 + (a_lr - 1.0) * ka_ref[...])
    b = kk * a_lr
    cum = jnp.stack([_cumsum_rows(tri, lw[j]) for j in ids])

    yield ("V", 80.0 * n)
    r = r_ref[grp]
    v = v_ref[grp]
    cum_last = cum[:, C - 1:C, :]
    e_in = jnp.exp(-cum)
    e_out = jnp.exp(cum_last - cum)
    at = -kk * jnp.exp(cum - lw)
    rt = r * jnp.exp(cum)
    bt = b * e_in
    kt = k2 * e_in
    stash.put("lhs", _bf(jnp.concatenate([stack_heads(at), stack_heads(rt)], axis=1)))
    stash.put("rhs", _bf(jnp.concatenate([bt, kt], axis=1)))
    stash.put("bt_hat", _bf(stack_heads(bt)))
    stash.put("atrt", _bf(jnp.concatenate([at, rt], axis=1)))
    stash.put("vb", _bf(v))
    stash.put("bkh", _bf(jnp.concatenate([b * e_out, k2 * e_out], axis=1)))
    stash.put("dec", jnp.exp(cum_last))
    stash.put("bonus", head_sum(r * k2 * rk_ref[...]) * v)


def _rwkv_consume(stash, lo, n, C, dot_cost, lnw_ref, lnb_ref, o_ref, s_scr):
    H = RWKV_HEAD
    ids = range(n)
    grp = slice(lo, lo + n)
    head_sum, stack_heads, merge_heads = _head_helpers(C)
    r2 = lax.broadcasted_iota(jnp.int32, (2 * C, 2 * C), 0)
    c2 = lax.broadcasted_iota(jnp.int32, (2 * C, 2 * C), 1)
    t2 = jnp.bitwise_and(r2, C - 1)
    s2 = jnp.bitwise_and(c2, C - 1)
    strict2 = t2 > s2
    incl2 = t2 >= s2
    strict_bd = jnp.logical_and(strict2, (r2 >= C) == (c2 >= C))
    eye2 = jnp.where(r2 == c2, 1.0, 0.0).astype(F32)
    bl_r = lax.broadcasted_iota(jnp.int32, (LANES, LANES), 0) < H
    bl_c = lax.broadcasted_iota(jnp.int32, (LANES, LANES), 1) < H
    blockdiag = bl_r == bl_c
    zeros_c = jnp.zeros((C, LANES), BF16)

    yield ("M", 2 * dot_cost)
    g_a, g_r, pw, inv = [], [], [], []
    for j in ids:
        lhs = stash.get("lhs", j)
        g = _dot(lhs, stash.get("rhs", j), NT)
        g_a.append(_bf(jnp.where(strict2, g[:2 * C], 0.0)))
        g_r.append(_bf(jnp.where(incl2, g[2 * C:], 0.0)))
        n_bd = jnp.where(strict_bd, _dot(lhs[:2 * C], stash.get("bt_hat", j), NT), 0.0)
        inv.append(eye2 + n_bd)
        pw.append(_bf(n_bd))
    yield ("M", 2 * dot_cost)
    S = [s_scr[lo + j] for j in ids]
    pq = [_dot(stash.get("atrt", j), _bf(S[j]), NT) for j in ids]
    pw = [_bf(_dot(pw[j], pw[j], NN)) for j in ids]
    span = 4
    while span < C:
        yield ("M", 2 * dot_cost)
        nxt = [_bf(_dot(pw[j], pw[j], NN)) for j in ids]
        inv = [inv[j] + _dot(_bf(inv[j]), pw[j], NN) for j in ids]
        pw = nxt
        span *= 2
    yield ("M", 2 * dot_cost)
    inv = [inv[j] + _dot(_bf(inv[j]), pw[j], NN) for j in ids]
    vb = [stash.get("vb", j) for j in ids]
    w_hat = [_bf(stack_heads(pq[j][:C] + merge_heads(_dot(g_a[j], jnp.concatenate([zeros_c, vb[j]], axis=0), NN))))
             for j in ids]
    yield ("M", dot_cost)
    z = []
    for j in ids:
        u2 = _dot(_bf(inv[j]), w_hat[j], NN)
        z.append(jnp.concatenate([_bf(u2[:C] + u2[C:]), vb[j]], axis=0))
    yield ("M", 2 * dot_cost)
    y = [pq[j][C:] + merge_heads(_dot(g_r[j], z[j], NN)) for j in ids]
    for j in ids:
        upd = _dot(z[j], stash.get("bkh", j), TN)
        s_scr[lo + j] = S[j] * stash.get("dec", j) + jnp.where(blockdiag, upd, 0.0)

    yield ("V", 40.0 * n)
    y = jnp.stack(y)
    mu = head_sum(y) * (1.0 / H)
    dy = y - mu
    var = head_sum(dy * dy) * (1.0 / H)
    yn = dy * lax.rsqrt(var + GN_EPS) * lnw_ref[...] + lnb_ref[...]
    o_ref[grp] = (yn + stash.get("bonus")) * stash.get("gate")


def _mid_rows(x, h):
    R = x.shape[0]
    row = lax.broadcasted_iota(jnp.int32, x.shape, 0)
    if 2 * h >= SUBLANES:
        return jnp.concatenate(
            [jnp.broadcast_to(x[b * 2 * h + h - 1:b * 2 * h + h], (2 * h, LANES)) for b in range(R // (2 * h))],
            axis=0)
    if h == 1:
        return jnp.where(jnp.bitwise_and(row, 1) == 1, pltpu.roll(x, 1, 0), x)
    picks = []
    for half in range(SUBLANES // (2 * h)):
        r0 = half * 2 * h + h - 1
        picks.append(jnp.concatenate(
            [jnp.broadcast_to(x[g * SUBLANES + r0:g * SUBLANES + r0 + 1], (SUBLANES, LANES))
             for g in range(R // SUBLANES)], axis=0))
    out = picks[-1]
    sub = jnp.bitwise_and(row, SUBLANES - 1)
    for half in range(len(picks) - 2, -1, -1):
        out = jnp.where(sub < (half + 1) * 2 * h, picks[half], out)
    return out


def _hgrn_produce(stash, lo, n, C, q_ref, lf_ref, i_ref):
    ids = range(n)
    grp = slice(lo, lo + n)
    t_i = lax.broadcasted_iota(jnp.int32, (C, C), 0)
    s_i = lax.broadcasted_iota(jnp.int32, (C, C), 1)
    tri = jnp.where(t_i >= s_i, 1.0, 0.0).astype(BF16)

    yield ("V", 20.0 * n)
    q = q_ref[grp]
    lf = lf_ref[grp]
    kin = 1.0 - jnp.exp(lf)
    bc = jnp.stack([_cumsum_rows(tri, lf[j]) for j in ids])

    yield ("V", 30.0 * n)
    b_last = bc[:, C - 1:C, :]
    stash.put("qs", _bf(q * jnp.exp(bc)))
    stash.put("ks", _bf(kin * jnp.exp(b_last - bc)))
    stash.put("iv", _bf(i_ref[grp]))
    stash.put("q", q)
    stash.put("kin", kin)
    stash.put("bc", bc)
    stash.put("dec", jnp.exp(b_last))


def _hgrn_consume(stash, lo, n, C, dot_cost, g_ref, hgw_ref, o_ref, s_scr):
    ids = range(n)
    grp = slice(lo, lo + n)
    t_i = lax.broadcasted_iota(jnp.int32, (C, C), 0)
    s_i = lax.broadcasted_iota(jnp.int32, (C, C), 1)
    halves = []
    h = C // 2
    while h >= 1:
        halves.append(h)
        h //= 2

    def level_mask(h):
        same = jnp.bitwise_and(t_i, -2 * h) == jnp.bitwise_and(s_i, -2 * h)
        return jnp.logical_and(same, jnp.logical_and(jnp.bitwise_and(t_i, h) != 0, jnp.bitwise_and(s_i, h) == 0))

    def seq(x, j):
        return _bf(x[j * C:(j + 1) * C])

    yield ("M", dot_cost)
    S = [s_scr[lo + j] for j in ids]
    o_state = [_dot(stash.get("qs", j), _bf(S[j]), NT) for j in ids]
    yield ("M", dot_cost)
    iv = [stash.get("iv", j) for j in ids]
    for j in ids:
        s_scr[lo + j] = S[j] * stash.get("dec", j) + _dot(iv[j], stash.get("ks", j), TN)
    yield ("M", dot_cost)
    q = stash.get("q").reshape(n * C, LANES)
    kin = stash.get("kin").reshape(n * C, LANES)
    bc = stash.get("bc").reshape(n * C, LANES)
    att = [jnp.where(t_i == s_i, _dot(seq(q, j), seq(kin, j), NT), 0.0) for j in ids]
    for h in halves:
        yield ("V", 25.0 * n)
        dmid = jnp.exp(-jnp.abs(bc - _mid_rows(bc, h)))
        ql = q * dmid
        kl = kin * dmid
        yield ("M", dot_cost)
        msk = level_mask(h)
        att = [jnp.where(msk, _dot(seq(ql, j), seq(kl, j), NT), att[j]) for j in ids]
    yield ("M", dot_cost)
    o_intra = [_dot(_bf(att[j]), iv[j], NN) for j in ids]

    yield ("V", 25.0 * n)
    o = jnp.stack([o_intra[j] + o_state[j] for j in ids])
    ms = jnp.mean(o * o, axis=-1, keepdims=True)
    o_ref[grp] = o * lax.rsqrt(ms + RMS_EPS) * hgw_ref[...] * g_ref[grp]


MXU_STAGE_COST = {True: (22.0, 15.0), False: (30.0, 20.0)}
N_MIXER_IN = 21
N_MIXER_OUT = 4
N_MIXER_STATE_SCRATCH = 2


def _mixer_kernel(*refs, bb, C, n_groups, pipelined):
    (r_ref, k_ref, v_ref, l_ref, w0_ref, a0_ref, kk_ref, ka_ref, rk_ref,
     lnw_ref, lnb_ref, w1_ref, a1_ref, g1_ref, srw0_ref,
     q_ref, lf_ref, i_ref, g_ref, hgw_ref, shg0_ref) = refs[:N_MIXER_IN]
    orw_ref, strw_ref, ohg_ref, sthg_ref = refs[N_MIXER_IN:N_MIXER_IN + N_MIXER_OUT]
    scr = refs[N_MIXER_IN + N_MIXER_OUT:]
    srw_scr, shg_scr = scr[:N_MIXER_STATE_SCRATCH]
    stash_refs = scr[N_MIXER_STATE_SCRATCH:]
    if pipelined:
        rw_refs = {name: stash_refs[k] for k, (name, _, _) in enumerate(RWKV_STASH)}
        hg_refs = {name: stash_refs[len(RWKV_STASH) + k] for k, (name, _, _) in enumerate(HGRN_STASH)}

    c = pl.program_id(2)
    last = pl.num_programs(2) - 1
    H = RWKV_HEAD

    @pl.when(c == 0)
    def _init():
        z = jnp.zeros((H, H), F32)
        for i in range(bb):
            top = jnp.concatenate([srw0_ref[i, 0], z], axis=1)
            bot = jnp.concatenate([z, srw0_ref[i, 1]], axis=1)
            srw_scr[i] = jnp.concatenate([top, bot], axis=0)
            shg_scr[i] = shg0_ref[i, 0].T
        if pipelined:
            for table in (rw_refs, hg_refs):
                for name, ref in table.items():
                    fill = jnp.ones if name == "dec" else jnp.zeros
                    ref[1] = fill(ref.shape[1:], ref.dtype)

    per = bb // n_groups
    groups = [(gidx * per, per) for gidx in range(n_groups)]

    def rwkv_produce(stash, lo, n):
        return _rwkv_produce(stash, lo, n, C, r_ref, k_ref, v_ref, l_ref,
                             w0_ref, a0_ref, kk_ref, ka_ref, rk_ref, w1_ref, a1_ref, g1_ref)

    def rwkv_consume(stash, lo, n):
        return _rwkv_consume(stash, lo, n, C, MXU_STAGE_COST[pipelined][0] * n, lnw_ref, lnb_ref, orw_ref, srw_scr)

    def hgrn_produce(stash, lo, n):
        return _hgrn_produce(stash, lo, n, C, q_ref, lf_ref, i_ref)

    def hgrn_consume(stash, lo, n):
        return _hgrn_consume(stash, lo, n, C, MXU_STAGE_COST[pipelined][1] * n, g_ref, hgw_ref, ohg_ref, shg_scr)

    if not pipelined:
        tasks = []
        for lo, n in groups:
            rw, hg = _ValueStash(), _ValueStash()
            tasks.append(_chain(rwkv_produce(rw, lo, n), rwkv_consume(rw, lo, n)))
            tasks.append(_chain(hgrn_produce(hg, lo, n), hgrn_consume(hg, lo, n)))
        _interleave(tasks)
    else:
        def step(write_slot):
            read_slot = 1 - write_slot
            tasks = []
            for lo, n in groups:
                tasks.append(rwkv_consume(_RefStash(rw_refs, read_slot, lo, n), lo, n))
                tasks.append(hgrn_consume(_RefStash(hg_refs, read_slot, lo, n), lo, n))
            for lo, n in groups:
                tasks.append(rwkv_produce(_RefStash(rw_refs, write_slot, lo, n), lo, n))
                tasks.append(hgrn_produce(_RefStash(hg_refs, write_slot, lo, n), lo, n))
            _interleave(tasks)

        parity = lax.rem(c, 2)
        pl.when(parity == 0)(functools.partial(step, 0))
        pl.when(parity == 1)(functools.partial(step, 1))

    @pl.when(c == last)
    def _fin():
        for i in range(bb):
            s_fin = srw_scr[i]
            strw_ref[i, 0] = s_fin[:H, :H]
            strw_ref[i, 1] = s_fin[H:, H:]
            sthg_ref[i, 0] = shg_scr[i].T


def _mixer(proj3, s_rwkv, s_hgrn, p, bb, C, n_groups, pipelined):
    B, T, _ = proj3.shape
    nc = T // C
    grid = (B // bb, PAIRS, nc + 1 if pipelined else nc)
    nb = RWKV_WIDTH // LANES
    hg0 = RWKV_PROJ // LANES
    lora_blk = LORA_COL // (2 * LANES)

    if pipelined:
        def produced(c):
            return jnp.minimum(c, nc - 1)

        def consumed(c):
            return jnp.maximum(c - 1, 0)
    else:
        produced = consumed = lambda c: c

    def tok(col0, chunk=produced):
        return pl.BlockSpec((bb, C, LANES), lambda b, h, c: (b, chunk(c), col0 + h))

    def vec(col0):
        return pl.BlockSpec((1, LANES), lambda b, h, c: (0, col0 + h))

    in_specs = [
        tok(0), tok(nb), tok(2 * nb),
        pl.BlockSpec((bb, C, 2 * LANES), lambda b, h, c: (b, produced(c), lora_blk)),
        vec(0), vec(0), vec(0), vec(0), vec(0), vec(0), vec(0),
        pl.BlockSpec((LANES, LANES), lambda b, h, c: (0, h)),
        pl.BlockSpec((LANES, LANES), lambda b, h, c: (0, h)),
        pl.BlockSpec((GATE_RANK, LANES), lambda b, h, c: (0, h)),
        pl.BlockSpec((bb, 2, RWKV_HEAD, RWKV_HEAD), lambda b, h, c: (b, h, 0, 0)),
        tok(hg0), tok(hg0 + nb), tok(hg0 + 2 * nb), tok(hg0 + 3 * nb, consumed),
        vec(0),
        pl.BlockSpec((bb, 1, HGRN_HEAD, HGRN_HEAD), lambda b, h, c: (b, h, 0, 0)),
    ]
    assert len(in_specs) == N_MIXER_IN
    out_specs = [
        pl.BlockSpec((bb, C, LANES), lambda b, h, c: (b, consumed(c), h)),
        pl.BlockSpec((bb, 2, RWKV_HEAD, RWKV_HEAD), lambda b, h, c: (b, h, 0, 0)),
        pl.BlockSpec((bb, C, LANES), lambda b, h, c: (b, consumed(c), h)),
        pl.BlockSpec((bb, 1, HGRN_HEAD, HGRN_HEAD), lambda b, h, c: (b, h, 0, 0)),
    ]
    out_shape = [jax.ShapeDtypeStruct((B, T, RWKV_WIDTH), F32),
                 jax.ShapeDtypeStruct((B, RWKV_HEADS, RWKV_HEAD, RWKV_HEAD), F32),
                 jax.ShapeDtypeStruct((B, T, HGRN_WIDTH), F32),
                 jax.ShapeDtypeStruct((B, HGRN_HEADS, HGRN_HEAD, HGRN_HEAD), F32)]
    scratch = [pltpu.VMEM((bb, LANES, LANES), F32), pltpu.VMEM((bb, HGRN_HEAD, HGRN_HEAD), F32)]
    assert len(scratch) == N_MIXER_STATE_SCRATCH
    if pipelined:
        scratch += [pltpu.VMEM((2, bb, rows * C if rows else 1, LANES), dt)
                    for _, rows, dt in RWKV_STASH + HGRN_STASH]
    return pl.pallas_call(
        functools.partial(_mixer_kernel, bb=bb, C=C, n_groups=n_groups, pipelined=pipelined),
        grid=grid, in_specs=in_specs, out_specs=out_specs, out_shape=out_shape,
        scratch_shapes=scratch,
        compiler_params=pltpu.CompilerParams(
            dimension_semantics=("arbitrary", "arbitrary", "arbitrary"),
            vmem_limit_bytes=VMEM_LIMIT),
        name="mixer",
    )(proj3, proj3, proj3, proj3,
      p["w0"], p["a0"], p["k_k"], p["k_a"], p["r_k"], p["ln_x_w"], p["ln_x_b"],
      p["w1u_pad"], p["a1u_pad"], p["g1u"], s_rwkv,
      proj3, proj3, proj3, proj3, p["hg_norm_w"], s_hgrn)


def _layer_norm(x, g, b):
    mu = jnp.mean(x, axis=-1, keepdims=True)
    d = x - mu
    var = jnp.mean(d * d, axis=-1, keepdims=True)
    return d * lax.rsqrt(var + LN_EPS) * g + b


def _post_kernel(x_ref, orw_ref, ohg_ref, wo1_ref, wo2_ref, g1_ref, b1_ref, wup_ref, wdn_ref,
                 g2_ref, b2_ref, y_ref):
    tm = x_ref.shape[0]
    rows = [slice(k * tm // POST_SPLIT, (k + 1) * tm // POST_SPLIT) for k in range(POST_SPLIT)]

    def mix_stage(r):
        return (jnp.dot(orw_ref[r, :].astype(BF16), wo1_ref[...], preferred_element_type=F32)
                + jnp.dot(ohg_ref[r, :].astype(BF16), wo2_ref[...], preferred_element_type=F32))

    def up_stage(r, mix):
        h1 = _layer_norm(ALPHA * x_ref[r, :] + mix, g1_ref[...], b1_ref[...])
        return h1, jnp.dot(h1.astype(BF16), wup_ref[...], preferred_element_type=F32)

    def down_stage(up):
        up = jnp.square(jnp.maximum(up, 0.0))
        return jnp.dot(up.astype(BF16), wdn_ref[...], preferred_element_type=F32)

    def out_stage(r, h1, ff):
        y_ref[r, :] = _layer_norm(ALPHA * h1 + ff, g2_ref[...], b2_ref[...])

    mix, h1, up, ff = {}, {}, {}, {}
    for step in range(POST_SPLIT + 3):
        if step < POST_SPLIT:
            mix[step] = mix_stage(rows[step])
        k = step - 1
        if 0 <= k < POST_SPLIT:
            h1[k], up[k] = up_stage(rows[k], mix.pop(k))
        k = step - 2
        if 0 <= k < POST_SPLIT:
            ff[k] = down_stage(up.pop(k))
        k = step - 3
        if 0 <= k < POST_SPLIT:
            out_stage(rows[k], h1.pop(k), ff.pop(k))


def _post(x2, orw2, ohg2, p, tm):
    n = x2.shape[0]

    def const(shape):
        return pl.BlockSpec(shape, lambda i: (0, 0), pipeline_mode=pl.Buffered(1))

    return pl.pallas_call(
        _post_kernel,
        grid=(n // tm,),
        in_specs=[pl.BlockSpec((tm, D_MODEL), lambda i: (i, 0)),
                  pl.BlockSpec((tm, RWKV_WIDTH), lambda i: (i, 0)),
                  pl.BlockSpec((tm, HGRN_WIDTH), lambda i: (i, 0)),
                  const((RWKV_WIDTH, D_MODEL)), const((HGRN_WIDTH, D_MODEL)),
                  const((1, D_MODEL)), const((1, D_MODEL)),
                  const((D_MODEL, D_FF)), const((D_FF, D_MODEL)),
                  const((1, D_MODEL)), const((1, D_MODEL))],
        out_specs=pl.BlockSpec((tm, D_MODEL), lambda i: (i, 0)),
        out_shape=jax.ShapeDtypeStruct((n, D_MODEL), F32),
        compiler_params=pltpu.CompilerParams(dimension_semantics=("arbitrary",),
                                             vmem_limit_bytes=VMEM_LIMIT),
        name="post",
    )(x2, orw2, ohg2, p["wo_rw"], p["wo_hg"], p["ln1_g"], p["ln1_b"], p["w_up"], p["w_down"],
      p["ln2_g"], p["ln2_b"])


def _prep_params(w_in, shift_mu, w0, w1u, a0, a1u, g1u, k_k, k_a, r_k, ln_x_w, ln_x_b, lb_logits,
                 hg_norm_w, w_out, ln1_g, ln1_b, w_up, w_down, ln2_g, ln2_b):
    zw = jnp.zeros((LANES - DECAY_RANK, RWKV_WIDTH), F32)
    za = jnp.zeros((LANES - AICL_RANK, RWKV_WIDTH), F32)
    w_out_bf = w_out[0].astype(BF16)
    return {
        "w_in": w_in[0],
        "shift_mu": shift_mu[0].reshape(1, RWKV_PROJ),
        "w0": w0[0].reshape(1, RWKV_WIDTH), "a0": a0[0].reshape(1, RWKV_WIDTH),
        "k_k": k_k[0].reshape(1, RWKV_WIDTH), "k_a": k_a[0].reshape(1, RWKV_WIDTH),
        "r_k": r_k[0].reshape(1, RWKV_WIDTH),
        "ln_x_w": ln_x_w[0].reshape(1, RWKV_WIDTH), "ln_x_b": ln_x_b[0].reshape(1, RWKV_WIDTH),
        "w1u_pad": jnp.concatenate([w1u[0], zw], axis=0),
        "a1u_pad": jnp.concatenate([za, a1u[0]], axis=0),
        "g1u": g1u[0],
        "lb_logits": lb_logits.astype(F32),
        "hg_norm_w": hg_norm_w[0].reshape(1, HGRN_WIDTH),
        "wo_rw": w_out_bf[:RWKV_WIDTH], "wo_hg": w_out_bf[RWKV_WIDTH:],
        "ln1_g": ln1_g[0].reshape(1, D_MODEL), "ln1_b": ln1_b[0].reshape(1, D_MODEL),
        "w_up": w_up[0].astype(BF16), "w_down": w_down[0].astype(BF16),
        "ln2_g": ln2_g[0].reshape(1, D_MODEL), "ln2_b": ln2_b[0].reshape(1, D_MODEL),
    }


def _run_group(x, s_rwkv, s_hgrn, s_shift, p, *, tm, bb, chunk, n_groups, pipelined):
    B, T, _ = x.shape
    proj3, last = _proj(x, s_shift.reshape(B, 1, RWKV_PROJ), p, tm)
    o_rw, st_rw, o_hg, st_hg = _mixer(proj3, s_rwkv, s_hgrn, p, bb, chunk, n_groups, pipelined)
    y2 = _post(x.reshape(B * T, D_MODEL), o_rw.reshape(B * T, RWKV_WIDTH), o_hg.reshape(B * T, HGRN_WIDTH),
               p, tm)
    return y2.reshape(B, T, D_MODEL), st_rw[None], st_hg[None], last.reshape(1, B, RWKV_PROJ)


PROMPT_CFG = dict(tm=512, bb=8, chunk=64, n_groups=1, pipelined=True)
SAMPLE_CFG = dict(tm=512, bb=64, chunk=8, n_groups=2, pipelined=False)


def kernel(x_prompt, x_sample, state_rwkv, state_hgrn, state_shift, w_in, shift_mu, w0, w1u, a0, a1u, g1u, k_k, k_a, r_k, ln_x_w, ln_x_b, lb_logits, hg_norm_w, w_out, ln1_g, ln1_b, w_up, w_down, ln2_g, ln2_b):
    assert w_in.shape[0] == DEPTH
    p = _prep_params(w_in, shift_mu, w0, w1u, a0, a1u, g1u, k_k, k_a, r_k, ln_x_w, ln_x_b, lb_logits,
                     hg_norm_w, w_out, ln1_g, ln1_b, w_up, w_down, ln2_g, ln2_b)
    bp = x_prompt.shape[0]
    z_rw = jnp.zeros((bp, RWKV_HEADS, RWKV_HEAD, RWKV_HEAD), F32)
    z_hg = jnp.zeros((bp, HGRN_HEADS, HGRN_HEAD, HGRN_HEAD), F32)
    z_sh = jnp.zeros((bp, RWKV_PROJ), F32)
    y_p, rw_p, hg_p, sh_p = _run_group(x_prompt, z_rw, z_hg, z_sh, p, **PROMPT_CFG)
    y_s, rw_s, hg_s, sh_s = _run_group(x_sample, state_rwkv[0].astype(F32), state_hgrn[0].astype(F32),
                                       state_shift[0].astype(F32), p, **SAMPLE_CFG)
    return (y_p, y_s, rw_p, rw_s, hg_p, hg_s, sh_p, sh_s)
```

```python
import functools
import math

import jax
import jax.numpy as jnp
from jax import lax
from jax.experimental import pallas as pl
from jax.experimental.pallas import tpu as pltpu

F32 = jnp.float32
BF16 = jnp.bfloat16

D_MODEL = 1024
RWKV_WIDTH = 512
RWKV_HEAD = 64
RWKV_HEADS = 8
HGRN_WIDTH = 512
HGRN_HEAD = 128
HGRN_HEADS = 4
DECAY_RANK = 64
AICL_RANK = 64
GATE_RANK = 128
RWKV_PROJ = 3 * RWKV_WIDTH + DECAY_RANK + AICL_RANK + GATE_RANK
HGRN_PROJ = 4 * HGRN_WIDTH
PROJ = RWKV_PROJ + HGRN_PROJ
D_FF = 4 * D_MODEL
DEPTH = 1
ALPHA = (2.0 * DEPTH) ** 0.25
LN_EPS = 1e-5
GN_EPS = RWKV_HEAD * 1e-5
RMS_EPS = 1e-6
KK_NORM_FLOOR = 1e-12
DECAY_SCALE = math.exp(-0.5)

LANES = 128
SUBLANES = 8
PAIRS = RWKV_HEADS // 2
LORA_COL = 3 * RWKV_WIDTH
VMEM_LIMIT = 56 * 1024 * 1024
PROJ_SECTION = 256
POST_SPLIT = 2

NN = ((1,), (0,))
NT = ((1,), (1,))
TN = ((0,), (0,))

RWKV_STASH = (("lhs", 4, BF16), ("rhs", 2, BF16), ("bt_hat", 2, BF16), ("atrt", 2, BF16), ("vb", 1, BF16),
              ("bkh", 2, BF16), ("dec", 0, F32), ("bonus", 1, F32), ("gate", 1, F32))
HGRN_STASH = (("qs", 1, BF16), ("ks", 1, BF16), ("iv", 1, BF16), ("q", 1, F32), ("kin", 1, F32),
              ("bc", 1, F32), ("dec", 0, F32))


def _dot(a, b, dims):
    return lax.dot_general(a, b, (dims, ((), ())), preferred_element_type=F32)


def _bf(x):
    return x.astype(BF16)


def _cumsum_rows(tri, x):
    p0 = _bf(x)
    r1 = x - p0.astype(F32)
    p1 = _bf(r1)
    p2 = _bf(r1 - p1.astype(F32))
    return _dot(tri, p0, NN) + _dot(tri, p1, NN) + _dot(tri, p2, NN)


def _sigmoid(x):
    return 0.5 * jnp.tanh(0.5 * x) + 0.5


def _interleave(tasks):
    gens = list(tasks)
    nxt = [next(g, None) for g in gens]
    spent = {"M": 0.0, "V": 0.0}
    turn = 0
    while any(k is not None for k in nxt):
        ready = {kind: [j for j, k in enumerate(nxt) if k is not None and k[0] == kind] for kind in spent}
        want = "M" if (ready["M"] and (spent["M"] <= spent["V"] or not ready["V"])) else "V"
        if want == "M":
            cands = ready["M"]
            t = cands[turn % len(cands)]
            turn += 1
        else:
            t = ready["V"][0]
        spent[want] += nxt[t][1]
        nxt[t] = next(gens[t], None)


def _chain(*gens):
    for g in gens:
        yield from g


class _ValueStash:
    def __init__(self):
        self.vals = {}

    def put(self, name, val):
        self.vals[name] = val

    def get(self, name, j=None):
        return self.vals[name] if j is None else self.vals[name][j]


class _RefStash:
    def __init__(self, refs, slot, lo, n):
        self.refs, self.slot, self.lo, self.n = refs, slot, lo, n

    def put(self, name, val):
        self.refs[name][self.slot, self.lo:self.lo + self.n] = val

    def get(self, name, j=None):
        if j is None:
            return self.refs[name][self.slot, self.lo:self.lo + self.n]
        return self.refs[name][self.slot, self.lo + j]


def _proj_kernel(x_ref, w_ref, first_ref, mu_ref, lbl_ref, o_ref, last_ref, wbf_scr, carry_scr, *, tiles_per_seq):
    nb, tt, _ = x_ref.shape
    tile = pl.program_id(0)

    @pl.when(tile == 0)
    def _cast_weights():
        wbf_scr[...] = w_ref[...].astype(BF16)

    xb = x_ref[...].reshape(nb * tt, D_MODEL).astype(BF16)

    def cols(lo, width):
        return jnp.dot(xb, wbf_scr[:, lo:lo + width], preferred_element_type=F32)

    first = first_ref[...]
    if tiles_per_seq > 1:
        first = jnp.where(lax.rem(tile, tiles_per_seq) == 0, first, carry_scr[...])
    last_rows = []
    for lo in range(0, RWKV_PROJ, PROJ_SECTION):
        width = PROJ_SECTION
        p = cols(lo, width)
        p3 = p.reshape(nb, tt, width)
        row = lax.broadcasted_iota(jnp.int32, p3.shape, 1)
        prev = jnp.where(row == 0, first[:, :, lo:lo + width], pltpu.roll(p, 1, 0).reshape(p3.shape))
        o_ref[:, :, lo:lo + width] = p3 + mu_ref[:, lo:lo + width] * (prev - p3)
        last_rows.append(p3[:, tt - 1:tt, :])
    last = jnp.concatenate(last_rows, axis=-1)
    last_ref[...] = last
    if tiles_per_seq > 1:
        carry_scr[...] = last

    logits = lbl_ref[...]
    ex = jnp.exp(logits - jnp.max(logits, axis=0, keepdims=True))
    lb = ex[0:1] / jnp.sum(ex, axis=0, keepdims=True)
    for part, act in enumerate(("silu", "log_gate", "copy", "silu")):
        for lo in range(0, HGRN_WIDTH, PROJ_SECTION):
            col = RWKV_PROJ + part * HGRN_WIDTH + lo
            h = cols(col, PROJ_SECTION)
            if act == "silu":
                h = h * _sigmoid(h)
            elif act == "log_gate":
                lb_s = lb[:, lo:lo + PROJ_SECTION]
                h = jnp.log(lb_s + (1.0 - lb_s) * _sigmoid(h))
            o_ref[:, :, col:col + PROJ_SECTION] = h.reshape(nb, tt, PROJ_SECTION)


def _proj(x3, s_shift3, p, tm):
    B, T, _ = x3.shape
    tt = min(T, tm)
    nb = tm // tt
    tiles_per_seq = T // tt

    def tile_map(i):
        return (i // tiles_per_seq, lax.rem(i, tiles_per_seq), 0)

    def seq_map(i):
        return (i // tiles_per_seq, 0, 0)

    def const(shape):
        return pl.BlockSpec(shape, lambda i: (0, 0), pipeline_mode=pl.Buffered(1))

    return pl.pallas_call(
        functools.partial(_proj_kernel, tiles_per_seq=tiles_per_seq),
        grid=(B * T // tm,),
        in_specs=[pl.BlockSpec((nb, tt, D_MODEL), tile_map),
                  const((D_MODEL, PROJ)),
                  pl.BlockSpec((nb, 1, RWKV_PROJ), seq_map),
                  const((1, RWKV_PROJ)), const((DEPTH + 1, HGRN_WIDTH))],
        out_specs=[pl.BlockSpec((nb, tt, PROJ), tile_map),
                   pl.BlockSpec((nb, 1, RWKV_PROJ), seq_map)],
        out_shape=[jax.ShapeDtypeStruct((B, T, PROJ), F32),
                   jax.ShapeDtypeStruct((B, 1, RWKV_PROJ), F32)],
        scratch_shapes=[pltpu.VMEM((D_MODEL, PROJ), BF16), pltpu.VMEM((nb, 1, RWKV_PROJ), F32)],
        compiler_params=pltpu.CompilerParams(dimension_semantics=("arbitrary",),
                                             vmem_limit_bytes=VMEM_LIMIT),
        name="proj",
    )(x3, p["w_in"], s_shift3, p["shift_mu"], p["lb_logits"])


def _head_helpers(C):
    lane = lax.broadcasted_iota(jnp.int32, (C, LANES), 1)
    m0 = lane < RWKV_HEAD

    def head_sum(x):
        s0 = jnp.sum(jnp.where(m0, x, 0.0), axis=-1, keepdims=True)
        s1 = jnp.sum(jnp.where(m0, 0.0, x), axis=-1, keepdims=True)
        return jnp.where(m0, s0, s1)

    def stack_heads(x):
        return jnp.concatenate([jnp.where(m0, x, 0.0), jnp.where(m0, 0.0, x)], axis=-2)

    def merge_heads(x):
        return jnp.where(m0, x[:C], x[C:])

    return head_sum, stack_heads, merge_heads


def _rwkv_produce(stash, lo, n, C, r_ref, k_ref, v_ref, l_ref,
                  w0_ref, a0_ref, kk_ref, ka_ref, rk_ref, w1_ref, a1_ref, g1_ref):
    ids = range(n)
    grp = slice(lo, lo + n)
    head_sum, stack_heads, _ = _head_helpers(C)
    rowc = lax.broadcasted_iota(jnp.int32, (C, C), 0)
    colc = lax.broadcasted_iota(jnp.int32, (C, C), 1)
    tri = jnp.where(rowc >= colc, 1.0, 0.0).astype(BF16)

    w1 = _bf(w1_ref[...])
    a1 = _bf(a1_ref[...])
    g1 = _bf(g1_ref[...])

    yield ("V", 30.0 * n)
    xl = l_ref[grp].reshape(n * C, 2 * LANES)
    xl_lo = xl[:, :LANES]
    dw = jnp.dot(_bf(jnp.tanh(xl_lo)), w1, preferred_element_type=F32)
    da = jnp.dot(_bf(xl_lo), a1, preferred_element_type=F32)
    gate = jnp.dot(_bf(_sigmoid(xl[:, LANES:])), g1, preferred_element_type=F32)
    stash.put("gate", gate.reshape(n, C, LANES))

    yield ("V", 60.0 * n)
    lw = (-DECAY_SCALE * _sigmoid(w0_ref[...] + dw)).reshape(n, C, LANES)
    a_lr = _sigmoid(a0_ref[...] + da).reshape(n, C, LANES)
    xk = k_ref[grp]
    kk = xk * kk_ref[...]
    kk = kk * jnp.minimum(lax.rsqrt(head_sum(kk * kk)), 1.0 / KK_NORM_FLOOR)
    k2 = xk * (1.0 + (a_lr - 1.0) * ka_ref[...])
    b = kk * a_lr
    cum = jnp.stack([_cumsum_rows(tri, lw[j]) for j in ids])

    yield ("V", 80.0 * n)
    r = r_ref[grp]
    v = v_ref[grp]
    cum_last = cum[:, C - 1:C, :]
    e_in = jnp.exp(-cum)
    e_out = jnp.exp(cum_last - cum)
    at = -kk * jnp.exp(cum - lw)
    rt = r * jnp.exp(cum)
    bt = b * e_in
    kt = k2 * e_in
    stash.put("lhs", _bf(jnp.concatenate([stack_heads(at), stack_heads(rt)], axis=1)))
    stash.put("rhs", _bf(jnp.concatenate([bt, kt], axis=1)))
    stash.put("bt_hat", _bf(stack_heads(bt)))
    stash.put("atrt", _bf(jnp.concatenate([at, rt], axis=1)))
    stash.put("vb", _bf(v))
    stash.put("bkh", _bf(jnp.concatenate([b * e_out, k2 * e_out], axis=1)))
    stash.put("dec", jnp.exp(cum_last))
    stash.put("bonus", head_sum(r * k2 * rk_ref[...]) * v)


def _rwkv_consume(stash, lo, n, C, dot_cost, lnw_ref, lnb_ref, o_ref, s_scr):
    H = RWKV_HEAD
    ids = range(n)
    grp = slice(lo, lo + n)
    head_sum, stack_heads, merge_heads = _head_helpers(C)
    r2 = lax.broadcasted_iota(jnp.int32, (2 * C, 2 * C), 0)
    c2 = lax.broadcasted_iota(jnp.int32, (2 * C, 2 * C), 1)
    t2 = jnp.bitwise_and(r2, C - 1)
    s2 = jnp.bitwise_and(c2, C - 1)
    strict2 = t2 > s2
    incl2 = t2 >= s2
    strict_bd = jnp.logical_and(strict2, (r2 >= C) == (c2 >= C))
    eye2 = jnp.where(r2 == c2, 1.0, 0.0).astype(F32)
    bl_r = lax.broadcasted_iota(jnp.int32, (LANES, LANES), 0) < H
    bl_c = lax.broadcasted_iota(jnp.int32, (LANES, LANES), 1) < H
    blockdiag = bl_r == bl_c
    zeros_c = jnp.zeros((C, LANES), BF16)

    yield ("M", 2 * dot_cost)
    g_a, g_r, pw, inv = [], [], [], []
    for j in ids:
        lhs = stash.get("lhs", j)
        g = _dot(lhs, stash.get("rhs", j), NT)
        g_a.append(_bf(jnp.where(strict2, g[:2 * C], 0.0)))
        g_r.append(_bf(jnp.where(incl2, g[2 * C:], 0.0)))
        n_bd = jnp.where(strict_bd, _dot(lhs[:2 * C], stash.get("bt_hat", j), NT), 0.0)
        inv.append(eye2 + n_bd)
        pw.append(_bf(n_bd))
    yield ("M", 2 * dot_cost)
    S = [s_scr[lo + j] for j in ids]
    pq = [_dot(stash.get("atrt", j), _bf(S[j]), NT) for j in ids]
    pw = [_bf(_dot(pw[j], pw[j], NN)) for j in ids]
    span = 4
    while span < C:
        yield ("M", 2 * dot_cost)
        nxt = [_bf(_dot(pw[j], pw[j], NN)) for j in ids]
        inv = [inv[j] + _dot(_bf(inv[j]), pw[j], NN) for j in ids]
        pw = nxt
        span *= 2
    yield ("M", 2 * dot_cost)
    inv = [inv[j] + _dot(_bf(inv[j]), pw[j], NN) for j in ids]
    vb = [stash.get("vb", j) for j in ids]
    w_hat = [_bf(stack_heads(pq[j][:C] + merge_heads(_dot(g_a[j], jnp.concatenate([zeros_c, vb[j]], axis=0), NN))))
             for j in ids]
    yield ("M", dot_cost)
    z = []
    for j in ids:
        u2 = _dot(_bf(inv[j]), w_hat[j], NN)
        z.append(jnp.concatenate([_bf(u2[:C] + u2[C:]), vb[j]], axis=0))
    yield ("M", 2 * dot_cost)
    y = [pq[j][C:] + merge_heads(_dot(g_r[j], z[j], NN)) for j in ids]
    for j in ids:
        upd = _dot(z[j], stash.get("bkh", j), TN)
        s_scr[lo + j] = S[j] * stash.get("dec", j) + jnp.where(blockdiag, upd, 0.0)

    yield ("V", 40.0 * n)
    y = jnp.stack(y)
    mu = head_sum(y) * (1.0 / H)
    dy = y - mu
    var = head_sum(dy * dy) * (1.0 / H)
    yn = dy * lax.rsqrt(var + GN_EPS) * lnw_ref[...] + lnb_ref[...]
    o_ref[grp] = (yn + stash.get("bonus")) * stash.get("gate")


def _mid_rows(x, h):
    R = x.shape[0]
    row = lax.broadcasted_iota(jnp.int32, x.shape, 0)
    if 2 * h >= SUBLANES:
        return jnp.concatenate(
            [jnp.broadcast_to(x[b * 2 * h + h - 1:b * 2 * h + h], (2 * h, LANES)) for b in range(R // (2 * h))],
            axis=0)
    if h == 1:
        return jnp.where(jnp.bitwise_and(row, 1) == 1, pltpu.roll(x, 1, 0), x)
    picks = []
    for half in range(SUBLANES // (2 * h)):
        r0 = half * 2 * h + h - 1
        picks.append(jnp.concatenate(
            [jnp.broadcast_to(x[g * SUBLANES + r0:g * SUBLANES + r0 + 1], (SUBLANES, LANES))
             for g in range(R // SUBLANES)], axis=0))
    out = picks[-1]
    sub = jnp.bitwise_and(row, SUBLANES - 1)
    for half in range(len(picks) - 2, -1, -1):
        out = jnp.where(sub < (half + 1) * 2 * h, picks[half], out)
    return out


def _hgrn_produce(stash, lo, n, C, q_ref, lf_ref, i_ref):
    ids = range(n)
    grp = slice(lo, lo + n)
    t_i = lax.broadcasted_iota(jnp.int32, (C, C), 0)
    s_i = lax.broadcasted_iota(jnp.int32, (C, C), 1)
    tri = jnp.where(t_i >= s_i, 1.0, 0.0).astype(BF16)

    yield ("V", 20.0 * n)
    q = q_ref[grp]
    lf = lf_ref[grp]
    kin = 1.0 - jnp.exp(lf)
    bc = jnp.stack([_cumsum_rows(tri, lf[j]) for j in ids])

    yield ("V", 30.0 * n)
    b_last = bc[:, C - 1:C, :]
    stash.put("qs", _bf(q * jnp.exp(bc)))
    stash.put("ks", _bf(kin * jnp.exp(b_last - bc)))
    stash.put("iv", _bf(i_ref[grp]))
    stash.put("q", q)
    stash.put("kin", kin)
    stash.put("bc", bc)
    stash.put("dec", jnp.exp(b_last))


def _hgrn_consume(stash, lo, n, C, dot_cost, g_ref, hgw_ref, o_ref, s_scr):
    ids = range(n)
    grp = slice(lo, lo + n)
    t_i = lax.broadcasted_iota(jnp.int32, (C, C), 0)
    s_i = lax.broadcasted_iota(jnp.int32, (C, C), 1)
    halves = []
    h = C // 2
    while h >= 1:
        halves.append(h)
        h //= 2

    def level_mask(h):
        same = jnp.bitwise_and(t_i, -2 * h) == jnp.bitwise_and(s_i, -2 * h)
        return jnp.logical_and(same, jnp.logical_and(jnp.bitwise_and(t_i, h) != 0, jnp.bitwise_and(s_i, h) == 0))

    def seq(x, j):
        return _bf(x[j * C:(j + 1) * C])

    yield ("M", dot_cost)
    S = [s_scr[lo + j] for j in ids]
    o_state = [_dot(stash.get("qs", j), _bf(S[j]), NT) for j in ids]
    yield ("M", dot_cost)
    iv = [stash.get("iv", j) for j in ids]
    for j in ids:
        s_scr[lo + j] = S[j] * stash.get("dec", j) + _dot(iv[j], stash.get("ks", j), TN)
    yield ("M", dot_cost)
    q = stash.get("q").reshape(n * C, LANES)
    kin = stash.get("kin").reshape(n * C, LANES)
    bc = stash.get("bc").reshape(n * C, LANES)
    att = [jnp.where(t_i == s_i, _dot(seq(q, j), seq(kin, j), NT), 0.0) for j in ids]
    for h in halves:
        yield ("V", 25.0 * n)
        dmid = jnp.exp(-jnp.abs(bc - _mid_rows(bc, h)))
        ql = q * dmid
        kl = kin * dmid
        yield ("M", dot_cost)
        msk = level_mask(h)
        att = [jnp.where(msk, _dot(seq(ql, j), seq(kl, j), NT), att[j]) for j in ids]
    yield ("M", dot_cost)
    o_intra = [_dot(_bf(att[j]), iv[j], NN) for j in ids]

    yield ("V", 25.0 * n)
    o = jnp.stack([o_intra[j] + o_state[j] for j in ids])
    ms = jnp.mean(o * o, axis=-1, keepdims=True)
    o_ref[grp] = o * lax.rsqrt(ms + RMS_EPS) * hgw_ref[...] * g_ref[grp]


MXU_STAGE_COST = {True: (22.0, 15.0), False: (30.0, 20.0)}
N_MIXER_IN = 21
N_MIXER_OUT = 4
N_MIXER_STATE_SCRATCH = 2


def _mixer_kernel(*refs, bb, C, n_groups, pipelined):
    (r_ref, k_ref, v_ref, l_ref, w0_ref, a0_ref, kk_ref, ka_ref, rk_ref,
     lnw_ref, lnb_ref, w1_ref, a1_ref, g1_ref, srw0_ref,
     q_ref, lf_ref, i_ref, g_ref, hgw_ref, shg0_ref) = refs[:N_MIXER_IN]
    orw_ref, strw_ref, ohg_ref, sthg_ref = refs[N_MIXER_IN:N_MIXER_IN + N_MIXER_OUT]
    scr = refs[N_MIXER_IN + N_MIXER_OUT:]
    srw_scr, shg_scr = scr[:N_MIXER_STATE_SCRATCH]
    stash_refs = scr[N_MIXER_STATE_SCRATCH:]
    if pipelined:
        rw_refs = {name: stash_refs[k] for k, (name, _, _) in enumerate(RWKV_STASH)}
        hg_refs = {name: stash_refs[len(RWKV_STASH) + k] for k, (name, _, _) in enumerate(HGRN_STASH)}

    c = pl.program_id(2)
    last = pl.num_programs(2) - 1
    H = RWKV_HEAD

    @pl.when(c == 0)
    def _init():
        z = jnp.zeros((H, H), F32)
        for i in range(bb):
            top = jnp.concatenate([srw0_ref[i, 0], z], axis=1)
            bot = jnp.concatenate([z, srw0_ref[i, 1]], axis=1)
            srw_scr[i] = jnp.concatenate([top, bot], axis=0)
            shg_scr[i] = shg0_ref[i, 0].T
        if pipelined:
            for table in (rw_refs, hg_refs):
                for name, ref in table.items():
                    fill = jnp.ones if name == "dec" else jnp.zeros
                    ref[1] = fill(ref.shape[1:], ref.dtype)

    per = bb // n_groups
    groups = [(gidx * per, per) for gidx in range(n_groups)]

    def rwkv_produce(stash, lo, n):
        return _rwkv_produce(stash, lo, n, C, r_ref, k_ref, v_ref, l_ref,
                             w0_ref, a0_ref, kk_ref, ka_ref, rk_ref, w1_ref, a1_ref, g1_ref)

    def rwkv_consume(stash, lo, n):
        return _rwkv_consume(stash, lo, n, C, MXU_STAGE_COST[pipelined][0] * n, lnw_ref, lnb_ref, orw_ref, srw_scr)

    def hgrn_produce(stash, lo, n):
        return _hgrn_produce(stash, lo, n, C, q_ref, lf_ref, i_ref)

    def hgrn_consume(stash, lo, n):
        return _hgrn_consume(stash, lo, n, C, MXU_STAGE_COST[pipelined][1] * n, g_ref, hgw_ref, ohg_ref, shg_scr)

    if not pipelined:
        tasks = []
        for lo, n in groups:
            rw, hg = _ValueStash(), _ValueStash()
            tasks.append(_chain(rwkv_produce(rw, lo, n), rwkv_consume(rw, lo, n)))
            tasks.append(_chain(hgrn_produce(hg, lo, n), hgrn_consume(hg, lo, n)))
        _interleave(tasks)
    else:
        def step(write_slot):
            read_slot = 1 - write_slot
            tasks = []
            for lo, n in groups:
                tasks.append(rwkv_consume(_RefStash(rw_refs, read_slot, lo, n), lo, n))
                tasks.append(hgrn_consume(_RefStash(hg_refs, read_slot, lo, n), lo, n))
            for lo, n in groups:
                tasks.append(rwkv_produce(_RefStash(rw_refs, write_slot, lo, n), lo, n))
                tasks.append(hgrn_produce(_RefStash(hg_refs, write_slot, lo, n), lo, n))
            _interleave(tasks)

        parity = lax.rem(c, 2)
        pl.when(parity == 0)(functools.partial(step, 0))
        pl.when(parity == 1)(functools.partial(step, 1))

    @pl.when(c == last)
    def _fin():
        for i in range(bb):
            s_fin = srw_scr[i]
            strw_ref[i, 0] = s_fin[:H, :H]
            strw_ref[i, 1] = s_fin[H:, H:]
            sthg_ref[i, 0] = shg_scr[i].T


def _mixer(proj3, s_rwkv, s_hgrn, p, bb, C, n_groups, pipelined):
    B, T, _ = proj3.shape
    nc = T // C
    grid = (B // bb, PAIRS, nc + 1 if pipelined else nc)
    nb = RWKV_WIDTH // LANES
    hg0 = RWKV_PROJ // LANES
    lora_blk = LORA_COL // (2 * LANES)

    if pipelined:
        def produced(c):
            return jnp.minimum(c, nc - 1)

        def consumed(c):
            return jnp.maximum(c - 1, 0)
    else:
        produced = consumed = lambda c: c

    def tok(col0, chunk=produced):
        return pl.BlockSpec((bb, C, LANES), lambda b, h, c: (b, chunk(c), col0 + h))

    def vec(col0):
        return pl.BlockSpec((1, LANES), lambda b, h, c: (0, col0 + h))

    in_specs = [
        tok(0), tok(nb), tok(2 * nb),
        pl.BlockSpec((bb, C, 2 * LANES), lambda b, h, c: (b, produced(c), lora_blk)),
        vec(0), vec(0), vec(0), vec(0), vec(0), vec(0), vec(0),
        pl.BlockSpec((LANES, LANES), lambda b, h, c: (0, h)),
        pl.BlockSpec((LANES, LANES), lambda b, h, c: (0, h)),
        pl.BlockSpec((GATE_RANK, LANES), lambda b, h, c: (0, h)),
        pl.BlockSpec((bb, 2, RWKV_HEAD, RWKV_HEAD), lambda b, h, c: (b, h, 0, 0)),
        tok(hg0), tok(hg0 + nb), tok(hg0 + 2 * nb), tok(hg0 + 3 * nb, consumed),
        vec(0),
        pl.BlockSpec((bb, 1, HGRN_HEAD, HGRN_HEAD), lambda b, h, c: (b, h, 0, 0)),
    ]
    assert len(in_specs) == N_MIXER_IN
    out_specs = [
        pl.BlockSpec((bb, C, LANES), lambda b, h, c: (b, consumed(c), h)),
        pl.BlockSpec((bb, 2, RWKV_HEAD, RWKV_HEAD), lambda b, h, c: (b, h, 0, 0)),
        pl.BlockSpec((bb, C, LANES), lambda b, h, c: (b, consumed(c), h)),
        pl.BlockSpec((bb, 1, HGRN_HEAD, HGRN_HEAD), lambda b, h, c: (b, h, 0, 0)),
    ]
    out_shape = [jax.ShapeDtypeStruct((B, T, RWKV_WIDTH), F32),
                 jax.ShapeDtypeStruct((B, RWKV_HEADS, RWKV_HEAD, RWKV_HEAD), F32),
                 jax.ShapeDtypeStruct((B, T, HGRN_WIDTH), F32),
                 jax.ShapeDtypeStruct((B, HGRN_HEADS, HGRN_HEAD, HGRN_HEAD), F32)]
    scratch = [pltpu.VMEM((bb, LANES, LANES), F32), pltpu.VMEM((bb, HGRN_HEAD, HGRN_HEAD), F32)]
    assert len(scratch) == N_MIXER_STATE_SCRATCH
    if pipelined:
        scratch += [pltpu.VMEM((2, bb, rows * C if rows else 1, LANES), dt)
                    for _, rows, dt in RWKV_STASH + HGRN_STASH]
    return pl.pallas_call(
        functools.partial(_mixer_kernel, bb=bb, C=C, n_groups=n_groups, pipelined=pipelined),
        grid=grid, in_specs=in_specs, out_specs=out_specs, out_shape=out_shape,
        scratch_shapes=scratch,
        compiler_params=pltpu.CompilerParams(
            dimension_semantics=("arbitrary", "arbitrary", "arbitrary"),
            vmem_limit_bytes=VMEM_LIMIT),
        name="mixer",
    )(proj3, proj3, proj3, proj3,
      p["w0"], p["a0"], p["k_k"], p["k_a"], p["r_k"], p["ln_x_w"], p["ln_x_b"],
      p["w1u_pad"], p["a1u_pad"], p["g1u"], s_rwkv,
      proj3, proj3, proj3, proj3, p["hg_norm_w"], s_hgrn)


def _layer_norm(x, g, b):
    mu = jnp.mean(x, axis=-1, keepdims=True)
    d = x - mu
    var = jnp.mean(d * d, axis=-1, keepdims=True)
    return d * lax.rsqrt(var + LN_EPS) * g + b


def _post_kernel(*refs, tiles):
    ng = len(tiles)
    weights = refs[3 * ng:3 * ng + 8]
    outs = refs[3 * ng + 8:]
    i = pl.program_id(0)
    lo = 0
    for g in range(ng):
        x_ref, orw_ref, ohg_ref = refs[3 * g:3 * g + 3]
        pl.when(jnp.logical_and(i >= lo, i < lo + tiles[g]))(
            functools.partial(_post_tile, x_ref, orw_ref, ohg_ref, *weights, outs[g]))
        lo += tiles[g]


def _post_tile(x_ref, orw_ref, ohg_ref, wo1_ref, wo2_ref, g1_ref, b1_ref, wup_ref, wdn_ref,
               g2_ref, b2_ref, y_ref):
    tm = x_ref.shape[0]
    rows = [slice(k * tm // POST_SPLIT, (k + 1) * tm // POST_SPLIT) for k in range(POST_SPLIT)]

    def mix_stage(r):
        return (jnp.dot(orw_ref[r, :].astype(BF16), wo1_ref[...], preferred_element_type=F32)
                + jnp.dot(ohg_ref[r, :].astype(BF16), wo2_ref[...], preferred_element_type=F32))

    def up_stage(r, mix):
        h1 = _layer_norm(ALPHA * x_ref[r, :] + mix, g1_ref[...], b1_ref[...])
        return h1, jnp.dot(h1.astype(BF16), wup_ref[...], preferred_element_type=F32)

    def down_stage(up):
        up = jnp.square(jnp.maximum(up, 0.0))
        return jnp.dot(up.astype(BF16), wdn_ref[...], preferred_element_type=F32)

    def out_stage(r, h1, ff):
        y_ref[r, :] = _layer_norm(ALPHA * h1 + ff, g2_ref[...], b2_ref[...])

    mix, h1, up, ff = {}, {}, {}, {}
    for step in range(POST_SPLIT + 3):
        if step < POST_SPLIT:
            mix[step] = mix_stage(rows[step])
        k = step - 1
        if 0 <= k < POST_SPLIT:
            h1[k], up[k] = up_stage(rows[k], mix.pop(k))
        k = step - 2
        if 0 <= k < POST_SPLIT:
            ff[k] = down_stage(up.pop(k))
        k = step - 3
        if 0 <= k < POST_SPLIT:
            out_stage(rows[k], h1.pop(k), ff.pop(k))


def _post(groups, p, tm):
    tiles = tuple(x2.shape[0] // tm for x2, _, _ in groups)
    starts = [sum(tiles[:g]) for g in range(len(tiles))]

    def const(shape):
        return pl.BlockSpec(shape, lambda i: (0, 0), pipeline_mode=pl.Buffered(1))

    def rows(width, g):
        clip = lambda i: (jnp.clip(i - starts[g], 0, tiles[g] - 1), 0)
        if g == 0:
            return pl.BlockSpec((tm, width), clip)
        return pl.BlockSpec((tm, width), clip, pipeline_mode=pl.Buffered(1))

    in_specs, operands = [], []
    for g, arrays in enumerate(groups):
        in_specs += [rows(D_MODEL, g), rows(RWKV_WIDTH, g), rows(HGRN_WIDTH, g)]
        operands += list(arrays)
    in_specs += [const((RWKV_WIDTH, D_MODEL)), const((HGRN_WIDTH, D_MODEL)),
                 const((1, D_MODEL)), const((1, D_MODEL)),
                 const((D_MODEL, D_FF)), const((D_FF, D_MODEL)),
                 const((1, D_MODEL)), const((1, D_MODEL))]
    return pl.pallas_call(
        functools.partial(_post_kernel, tiles=tiles),
        grid=(sum(tiles),),
        in_specs=in_specs,
        out_specs=[pl.BlockSpec((tm, D_MODEL), lambda i, g=g: (jnp.clip(i - starts[g], 0, tiles[g] - 1), 0))
                   for g in range(len(groups))],
        out_shape=[jax.ShapeDtypeStruct((x2.shape[0], D_MODEL), F32) for x2, _, _ in groups],
        compiler_params=pltpu.CompilerParams(dimension_semantics=("arbitrary",),
                                             vmem_limit_bytes=VMEM_LIMIT),
        name="post",
    )(*operands, p["wo_rw"], p["wo_hg"], p["ln1_g"], p["ln1_b"], p["w_up"], p["w_down"],
      p["ln2_g"], p["ln2_b"])


def _prep_params(w_in, shift_mu, w0, w1u, a0, a1u, g1u, k_k, k_a, r_k, ln_x_w, ln_x_b, lb_logits,
                 hg_norm_w, w_out, ln1_g, ln1_b, w_up, w_down, ln2_g, ln2_b):
    zw = jnp.zeros((LANES - DECAY_RANK, RWKV_WIDTH), F32)
    za = jnp.zeros((LANES - AICL_RANK, RWKV_WIDTH), F32)
    w_out_bf = w_out[0].astype(BF16)
    return {
        "w_in": w_in[0],
        "shift_mu": shift_mu[0].reshape(1, RWKV_PROJ),
        "w0": w0[0].reshape(1, RWKV_WIDTH), "a0": a0[0].reshape(1, RWKV_WIDTH),
        "k_k": k_k[0].reshape(1, RWKV_WIDTH), "k_a": k_a[0].reshape(1, RWKV_WIDTH),
        "r_k": r_k[0].reshape(1, RWKV_WIDTH),
        "ln_x_w": ln_x_w[0].reshape(1, RWKV_WIDTH), "ln_x_b": ln_x_b[0].reshape(1, RWKV_WIDTH),
        "w1u_pad": jnp.concatenate([w1u[0], zw], axis=0),
        "a1u_pad": jnp.concatenate([za, a1u[0]], axis=0),
        "g1u": g1u[0],
        "lb_logits": lb_logits.astype(F32),
        "hg_norm_w": hg_norm_w[0].reshape(1, HGRN_WIDTH),
        "wo_rw": w_out_bf[:RWKV_WIDTH], "wo_hg": w_out_bf[RWKV_WIDTH:],
        "ln1_g": ln1_g[0].reshape(1, D_MODEL), "ln1_b": ln1_b[0].reshape(1, D_MODEL),
        "w_up": w_up[0].astype(BF16), "w_down": w_down[0].astype(BF16),
        "ln2_g": ln2_g[0].reshape(1, D_MODEL), "ln2_b": ln2_b[0].reshape(1, D_MODEL),
    }


def _mix_group(x, s_rwkv, s_hgrn, s_shift, p, *, tm, bb, chunk, n_groups, pipelined):
    B, T, _ = x.shape
    proj3, last = _proj(x, s_shift.reshape(B, 1, RWKV_PROJ), p, tm)
    o_rw, st_rw, o_hg, st_hg = _mixer(proj3, s_rwkv, s_hgrn, p, bb, chunk, n_groups, pipelined)
    post_in = (x.reshape(B * T, D_MODEL), o_rw.reshape(B * T, RWKV_WIDTH), o_hg.reshape(B * T, HGRN_WIDTH))
    return post_in, (st_rw[None], st_hg[None], last.reshape(1, B, RWKV_PROJ))


def _run_group(x, s_rwkv, s_hgrn, s_shift, p, **cfg):
    post_in, states = _mix_group(x, s_rwkv, s_hgrn, s_shift, p, **cfg)
    (y2,) = _post([post_in], p, cfg["tm"])
    return (y2.reshape(x.shape),) + states


PROMPT_CFG = dict(tm=512, bb=8, chunk=64, n_groups=1, pipelined=True)
SAMPLE_CFG = dict(tm=512, bb=64, chunk=8, n_groups=2, pipelined=False)


def kernel(x_prompt, x_sample, state_rwkv, state_hgrn, state_shift, w_in, shift_mu, w0, w1u, a0, a1u, g1u, k_k, k_a, r_k, ln_x_w, ln_x_b, lb_logits, hg_norm_w, w_out, ln1_g, ln1_b, w_up, w_down, ln2_g, ln2_b):
    assert w_in.shape[0] == DEPTH
    p = _prep_params(w_in, shift_mu, w0, w1u, a0, a1u, g1u, k_k, k_a, r_k, ln_x_w, ln_x_b, lb_logits,
                     hg_norm_w, w_out, ln1_g, ln1_b, w_up, w_down, ln2_g, ln2_b)
    bp = x_prompt.shape[0]
    z_rw = jnp.zeros((bp, RWKV_HEADS, RWKV_HEAD, RWKV_HEAD), F32)
    z_hg = jnp.zeros((bp, HGRN_HEADS, HGRN_HEAD, HGRN_HEAD), F32)
    z_sh = jnp.zeros((bp, RWKV_PROJ), F32)
    post_p, (rw_p, hg_p, sh_p) = _mix_group(x_prompt, z_rw, z_hg, z_sh, p, **PROMPT_CFG)
    post_s, (rw_s, hg_s, sh_s) = _mix_group(x_sample, state_rwkv[0].astype(F32), state_hgrn[0].astype(F32),
                                            state_shift[0].astype(F32), p, **SAMPLE_CFG)
    assert PROMPT_CFG["tm"] == SAMPLE_CFG["tm"]
    y_p, y_s = _post([post_p, post_s], p, PROMPT_CFG["tm"])
    return (y_p.reshape(x_prompt.shape), y_s.reshape(x_sample.shape), rw_p, rw_s, hg_p, hg_s, sh_p, sh_s)
```

```python
import functools
import math

import jax
import jax.numpy as jnp
from jax import lax
from jax.experimental import pallas as pl
from jax.experimental.pallas import tpu as pltpu

F32 = jnp.float32
BF16 = jnp.bfloat16

D_MODEL = 1024
RWKV_WIDTH = 512
RWKV_HEAD = 64
RWKV_HEADS = 8
HGRN_WIDTH = 512
HGRN_HEAD = 128
HGRN_HEADS = 4
DECAY_RANK = 64
AICL_RANK = 64
GATE_RANK = 128
RWKV_PROJ = 3 * RWKV_WIDTH + DECAY_RANK + AICL_RANK + GATE_RANK
HGRN_PROJ = 4 * HGRN_WIDTH
PROJ = RWKV_PROJ + HGRN_PROJ
D_FF = 4 * D_MODEL
DEPTH = 1
ALPHA = (2.0 * DEPTH) ** 0.25
LN_EPS = 1e-5
GN_EPS = RWKV_HEAD * 1e-5
RMS_EPS = 1e-6
KK_NORM_FLOOR = 1e-12
DECAY_SCALE = math.exp(-0.5)

LANES = 128
SUBLANES = 8
PAIRS = RWKV_HEADS // 2
LORA_COL = 3 * RWKV_WIDTH
VMEM_LIMIT = 56 * 1024 * 1024
PROJ_SECTION = 256
POST_SPLIT = 2

NN = ((1,), (0,))
NT = ((1,), (1,))
TN = ((0,), (0,))

RWKV_STASH = (("lhs", 4, BF16), ("rhs", 2, BF16), ("bt_hat", 2, BF16), ("atrt", 2, BF16), ("vb", 1, BF16),
              ("bkh", 2, BF16), ("dec", 0, F32), ("bonus", 1, F32), ("gate", 1, F32))
HGRN_STASH = (("qs", 1, BF16), ("ks", 1, BF16), ("iv", 1, BF16), ("q", 1, F32), ("kin", 1, F32),
              ("bc", 1, F32), ("dec", 0, F32))


def _dot(a, b, dims):
    return lax.dot_general(a, b, (dims, ((), ())), preferred_element_type=F32)


def _bf(x):
    return x.astype(BF16)


def _cumsum_rows(tri, x):
    p0 = _bf(x)
    r1 = x - p0.astype(F32)
    p1 = _bf(r1)
    p2 = _bf(r1 - p1.astype(F32))
    return _dot(tri, p0, NN) + _dot(tri, p1, NN) + _dot(tri, p2, NN)


def _sigmoid(x):
    return 0.5 * jnp.tanh(0.5 * x) + 0.5


def _interleave(tasks):
    gens = list(tasks)
    nxt = [next(g, None) for g in gens]
    spent = {"M": 0.0, "V": 0.0}
    turn = 0
    while any(k is not None for k in nxt):
        ready = {kind: [j for j, k in enumerate(nxt) if k is not None and k[0] == kind] for kind in spent}
        want = "M" if (ready["M"] and (spent["M"] <= spent["V"] or not ready["V"])) else "V"
        if want == "M":
            cands = ready["M"]
            t = cands[turn % len(cands)]
            turn += 1
        else:
            t = ready["V"][0]
        spent[want] += nxt[t][1]
        nxt[t] = next(gens[t], None)


def _chain(*gens):
    for g in gens:
        yield from g


class _ValueStash:
    def __init__(self):
        self.vals = {}

    def put(self, name, val):
        self.vals[name] = val

    def get(self, name, j=None):
        return self.vals[name] if j is None else self.vals[name][j]


class _RefStash:
    def __init__(self, refs, slot, lo, n):
        self.refs, self.slot, self.lo, self.n = refs, slot, lo, n

    def put(self, name, val):
        self.refs[name][self.slot, self.lo:self.lo + self.n] = val

    def get(self, name, j=None):
        if j is None:
            return self.refs[name][self.slot, self.lo:self.lo + self.n]
        return self.refs[name][self.slot, self.lo + j]


def _proj_kernel(x_ref, w_ref, first_ref, mu_ref, lbl_ref, o_ref, last_ref, wbf_scr, carry_scr, *, tiles_per_seq,
                 tile=None):
    nb, tt, _ = x_ref.shape
    if tile is None:
        tile = pl.program_id(0)

        @pl.when(tile == 0)
        def _cast_weights():
            wbf_scr[...] = w_ref[...].astype(BF16)

    xb = x_ref[...].reshape(nb * tt, D_MODEL).astype(BF16)

    def cols(lo, width):
        return jnp.dot(xb, wbf_scr[:, lo:lo + width], preferred_element_type=F32)

    first = first_ref[...]
    if tiles_per_seq > 1:
        first = jnp.where(lax.rem(tile, tiles_per_seq) == 0, first, carry_scr[...])
    last_rows = []
    for lo in range(0, RWKV_PROJ, PROJ_SECTION):
        width = PROJ_SECTION
        p = cols(lo, width)
        p3 = p.reshape(nb, tt, width)
        row = lax.broadcasted_iota(jnp.int32, p3.shape, 1)
        prev = jnp.where(row == 0, first[:, :, lo:lo + width], pltpu.roll(p, 1, 0).reshape(p3.shape))
        o_ref[:, :, lo:lo + width] = p3 + mu_ref[:, lo:lo + width] * (prev - p3)
        last_rows.append(p3[:, tt - 1:tt, :])
    last = jnp.concatenate(last_rows, axis=-1)
    last_ref[...] = last
    if tiles_per_seq > 1:
        carry_scr[...] = last

    logits = lbl_ref[...]
    ex = jnp.exp(logits - jnp.max(logits, axis=0, keepdims=True))
    lb = ex[0:1] / jnp.sum(ex, axis=0, keepdims=True)
    for part, act in enumerate(("silu", "log_gate", "copy", "silu")):
        for lo in range(0, HGRN_WIDTH, PROJ_SECTION):
            col = RWKV_PROJ + part * HGRN_WIDTH + lo
            h = cols(col, PROJ_SECTION)
            if act == "silu":
                h = h * _sigmoid(h)
            elif act == "log_gate":
                lb_s = lb[:, lo:lo + PROJ_SECTION]
                h = jnp.log(lb_s + (1.0 - lb_s) * _sigmoid(h))
            o_ref[:, :, col:col + PROJ_SECTION] = h.reshape(nb, tt, PROJ_SECTION)


def _proj(x3, s_shift3, p, tm):
    B, T, _ = x3.shape
    tt = min(T, tm)
    nb = tm // tt
    tiles_per_seq = T // tt

    def tile_map(i):
        return (i // tiles_per_seq, lax.rem(i, tiles_per_seq), 0)

    def seq_map(i):
        return (i // tiles_per_seq, 0, 0)

    def const(shape):
        return pl.BlockSpec(shape, lambda i: (0, 0), pipeline_mode=pl.Buffered(1))

    return pl.pallas_call(
        functools.partial(_proj_kernel, tiles_per_seq=tiles_per_seq),
        grid=(B * T // tm,),
        in_specs=[pl.BlockSpec((nb, tt, D_MODEL), tile_map),
                  const((D_MODEL, PROJ)),
                  pl.BlockSpec((nb, 1, RWKV_PROJ), seq_map),
                  const((1, RWKV_PROJ)), const((DEPTH + 1, HGRN_WIDTH))],
        out_specs=[pl.BlockSpec((nb, tt, PROJ), tile_map),
                   pl.BlockSpec((nb, 1, RWKV_PROJ), seq_map)],
        out_shape=[jax.ShapeDtypeStruct((B, T, PROJ), F32),
                   jax.ShapeDtypeStruct((B, 1, RWKV_PROJ), F32)],
        scratch_shapes=[pltpu.VMEM((D_MODEL, PROJ), BF16), pltpu.VMEM((nb, 1, RWKV_PROJ), F32)],
        compiler_params=pltpu.CompilerParams(dimension_semantics=("arbitrary",),
                                             vmem_limit_bytes=VMEM_LIMIT),
        name="proj",
    )(x3, p["w_in"], s_shift3, p["shift_mu"], p["lb_logits"])


def _proj_groups_kernel(*refs, meta):
    ng = len(meta)
    w_ref, mu_ref, lbl_ref = refs[2 * ng:2 * ng + 3]
    outs = refs[2 * ng + 3:4 * ng + 3]
    wbf_scr = refs[4 * ng + 3]
    carries = refs[4 * ng + 4:]
    i = pl.program_id(0)

    @pl.when(i == 0)
    def _cast_weights():
        wbf_scr[...] = w_ref[...].astype(BF16)

    for g, (start, tiles, tiles_per_seq) in enumerate(meta):
        x_ref, first_ref = refs[2 * g:2 * g + 2]
        o_ref, last_ref = outs[2 * g:2 * g + 2]
        pl.when(jnp.logical_and(i >= start, i < start + tiles))(functools.partial(
            _proj_kernel, x_ref, w_ref, first_ref, mu_ref, lbl_ref, o_ref, last_ref, wbf_scr, carries[g],
            tiles_per_seq=tiles_per_seq, tile=i - start))


def _proj_groups(groups, p, tms):
    def const(shape):
        return pl.BlockSpec(shape, lambda i: (0, 0), pipeline_mode=pl.Buffered(1))

    meta, in_specs, out_specs, out_shape, scratch, operands = [], [], [], [], [], []
    start = 0
    for g, ((x3, s_shift3), tm) in enumerate(zip(groups, tms)):
        B, T, _ = x3.shape
        tt = min(T, tm)
        nb = tm // tt
        tps = T // tt
        tiles = B * T // tm
        meta.append((start, tiles, tps))

        def local(i, start=start, tiles=tiles):
            return jnp.clip(i - start, 0, tiles - 1)

        def tile_map(i, tps=tps, local=local):
            return (local(i) // tps, lax.rem(local(i), tps), 0)

        def seq_map(i, tps=tps, local=local):
            return (local(i) // tps, 0, 0)

        mode = {} if g == 0 else dict(pipeline_mode=pl.Buffered(1))
        in_specs += [pl.BlockSpec((nb, tt, D_MODEL), tile_map, **mode), pl.BlockSpec((nb, 1, RWKV_PROJ), seq_map, **mode)]
        operands += [x3, s_shift3]
        out_specs += [pl.BlockSpec((nb, tt, PROJ), tile_map), pl.BlockSpec((nb, 1, RWKV_PROJ), seq_map)]
        out_shape += [jax.ShapeDtypeStruct((B, T, PROJ), F32), jax.ShapeDtypeStruct((B, 1, RWKV_PROJ), F32)]
        scratch.append(pltpu.VMEM((nb, 1, RWKV_PROJ), F32))
        start += tiles
    in_specs += [const((D_MODEL, PROJ)), const((1, RWKV_PROJ)), const((DEPTH + 1, HGRN_WIDTH))]
    outs = pl.pallas_call(
        functools.partial(_proj_groups_kernel, meta=tuple(meta)),
        grid=(start,),
        in_specs=in_specs, out_specs=out_specs, out_shape=out_shape,
        scratch_shapes=[pltpu.VMEM((D_MODEL, PROJ), BF16)] + scratch,
        compiler_params=pltpu.CompilerParams(dimension_semantics=("arbitrary",),
                                             vmem_limit_bytes=VMEM_LIMIT),
        name="proj",
    )(*operands, p["w_in"], p["shift_mu"], p["lb_logits"])
    return [(outs[2 * g], outs[2 * g + 1]) for g in range(len(groups))]


def _head_helpers(C):
    lane = lax.broadcasted_iota(jnp.int32, (C, LANES), 1)
    m0 = lane < RWKV_HEAD

    def head_sum(x):
        s0 = jnp.sum(jnp.where(m0, x, 0.0), axis=-1, keepdims=True)
        s1 = jnp.sum(jnp.where(m0, 0.0, x), axis=-1, keepdims=True)
        return jnp.where(m0, s0, s1)

    def stack_heads(x):
        return jnp.concatenate([jnp.where(m0, x, 0.0), jnp.where(m0, 0.0, x)], axis=-2)

    def merge_heads(x):
        return jnp.where(m0, x[:C], x[C:])

    return head_sum, stack_heads, merge_heads


def _rwkv_produce(stash, lo, n, C, r_ref, k_ref, v_ref, l_ref,
                  w0_ref, a0_ref, kk_ref, ka_ref, rk_ref, w1_ref, a1_ref, g1_ref):
    ids = range(n)
    grp = slice(lo, lo + n)
    head_sum, stack_heads, _ = _head_helpers(C)
    rowc = lax.broadcasted_iota(jnp.int32, (C, C), 0)
    colc = lax.broadcasted_iota(jnp.int32, (C, C), 1)
    tri = jnp.where(rowc >= colc, 1.0, 0.0).astype(BF16)

    w1 = _bf(w1_ref[...])
    a1 = _bf(a1_ref[...])
    g1 = _bf(g1_ref[...])

    yield ("V", 30.0 * n)
    xl = l_ref[grp].reshape(n * C, 2 * LANES)
    xl_lo = xl[:, :LANES]
    dw = jnp.dot(_bf(jnp.tanh(xl_lo)), w1, preferred_element_type=F32)
    da = jnp.dot(_bf(xl_lo), a1, preferred_element_type=F32)
    gate = jnp.dot(_bf(_sigmoid(xl[:, LANES:])), g1, preferred_element_type=F32)
    stash.put("gate", gate.reshape(n, C, LANES))

    yield ("V", 60.0 * n)
    lw = (-DECAY_SCALE * _sigmoid(w0_ref[...] + dw)).reshape(n, C, LANES)
    a_lr = _sigmoid(a0_ref[...] + da).reshape(n, C, LANES)
    xk = k_ref[grp]
    kk = xk * kk_ref[...]
    kk = kk * jnp.minimum(lax.rsqrt(head_sum(kk * kk)), 1.0 / KK_NORM_FLOOR)
    k2 = xk * (1.0 + (a_lr - 1.0) * ka_ref[...])
    b = kk * a_lr
    cum = jnp.stack([_cumsum_rows(tri, lw[j]) for j in ids])

    yield ("V", 80.0 * n)
    r = r_ref[grp]
    v = v_ref[grp]
    cum_last = cum[:, C - 1:C, :]
    e_in = jnp.exp(-cum)
    e_out = jnp.exp(cum_last - cum)
    at = -kk * jnp.exp(cum - lw)
    rt = r * jnp.exp(cum)
    bt = b * e_in
    kt = k2 * e_in
    stash.put("lhs", _bf(jnp.concatenate([stack_heads(at), stack_heads(rt)], axis=1)))
    stash.put("rhs", _bf(jnp.concatenate([bt, kt], axis=1)))
    stash.put("bt_hat", _bf(stack_heads(bt)))
    stash.put("atrt", _bf(jnp.concatenate([at, rt], axis=1)))
    stash.put("vb", _bf(v))
    stash.put("bkh", _bf(jnp.concatenate([b * e_out, k2 * e_out], axis=1)))
    stash.put("dec", jnp.exp(cum_last))
    stash.put("bonus", head_sum(r * k2 * rk_ref[...]) * v)


def _rwkv_consume(stash, lo, n, C, dot_cost, lnw_ref, lnb_ref, o_ref, s_scr):
    H = RWKV_HEAD
    ids = range(n)
    grp = slice(lo, lo + n)
    head_sum, stack_heads, merge_heads = _head_helpers(C)
    r2 = lax.broadcasted_iota(jnp.int32, (2 * C, 2 * C), 0)
    c2 = lax.broadcasted_iota(jnp.int32, (2 * C, 2 * C), 1)
    t2 = jnp.bitwise_and(r2, C - 1)
    s2 = jnp.bitwise_and(c2, C - 1)
    strict2 = t2 > s2
    incl2 = t2 >= s2
    strict_bd = jnp.logical_and(strict2, (r2 >= C) == (c2 >= C))
    eye2 = jnp.where(r2 == c2, 1.0, 0.0).astype(F32)
    bl_r = lax.broadcasted_iota(jnp.int32, (LANES, LANES), 0) < H
    bl_c = lax.broadcasted_iota(jnp.int32, (LANES, LANES), 1) < H
    blockdiag = bl_r == bl_c
    zeros_c = jnp.zeros((C, LANES), BF16)

    yield ("M", 2 * dot_cost)
    g_a, g_r, pw, inv = [], [], [], []
    for j in ids:
        lhs = stash.get("lhs", j)
        g = _dot(lhs, stash.get("rhs", j), NT)
        g_a.append(_bf(jnp.where(strict2, g[:2 * C], 0.0)))
        g_r.append(_bf(jnp.where(incl2, g[2 * C:], 0.0)))
        n_bd = jnp.where(strict_bd, _dot(lhs[:2 * C], stash.get("bt_hat", j), NT), 0.0)
        inv.append(eye2 + n_bd)
        pw.append(_bf(n_bd))
    yield ("M", 2 * dot_cost)
    S = [s_scr[lo + j] for j in ids]
    pq = [_dot(stash.get("atrt", j), _bf(S[j]), NT) for j in ids]
    pw = [_bf(_dot(pw[j], pw[j], NN)) for j in ids]
    span = 4
    while span < C:
        yield ("M", 2 * dot_cost)
        nxt = [_bf(_dot(pw[j], pw[j], NN)) for j in ids]
        inv = [inv[j] + _dot(_bf(inv[j]), pw[j], NN) for j in ids]
        pw = nxt
        span *= 2
    yield ("M", 2 * dot_cost)
    inv = [inv[j] + _dot(_bf(inv[j]), pw[j], NN) for j in ids]
    vb = [stash.get("vb", j) for j in ids]
    w_hat = [_bf(stack_heads(pq[j][:C] + merge_heads(_dot(g_a[j], jnp.concatenate([zeros_c, vb[j]], axis=0), NN))))
             for j in ids]
    yield ("M", dot_cost)
    z = []
    for j in ids:
        u2 = _dot(_bf(inv[j]), w_hat[j], NN)
        z.append(jnp.concatenate([_bf(u2[:C] + u2[C:]), vb[j]], axis=0))
    yield ("M", 2 * dot_cost)
    y = [pq[j][C:] + merge_heads(_dot(g_r[j], z[j], NN)) for j in ids]
    for j in ids:
        upd = _dot(z[j], stash.get("bkh", j), TN)
        s_scr[lo + j] = S[j] * stash.get("dec", j) + jnp.where(blockdiag, upd, 0.0)

    yield ("V", 40.0 * n)
    y = jnp.stack(y)
    mu = head_sum(y) * (1.0 / H)
    dy = y - mu
    var = head_sum(dy * dy) * (1.0 / H)
    yn = dy * lax.rsqrt(var + GN_EPS) * lnw_ref[...] + lnb_ref[...]
    o_ref[grp] = (yn + stash.get("bonus")) * stash.get("gate")


def _mid_rows(x, h):
    R = x.shape[0]
    row = lax.broadcasted_iota(jnp.int32, x.shape, 0)
    if 2 * h >= SUBLANES:
        return jnp.concatenate(
            [jnp.broadcast_to(x[b * 2 * h + h - 1:b * 2 * h + h], (2 * h, LANES)) for b in range(R // (2 * h))],
            axis=0)
    if h == 1:
        return jnp.where(jnp.bitwise_and(row, 1) == 1, pltpu.roll(x, 1, 0), x)
    picks = []
    for half in range(SUBLANES // (2 * h)):
        r0 = half * 2 * h + h - 1
        picks.append(jnp.concatenate(
            [jnp.broadcast_to(x[g * SUBLANES + r0:g * SUBLANES + r0 + 1], (SUBLANES, LANES))
             for g in range(R // SUBLANES)], axis=0))
    out = picks[-1]
    sub = jnp.bitwise_and(row, SUBLANES - 1)
    for half in range(len(picks) - 2, -1, -1):
        out = jnp.where(sub < (half + 1) * 2 * h, picks[half], out)
    return out


def _hgrn_produce(stash, lo, n, C, q_ref, lf_ref, i_ref):
    ids = range(n)
    grp = slice(lo, lo + n)
    t_i = lax.broadcasted_iota(jnp.int32, (C, C), 0)
    s_i = lax.broadcasted_iota(jnp.int32, (C, C), 1)
    tri = jnp.where(t_i >= s_i, 1.0, 0.0).astype(BF16)

    yield ("V", 20.0 * n)
    q = q_ref[grp]
    lf = lf_ref[grp]
    kin = 1.0 - jnp.exp(lf)
    bc = jnp.stack([_cumsum_rows(tri, lf[j]) for j in ids])

    yield ("V", 30.0 * n)
    b_last = bc[:, C - 1:C, :]
    stash.put("qs", _bf(q * jnp.exp(bc)))
    stash.put("ks", _bf(kin * jnp.exp(b_last - bc)))
    stash.put("iv", _bf(i_ref[grp]))
    stash.put("q", q)
    stash.put("kin", kin)
    stash.put("bc", bc)
    stash.put("dec", jnp.exp(b_last))


def _hgrn_consume(stash, lo, n, C, dot_cost, g_ref, hgw_ref, o_ref, s_scr):
    ids = range(n)
    grp = slice(lo, lo + n)
    t_i = lax.broadcasted_iota(jnp.int32, (C, C), 0)
    s_i = lax.broadcasted_iota(jnp.int32, (C, C), 1)
    halves = []
    h = C // 2
    while h >= 1:
        halves.append(h)
        h //= 2

    def level_mask(h):
        same = jnp.bitwise_and(t_i, -2 * h) == jnp.bitwise_and(s_i, -2 * h)
        return jnp.logical_and(same, jnp.logical_and(jnp.bitwise_and(t_i, h) != 0, jnp.bitwise_and(s_i, h) == 0))

    def seq(x, j):
        return _bf(x[j * C:(j + 1) * C])

    yield ("M", dot_cost)
    S = [s_scr[lo + j] for j in ids]
    o_state = [_dot(stash.get("qs", j), _bf(S[j]), NT) for j in ids]
    yield ("M", dot_cost)
    iv = [stash.get("iv", j) for j in ids]
    for j in ids:
        s_scr[lo + j] = S[j] * stash.get("dec", j) + _dot(iv[j], stash.get("ks", j), TN)
    yield ("M", dot_cost)
    q = stash.get("q").reshape(n * C, LANES)
    kin = stash.get("kin").reshape(n * C, LANES)
    bc = stash.get("bc").reshape(n * C, LANES)
    att = [jnp.where(t_i == s_i, _dot(seq(q, j), seq(kin, j), NT), 0.0) for j in ids]
    for h in halves:
        yield ("V", 25.0 * n)
        dmid = jnp.exp(-jnp.abs(bc - _mid_rows(bc, h)))
        ql = q * dmid
        kl = kin * dmid
        yield ("M", dot_cost)
        msk = level_mask(h)
        att = [jnp.where(msk, _dot(seq(ql, j), seq(kl, j), NT), att[j]) for j in ids]
    yield ("M", dot_cost)
    o_intra = [_dot(_bf(att[j]), iv[j], NN) for j in ids]

    yield ("V", 25.0 * n)
    o = jnp.stack([o_intra[j] + o_state[j] for j in ids])
    ms = jnp.mean(o * o, axis=-1, keepdims=True)
    o_ref[grp] = o * lax.rsqrt(ms + RMS_EPS) * hgw_ref[...] * g_ref[grp]


MXU_STAGE_COST = {True: (22.0, 15.0), False: (30.0, 20.0)}
N_MIXER_IN = 21
N_MIXER_OUT = 4
N_MIXER_STATE_SCRATCH = 2


def _mixer_kernel(*refs, bb, C, n_groups, pipelined):
    (r_ref, k_ref, v_ref, l_ref, w0_ref, a0_ref, kk_ref, ka_ref, rk_ref,
     lnw_ref, lnb_ref, w1_ref, a1_ref, g1_ref, srw0_ref,
     q_ref, lf_ref, i_ref, g_ref, hgw_ref, shg0_ref) = refs[:N_MIXER_IN]
    orw_ref, strw_ref, ohg_ref, sthg_ref = refs[N_MIXER_IN:N_MIXER_IN + N_MIXER_OUT]
    scr = refs[N_MIXER_IN + N_MIXER_OUT:]
    srw_scr, shg_scr = scr[:N_MIXER_STATE_SCRATCH]
    stash_refs = scr[N_MIXER_STATE_SCRATCH:]
    if pipelined:
        rw_refs = {name: stash_refs[k] for k, (name, _, _) in enumerate(RWKV_STASH)}
        hg_refs = {name: stash_refs[len(RWKV_STASH) + k] for k, (name, _, _) in enumerate(HGRN_STASH)}

    c = pl.program_id(2)
    last = pl.num_programs(2) - 1
    H = RWKV_HEAD

    @pl.when(c == 0)
    def _init():
        z = jnp.zeros((H, H), F32)
        for i in range(bb):
            top = jnp.concatenate([srw0_ref[i, 0], z], axis=1)
            bot = jnp.concatenate([z, srw0_ref[i, 1]], axis=1)
            srw_scr[i] = jnp.concatenate([top, bot], axis=0)
            shg_scr[i] = shg0_ref[i, 0].T
        if pipelined:
            for table in (rw_refs, hg_refs):
                for name, ref in table.items():
                    fill = jnp.ones if name == "dec" else jnp.zeros
                    ref[1] = fill(ref.shape[1:], ref.dtype)

    per = bb // n_groups
    groups = [(gidx * per, per) for gidx in range(n_groups)]

    def rwkv_produce(stash, lo, n):
        return _rwkv_produce(stash, lo, n, C, r_ref, k_ref, v_ref, l_ref,
                             w0_ref, a0_ref, kk_ref, ka_ref, rk_ref, w1_ref, a1_ref, g1_ref)

    def rwkv_consume(stash, lo, n):
        return _rwkv_consume(stash, lo, n, C, MXU_STAGE_COST[pipelined][0] * n, lnw_ref, lnb_ref, orw_ref, srw_scr)

    def hgrn_produce(stash, lo, n):
        return _hgrn_produce(stash, lo, n, C, q_ref, lf_ref, i_ref)

    def hgrn_consume(stash, lo, n):
        return _hgrn_consume(stash, lo, n, C, MXU_STAGE_COST[pipelined][1] * n, g_ref, hgw_ref, ohg_ref, shg_scr)

    if not pipelined:
        tasks = []
        for lo, n in groups:
            rw, hg = _ValueStash(), _ValueStash()
            tasks.append(_chain(rwkv_produce(rw, lo, n), rwkv_consume(rw, lo, n)))
            tasks.append(_chain(hgrn_produce(hg, lo, n), hgrn_consume(hg, lo, n)))
        _interleave(tasks)
    else:
        def step(write_slot):
            read_slot = 1 - write_slot
            tasks = []
            for lo, n in groups:
                tasks.append(rwkv_consume(_RefStash(rw_refs, read_slot, lo, n), lo, n))
                tasks.append(hgrn_consume(_RefStash(hg_refs, read_slot, lo, n), lo, n))
            for lo, n in groups:
                tasks.append(rwkv_produce(_RefStash(rw_refs, write_slot, lo, n), lo, n))
                tasks.append(hgrn_produce(_RefStash(hg_refs, write_slot, lo, n), lo, n))
            _interleave(tasks)

        parity = lax.rem(c, 2)
        pl.when(parity == 0)(functools.partial(step, 0))
        pl.when(parity == 1)(functools.partial(step, 1))

    @pl.when(c == last)
    def _fin():
        for i in range(bb):
            s_fin = srw_scr[i]
            strw_ref[i, 0] = s_fin[:H, :H]
            strw_ref[i, 1] = s_fin[H:, H:]
            sthg_ref[i, 0] = shg_scr[i].T


def _mixer(proj3, s_rwkv, s_hgrn, p, bb, C, n_groups, pipelined):
    B, T, _ = proj3.shape
    nc = T // C
    grid = (B // bb, PAIRS, nc + 1 if pipelined else nc)
    nb = RWKV_WIDTH // LANES
    hg0 = RWKV_PROJ // LANES
    lora_blk = LORA_COL // (2 * LANES)

    if pipelined:
        def produced(c):
            return jnp.minimum(c, nc - 1)

        def consumed(c):
            return jnp.maximum(c - 1, 0)
    else:
        produced = consumed = lambda c: c

    def tok(col0, chunk=produced):
        return pl.BlockSpec((bb, C, LANES), lambda b, h, c: (b, chunk(c), col0 + h))

    def vec(col0):
        return pl.BlockSpec((1, LANES), lambda b, h, c: (0, col0 + h))

    in_specs = [
        tok(0), tok(nb), tok(2 * nb),
        pl.BlockSpec((bb, C, 2 * LANES), lambda b, h, c: (b, produced(c), lora_blk)),
        vec(0), vec(0), vec(0), vec(0), vec(0), vec(0), vec(0),
        pl.BlockSpec((LANES, LANES), lambda b, h, c: (0, h)),
        pl.BlockSpec((LANES, LANES), lambda b, h, c: (0, h)),
        pl.BlockSpec((GATE_RANK, LANES), lambda b, h, c: (0, h)),
        pl.BlockSpec((bb, 2, RWKV_HEAD, RWKV_HEAD), lambda b, h, c: (b, h, 0, 0)),
        tok(hg0), tok(hg0 + nb), tok(hg0 + 2 * nb), tok(hg0 + 3 * nb, consumed),
        vec(0),
        pl.BlockSpec((bb, 1, HGRN_HEAD, HGRN_HEAD), lambda b, h, c: (b, h, 0, 0)),
    ]
    assert len(in_specs) == N_MIXER_IN
    out_specs = [
        pl.BlockSpec((bb, C, LANES), lambda b, h, c: (b, consumed(c), h)),
        pl.BlockSpec((bb, 2, RWKV_HEAD, RWKV_HEAD), lambda b, h, c: (b, h, 0, 0)),
        pl.BlockSpec((bb, C, LANES), lambda b, h, c: (b, consumed(c), h)),
        pl.BlockSpec((bb, 1, HGRN_HEAD, HGRN_HEAD), lambda b, h, c: (b, h, 0, 0)),
    ]
    out_shape = [jax.ShapeDtypeStruct((B, T, RWKV_WIDTH), F32),
                 jax.ShapeDtypeStruct((B, RWKV_HEADS, RWKV_HEAD, RWKV_HEAD), F32),
                 jax.ShapeDtypeStruct((B, T, HGRN_WIDTH), F32),
                 jax.ShapeDtypeStruct((B, HGRN_HEADS, HGRN_HEAD, HGRN_HEAD), F32)]
    scratch = [pltpu.VMEM((bb, LANES, LANES), F32), pltpu.VMEM((bb, HGRN_HEAD, HGRN_HEAD), F32)]
    assert len(scratch) == N_MIXER_STATE_SCRATCH
    if pipelined:
        scratch += [pltpu.VMEM((2, bb, rows * C if rows else 1, LANES), dt)
                    for _, rows, dt in RWKV_STASH + HGRN_STASH]
    return pl.pallas_call(
        functools.partial(_mixer_kernel, bb=bb, C=C, n_groups=n_groups, pipelined=pipelined),
        grid=grid, in_specs=in_specs, out_specs=out_specs, out_shape=out_shape,
        scratch_shapes=scratch,
        compiler_params=pltpu.CompilerParams(
            dimension_semantics=("arbitrary", "arbitrary", "arbitrary"),
            vmem_limit_bytes=VMEM_LIMIT),
        name="mixer",
    )(proj3, proj3, proj3, proj3,
      p["w0"], p["a0"], p["k_k"], p["k_a"], p["r_k"], p["ln_x_w"], p["ln_x_b"],
      p["w1u_pad"], p["a1u_pad"], p["g1u"], s_rwkv,
      proj3, proj3, proj3, proj3, p["hg_norm_w"], s_hgrn)


def _layer_norm(x, g, b):
    mu = jnp.mean(x, axis=-1, keepdims=True)
    d = x - mu
    var = jnp.mean(d * d, axis=-1, keepdims=True)
    return d * lax.rsqrt(var + LN_EPS) * g + b


def _post_kernel(*refs, tiles):
    ng = len(tiles)
    weights = refs[3 * ng:3 * ng + 8]
    outs = refs[3 * ng + 8:]
    i = pl.program_id(0)
    lo = 0
    for g in range(ng):
        x_ref, orw_ref, ohg_ref = refs[3 * g:3 * g + 3]
        pl.when(jnp.logical_and(i >= lo, i < lo + tiles[g]))(
            functools.partial(_post_tile, x_ref, orw_ref, ohg_ref, *weights, outs[g]))
        lo += tiles[g]


def _post_tile(x_ref, orw_ref, ohg_ref, wo1_ref, wo2_ref, g1_ref, b1_ref, wup_ref, wdn_ref,
               g2_ref, b2_ref, y_ref):
    tm = x_ref.shape[0]
    rows = [slice(k * tm // POST_SPLIT, (k + 1) * tm // POST_SPLIT) for k in range(POST_SPLIT)]

    def mix_stage(r):
        return (jnp.dot(orw_ref[r, :].astype(BF16), wo1_ref[...], preferred_element_type=F32)
                + jnp.dot(ohg_ref[r, :].astype(BF16), wo2_ref[...], preferred_element_type=F32))

    def up_stage(r, mix):
        h1 = _layer_norm(ALPHA * x_ref[r, :] + mix, g1_ref[...], b1_ref[...])
        return h1, jnp.dot(h1.astype(BF16), wup_ref[...], preferred_element_type=F32)

    def down_stage(up):
        up = jnp.square(jnp.maximum(up, 0.0))
        return jnp.dot(up.astype(BF16), wdn_ref[...], preferred_element_type=F32)

    def out_stage(r, h1, ff):
        y_ref[r, :] = _layer_norm(ALPHA * h1 + ff, g2_ref[...], b2_ref[...])

    mix, h1, up, ff = {}, {}, {}, {}
    for step in range(POST_SPLIT + 3):
        if step < POST_SPLIT:
            mix[step] = mix_stage(rows[step])
        k = step - 1
        if 0 <= k < POST_SPLIT:
            h1[k], up[k] = up_stage(rows[k], mix.pop(k))
        k = step - 2
        if 0 <= k < POST_SPLIT:
            ff[k] = down_stage(up.pop(k))
        k = step - 3
        if 0 <= k < POST_SPLIT:
            out_stage(rows[k], h1.pop(k), ff.pop(k))


def _post(groups, p, tm):
    tiles = tuple(x2.shape[0] // tm for x2, _, _ in groups)
    starts = [sum(tiles[:g]) for g in range(len(tiles))]

    def const(shape):
        return pl.BlockSpec(shape, lambda i: (0, 0), pipeline_mode=pl.Buffered(1))

    def rows(width, g):
        clip = lambda i: (jnp.clip(i - starts[g], 0, tiles[g] - 1), 0)
        if g == 0:
            return pl.BlockSpec((tm, width), clip)
        return pl.BlockSpec((tm, width), clip, pipeline_mode=pl.Buffered(1))

    in_specs, operands = [], []
    for g, arrays in enumerate(groups):
        in_specs += [rows(D_MODEL, g), rows(RWKV_WIDTH, g), rows(HGRN_WIDTH, g)]
        operands += list(arrays)
    in_specs += [const((RWKV_WIDTH, D_MODEL)), const((HGRN_WIDTH, D_MODEL)),
                 const((1, D_MODEL)), const((1, D_MODEL)),
                 const((D_MODEL, D_FF)), const((D_FF, D_MODEL)),
                 const((1, D_MODEL)), const((1, D_MODEL))]
    return pl.pallas_call(
        functools.partial(_post_kernel, tiles=tiles),
        grid=(sum(tiles),),
        in_specs=in_specs,
        out_specs=[pl.BlockSpec((tm, D_MODEL), lambda i, g=g: (jnp.clip(i - starts[g], 0, tiles[g] - 1), 0))
                   for g in range(len(groups))],
        out_shape=[jax.ShapeDtypeStruct((x2.shape[0], D_MODEL), F32) for x2, _, _ in groups],
        compiler_params=pltpu.CompilerParams(dimension_semantics=("arbitrary",),
                                             vmem_limit_bytes=VMEM_LIMIT),
        name="post",
    )(*operands, p["wo_rw"], p["wo_hg"], p["ln1_g"], p["ln1_b"], p["w_up"], p["w_down"],
      p["ln2_g"], p["ln2_b"])


def _prep_params(w_in, shift_mu, w0, w1u, a0, a1u, g1u, k_k, k_a, r_k, ln_x_w, ln_x_b, lb_logits,
                 hg_norm_w, w_out, ln1_g, ln1_b, w_up, w_down, ln2_g, ln2_b):
    zw = jnp.zeros((LANES - DECAY_RANK, RWKV_WIDTH), F32)
    za = jnp.zeros((LANES - AICL_RANK, RWKV_WIDTH), F32)
    w_out_bf = w_out[0].astype(BF16)
    return {
        "w_in": w_in[0],
        "shift_mu": shift_mu[0].reshape(1, RWKV_PROJ),
        "w0": w0[0].reshape(1, RWKV_WIDTH), "a0": a0[0].reshape(1, RWKV_WIDTH),
        "k_k": k_k[0].reshape(1, RWKV_WIDTH), "k_a": k_a[0].reshape(1, RWKV_WIDTH),
        "r_k": r_k[0].reshape(1, RWKV_WIDTH),
        "ln_x_w": ln_x_w[0].reshape(1, RWKV_WIDTH), "ln_x_b": ln_x_b[0].reshape(1, RWKV_WIDTH),
        "w1u_pad": jnp.concatenate([w1u[0], zw], axis=0),
        "a1u_pad": jnp.concatenate([za, a1u[0]], axis=0),
        "g1u": g1u[0],
        "lb_logits": lb_logits.astype(F32),
        "hg_norm_w": hg_norm_w[0].reshape(1, HGRN_WIDTH),
        "wo_rw": w_out_bf[:RWKV_WIDTH], "wo_hg": w_out_bf[RWKV_WIDTH:],
        "ln1_g": ln1_g[0].reshape(1, D_MODEL), "ln1_b": ln1_b[0].reshape(1, D_MODEL),
        "w_up": w_up[0].astype(BF16), "w_down": w_down[0].astype(BF16),
        "ln2_g": ln2_g[0].reshape(1, D_MODEL), "ln2_b": ln2_b[0].reshape(1, D_MODEL),
    }


def _mix_group(x, s_rwkv, s_hgrn, s_shift, p, *, tm, bb, chunk, n_groups, pipelined, proj=None):
    B, T, _ = x.shape
    proj3, last = _proj(x, s_shift.reshape(B, 1, RWKV_PROJ), p, tm) if proj is None else proj
    o_rw, st_rw, o_hg, st_hg = _mixer(proj3, s_rwkv, s_hgrn, p, bb, chunk, n_groups, pipelined)
    post_in = (x.reshape(B * T, D_MODEL), o_rw.reshape(B * T, RWKV_WIDTH), o_hg.reshape(B * T, HGRN_WIDTH))
    return post_in, (st_rw[None], st_hg[None], last.reshape(1, B, RWKV_PROJ))


def _run_group(x, s_rwkv, s_hgrn, s_shift, p, **cfg):
    post_in, states = _mix_group(x, s_rwkv, s_hgrn, s_shift, p, **cfg)
    (y2,) = _post([post_in], p, cfg["tm"])
    return (y2.reshape(x.shape),) + states


PROMPT_CFG = dict(tm=512, bb=8, chunk=64, n_groups=1, pipelined=True)
SAMPLE_CFG = dict(tm=512, bb=64, chunk=8, n_groups=2, pipelined=False)
SAMPLE_PROJ_TM = 256


def kernel(x_prompt, x_sample, state_rwkv, state_hgrn, state_shift, w_in, shift_mu, w0, w1u, a0, a1u, g1u, k_k, k_a, r_k, ln_x_w, ln_x_b, lb_logits, hg_norm_w, w_out, ln1_g, ln1_b, w_up, w_down, ln2_g, ln2_b):
    assert w_in.shape[0] == DEPTH
    p = _prep_params(w_in, shift_mu, w0, w1u, a0, a1u, g1u, k_k, k_a, r_k, ln_x_w, ln_x_b, lb_logits,
                     hg_norm_w, w_out, ln1_g, ln1_b, w_up, w_down, ln2_g, ln2_b)
    bp = x_prompt.shape[0]
    z_rw = jnp.zeros((bp, RWKV_HEADS, RWKV_HEAD, RWKV_HEAD), F32)
    z_hg = jnp.zeros((bp, HGRN_HEADS, HGRN_HEAD, HGRN_HEAD), F32)
    z_sh = jnp.zeros((bp, RWKV_PROJ), F32)
    s_sh = state_shift[0].astype(F32)
    bs = x_sample.shape[0]
    proj_p, proj_s = _proj_groups([(x_prompt, z_sh.reshape(bp, 1, RWKV_PROJ)), (x_sample, s_sh.reshape(bs, 1, RWKV_PROJ))],
                                  p, (PROMPT_CFG["tm"], SAMPLE_PROJ_TM))
    post_p, (rw_p, hg_p, sh_p) = _mix_group(x_prompt, z_rw, z_hg, z_sh, p, proj=proj_p, **PROMPT_CFG)
    post_s, (rw_s, hg_s, sh_s) = _mix_group(x_sample, state_rwkv[0].astype(F32), state_hgrn[0].astype(F32),
                                            s_sh, p, proj=proj_s, **SAMPLE_CFG)
    assert PROMPT_CFG["tm"] == SAMPLE_CFG["tm"]
    y_p, y_s = _post([post_p, post_s], p, PROMPT_CFG["tm"])
    return (y_p.reshape(x_prompt.shape), y_s.reshape(x_sample.shape), rw_p, rw_s, hg_p, hg_s, sh_p, sh_s)
```
